```python
import math
import jax, jax.numpy as jnp
from jax import lax
import numpy as np

D_MODEL = 1024
BATCH = 2
SEQ = 8192
DEPTH = 2
DEC_BATCH = 128
DEC_SEQ = 8
PAST_LEN = 16384
PAGE_SIZE = 128

HEAD_DIM = 64
A_HEADS = 16
A_KV_HEADS = 2
MOBA_BLOCK = 256
MOBA_TOPK = 3
MOBA_QCHUNK = 64
B_CHANNELS = 1024
B_CONV_WIDTH = 31
C_HEADS = 16
C_KV_HEADS = 2
C_WINDOW = 128
N_BUCKETS = 32
MAX_DISTANCE = 128
REL_HEADS = 16
EPS = 1e-6
N_A_LAYERS = (DEPTH + 1) // 2
N_C_LAYERS = DEPTH // 2
A_Q = A_HEADS * HEAD_DIM
A_KV = A_KV_HEADS * HEAD_DIM
A_IN = 2 * A_Q + 2 * A_KV + 3 * B_CHANNELS
A_MIX = A_Q + B_CHANNELS
C_Q = C_HEADS * HEAD_DIM
C_KV = C_KV_HEADS * HEAD_DIM
C_IN = 2 * C_Q + 2 * C_KV

kernel_name = 'moba_conformer_swa_hybrid_step'

F32 = jnp.float32


def _split(x, sizes):
    cuts, s = [], 0
    for n in sizes[:-1]:
        s += n
        cuts.append(s)
    return jnp.split(x, cuts, axis=-1)


def rmsnorm(x, g):
    xf = x.astype(F32)
    y = xf * lax.rsqrt(jnp.mean(xf * xf, axis=-1, keepdims=True) + EPS)
    return (y * g.astype(F32)).astype(x.dtype)


def adaln(c, w, b):
    m = jnp.dot(jax.nn.silu(c), w) + b
    shift, scale, gate = jnp.split(m, 3, axis=-1)
    return shift[:, None], scale[:, None], gate[:, None]


def t5_bucket(n):
    n = jnp.maximum(n, 0)
    max_exact = N_BUCKETS // 2
    nf = jnp.maximum(n, 1).astype(F32)
    large = max_exact + (jnp.log(nf / max_exact) / math.log(MAX_DISTANCE / max_exact)
                         * (N_BUCKETS - max_exact)).astype(jnp.int32)
    large = jnp.minimum(large, N_BUCKETS - 1)
    return jnp.where(n < max_exact, n, large)


def rel_bias_lookup(rel_bias, dist):
    hidx = jnp.arange(rel_bias.shape[1]).reshape((-1,) + (1,) * (dist.ndim - 1))
    return rel_bias.astype(F32).T[hidx, t5_bucket(dist)]


def moba_select(q, kbar, n_past, k):
    Q, H, dh = q.shape
    KV = kbar.shape[1]
    G = H // KV
    s = jnp.einsum('qkgd,nkd->kgqn', q.astype(F32).reshape(Q, KV, G, dh), kbar).reshape(H, Q, -1)
    s = jnp.where(jnp.arange(kbar.shape[0]) < n_past, s, -jnp.inf)
    _, sel = lax.top_k(s, k)
    return sel, sel < n_past


def gather_blocks(kb, vb, sel, ok):
    H, Q, K = sel.shape
    kvh = (jnp.arange(H) // (H // kb.shape[1]))[:, None, None]
    ks = kb[sel, kvh].reshape(H, Q, K * MOBA_BLOCK, -1)
    vs = vb[sel, kvh].reshape(H, Q, K * MOBA_BLOCK, -1)
    pos = (sel[..., None] * MOBA_BLOCK + jnp.arange(MOBA_BLOCK)).reshape(H, Q, -1)
    okk = jnp.broadcast_to(ok[..., None], (H, Q, K, MOBA_BLOCK)).reshape(H, Q, -1)
    return ks, vs, pos, okk


def moba_core(q, q_pos, sel_kv, k_own, v_own, pos_own, rel_bias):
    Q, H, dh = q.shape
    KV = k_own.shape[1]
    G = H // KV
    qf = q.astype(F32) * (dh ** -0.5)
    d_own = q_pos[:, None] - pos_own[None, :]
    lo = jnp.einsum('qkgd,lkd->kgql', qf.reshape(Q, KV, G, dh), k_own.astype(F32)).reshape(H, Q, -1)
    lo = jnp.where((d_own >= 0)[None], lo + rel_bias_lookup(rel_bias, d_own[None]), -jnp.inf)
    if sel_kv is None:
        p_own = jax.nn.softmax(lo, axis=-1)
        out = jnp.zeros((Q, H, dh), F32)
    else:
        k_sel, v_sel, pos_sel, ok_sel = sel_kv
        ls = jnp.einsum('qhd,hqld->hql', qf, k_sel.astype(F32))
        ls = jnp.where(ok_sel, ls + rel_bias_lookup(rel_bias, q_pos[None, :, None] - pos_sel), -jnp.inf)
        L = ls.shape[-1]
        p = jax.nn.softmax(jnp.concatenate([ls, lo], axis=-1), axis=-1)
        p_own = p[..., L:]
        out = jnp.einsum('hql,hqld->qhd', p[..., :L], v_sel.astype(F32))
    out = out + jnp.einsum('kgql,lkd->qkgd', p_own.reshape(KV, G, Q, -1), v_own.astype(F32)).reshape(Q, H, dh)
    return out.astype(q.dtype)


def moba_prompt(q, k, v, rel_bias):
    B, S, H, dh = q.shape
    KV = k.shape[2]
    nblk = -(-S // MOBA_BLOCK)
    pad = nblk * MOBA_BLOCK - S
    kp = jnp.pad(k, ((0, 0), (0, pad), (0, 0), (0, 0))).reshape(B, nblk, MOBA_BLOCK, KV, dh).transpose(0, 1, 3, 2, 4)
    vp = jnp.pad(v, ((0, 0), (0, pad), (0, 0), (0, 0))).reshape(B, nblk, MOBA_BLOCK, KV, dh).transpose(0, 1, 3, 2, 4)
    kbar = kp.astype(F32).mean(axis=3)
    topk = min(MOBA_TOPK, nblk)
    nc = S // MOBA_QCHUNK
    qc = q.reshape(B * nc, MOBA_QCHUNK, H, dh)

    def step(args):
        qi, j = args
        b = j // nc
        ci = j % nc
        q_pos = ci * MOBA_QCHUNK + jnp.arange(MOBA_QCHUNK)
        blk = (ci * MOBA_QCHUNK) // MOBA_BLOCK
        kb, vb = kp[b], vp[b]
        sel, ok = moba_select(qi, kbar[b], blk, topk)
        sel_kv = gather_blocks(kb, vb, sel, ok)
        pos_own = blk * MOBA_BLOCK + jnp.arange(MOBA_BLOCK)
        return moba_core(qi, q_pos, sel_kv, kb[blk].transpose(1, 0, 2), vb[blk].transpose(1, 0, 2), pos_own, rel_bias)

    out = lax.map(step, (qc, jnp.arange(B * nc)))
    return out.reshape(B, S, H, dh)


def moba_sample(q, k, v, k_pool, v_pool, page_table, rel_bias):
    N, T, H, dh = q.shape
    KV = k.shape[2]
    nb = PAST_LEN // MOBA_BLOCK
    topk = min(MOBA_TOPK, nb)
    q_pos = PAST_LEN + jnp.arange(T)
    pos_own = jnp.arange(nb * MOBA_BLOCK, PAST_LEN + T)

    def step(args):
        qi, kn, vn, pt = args
        kpast = k_pool[pt].reshape(PAST_LEN, KV, dh)
        vpast = v_pool[pt].reshape(PAST_LEN, KV, dh)
        k_own = jnp.concatenate([kpast[nb * MOBA_BLOCK:], kn], axis=0)
        v_own = jnp.concatenate([vpast[nb * MOBA_BLOCK:], vn], axis=0)
        sel_kv = None
        if topk > 0:
            kb = kpast[:nb * MOBA_BLOCK].reshape(nb, MOBA_BLOCK, KV, dh).transpose(0, 2, 1, 3)
            vb = vpast[:nb * MOBA_BLOCK].reshape(nb, MOBA_BLOCK, KV, dh).transpose(0, 2, 1, 3)
            sel, ok = moba_select(qi, kb.astype(F32).mean(axis=2), nb, topk)
            sel_kv = gather_blocks(kb, vb, sel, ok)
        return moba_core(qi, q_pos, sel_kv, k_own, v_own, pos_own, rel_bias)

    return lax.map(step, (q, k, v, page_table))


def conformer_conv(u, hist, conv_w, conv_b, ln_g, ln_b):
    C = u.shape[-1]
    W = conv_w.shape[0]
    xp = jnp.concatenate([hist.astype(u.dtype), u], axis=1)
    y = lax.conv_general_dilated(xp, conv_w[:, None, :].astype(u.dtype), (1,), 'VALID',
                                 dimension_numbers=('NWC', 'WIO', 'NWC'), feature_group_count=C)
    y = y.astype(F32) + conv_b.astype(F32)
    mu = jnp.mean(y, axis=-1, keepdims=True)
    var = jnp.mean(jnp.square(y - mu), axis=-1, keepdims=True)
    y = (y - mu) * lax.rsqrt(var + EPS) * ln_g.astype(F32) + ln_b.astype(F32)
    return jax.nn.silu(y).astype(u.dtype), xp[:, xp.shape[1] - (W - 1):]


def swa_prompt(q, k, v, sinks, rel_bias):
    B, S, H, dh = q.shape
    KV = k.shape[2]
    G = H // KV
    W = C_WINDOW
    nb = S // W
    qb = (q.astype(F32) * (dh ** -0.5)).reshape(B, nb, W, KV, G, dh)
    kb = k.reshape(B, nb, W, KV, dh)
    vb = v.reshape(B, nb, W, KV, dh)
    padspec = ((0, 0), (1, 0), (0, 0), (0, 0), (0, 0))
    kk = jnp.concatenate([jnp.pad(kb, padspec)[:, :-1], kb], axis=2)
    vv = jnp.concatenate([jnp.pad(vb, padspec)[:, :-1], vb], axis=2)
    logits = jnp.einsum('bnqkgd,bnlkd->bnkgql', qb, kk.astype(F32)).reshape(B, nb, H, W, 2 * W)
    dist = W + jnp.arange(W)[:, None] - jnp.arange(2 * W)[None, :]
    kpos = (jnp.arange(nb)[:, None] - 1) * W + jnp.arange(2 * W)[None, :]
    mask = ((dist >= 0) & (dist < W))[None] & (kpos >= 0)[:, None, :]
    logits = jnp.where(mask[None, :, None], logits + rel_bias_lookup(rel_bias, dist[None]), -jnp.inf)
    sink = jnp.broadcast_to(sinks.astype(F32)[:, None, None], (B, nb, H, W, 1))
    p = jax.nn.softmax(jnp.concatenate([logits, sink], axis=-1), axis=-1)[..., :-1]
    out = jnp.einsum('bnkgql,bnlkd->bnqkgd', p.reshape(B, nb, KV, G, W, 2 * W), vv.astype(F32)).reshape(B, S, H, dh)
    wb = min(W, S)
    return out.astype(q.dtype), k[:, S - wb:], v[:, S - wb:]


def swa_sample(q, k, v, k_buf, v_buf, sinks, rel_bias):
    N, T, H, dh = q.shape
    KV = k.shape[2]
    G = H // KV
    Wb = k_buf.shape[1]
    kk = jnp.concatenate([k_buf.astype(k.dtype), k], axis=1)
    vv = jnp.concatenate([v_buf.astype(v.dtype), v], axis=1)
    q_pos = PAST_LEN + jnp.arange(T)
    k_pos = PAST_LEN - Wb + jnp.arange(Wb + T)
    dist = q_pos[:, None] - k_pos[None, :]
    mask = (dist >= 0) & (dist < C_WINDOW)
    logits = jnp.einsum('ntkgd,nlkd->nkgtl', (q.astype(F32) * (dh ** -0.5)).reshape(N, T, KV, G, dh),
                        kk.astype(F32)).reshape(N, H, T, -1)
    logits = jnp.where(mask, logits + rel_bias_lookup(rel_bias, dist[None]), -jnp.inf)
    sink = jnp.broadcast_to(sinks.astype(F32)[:, None, None], (N, H, T, 1))
    p = jax.nn.softmax(jnp.concatenate([logits, sink], axis=-1), axis=-1)[..., :-1]
    out = jnp.einsum('nkgtl,nlkd->ntkgd', p.reshape(N, KV, G, T, -1), vv.astype(F32)).reshape(N, T, H, dh)
    return out.astype(q.dtype), kk[:, kk.shape[1] - Wb:], vv[:, vv.shape[1] - Wb:]


def ab_layer(x, c, norm_g, mod_w, mod_b, w_in, conv_w, conv_b, ln_g, ln_b, w_out, attend, conv_hist):
    N, L, _ = x.shape
    shift, scale, gate = adaln(c, mod_w, mod_b)
    h = rmsnorm(x, norm_g) * (1 + scale) + shift
    q, k, v, za, ga, gb, zb = _split(jnp.matmul(h, w_in), [A_Q, A_KV, A_KV, A_Q, B_CHANNELS, B_CHANNELS, B_CHANNELS])
    k = k.reshape(N, L, A_KV_HEADS, HEAD_DIM)
    v = v.reshape(N, L, A_KV_HEADS, HEAD_DIM)
    oa = attend(q.reshape(N, L, A_HEADS, HEAD_DIM), k, v).reshape(N, L, A_Q) * jax.nn.silu(za)
    ob, new_hist = conformer_conv(ga * jax.nn.sigmoid(gb), conv_hist, conv_w, conv_b, ln_g, ln_b)
    ob = ob * jax.nn.silu(zb)
    y = jnp.matmul(jnp.concatenate([oa, ob], axis=-1), w_out)
    return x + gate * y, k, v, new_hist


def c_layer(x, c, norm_g, mod_w, mod_b, w_in, sinks, w_out, attend):
    N, L, _ = x.shape
    shift, scale, gate = adaln(c, mod_w, mod_b)
    h = rmsnorm(x, norm_g) * (1 + scale) + shift
    q, k, v, z = _split(jnp.matmul(h, w_in), [C_Q, C_KV, C_KV, C_Q])
    o, kbuf, vbuf = attend(q.reshape(N, L, C_HEADS, HEAD_DIM), k.reshape(N, L, C_KV_HEADS, HEAD_DIM),
                           v.reshape(N, L, C_KV_HEADS, HEAD_DIM), sinks)
    y = jnp.matmul(o.reshape(N, L, C_Q) * jax.nn.silu(z), w_out)
    return x + gate * y, kbuf, vbuf


def setup_inputs(seed: int = 0) -> dict:
    key = jax.random.key(seed)
    ks = jax.random.split(key, 32)
    n_pages = PAST_LEN // PAGE_SIZE
    n_pool = (5 * DEC_BATCH * n_pages + 3) // 4
    w_buf = min(C_WINDOW, PAST_LEN)

    def nrm(k, shape, s=1.0):
        return s * jax.random.normal(k, shape, F32)

    perm = jax.random.permutation(ks[6], n_pool)
    page_table = perm[: DEC_BATCH * n_pages].reshape(DEC_BATCH, n_pages).astype(jnp.int32)
    return {
        'x_prompt': nrm(ks[0], (BATCH, SEQ, D_MODEL)),
        'x_sample': nrm(ks[1], (DEC_BATCH, DEC_SEQ, D_MODEL)),
        'c_prompt': nrm(ks[2], (BATCH, D_MODEL)),
        'c_sample': nrm(ks[3], (DEC_BATCH, D_MODEL)),
        'cache_a_k': nrm(ks[4], (N_A_LAYERS, n_pool, PAGE_SIZE, A_KV_HEADS, HEAD_DIM)),
        'cache_a_v': nrm(ks[5], (N_A_LAYERS, n_pool, PAGE_SIZE, A_KV_HEADS, HEAD_DIM)),
        'page_table': page_table,
        'cache_b_conv': nrm(ks[7], (N_A_LAYERS, DEC_BATCH, B_CONV_WIDTH - 1, B_CHANNELS), 0.5),
        'cache_c_k': nrm(ks[8], (N_C_LAYERS, DEC_BATCH, w_buf, C_KV_HEADS, HEAD_DIM)),
        'cache_c_v': nrm(ks[9], (N_C_LAYERS, DEC_BATCH, w_buf, C_KV_HEADS, HEAD_DIM)),
        'rel_bias': nrm(ks[10], (N_BUCKETS, REL_HEADS), 0.3),
        'norm_a': 1.0 + nrm(ks[11], (N_A_LAYERS, D_MODEL), 0.1),
        'mod_w_a': nrm(ks[12], (N_A_LAYERS, D_MODEL, 3 * D_MODEL), 0.5 * D_MODEL ** -0.5),
        'mod_b_a': nrm(ks[13], (N_A_LAYERS, 3 * D_MODEL), 0.01),
        'w_in_a': nrm(ks[14], (N_A_LAYERS, D_MODEL, A_IN), D_MODEL ** -0.5),
        'conv_w_b': nrm(ks[15], (N_A_LAYERS, B_CONV_WIDTH, B_CHANNELS), B_CONV_WIDTH ** -0.5),
        'conv_b_b': nrm(ks[16], (N_A_LAYERS, B_CHANNELS), 0.01),
        'ln_g_b': 1.0 + nrm(ks[17], (N_A_LAYERS, B_CHANNELS), 0.1),
        'ln_b_b': nrm(ks[18], (N_A_LAYERS, B_CHANNELS), 0.01),
        'w_out_a': nrm(ks[19], (N_A_LAYERS, A_MIX, D_MODEL), A_MIX ** -0.5),
        'norm_c': 1.0 + nrm(ks[20], (N_C_LAYERS, D_MODEL), 0.1),
        'mod_w_c': nrm(ks[21], (N_C_LAYERS, D_MODEL, 3 * D_MODEL), 0.5 * D_MODEL ** -0.5),
        'mod_b_c': nrm(ks[22], (N_C_LAYERS, 3 * D_MODEL), 0.01),
        'w_in_c': nrm(ks[23], (N_C_LAYERS, D_MODEL, C_IN), D_MODEL ** -0.5),
        'sinks_c': nrm(ks[24], (N_C_LAYERS, C_HEADS), 0.5),
        'w_out_c': nrm(ks[25], (N_C_LAYERS, C_Q, D_MODEL), C_Q ** -0.5),
        'final_norm': 1.0 + nrm(ks[26], (D_MODEL,), 0.1),
    }


def reference(x_prompt, x_sample, c_prompt, c_sample, cache_a_k, cache_a_v, page_table, cache_b_conv,
              cache_c_k, cache_c_v, rel_bias, norm_a, mod_w_a, mod_b_a, w_in_a, conv_w_b, conv_b_b, ln_g_b,
              ln_b_b, w_out_a, norm_c, mod_w_c, mod_b_c, w_in_c, sinks_c, w_out_c, final_norm):
    xp, xs = x_prompt, x_sample
    ak_p, av_p, ak_s, av_s, bc_p, bc_s = [], [], [], [], [], []
    ck_p, cv_p, ck_s, cv_s = [], [], [], []
    for layer in range(DEPTH):
        i = layer // 2
        if layer % 2 == 0:
            wa = (norm_a[i], mod_w_a[i], mod_b_a[i], w_in_a[i], conv_w_b[i], conv_b_b[i], ln_g_b[i], ln_b_b[i], w_out_a[i])
            hist0 = jnp.zeros((xp.shape[0], B_CONV_WIDTH - 1, B_CHANNELS), xp.dtype)
            xp, k, v, hist = ab_layer(xp, c_prompt, *wa, lambda q, k, v: moba_prompt(q, k, v, rel_bias), hist0)
            ak_p.append(k); av_p.append(v); bc_p.append(hist)
            kpool, vpool = cache_a_k[i], cache_a_v[i]
            xs, k, v, hist = ab_layer(xs, c_sample, *wa,
                                     lambda q, k, v: moba_sample(q, k, v, kpool, vpool, page_table, rel_bias),
                                     cache_b_conv[i])
            ak_s.append(k); av_s.append(v); bc_s.append(hist)
        else:
            wc = (norm_c[i], mod_w_c[i], mod_b_c[i], w_in_c[i], sinks_c[i], w_out_c[i])
            xp, kb, vb = c_layer(xp, c_prompt, *wc, lambda q, k, v, s: swa_prompt(q, k, v, s, rel_bias))
            ck_p.append(kb); cv_p.append(vb)
            kbuf, vbuf = cache_c_k[i], cache_c_v[i]
            xs, kb, vb = c_layer(xs, c_sample, *wc,
                                 lambda q, k, v, s: swa_sample(q, k, v, kbuf, vbuf, s, rel_bias))
            ck_s.append(kb); cv_s.append(vb)
    y_prompt = rmsnorm(xp, final_norm)
    y_sample = rmsnorm(xs, final_norm)
    return (y_prompt, y_sample, jnp.stack(ak_p), jnp.stack(av_p), jnp.stack(ak_s), jnp.stack(av_s),
            jnp.stack(bc_p), jnp.stack(bc_s), jnp.stack(ck_p), jnp.stack(cv_p), jnp.stack(ck_s), jnp.stack(cv_s))
```

```python
import functools
import math

import jax
import jax.numpy as jnp
import numpy as np
from jax import lax
from jax.experimental import pallas as pl
from jax.experimental.pallas import tpu as pltpu

F32 = jnp.float32
BF16 = jnp.bfloat16
NEG_INF = float("-inf")
MASK_NEG = -1e30

HEAD_DIM = 64
N_HEADS = 16
N_KV = 2
GROUP = N_HEADS // N_KV
KV_W = N_KV * HEAD_DIM
MOBA_BLOCK = 256
MOBA_TOPK = 3
CONV_W = 31
SWA_WINDOW = 128
N_BUCKETS = 32
MAX_DISTANCE = 128
EPS = 1e-6
SCALE = HEAD_DIM ** -0.5

VMEM_LIMIT = 56 * 2**20


def _params(*sem):
    return pltpu.CompilerParams(dimension_semantics=sem, vmem_limit_bytes=VMEM_LIMIT)


def _silu(z):
    return z * jax.nn.sigmoid(z)


def _dot(a, b):
    return jnp.dot(a, b, preferred_element_type=F32)


def _dot_nt(a, b):
    return lax.dot_general(a, b, (((1,), (1,)), ((), ())), preferred_element_type=F32)


def _dot_f32(a, b):
    return jnp.dot(a, b, preferred_element_type=F32, precision=lax.Precision.HIGHEST)


def _t5_bucket_np(n):
    n = np.maximum(n, 0)
    max_exact = N_BUCKETS // 2
    nf = np.maximum(n, 1).astype(np.float32)
    large = max_exact + (np.log(nf / np.float32(max_exact)) / np.float32(math.log(MAX_DISTANCE / max_exact))
                         * np.float32(N_BUCKETS - max_exact)).astype(np.int32)
    large = np.minimum(large, N_BUCKETS - 1)
    return np.where(n < max_exact, n, large).astype(np.int32)


def _bucket_matrix(dist, valid):
    return np.where(valid, _t5_bucket_np(dist), -1).astype(np.int32)


def _bias_kernel(bm_ref, rb_ref, sub_ref, o_ref):
    bm = bm_ref[...]
    acc = jnp.zeros(bm.shape, F32)
    for b in range(N_BUCKETS):
        acc = jnp.where(bm == b, rb_ref[b:b + 1, :], acc)
    o_ref[...] = jnp.where(bm < 0, NEG_INF, acc - sub_ref[...])


def bias_table(bm, rbx, sub):
    g, _, c = rbx.shape
    r = bm.shape[0]
    return pl.pallas_call(
        _bias_kernel,
        grid=(g,),
        in_specs=[pl.BlockSpec((r, c), lambda i: (0, 0)),
                  pl.BlockSpec((None, N_BUCKETS, c), lambda i: (i, 0, 0)),
                  pl.BlockSpec((None, 1, c), lambda i: (i, 0, 0))],
        out_specs=pl.BlockSpec((None, r, c), lambda i: (i, 0, 0)),
        out_shape=jax.ShapeDtypeStruct((g, r, c), F32),
        compiler_params=_params("arbitrary"),
        name="bias_table",
    )(jnp.asarray(bm), rbx, sub)


def _mod_kernel(c_ref, w_ref, b_ref, o_ref):
    o_ref[...] = _dot_f32(_silu(c_ref[...]), w_ref[...]) + b_ref[...]


def modulation(c, w, b):
    n, d = c.shape
    m = w.shape[1]
    tn = 512
    return pl.pallas_call(
        _mod_kernel,
        grid=(m // tn,),
        in_specs=[pl.BlockSpec((n, d), lambda j: (0, 0)),
                  pl.BlockSpec((d, tn), lambda j: (0, j)),
                  pl.BlockSpec((1, tn), lambda j: (0, j))],
        out_specs=pl.BlockSpec((n, tn), lambda j: (0, j)),
        out_shape=jax.ShapeDtypeStruct((n, m), F32),
        compiler_params=_params("arbitrary"),
        name="modulation",
    )(c, w, b.reshape(1, m))


def _ln_inproj_kernel(x_ref, shift_ref, scale_ref, g_ref, w_ref, *out_refs, segs):
    x = x_ref[...]
    y = x * lax.rsqrt(jnp.mean(x * x, axis=-1, keepdims=True) + EPS)
    h = (y * g_ref[...]) * (1.0 + scale_ref[...]) + shift_ref[...]
    h16 = h.astype(BF16)
    for o_ref, seg in zip(out_refs, segs, strict=True):
        kind, lo, hi = seg[0], seg[1], seg[2]
        z = _dot(h16, w_ref[:, lo:hi])
        if kind == "silu":
            z = _silu(z)
        elif kind == "glu":
            z = z * jax.nn.sigmoid(_dot(h16, w_ref[:, seg[3]:seg[4]]))
        o_ref[...] = z.astype(o_ref.dtype)


def ln_inproj(x, shift, scale, norm_g, w16, segs, out_dtypes, tm):
    r, d = x.shape
    nt = r // tm
    per_group = nt // shift.shape[0]
    mrows = shift.shape[1]
    mod_spec = pl.BlockSpec((None, mrows, d), lambda i: (i // per_group, 0, 0))
    out_shape = [jax.ShapeDtypeStruct((r, s[2] - s[1]), dt) for s, dt in zip(segs, out_dtypes, strict=True)]
    out_specs = [pl.BlockSpec((tm, s[2] - s[1]), lambda i: (i, 0)) for s in segs]
    return pl.pallas_call(
        functools.partial(_ln_inproj_kernel, segs=segs),
        grid=(nt,),
        in_specs=[pl.BlockSpec((tm, d), lambda i: (i, 0)), mod_spec, mod_spec,
                  pl.BlockSpec((1, d), lambda i: (0, 0)),
                  pl.BlockSpec(w16.shape, lambda i: (0, 0))],
        out_specs=out_specs,
        out_shape=out_shape,
        compiler_params=_params("arbitrary"),
        name="ln_inproj",
    )(x, shift, scale, norm_g.reshape(1, d), w16)


def _out_proj_kernel(*refs, n_gated, n_plain, final_norm):
    it = iter(refs)
    y = None
    for _ in range(n_gated):
        a_ref, m_ref, w_ref = next(it), next(it), next(it)
        t = _dot((a_ref[...] * m_ref[...].astype(F32)).astype(BF16), w_ref[...])
        y = t if y is None else y + t
    for _ in range(n_plain):
        a_ref, w_ref = next(it), next(it)
        t = _dot(a_ref[...], w_ref[...])
        y = t if y is None else y + t
    x_ref, gate_ref = next(it), next(it)
    xn = x_ref[...] + gate_ref[...] * y
    if final_norm:
        fg_ref = next(it)
        xn = xn * lax.rsqrt(jnp.mean(xn * xn, axis=-1, keepdims=True) + EPS) * fg_ref[...]
    o_ref = next(it)
    o_ref[...] = xn


def out_proj(gated, plain, x, gate, final_g, tm):
    r, d = x.shape
    nt = r // tm
    per_group = nt // gate.shape[0]
    mrows = gate.shape[1]
    args, specs = [], []
    for a, m, w in gated:
        args += [a, m, w]
        specs += [pl.BlockSpec((tm, a.shape[1]), lambda i: (i, 0)),
                  pl.BlockSpec((tm, m.shape[1]), lambda i: (i, 0)),
                  pl.BlockSpec(w.shape, lambda i: (0, 0))]
    for a, w in plain:
        args += [a, w]
        specs += [pl.BlockSpec((tm, a.shape[1]), lambda i: (i, 0)),
                  pl.BlockSpec(w.shape, lambda i: (0, 0))]
    args += [x, gate]
    specs += [pl.BlockSpec((tm, d), lambda i: (i, 0)),
              pl.BlockSpec((None, mrows, d), lambda i: (i // per_group, 0, 0))]
    if final_g is not None:
        args.append(final_g.reshape(1, d))
        specs.append(pl.BlockSpec((1, d), lambda i: (0, 0)))
    return pl.pallas_call(
        functools.partial(_out_proj_kernel, n_gated=len(gated), n_plain=len(plain),
                          final_norm=final_g is not None),
        grid=(nt,),
        in_specs=specs,
        out_specs=pl.BlockSpec((tm, d), lambda i: (i, 0)),
        out_shape=jax.ShapeDtypeStruct((r, d), F32),
        compiler_params=_params("arbitrary"),
        name="out_proj",
    )(*args)


def _top3_mask(scores, n_valid):
    nb = scores.shape[0]
    blk = lax.broadcasted_iota(jnp.int32, scores.shape, 0)
    s = jnp.where(blk < n_valid, scores, NEG_INF)
    picked = jnp.zeros(scores.shape, F32)
    for _ in range(MOBA_TOPK):
        mx = jnp.max(s, axis=0, keepdims=True)
        first = jnp.min(jnp.where(s == mx, blk, nb), axis=0, keepdims=True)
        hit = blk == first
        picked = jnp.where(hit, 1.0, picked)
        s = jnp.where(hit, NEG_INF, s)
    return jnp.where(blk < n_valid, picked, 0.0)


def _moba_prompt_kernel(q_ref, k_ref, v_ref, bown_ref, badj_ref, o_ref,
                        k16_s, vT_s, kbar_s, qT_s, qT16_s, sel_s, m_s, acc_s, *, nblk, chunk):
    g = pl.program_id(1)
    i = pl.program_id(2)
    rows = GROUP * MOBA_BLOCK

    @pl.when((g == 0) & (i == 0))
    def _per_batch():
        for j in range(nblk):
            kb = k_ref[j * MOBA_BLOCK:(j + 1) * MOBA_BLOCK, :]
            k16_s[j * MOBA_BLOCK:(j + 1) * MOBA_BLOCK, :] = kb.astype(BF16)
            kbar_s[j:j + 1, :] = jnp.mean(kb, axis=0, keepdims=True)
            vT_s[j] = v_ref[j * MOBA_BLOCK:(j + 1) * MOBA_BLOCK, :].T.astype(BF16)

    qT = (q_ref[...] * SCALE).T
    qs = jnp.concatenate([qT[h * HEAD_DIM:(h + 1) * HEAD_DIM, :] for h in range(GROUP)], axis=1)
    zeros = jnp.zeros_like(qs)
    qpad = jnp.where(g == 0, jnp.concatenate([qs, zeros], axis=0), jnp.concatenate([zeros, qs], axis=0))
    qT_s[...] = qpad
    qT16_s[...] = qpad.astype(BF16)

    sel_s[...] = _top3_mask(_dot_f32(kbar_s[...], qT_s[...]), i)

    row_id = lax.broadcasted_iota(jnp.int32, (KV_W, MOBA_BLOCK), 0)
    mine = (row_id // HEAD_DIM) == g

    def v_aug(j):
        return jnp.where(mine, vT_s[j], jnp.ones((), BF16))

    def k_blk(j):
        return k16_s[pl.ds(pl.multiple_of(j * MOBA_BLOCK, MOBA_BLOCK), MOBA_BLOCK), :]

    k_own, v_own = k_blk(i), v_aug(i)
    for c in range(rows // chunk):
        ln = slice(c * chunk, (c + 1) * chunk)
        s_t = _dot(k_own, qT16_s[:, ln]) + bown_ref[:, ln]
        m0 = jnp.max(s_t, axis=0, keepdims=True)
        p = jnp.exp(s_t - m0)
        acc_s[:, ln] = _dot(v_own, p.astype(BF16))
        m_s[:, ln] = m0

    def update(j, bias_ref):
        kb, vb = k_blk(j), v_aug(j)
        for c in range(rows // chunk):
            ln = slice(c * chunk, (c + 1) * chunk)
            s_t = _dot(kb, qT16_s[:, ln])
            if bias_ref is not None:
                s_t = s_t + bias_ref[:, ln]
            on = sel_s[pl.ds(j, 1), ln] > 0.0
            m_old = m_s[:, ln]
            m_new = jnp.maximum(m_old, jnp.where(on, jnp.max(s_t, axis=0, keepdims=True), NEG_INF))
            p = jnp.exp(s_t - jnp.where(on, m_new, jnp.inf))
            acc_s[:, ln] = acc_s[:, ln] * jnp.exp(m_old - m_new) + _dot(vb, p.astype(BF16))
            m_s[:, ln] = m_new

    @pl.when(i >= 1)
    def _adjacent():
        update(i - 1, badj_ref)

    def far(j, carry):
        update(j, None)
        return carry

    lax.fori_loop(0, jnp.maximum(i - 1, 0), far, 0)

    acc = acc_s[...]
    num = jnp.where(g == 0, acc[:HEAD_DIM], acc[HEAD_DIM:])
    den = jnp.where(g == 0, acc[HEAD_DIM:HEAD_DIM + 1], acc[0:1])
    o_t = num / den
    o_cat = jnp.concatenate([o_t[:, h * MOBA_BLOCK:(h + 1) * MOBA_BLOCK] for h in range(GROUP)], axis=0)
    o_ref[...] = o_cat.T


def moba_prompt(q, k, v, rel_bias, batch, seq):
    nblk = seq // MOBA_BLOCK
    rows = GROUP * MOBA_BLOCK
    kk = np.arange(MOBA_BLOCK)[:, None]
    qq = (np.arange(rows) % MOBA_BLOCK)[None, :]
    bm_own = _bucket_matrix(qq - kk, qq >= kk)
    bm_adj = _bucket_matrix(MOBA_BLOCK + qq - kk, np.ones((MOBA_BLOCK, rows), bool))
    rbx = jnp.repeat(rel_bias.reshape(N_BUCKETS, N_KV, GROUP).transpose(1, 0, 2), MOBA_BLOCK, axis=2)
    far = rbx[:, N_BUCKETS - 1:N_BUCKETS, :]
    b_own = bias_table(bm_own, rbx, far)
    b_adj = bias_table(bm_adj, rbx, far)
    qw = GROUP * HEAD_DIM
    return pl.pallas_call(
        functools.partial(_moba_prompt_kernel, nblk=nblk, chunk=512),
        grid=(batch, N_KV, nblk),
        in_specs=[pl.BlockSpec((MOBA_BLOCK, qw), lambda b, g, i: (b * nblk + i, g)),
                  pl.BlockSpec((None, seq, KV_W), lambda b, g, i: (b, 0, 0)),
                  pl.BlockSpec((None, seq, KV_W), lambda b, g, i: (b, 0, 0)),
                  pl.BlockSpec((None, MOBA_BLOCK, rows), lambda b, g, i: (g, 0, 0)),
                  pl.BlockSpec((None, MOBA_BLOCK, rows), lambda b, g, i: (g, 0, 0))],
        out_specs=pl.BlockSpec((MOBA_BLOCK, qw), lambda b, g, i: (b * nblk + i, g)),
        out_shape=jax.ShapeDtypeStruct(q.shape, F32),
        scratch_shapes=[pltpu.VMEM((seq, KV_W), BF16),
                        pltpu.VMEM((nblk, KV_W, MOBA_BLOCK), BF16),
                        pltpu.VMEM((nblk, KV_W), F32),
                        pltpu.VMEM((KV_W, rows), F32),
                        pltpu.VMEM((KV_W, rows), BF16),
                        pltpu.VMEM((nblk, rows), F32),
                        pltpu.VMEM((1, rows), F32),
                        pltpu.VMEM((KV_W, rows), F32)],
        compiler_params=_params("arbitrary", "arbitrary", "arbitrary"),
        name="moba_prompt",
    )(q, k, v, b_own, b_adj)


def _moba_sample_kernel(pt_ref, q_ref, kn_ref, vn_ref, blast_ref, bown_ref, e_ref, kpool, vpool, o_ref,
                        kbuf, vbuf, ksem, vsem, s_s, kbar_s, *, n_pages, page, n_samples):
    s = pl.program_id(0)
    slot = s % 2
    nblk = n_pages * page // MOBA_BLOCK
    ppb = MOBA_BLOCK // page
    rows = N_HEADS * q_ref.shape[0] // N_HEADS

    def k_copy(smp, sl, p):
        return pltpu.make_async_copy(kpool.at[pt_ref[smp, p]], kbuf.at[sl, p], ksem.at[sl])

    def v_copy(smp, sl, p):
        return pltpu.make_async_copy(vpool.at[pt_ref[smp, p]], vbuf.at[sl, p], vsem.at[sl])

    def start_all(smp, sl):
        def body(p, c):
            k_copy(smp, sl, p).start()
            v_copy(smp, sl, p).start()
            return c
        lax.fori_loop(0, n_pages, body, 0)

    @pl.when(s == 0)
    def _first():
        start_all(0, 0)

    @pl.when(s + 1 < n_samples)
    def _prefetch():
        start_all(s + 1, 1 - slot)

    def wait_k(p, c):
        k_copy(s, slot, p).wait()
        return c

    lax.fori_loop(0, n_pages, wait_k, 0)

    def kbar_body(j, c):
        kb = kbuf[slot, pl.ds(j * ppb, ppb)].reshape(MOBA_BLOCK, KV_W)
        kbar_s[pl.ds(j, 1), :] = jnp.mean(kb, axis=0, keepdims=True)
        return c

    lax.fori_loop(0, nblk, kbar_body, 0)

    q = q_ref[...] * SCALE
    r_id = lax.broadcasted_iota(jnp.int32, (rows, HEAD_DIM), 0)
    zq = jnp.zeros_like(q)
    qbd = jnp.concatenate([jnp.where(r_id < rows // N_KV, q, zq), jnp.where(r_id < rows // N_KV, zq, q)], axis=1)
    qbd16 = qbd.astype(BF16)

    scores = _dot_f32(kbar_s[...], qbd.T)
    sel_t = _top3_mask(scores, nblk)
    negm16 = jnp.where(sel_t.T > 0.0, 0.0, MASK_NEG).astype(BF16)

    def logits(j):
        kb = kbuf[slot, pl.ds(j * ppb, ppb)].reshape(MOBA_BLOCK, KV_W).astype(BF16)
        onehot = e_ref[:, pl.ds(pl.multiple_of(j * MOBA_BLOCK, MOBA_BLOCK), MOBA_BLOCK)]
        return _dot_nt(qbd16, kb) + _dot(negm16, onehot)

    def far(j, mrun):
        st = logits(j)
        s_s[:, pl.ds(pl.multiple_of(j * MOBA_BLOCK, MOBA_BLOCK), MOBA_BLOCK)] = st
        return jnp.maximum(mrun, st)

    mrun = lax.fori_loop(0, nblk - 1, far, jnp.full((rows, MOBA_BLOCK), NEG_INF, F32))
    s_last = logits(nblk - 1) + blast_ref[...]
    s_s[:, (nblk - 1) * MOBA_BLOCK:nblk * MOBA_BLOCK] = s_last
    mrun = jnp.maximum(mrun, s_last)
    kn16 = kn_ref[...].astype(BF16)
    s_own = _dot_nt(qbd16, kn16) + bown_ref[...]
    mrun = jnp.maximum(mrun, s_own)
    m = jnp.max(mrun, axis=1, keepdims=True)

    def wait_v(p, c):
        v_copy(s, slot, p).wait()
        return c

    lax.fori_loop(0, n_pages, wait_v, 0)

    def pv(j, carry):
        acc, lsum = carry
        st = s_s[:, pl.ds(pl.multiple_of(j * MOBA_BLOCK, MOBA_BLOCK), MOBA_BLOCK)]
        p = jnp.exp(st - m)
        vb = vbuf[slot, pl.ds(j * ppb, ppb)].reshape(MOBA_BLOCK, KV_W).astype(BF16)
        return acc + _dot(p.astype(BF16), vb), lsum + p

    p_own = jnp.exp(s_own - m)
    acc0 = _dot(p_own.astype(BF16), vn_ref[...].astype(BF16))
    acc, lsum = lax.fori_loop(0, nblk, pv, (acc0, p_own))
    den = jnp.sum(lsum, axis=1, keepdims=True)
    r_id = lax.broadcasted_iota(jnp.int32, (rows, HEAD_DIM), 0)
    o_ref[...] = jnp.where(r_id < rows // N_KV, acc[:, :HEAD_DIM], acc[:, HEAD_DIM:]) / den


def moba_sample(q, k_new, v_new, k_pool, v_pool, page_table, rel_bias, t_new):
    n, n_pages = page_table.shape
    page = k_pool.shape[1]
    past = n_pages * page
    nblk = past // MOBA_BLOCK
    rows = N_HEADS * t_new
    q_r = q.reshape(n, t_new, N_HEADS, HEAD_DIM).transpose(0, 2, 1, 3).reshape(n, rows, HEAD_DIM)
    pad = ((0, 0), (0, MOBA_BLOCK - t_new), (0, 0))
    kn = jnp.pad(k_new.reshape(n, t_new, KV_W), pad)
    vn = jnp.pad(v_new.reshape(n, t_new, KV_W), pad)
    tok = (np.arange(rows) % t_new)[None, :]
    kk = np.arange(MOBA_BLOCK)[:, None]
    bm_last = _bucket_matrix(MOBA_BLOCK + tok - kk, np.ones((MOBA_BLOCK, rows), bool))
    bm_own = _bucket_matrix(tok - kk, (kk <= tok) & (kk < t_new))
    rbx = jnp.repeat(rel_bias, t_new, axis=1)[None]
    far = rbx[:, N_BUCKETS - 1:N_BUCKETS, :]
    b_last = bias_table(bm_last, rbx, far)[0].T
    b_own = bias_table(bm_own, rbx, far)[0].T
    onehot = jnp.asarray(np.repeat(np.eye(nblk, dtype=np.float32), MOBA_BLOCK, axis=1), BF16)

    grid_spec = pltpu.PrefetchScalarGridSpec(
        num_scalar_prefetch=1,
        grid=(n,),
        in_specs=[pl.BlockSpec((None, rows, HEAD_DIM), lambda s, pt: (s, 0, 0)),
                  pl.BlockSpec((None, MOBA_BLOCK, KV_W), lambda s, pt: (s, 0, 0)),
                  pl.BlockSpec((None, MOBA_BLOCK, KV_W), lambda s, pt: (s, 0, 0)),
                  pl.BlockSpec((rows, MOBA_BLOCK), lambda s, pt: (0, 0)),
                  pl.BlockSpec((rows, MOBA_BLOCK), lambda s, pt: (0, 0)),
                  pl.BlockSpec((nblk, past), lambda s, pt: (0, 0)),
                  pl.BlockSpec(memory_space=pl.ANY),
                  pl.BlockSpec(memory_space=pl.ANY)],
        out_specs=pl.BlockSpec((None, rows, HEAD_DIM), lambda s, pt: (s, 0, 0)),
        scratch_shapes=[pltpu.VMEM((2, n_pages, page, KV_W), F32),
                        pltpu.VMEM((2, n_pages, page, KV_W), F32),
                        pltpu.SemaphoreType.DMA((2,)),
                        pltpu.SemaphoreType.DMA((2,)),
                        pltpu.VMEM((rows, past), F32),
                        pltpu.VMEM((nblk, KV_W), F32)],
    )
    o = pl.pallas_call(
        functools.partial(_moba_sample_kernel, n_pages=n_pages, page=page, n_samples=n),
        grid_spec=grid_spec,
        out_shape=jax.ShapeDtypeStruct((n, rows, HEAD_DIM), F32),
        compiler_params=_params("arbitrary"),
        name="moba_sample",
    )(page_table, q_r, kn, vn, b_last, b_own, onehot, k_pool, v_pool)
    return o.reshape(n, N_HEADS, t_new, HEAD_DIM).transpose(0, 2, 1, 3).reshape(n * t_new, N_HEADS * HEAD_DIM)


def _conv_tail(y, cb_ref, lg_ref, lb_ref, gate):
    y = y + cb_ref[...]
    mu = jnp.mean(y, axis=-1, keepdims=True)
    var = jnp.mean(jnp.square(y - mu), axis=-1, keepdims=True)
    yn = (y - mu) * lax.rsqrt(var + EPS) * lg_ref[...] + lb_ref[...]
    return _silu(yn) * gate


HALO = 32


def _conv_prompt_kernel(u_ref, prev_ref, g_ref, w_ref, cb_ref, lg_ref, lb_ref, o_ref, xs, *, tl):
    t = pl.program_id(1)
    xs[0:HALO, :] = jnp.where(t > 0, prev_ref[...], 0.0)
    xs[HALO:, :] = u_ref[...]
    off = HALO - (CONV_W - 1)
    y = jnp.zeros((tl, u_ref.shape[1]), F32)
    for k in range(CONV_W):
        y = y + xs[pl.ds(off + k, tl), :] * w_ref[k:k + 1, :]
    o_ref[...] = _conv_tail(y, cb_ref, lg_ref, lb_ref, g_ref[...].astype(F32)).astype(o_ref.dtype)


def conv_prompt(u, szb, conv_w, conv_b, ln_g, ln_b, batch, seq, tl=256):
    c = u.shape[1]
    nt = seq // tl
    vec = pl.BlockSpec((1, c), lambda b, t: (0, 0))
    return pl.pallas_call(
        functools.partial(_conv_prompt_kernel, tl=tl),
        grid=(batch, nt),
        in_specs=[pl.BlockSpec((tl, c), lambda b, t: (b * nt + t, 0)),
                  pl.BlockSpec((HALO, c), lambda b, t: (jnp.maximum((b * nt + t) * (tl // HALO) - 1, 0), 0)),
                  pl.BlockSpec((tl, c), lambda b, t: (b * nt + t, 0)),
                  pl.BlockSpec((CONV_W, c), lambda b, t: (0, 0)), vec, vec, vec],
        out_specs=pl.BlockSpec((tl, c), lambda b, t: (b * nt + t, 0)),
        out_shape=jax.ShapeDtypeStruct(u.shape, BF16),
        scratch_shapes=[pltpu.VMEM((HALO + tl, c), F32)],
        compiler_params=_params("arbitrary", "arbitrary"),
        name="conv_prompt",
    )(u, u, szb, conv_w, conv_b.reshape(1, c), ln_g.reshape(1, c), ln_b.reshape(1, c))


def _conv_sample_kernel(xp_ref, g_ref, w_ref, cb_ref, lg_ref, lb_ref, o_ref, *, t_new):
    ns, _, c = xp_ref.shape
    y = jnp.zeros((ns, t_new, c), F32)
    for k in range(CONV_W):
        y = y + xp_ref[:, pl.ds(k, t_new), :] * w_ref[k:k + 1, :]
    o_ref[...] = _conv_tail(y, cb_ref, lg_ref, lb_ref, g_ref[...].astype(F32)).astype(o_ref.dtype)


def conv_sample(xp, szb, conv_w, conv_b, ln_g, ln_b, t_new, ns=8):
    n, rows, c = xp.shape
    vec = pl.BlockSpec((1, c), lambda i: (0, 0))
    return pl.pallas_call(
        functools.partial(_conv_sample_kernel, t_new=t_new),
        grid=(n // ns,),
        in_specs=[pl.BlockSpec((ns, rows, c), lambda i: (i, 0, 0)),
                  pl.BlockSpec((ns, t_new, c), lambda i: (i, 0, 0)),
                  pl.BlockSpec((CONV_W, c), lambda i: (0, 0)), vec, vec, vec],
        out_specs=pl.BlockSpec((ns, t_new, c), lambda i: (i, 0, 0)),
        out_shape=jax.ShapeDtypeStruct((n, t_new, c), BF16),
        compiler_params=_params("arbitrary"),
        name="conv_sample",
    )(xp, szb, conv_w, conv_b.reshape(1, c), ln_g.reshape(1, c), ln_b.reshape(1, c))


def _swa_prompt_kernel(q_ref, kc_ref, kp_ref, vc_ref, vp_ref, bias_ref, sink_ref, o_ref):
    n = pl.program_id(1)
    w = SWA_WINDOW
    rows = GROUP * w
    kcat = jnp.concatenate([kp_ref[...], kc_ref[...]], axis=0).astype(BF16)
    vcat_t = jnp.concatenate([vp_ref[...], vc_ref[...]], axis=0).T.astype(BF16)
    key_id = lax.broadcasted_iota(jnp.int32, (2 * w, rows), 0)
    outs = []
    for g in range(N_KV):
        q_t = (q_ref[:, g * GROUP * HEAD_DIM:(g + 1) * GROUP * HEAD_DIM] * SCALE).T
        qs = jnp.concatenate([q_t[h * HEAD_DIM:(h + 1) * HEAD_DIM, :] for h in range(GROUP)], axis=1)
        zeros = jnp.zeros_like(qs)
        qpad = jnp.concatenate([qs, zeros] if g == 0 else [zeros, qs], axis=0).astype(BF16)
        s_t = _dot(kcat, qpad) + bias_ref[g]
        s_t = jnp.where((key_id >= w) | (n > 0), s_t, NEG_INF)
        sink = sink_ref[g]
        m = jnp.maximum(jnp.max(s_t, axis=0, keepdims=True), sink)
        p = jnp.exp(s_t - m)
        den = jnp.sum(p, axis=0, keepdims=True) + jnp.exp(sink - m)
        o_t = _dot(vcat_t, p.astype(BF16))[g * HEAD_DIM:(g + 1) * HEAD_DIM, :] / den
        outs.append(jnp.concatenate([o_t[:, h * w:(h + 1) * w] for h in range(GROUP)], axis=0).T)
    o_ref[...] = jnp.concatenate(outs, axis=1)


def swa_prompt(q, k, v, sinks, rel_bias, batch, seq):
    w = SWA_WINDOW
    nb = seq // w
    rows = GROUP * w
    kidx = np.arange(2 * w)[:, None]
    qq = (np.arange(rows) % w)[None, :]
    dist = w + qq - kidx
    bm = _bucket_matrix(dist, (dist >= 0) & (dist < w))
    rbx = jnp.repeat(rel_bias.reshape(N_BUCKETS, N_KV, GROUP).transpose(1, 0, 2), w, axis=2)
    bias = bias_table(bm, rbx, jnp.zeros((N_KV, 1, rows), F32))
    sink_x = jnp.repeat(sinks.reshape(N_KV, 1, GROUP), w, axis=2)
    d = q.shape[1]
    cur = lambda b, n: (b * nb + n, 0)
    prev = lambda b, n: (jnp.maximum(b * nb + n - 1, 0), 0)
    return pl.pallas_call(
        _swa_prompt_kernel,
        grid=(batch, nb),
        in_specs=[pl.BlockSpec((w, d), cur),
                  pl.BlockSpec((w, KV_W), cur), pl.BlockSpec((w, KV_W), prev),
                  pl.BlockSpec((w, KV_W), cur), pl.BlockSpec((w, KV_W), prev),
                  pl.BlockSpec((N_KV, 2 * w, rows), lambda b, n: (0, 0, 0)),
                  pl.BlockSpec((N_KV, 1, rows), lambda b, n: (0, 0, 0))],
        out_specs=pl.BlockSpec((w, d), cur),
        out_shape=jax.ShapeDtypeStruct(q.shape, F32),
        compiler_params=_params("arbitrary", "arbitrary"),
        name="swa_prompt",
    )(q, k, k, v, v, bias, sink_x)


def _swa_sample_kernel(q_ref, k_ref, v_ref, bias_ref, sink_ref, o_ref):
    ns, rows, _ = q_ref.shape
    r_id = lax.broadcasted_iota(jnp.int32, (rows, HEAD_DIM), 0)
    lower = r_id < rows // N_KV
    for i in range(ns):
        q = q_ref[i] * SCALE
        zq = jnp.zeros_like(q)
        qbd = jnp.concatenate([jnp.where(lower, q, zq), jnp.where(lower, zq, q)], axis=1).astype(BF16)
        s = _dot_nt(qbd, k_ref[i].astype(BF16)) + bias_ref[...]
        sink = sink_ref[...]
        m = jnp.maximum(jnp.max(s, axis=1, keepdims=True), sink)
        p = jnp.exp(s - m)
        den = jnp.sum(p, axis=1, keepdims=True) + jnp.exp(sink - m)
        acc = _dot(p.astype(BF16), v_ref[i].astype(BF16))
        o_ref[i] = jnp.where(lower, acc[:, :HEAD_DIM], acc[:, HEAD_DIM:]) / den


def swa_sample(q, kcat, vcat, sinks, rel_bias, t_new, ns=4):
    n, nk, _ = kcat.shape
    wb = nk - t_new
    rows = N_HEADS * t_new
    keys = -(-nk // 128) * 128
    q_r = q.reshape(n, t_new, N_HEADS, HEAD_DIM).transpose(0, 2, 1, 3).reshape(n, rows, HEAD_DIM)
    pad = ((0, 0), (0, keys - nk), (0, 0))
    kp, vp = jnp.pad(kcat, pad), jnp.pad(vcat, pad)
    tok = (np.arange(rows) % t_new)[None, :]
    idx = np.arange(keys)[:, None]
    dist = tok + wb - idx
    bm = _bucket_matrix(dist, (dist >= 0) & (dist < SWA_WINDOW) & (idx < nk))
    rbx = jnp.repeat(rel_bias, t_new, axis=1)[None]
    bias = bias_table(bm, rbx, jnp.zeros((1, 1, rows), F32))[0].T
    sink_r = jnp.repeat(sinks, t_new).reshape(rows, 1)
    o = pl.pallas_call(
        _swa_sample_kernel,
        grid=(n // ns,),
        in_specs=[pl.BlockSpec((ns, rows, HEAD_DIM), lambda i: (i, 0, 0)),
                  pl.BlockSpec((ns, keys, KV_W), lambda i: (i, 0, 0)),
                  pl.BlockSpec((ns, keys, KV_W), lambda i: (i, 0, 0)),
                  pl.BlockSpec((rows, keys), lambda i: (0, 0)),
                  pl.BlockSpec((rows, 1), lambda i: (0, 0))],
        out_specs=pl.BlockSpec((ns, rows, HEAD_DIM), lambda i: (i, 0, 0)),
        out_shape=jax.ShapeDtypeStruct((n, rows, HEAD_DIM), F32),
        compiler_params=_params("arbitrary"),
        name="swa_sample",
    )(q_r, kp, vp, bias, sink_r)
    return o.reshape(n, N_HEADS, t_new, HEAD_DIM).transpose(0, 2, 1, 3).reshape(n * t_new, N_HEADS * HEAD_DIM)


def kernel(x_prompt, x_sample, c_prompt, c_sample, cache_a_k, cache_a_v, page_table, cache_b_conv, cache_c_k, cache_c_v, rel_bias, norm_a, mod_w_a, mod_b_a, w_in_a, conv_w_b, conv_b_b, ln_g_b, ln_b_b, w_out_a, norm_c, mod_w_c, mod_b_c, w_in_c, sinks_c, w_out_c, final_norm):
    batch, seq, d = x_prompt.shape
    n, t_new, _ = x_sample.shape
    n_pool, page = cache_a_k.shape[1], cache_a_k.shape[2]
    hq = N_HEADS * HEAD_DIM
    cb = conv_w_b.shape[2]
    assert norm_a.shape[0] == 1 and norm_c.shape[0] == 1, "one A/B layer followed by one C layer"

    tm_p = 512
    tm_s = min(256, n * t_new)
    xp = x_prompt.reshape(batch * seq, d)
    xs = x_sample.reshape(n * t_new, d)

    c_all = jnp.concatenate([c_prompt, c_sample], axis=0)
    c_rows = -(-c_all.shape[0] // 8) * 8
    c_all = jnp.pad(c_all, ((0, c_rows - c_all.shape[0]), (0, 0)))

    def split_mod(m):
        mp = m[:batch].reshape(batch, 1, 3, d)
        ms = jnp.repeat(m[batch:batch + n], t_new, axis=0).reshape(n * t_new // tm_s, tm_s, 3, d)
        return [(mp[:, :, j], ms[:, :, j]) for j in range(3)]

    (sh_a, sc_a, gt_a) = split_mod(modulation(c_all, mod_w_a[0], mod_b_a[0]))
    (sh_c, sc_c, gt_c) = split_mod(modulation(c_all, mod_w_c[0], mod_b_c[0]))

    kv0, kv1 = hq, hq + 2 * KV_W
    za0 = kv1
    ga0 = za0 + hq
    gb0 = ga0 + cb
    zb0 = gb0 + cb
    segs_a = (("raw", 0, hq), ("raw", kv0, kv0 + KV_W), ("raw", kv0 + KV_W, kv1),
              ("silu", za0, ga0), ("glu", ga0, gb0, gb0, zb0), ("silu", zb0, zb0 + cb))
    dts_a = (F32, F32, F32, BF16, F32, BF16)
    w_in_a16 = w_in_a[0].astype(BF16)
    w_out_a16 = w_out_a[0].astype(BF16)
    k_pool = cache_a_k[0].reshape(n_pool, page, KV_W)
    v_pool = cache_a_v[0].reshape(n_pool, page, KV_W)

    q, k, v, sza, u, szb = ln_inproj(xp, sh_a[0], sc_a[0], norm_a[0], w_in_a16, segs_a, dts_a, tm_p)
    oa = moba_prompt(q, k.reshape(batch, seq, KV_W), v.reshape(batch, seq, KV_W), rel_bias, batch, seq)
    ob = conv_prompt(u, szb, conv_w_b[0], conv_b_b[0], ln_g_b[0], ln_b_b[0], batch, seq)
    xp1 = out_proj([(oa, sza, w_out_a16[:hq])], [(ob, w_out_a16[hq:])], xp, gt_a[0], None, tm_p)
    ak_p, av_p = k, v
    bc_p = u.reshape(batch, seq, cb)[:, seq - (CONV_W - 1):]

    q, k, v, sza, u, szb = ln_inproj(xs, sh_a[1], sc_a[1], norm_a[0], w_in_a16, segs_a, dts_a, tm_s)
    oa = moba_sample(q, k, v, k_pool, v_pool, page_table, rel_bias, t_new)
    xcat = jnp.concatenate([cache_b_conv[0], u.reshape(n, t_new, cb)], axis=1)
    ob = conv_sample(xcat, szb.reshape(n, t_new, cb), conv_w_b[0], conv_b_b[0], ln_g_b[0], ln_b_b[0], t_new)
    xs1 = out_proj([(oa, sza, w_out_a16[:hq])], [(ob.reshape(n * t_new, cb), w_out_a16[hq:])],
                   xs, gt_a[1], None, tm_s)
    ak_s, av_s = k, v
    bc_s = xcat[:, t_new:]

    segs_c = (("raw", 0, hq), ("raw", hq, hq + KV_W), ("raw", hq + KV_W, hq + 2 * KV_W),
              ("silu", hq + 2 * KV_W, 2 * hq + 2 * KV_W))
    dts_c = (F32, F32, F32, BF16)
    w_in_c16 = w_in_c[0].astype(BF16)
    w_out_c16 = w_out_c[0].astype(BF16)

    q, k, v, sz = ln_inproj(xp1, sh_c[0], sc_c[0], norm_c[0], w_in_c16, segs_c, dts_c, tm_p)
    o = swa_prompt(q, k, v, sinks_c[0], rel_bias, batch, seq)
    y_prompt = out_proj([(o, sz, w_out_c16)], [], xp1, gt_c[0], final_norm, tm_p)
    wb_p = min(SWA_WINDOW, seq)
    ck_p = k.reshape(batch, seq, KV_W)[:, seq - wb_p:]
    cv_p = v.reshape(batch, seq, KV_W)[:, seq - wb_p:]

    q, k, v, sz = ln_inproj(xs1, sh_c[1], sc_c[1], norm_c[0], w_in_c16, segs_c, dts_c, tm_s)
    wb_s = cache_c_k.shape[2]
    kcat = jnp.concatenate([cache_c_k[0].reshape(n, wb_s, KV_W), k.reshape(n, t_new, KV_W)], axis=1)
    vcat = jnp.concatenate([cache_c_v[0].reshape(n, wb_s, KV_W), v.reshape(n, t_new, KV_W)], axis=1)
    o = swa_sample(q, kcat, vcat, sinks_c[0], rel_bias, t_new)
    y_sample = out_proj([(o, sz, w_out_c16)], [], xs1, gt_c[1], final_norm, tm_s)
    ck_s, cv_s = kcat[:, t_new:], vcat[:, t_new:]

    def kv5(a, lead):
        return a.reshape((1,) + lead + (N_KV, HEAD_DIM))

    return (y_prompt.reshape(batch, seq, d), y_sample.reshape(n, t_new, d),
            kv5(ak_p, (batch, seq)), kv5(av_p, (batch, seq)),
            kv5(ak_s, (n, t_new)), kv5(av_s, (n, t_new)),
            bc_p[None], bc_s[None],
            kv5(ck_p, (batch, wb_p)), kv5(cv_p, (batch, wb_p)),
            kv5(ck_s, (n, wb_s)), kv5(cv_s, (n, wb_s)))
```

```python
import functools
import math

import jax
import jax.numpy as jnp
import numpy as np
from jax import lax
from jax.experimental import pallas as pl
from jax.experimental.pallas import tpu as pltpu

F32 = jnp.float32
BF16 = jnp.bfloat16
NEG_INF = float("-inf")
MASK_NEG = -1e30

HEAD_DIM = 64
N_HEADS = 16
N_KV = 2
GROUP = N_HEADS // N_KV
KV_W = N_KV * HEAD_DIM
MOBA_BLOCK = 256
MOBA_TOPK = 3
CONV_W = 31
SWA_WINDOW = 128
N_BUCKETS = 32
MAX_DISTANCE = 128
EPS = 1e-6
SCALE = HEAD_DIM ** -0.5
LOG2E = math.log2(math.e)

VMEM_LIMIT = 56 * 2**20


def _params(*sem):
    return pltpu.CompilerParams(dimension_semantics=sem, vmem_limit_bytes=VMEM_LIMIT)


def _silu(z):
    return z * jax.nn.sigmoid(z)


def _dot(a, b):
    return jnp.dot(a, b, preferred_element_type=F32)


def _dot_nt(a, b):
    return lax.dot_general(a, b, (((1,), (1,)), ((), ())), preferred_element_type=F32)


def _dot_f32(a, b):
    return jnp.dot(a, b, preferred_element_type=F32, precision=lax.Precision.HIGHEST)


def _t5_bucket_np(n):
    n = np.maximum(n, 0)
    max_exact = N_BUCKETS // 2
    nf = np.maximum(n, 1).astype(np.float32)
    large = max_exact + (np.log(nf / np.float32(max_exact)) / np.float32(math.log(MAX_DISTANCE / max_exact))
                         * np.float32(N_BUCKETS - max_exact)).astype(np.int32)
    large = np.minimum(large, N_BUCKETS - 1)
    return np.where(n < max_exact, n, large).astype(np.int32)


def _bucket_matrix(dist, valid):
    return np.where(valid, _t5_bucket_np(dist), -1).astype(np.int32)


def _bias_kernel(bm_ref, rb_ref, sub_ref, o_ref, *, mul):
    bm = bm_ref[...]
    acc = jnp.zeros(bm.shape, F32)
    for b in range(N_BUCKETS):
        acc = jnp.where(bm == b, rb_ref[b:b + 1, :], acc)
    o_ref[...] = jnp.where(bm < 0, NEG_INF, (acc - sub_ref[...]) * mul)


def bias_table(bm, rbx, sub, mul=1.0):
    g, _, c = rbx.shape
    r = bm.shape[0]
    return pl.pallas_call(
        functools.partial(_bias_kernel, mul=mul),
        grid=(g,),
        in_specs=[pl.BlockSpec((r, c), lambda i: (0, 0)),
                  pl.BlockSpec((None, N_BUCKETS, c), lambda i: (i, 0, 0)),
                  pl.BlockSpec((None, 1, c), lambda i: (i, 0, 0))],
        out_specs=pl.BlockSpec((None, r, c), lambda i: (i, 0, 0)),
        out_shape=jax.ShapeDtypeStruct((g, r, c), F32),
        compiler_params=_params("arbitrary"),
        name="bias_table",
    )(jnp.asarray(bm), rbx, sub)


def _mod_kernel(c_ref, w_ref, b_ref, o_ref):
    o_ref[...] = _dot_f32(_silu(c_ref[...]), w_ref[...]) + b_ref[...]


def modulation(c, w, b):
    n, d = c.shape
    m = w.shape[1]
    tn = 512
    return pl.pallas_call(
        _mod_kernel,
        grid=(m // tn,),
        in_specs=[pl.BlockSpec((n, d), lambda j: (0, 0)),
                  pl.BlockSpec((d, tn), lambda j: (0, j)),
                  pl.BlockSpec((1, tn), lambda j: (0, j))],
        out_specs=pl.BlockSpec((n, tn), lambda j: (0, j)),
        out_shape=jax.ShapeDtypeStruct((n, m), F32),
        compiler_params=_params("arbitrary"),
        name="modulation",
    )(c, w, b.reshape(1, m))


def _ln_inproj_kernel(x_ref, shift_ref, scale_ref, g_ref, w_ref, *out_refs, segs):
    x = x_ref[...]
    y = x * lax.rsqrt(jnp.mean(x * x, axis=-1, keepdims=True) + EPS)
    h = (y * g_ref[...]) * (1.0 + scale_ref[...]) + shift_ref[...]
    h16 = h.astype(BF16)
    for o_ref, seg in zip(out_refs, segs, strict=True):
        kind, lo, hi = seg[0], seg[1], seg[2]
        z = _dot(h16, w_ref[:, lo:hi])
        if kind == "silu":
            z = _silu(z)
        elif kind == "glu":
            z = z * jax.nn.sigmoid(_dot(h16, w_ref[:, seg[3]:seg[4]]))
        o_ref[...] = z.astype(o_ref.dtype)


def ln_inproj(x, shift, scale, norm_g, w16, segs, out_dtypes, tm):
    r, d = x.shape
    nt = r // tm
    per_group = nt // shift.shape[0]
    mrows = shift.shape[1]
    mod_spec = pl.BlockSpec((None, mrows, d), lambda i: (i // per_group, 0, 0))
    out_shape = [jax.ShapeDtypeStruct((r, s[2] - s[1]), dt) for s, dt in zip(segs, out_dtypes, strict=True)]
    out_specs = [pl.BlockSpec((tm, s[2] - s[1]), lambda i: (i, 0)) for s in segs]
    return pl.pallas_call(
        functools.partial(_ln_inproj_kernel, segs=segs),
        grid=(nt,),
        in_specs=[pl.BlockSpec((tm, d), lambda i: (i, 0)), mod_spec, mod_spec,
                  pl.BlockSpec((1, d), lambda i: (0, 0)),
                  pl.BlockSpec(w16.shape, lambda i: (0, 0))],
        out_specs=out_specs,
        out_shape=out_shape,
        compiler_params=_params("arbitrary"),
        name="ln_inproj",
    )(x, shift, scale, norm_g.reshape(1, d), w16)


def _out_proj_kernel(*refs, n_gated, n_plain, final_norm):
    it = iter(refs)
    y = None
    for _ in range(n_gated):
        a_ref, m_ref, w_ref = next(it), next(it), next(it)
        t = _dot((a_ref[...] * m_ref[...].astype(F32)).astype(BF16), w_ref[...])
        y = t if y is None else y + t
    for _ in range(n_plain):
        a_ref, w_ref = next(it), next(it)
        t = _dot(a_ref[...], w_ref[...])
        y = t if y is None else y + t
    x_ref, gate_ref = next(it), next(it)
    xn = x_ref[...] + gate_ref[...] * y
    if final_norm:
        fg_ref = next(it)
        xn = xn * lax.rsqrt(jnp.mean(xn * xn, axis=-1, keepdims=True) + EPS) * fg_ref[...]
    o_ref = next(it)
    o_ref[...] = xn


def out_proj(gated, plain, x, gate, final_g, tm):
    r, d = x.shape
    nt = r // tm
    per_group = nt // gate.shape[0]
    mrows = gate.shape[1]
    args, specs = [], []
    for a, m, w in gated:
        args += [a, m, w]
        specs += [pl.BlockSpec((tm, a.shape[1]), lambda i: (i, 0)),
                  pl.BlockSpec((tm, m.shape[1]), lambda i: (i, 0)),
                  pl.BlockSpec(w.shape, lambda i: (0, 0))]
    for a, w in plain:
        args += [a, w]
        specs += [pl.BlockSpec((tm, a.shape[1]), lambda i: (i, 0)),
                  pl.BlockSpec(w.shape, lambda i: (0, 0))]
    args += [x, gate]
    specs += [pl.BlockSpec((tm, d), lambda i: (i, 0)),
              pl.BlockSpec((None, mrows, d), lambda i: (i // per_group, 0, 0))]
    if final_g is not None:
        args.append(final_g.reshape(1, d))
        specs.append(pl.BlockSpec((1, d), lambda i: (0, 0)))
    return pl.pallas_call(
        functools.partial(_out_proj_kernel, n_gated=len(gated), n_plain=len(plain),
                          final_norm=final_g is not None),
        grid=(nt,),
        in_specs=specs,
        out_specs=pl.BlockSpec((tm, d), lambda i: (i, 0)),
        out_shape=jax.ShapeDtypeStruct((r, d), F32),
        compiler_params=_params("arbitrary"),
        name="out_proj",
    )(*args)


def _top3_mask(scores, n_valid):
    nb = scores.shape[0]
    blk = lax.broadcasted_iota(jnp.int32, scores.shape, 0)
    s = jnp.where(blk < n_valid, scores, NEG_INF)
    picked = jnp.zeros(scores.shape, F32)
    for _ in range(MOBA_TOPK):
        mx = jnp.max(s, axis=0, keepdims=True)
        first = jnp.min(jnp.where(s == mx, blk, nb), axis=0, keepdims=True)
        hit = blk == first
        picked = jnp.where(hit, 1.0, picked)
        s = jnp.where(hit, NEG_INF, s)
    return jnp.where(blk < n_valid, picked, 0.0)


V_AUG = HEAD_DIM + 16


def _moba_prompt_kernel(q_ref, k_ref, v_ref, bown_ref, badj_ref, o_ref,
                        k16_s, vT_s, kbar_s, qT_s, qT16_s, sel_s, m_s, acc_s,
                        s0_s, s1_s, p0_s, p1_s, a0_s, a1_s, *, nblk, chunk):
    g = pl.program_id(1)
    i = pl.program_id(2)
    rows = GROUP * MOBA_BLOCK
    nch = rows // chunk
    sub = chunk // 2

    @pl.when((g == 0) & (i == 0))
    def _per_batch():
        ones = jnp.ones((V_AUG - HEAD_DIM, MOBA_BLOCK), BF16)
        for j in range(nblk):
            kb = k_ref[j * MOBA_BLOCK:(j + 1) * MOBA_BLOCK, :]
            k16_s[j * MOBA_BLOCK:(j + 1) * MOBA_BLOCK, :] = kb.astype(BF16)
            kbar_s[j:j + 1, :] = jnp.mean(kb, axis=0, keepdims=True)
            v_t = v_ref[j * MOBA_BLOCK:(j + 1) * MOBA_BLOCK, :].T.astype(BF16)
            for kv in range(N_KV):
                vT_s[kv, j] = jnp.concatenate([v_t[kv * HEAD_DIM:(kv + 1) * HEAD_DIM], ones], axis=0)

    qT = (q_ref[...] * (SCALE * LOG2E)).T
    qs = jnp.concatenate([qT[h * HEAD_DIM:(h + 1) * HEAD_DIM, :] for h in range(GROUP)], axis=1)
    zeros = jnp.zeros_like(qs)
    qpad = jnp.where(g == 0, jnp.concatenate([qs, zeros], axis=0), jnp.concatenate([zeros, qs], axis=0))
    qT_s[...] = qpad
    qT16_s[...] = qpad.astype(BF16)
    sel_s[...] = _top3_mask(_dot_f32(kbar_s[...], qT_s[...]), i)
    m_s[...] = jnp.full(m_s.shape, NEG_INF, F32)
    acc_s[...] = jnp.zeros(acc_s.shape, F32)

    n_far = i - 1
    sbuf, pbuf, abuf = (s0_s, s1_s), (p0_s, p1_s), (a0_s, a1_s)

    def blk_of(t):
        far_j = jnp.clip(t - 2, 0, jnp.maximum(n_far - 1, 0))
        return jnp.where(t == 0, i, jnp.where(t == 1, jnp.maximum(i - 1, 0), far_j))

    def stage_qk(t, slot, c):
        ln = slice(c * chunk, (c + 1) * chunk)
        kb = k16_s[pl.ds(pl.multiple_of(blk_of(t) * MOBA_BLOCK, MOBA_BLOCK), MOBA_BLOCK), :]
        sbuf[slot][:, ln] = _dot(kb, qT16_s[:, ln])

    def stage_softmax(slot, c, on, bias_ref):
        ln = slice(c * sub, (c + 1) * sub)
        s_t = sbuf[slot][:, ln]
        if bias_ref is not None:
            s_t = s_t + bias_ref[:, ln]
        m_old = m_s[:, ln]
        m_new = jnp.maximum(m_old, jnp.where(on, jnp.max(s_t, axis=0, keepdims=True), NEG_INF))
        pbuf[slot][:, ln] = jnp.exp2(s_t - jnp.where(on, m_new, jnp.inf)).astype(BF16)
        abuf[slot][:, ln] = jnp.exp2(m_old - m_new)
        m_s[:, ln] = m_new

    def stage_pv(t, slot, c):
        ln = slice(c * chunk, (c + 1) * chunk)
        acc_s[:, ln] = acc_s[:, ln] * abuf[slot][:, ln] + _dot(vT_s[g, blk_of(t)], pbuf[slot][:, ln])

    def pipe_half(t, slot, bias_ref=None, pv=True):
        own = bias_ref is bown_ref
        valid = None if own else jnp.where(t == 1, i >= 1, t - 2 < n_far)
        j = blk_of(t)
        for c in range(nch):
            stage_qk(t + 1, 1 - slot, c)
            if own:
                on = jnp.full((1, chunk), True)
            else:
                on = (sel_s[pl.ds(j, 1), c * chunk:(c + 1) * chunk] > 0.0) & valid
            stage_softmax(slot, 2 * c, on[:, :sub], bias_ref)
            stage_softmax(slot, 2 * c + 1, on[:, sub:], bias_ref)
            if pv:
                stage_pv(t - 1, 1 - slot, c)

    for c in range(nch):
        stage_qk(0, 0, c)
    pipe_half(0, 0, bown_ref, pv=False)
    pipe_half(1, 1, badj_ref)

    def pair(u, carry):
        pipe_half(2 + 2 * u, 0)
        pipe_half(3 + 2 * u, 1)
        return carry

    n_pairs = (jnp.maximum(n_far, 0) + 1) // 2
    lax.fori_loop(0, n_pairs, pair, 0)
    for c in range(nch):
        stage_pv(1 + 2 * n_pairs, 1, c)

    acc = acc_s[...]
    o_t = acc[:HEAD_DIM] / acc[HEAD_DIM:HEAD_DIM + 1]
    o_cat = jnp.concatenate([o_t[:, h * MOBA_BLOCK:(h + 1) * MOBA_BLOCK] for h in range(GROUP)], axis=0)
    o_ref[...] = o_cat.T


def moba_prompt(q, k, v, rel_bias, batch, seq):
    nblk = seq // MOBA_BLOCK
    rows = GROUP * MOBA_BLOCK
    kk = np.arange(MOBA_BLOCK)[:, None]
    qq = (np.arange(rows) % MOBA_BLOCK)[None, :]
    bm_own = _bucket_matrix(qq - kk, qq >= kk)
    bm_adj = _bucket_matrix(MOBA_BLOCK + qq - kk, np.ones((MOBA_BLOCK, rows), bool))
    rbx = jnp.repeat(rel_bias.reshape(N_BUCKETS, N_KV, GROUP).transpose(1, 0, 2), MOBA_BLOCK, axis=2)
    far = rbx[:, N_BUCKETS - 1:N_BUCKETS, :]
    b_own = bias_table(bm_own, rbx, far, LOG2E)
    b_adj = bias_table(bm_adj, rbx, far, LOG2E)
    qw = GROUP * HEAD_DIM
    return pl.pallas_call(
        functools.partial(_moba_prompt_kernel, nblk=nblk, chunk=256),
        grid=(batch, N_KV, nblk),
        in_specs=[pl.BlockSpec((MOBA_BLOCK, qw), lambda b, g, i: (b * nblk + i, g)),
                  pl.BlockSpec((None, seq, KV_W), lambda b, g, i: (b, 0, 0)),
                  pl.BlockSpec((None, seq, KV_W), lambda b, g, i: (b, 0, 0)),
                  pl.BlockSpec((None, MOBA_BLOCK, rows), lambda b, g, i: (g, 0, 0)),
                  pl.BlockSpec((None, MOBA_BLOCK, rows), lambda b, g, i: (g, 0, 0))],
        out_specs=pl.BlockSpec((MOBA_BLOCK, qw), lambda b, g, i: (b * nblk + i, g)),
        out_shape=jax.ShapeDtypeStruct(q.shape, F32),
        scratch_shapes=[pltpu.VMEM((seq, KV_W), BF16),
                        pltpu.VMEM((N_KV, nblk, V_AUG, MOBA_BLOCK), BF16),
                        pltpu.VMEM((nblk, KV_W), F32),
                        pltpu.VMEM((KV_W, rows), F32),
                        pltpu.VMEM((KV_W, rows), BF16),
                        pltpu.VMEM((nblk, rows), F32),
                        pltpu.VMEM((1, rows), F32),
                        pltpu.VMEM((V_AUG, rows), F32),
                        pltpu.VMEM((MOBA_BLOCK, rows), F32), pltpu.VMEM((MOBA_BLOCK, rows), F32),
                        pltpu.VMEM((MOBA_BLOCK, rows), BF16), pltpu.VMEM((MOBA_BLOCK, rows), BF16),
                        pltpu.VMEM((1, rows), F32), pltpu.VMEM((1, rows), F32)],
        compiler_params=_params("arbitrary", "arbitrary", "arbitrary"),
        name="moba_prompt",
    )(q, k, v, b_own, b_adj)


def _moba_sample_kernel(pt_ref, q_ref, kn_ref, vn_ref, blast_ref, bown_ref, e_ref, kpool, vpool, o_ref,
                        kbuf, vbuf, ksem, vsem, s_s, *, n_pages, page, n_samples, unroll):
    s = pl.program_id(0)
    slot = s % 2
    ppb = MOBA_BLOCK // page
    nblk = n_pages // ppb
    rows = q_ref.shape[0]

    def k_copy(smp, sl, p):
        return pltpu.make_async_copy(kpool.at[pt_ref[smp, p]], kbuf.at[sl, p], ksem.at[sl])

    def v_copy(smp, sl, p):
        return pltpu.make_async_copy(vpool.at[pt_ref[smp, p]], vbuf.at[sl, p], vsem.at[sl])

    def start_all(smp, sl):
        def body(p, c):
            k_copy(smp, sl, p).start()
            v_copy(smp, sl, p).start()
            return c
        lax.fori_loop(0, n_pages, body, 0, unroll=8)

    @pl.when(s == 0)
    def _first():
        start_all(0, 0)

    @pl.when(s + 1 < n_samples)
    def _prefetch():
        start_all(s + 1, 1 - slot)

    def wait_k(p, c):
        k_copy(s, slot, p).wait()
        return c

    lax.fori_loop(0, n_pages, wait_k, 0, unroll=8)

    def block_t(buf, j):
        return jnp.concatenate([buf[slot, j * ppb + t] for t in range(ppb)], axis=1)

    blk_lane = lax.broadcasted_iota(jnp.int32, (KV_W, nblk), 1)

    def kbar_body(j, kbar_t):
        col = jnp.sum(block_t(kbuf, j), axis=1, keepdims=True) * (1.0 / MOBA_BLOCK)
        return jnp.where(blk_lane == j, col, kbar_t)

    kbar_t = lax.fori_loop(0, nblk, kbar_body, jnp.zeros((KV_W, nblk), F32), unroll=unroll)

    q = q_ref[...] * (SCALE * LOG2E)
    lower = lax.broadcasted_iota(jnp.int32, (rows, HEAD_DIM), 0) < rows // N_KV
    zq = jnp.zeros_like(q)
    qbd = jnp.concatenate([jnp.where(lower, q, zq), jnp.where(lower, zq, q)], axis=1)
    qbd16 = qbd.astype(BF16)

    scores = _dot_f32(qbd, kbar_t)
    sel_t = _top3_mask(scores.T, nblk)
    negm16 = jnp.where(sel_t.T > 0.0, 0.0, MASK_NEG).astype(BF16)

    def blk_slice(j):
        return pl.ds(pl.multiple_of(j * MOBA_BLOCK, MOBA_BLOCK), MOBA_BLOCK)

    def logits(j):
        return _dot(qbd16, block_t(kbuf, j).astype(BF16)) + _dot(negm16, e_ref[:, blk_slice(j)])

    def far(j, mrun):
        st = logits(j)
        s_s[:, blk_slice(j)] = st
        return jnp.maximum(mrun, st)

    mrun = lax.fori_loop(0, nblk - 1, far, jnp.full((rows, MOBA_BLOCK), NEG_INF, F32), unroll=unroll)
    s_last = logits(nblk - 1) + blast_ref[...]
    s_s[:, (nblk - 1) * MOBA_BLOCK:nblk * MOBA_BLOCK] = s_last
    mrun = jnp.maximum(mrun, s_last)
    kn16 = kn_ref[...].astype(BF16)
    s_own = _dot_nt(qbd16, kn16) + bown_ref[...]
    mrun = jnp.maximum(mrun, s_own)
    m = jnp.max(mrun, axis=1, keepdims=True)

    def wait_v(p, c):
        v_copy(s, slot, p).wait()
        return c

    lax.fori_loop(0, n_pages, wait_v, 0, unroll=8)

    def pv(j, carry):
        acc, lsum = carry
        p = jnp.exp2(s_s[:, blk_slice(j)] - m)
        return acc + _dot_nt(p.astype(BF16), block_t(vbuf, j).astype(BF16)), lsum + p

    p_own = jnp.exp2(s_own - m)
    acc0 = _dot(p_own.astype(BF16), vn_ref[...].astype(BF16))
    acc, lsum = lax.fori_loop(0, nblk, pv, (acc0, p_own), unroll=unroll)
    den = jnp.sum(lsum, axis=1, keepdims=True)
    o_ref[...] = jnp.where(lower, acc[:, :HEAD_DIM], acc[:, HEAD_DIM:]) / den


def moba_sample(q, k_new, v_new, k_pool, v_pool, page_table, rel_bias, t_new):
    n, n_pages = page_table.shape
    page = k_pool.shape[2]
    past = n_pages * page
    nblk = past // MOBA_BLOCK
    rows = N_HEADS * t_new
    q_r = q.reshape(n, t_new, N_HEADS, HEAD_DIM).transpose(0, 2, 1, 3).reshape(n, rows, HEAD_DIM)
    pad = ((0, 0), (0, MOBA_BLOCK - t_new), (0, 0))
    kn = jnp.pad(k_new.reshape(n, t_new, KV_W), pad)
    vn = jnp.pad(v_new.reshape(n, t_new, KV_W), pad)
    tok = (np.arange(rows) % t_new)[None, :]
    kk = np.arange(MOBA_BLOCK)[:, None]
    bm_last = _bucket_matrix(MOBA_BLOCK + tok - kk, np.ones((MOBA_BLOCK, rows), bool))
    bm_own = _bucket_matrix(tok - kk, (kk <= tok) & (kk < t_new))
    rbx = jnp.repeat(rel_bias, t_new, axis=1)[None]
    far = rbx[:, N_BUCKETS - 1:N_BUCKETS, :]
    b_last = bias_table(bm_last, rbx, far, LOG2E)[0].T
    b_own = bias_table(bm_own, rbx, far, LOG2E)[0].T
    onehot = jnp.asarray(np.repeat(np.eye(nblk, dtype=np.float32), MOBA_BLOCK, axis=1), BF16)

    grid_spec = pltpu.PrefetchScalarGridSpec(
        num_scalar_prefetch=1,
        grid=(n,),
        in_specs=[pl.BlockSpec((None, rows, HEAD_DIM), lambda s, pt: (s, 0, 0)),
                  pl.BlockSpec((None, MOBA_BLOCK, KV_W), lambda s, pt: (s, 0, 0)),
                  pl.BlockSpec((None, MOBA_BLOCK, KV_W), lambda s, pt: (s, 0, 0)),
                  pl.BlockSpec((rows, MOBA_BLOCK), lambda s, pt: (0, 0)),
                  pl.BlockSpec((rows, MOBA_BLOCK), lambda s, pt: (0, 0)),
                  pl.BlockSpec((nblk, past), lambda s, pt: (0, 0)),
                  pl.BlockSpec(memory_space=pl.ANY),
                  pl.BlockSpec(memory_space=pl.ANY)],
        out_specs=pl.BlockSpec((None, rows, HEAD_DIM), lambda s, pt: (s, 0, 0)),
        scratch_shapes=[pltpu.VMEM((2, n_pages, KV_W, page), F32),
                        pltpu.VMEM((2, n_pages, KV_W, page), F32),
                        pltpu.SemaphoreType.DMA((2,)),
                        pltpu.SemaphoreType.DMA((2,)),
                        pltpu.VMEM((rows, past), F32)],
    )
    o = pl.pallas_call(
        functools.partial(_moba_sample_kernel, n_pages=n_pages, page=page, n_samples=n, unroll=4),
        grid_spec=grid_spec,
        out_shape=jax.ShapeDtypeStruct((n, rows, HEAD_DIM), F32),
        compiler_params=_params("arbitrary"),
        name="moba_sample",
    )(page_table, q_r, kn, vn, b_last, b_own, onehot, k_pool, v_pool)
    return o.reshape(n, N_HEADS, t_new, HEAD_DIM).transpose(0, 2, 1, 3).reshape(n * t_new, N_HEADS * HEAD_DIM)


def _conv_tail(y, cb_ref, lg_ref, lb_ref, gate):
    y = y + cb_ref[...]
    mu = jnp.mean(y, axis=-1, keepdims=True)
    var = jnp.mean(jnp.square(y - mu), axis=-1, keepdims=True)
    yn = (y - mu) * lax.rsqrt(var + EPS) * lg_ref[...] + lb_ref[...]
    return _silu(yn) * gate


HALO = 32


def _conv_prompt_kernel(u_ref, prev_ref, g_ref, w_ref, cb_ref, lg_ref, lb_ref, o_ref, xs, *, tl):
    t = pl.program_id(1)
    xs[0:HALO, :] = jnp.where(t > 0, prev_ref[...], 0.0)
    xs[HALO:, :] = u_ref[...]
    off = HALO - (CONV_W - 1)
    y = jnp.zeros((tl, u_ref.shape[1]), F32)
    for k in range(CONV_W):
        y = y + xs[pl.ds(off + k, tl), :] * w_ref[k:k + 1, :]
    o_ref[...] = _conv_tail(y, cb_ref, lg_ref, lb_ref, g_ref[...].astype(F32)).astype(o_ref.dtype)


def conv_prompt(u, szb, conv_w, conv_b, ln_g, ln_b, batch, seq, tl=256):
    c = u.shape[1]
    nt = seq // tl
    vec = pl.BlockSpec((1, c), lambda b, t: (0, 0))
    return pl.pallas_call(
        functools.partial(_conv_prompt_kernel, tl=tl),
        grid=(batch, nt),
        in_specs=[pl.BlockSpec((tl, c), lambda b, t: (b * nt + t, 0)),
                  pl.BlockSpec((HALO, c), lambda b, t: (jnp.maximum((b * nt + t) * (tl // HALO) - 1, 0), 0)),
                  pl.BlockSpec((tl, c), lambda b, t: (b * nt + t, 0)),
                  pl.BlockSpec((CONV_W, c), lambda b, t: (0, 0)), vec, vec, vec],
        out_specs=pl.BlockSpec((tl, c), lambda b, t: (b * nt + t, 0)),
        out_shape=jax.ShapeDtypeStruct(u.shape, BF16),
        scratch_shapes=[pltpu.VMEM((HALO + tl, c), F32)],
        compiler_params=_params("arbitrary", "arbitrary"),
        name="conv_prompt",
    )(u, u, szb, conv_w, conv_b.reshape(1, c), ln_g.reshape(1, c), ln_b.reshape(1, c))


def _conv_sample_kernel(xp_ref, g_ref, w_ref, cb_ref, lg_ref, lb_ref, o_ref, *, t_new):
    ns, _, c = xp_ref.shape
    y = jnp.zeros((ns, t_new, c), F32)
    for k in range(CONV_W):
        y = y + xp_ref[:, pl.ds(k, t_new), :] * w_ref[k:k + 1, :]
    o_ref[...] = _conv_tail(y, cb_ref, lg_ref, lb_ref, g_ref[...].astype(F32)).astype(o_ref.dtype)


def conv_sample(xp, szb, conv_w, conv_b, ln_g, ln_b, t_new, ns=8):
    n, rows, c = xp.shape
    vec = pl.BlockSpec((1, c), lambda i: (0, 0))
    return pl.pallas_call(
        functools.partial(_conv_sample_kernel, t_new=t_new),
        grid=(n // ns,),
        in_specs=[pl.BlockSpec((ns, rows, c), lambda i: (i, 0, 0)),
                  pl.BlockSpec((ns, t_new, c), lambda i: (i, 0, 0)),
                  pl.BlockSpec((CONV_W, c), lambda i: (0, 0)), vec, vec, vec],
        out_specs=pl.BlockSpec((ns, t_new, c), lambda i: (i, 0, 0)),
        out_shape=jax.ShapeDtypeStruct((n, t_new, c), BF16),
        compiler_params=_params("arbitrary"),
        name="conv_sample",
    )(xp, szb, conv_w, conv_b.reshape(1, c), ln_g.reshape(1, c), ln_b.reshape(1, c))


def _swa_prompt_kernel(q_ref, kc_ref, kp_ref, vc_ref, vp_ref, bias_ref, sink_ref, o_ref):
    n = pl.program_id(1)
    w = SWA_WINDOW
    rows = GROUP * w
    kcat = jnp.concatenate([kp_ref[...], kc_ref[...]], axis=0).astype(BF16)
    vcat_t = jnp.concatenate([vp_ref[...], vc_ref[...]], axis=0).T.astype(BF16)
    key_id = lax.broadcasted_iota(jnp.int32, (2 * w, rows), 0)
    outs = []
    for g in range(N_KV):
        q_t = (q_ref[:, g * GROUP * HEAD_DIM:(g + 1) * GROUP * HEAD_DIM] * SCALE).T
        qs = jnp.concatenate([q_t[h * HEAD_DIM:(h + 1) * HEAD_DIM, :] for h in range(GROUP)], axis=1)
        zeros = jnp.zeros_like(qs)
        qpad = jnp.concatenate([qs, zeros] if g == 0 else [zeros, qs], axis=0).astype(BF16)
        s_t = _dot(kcat, qpad) + bias_ref[g]
        s_t = jnp.where((key_id >= w) | (n > 0), s_t, NEG_INF)
        sink = sink_ref[g]
        m = jnp.maximum(jnp.max(s_t, axis=0, keepdims=True), sink)
        p = jnp.exp(s_t - m)
        den = jnp.sum(p, axis=0, keepdims=True) + jnp.exp(sink - m)
        o_t = _dot(vcat_t, p.astype(BF16))[g * HEAD_DIM:(g + 1) * HEAD_DIM, :] / den
        outs.append(jnp.concatenate([o_t[:, h * w:(h + 1) * w] for h in range(GROUP)], axis=0).T)
    o_ref[...] = jnp.concatenate(outs, axis=1)


def swa_prompt(q, k, v, sinks, rel_bias, batch, seq):
    w = SWA_WINDOW
    nb = seq // w
    rows = GROUP * w
    kidx = np.arange(2 * w)[:, None]
    qq = (np.arange(rows) % w)[None, :]
    dist = w + qq - kidx
    bm = _bucket_matrix(dist, (dist >= 0) & (dist < w))
    rbx = jnp.repeat(rel_bias.reshape(N_BUCKETS, N_KV, GROUP).transpose(1, 0, 2), w, axis=2)
    bias = bias_table(bm, rbx, jnp.zeros((N_KV, 1, rows), F32))
    sink_x = jnp.repeat(sinks.reshape(N_KV, 1, GROUP), w, axis=2)
    d = q.shape[1]
    cur = lambda b, n: (b * nb + n, 0)
    prev = lambda b, n: (jnp.maximum(b * nb + n - 1, 0), 0)
    return pl.pallas_call(
        _swa_prompt_kernel,
        grid=(batch, nb),
        in_specs=[pl.BlockSpec((w, d), cur),
                  pl.BlockSpec((w, KV_W), cur), pl.BlockSpec((w, KV_W), prev),
                  pl.BlockSpec((w, KV_W), cur), pl.BlockSpec((w, KV_W), prev),
                  pl.BlockSpec((N_KV, 2 * w, rows), lambda b, n: (0, 0, 0)),
                  pl.BlockSpec((N_KV, 1, rows), lambda b, n: (0, 0, 0))],
        out_specs=pl.BlockSpec((w, d), cur),
        out_shape=jax.ShapeDtypeStruct(q.shape, F32),
        compiler_params=_params("arbitrary", "arbitrary"),
        name="swa_prompt",
    )(q, k, k, v, v, bias, sink_x)


def _swa_sample_kernel(q_ref, k_ref, v_ref, bias_ref, sink_ref, o_ref):
    ns, rows, _ = q_ref.shape
    r_id = lax.broadcasted_iota(jnp.int32, (rows, HEAD_DIM), 0)
    lower = r_id < rows // N_KV
    for i in range(ns):
        q = q_ref[i] * SCALE
        zq = jnp.zeros_like(q)
        qbd = jnp.concatenate([jnp.where(lower, q, zq), jnp.where(lower, zq, q)], axis=1).astype(BF16)
        s = _dot_nt(qbd, k_ref[i].astype(BF16)) + bias_ref[...]
        sink = sink_ref[...]
        m = jnp.maximum(jnp.max(s, axis=1, keepdims=True), sink)
        p = jnp.exp(s - m)
        den = jnp.sum(p, axis=1, keepdims=True) + jnp.exp(sink - m)
        acc = _dot(p.astype(BF16), v_ref[i].astype(BF16))
        o_ref[i] = jnp.where(lower, acc[:, :HEAD_DIM], acc[:, HEAD_DIM:]) / den


def swa_sample(q, kcat, vcat, sinks, rel_bias, t_new, ns=4):
    n, nk, _ = kcat.shape
    wb = nk - t_new
    rows = N_HEADS * t_new
    keys = -(-nk // 128) * 128
    q_r = q.reshape(n, t_new, N_HEADS, HEAD_DIM).transpose(0, 2, 1, 3).reshape(n, rows, HEAD_DIM)
    pad = ((0, 0), (0, keys - nk), (0, 0))
    kp, vp = jnp.pad(kcat, pad), jnp.pad(vcat, pad)
    tok = (np.arange(rows) % t_new)[None, :]
    idx = np.arange(keys)[:, None]
    dist = tok + wb - idx
    bm = _bucket_matrix(dist, (dist >= 0) & (dist < SWA_WINDOW) & (idx < nk))
    rbx = jnp.repeat(rel_bias, t_new, axis=1)[None]
    bias = bias_table(bm, rbx, jnp.zeros((1, 1, rows), F32))[0].T
    sink_r = jnp.repeat(sinks, t_new).reshape(rows, 1)
    o = pl.pallas_call(
        _swa_sample_kernel,
        grid=(n // ns,),
        in_specs=[pl.BlockSpec((ns, rows, HEAD_DIM), lambda i: (i, 0, 0)),
                  pl.BlockSpec((ns, keys, KV_W), lambda i: (i, 0, 0)),
                  pl.BlockSpec((ns, keys, KV_W), lambda i: (i, 0, 0)),
                  pl.BlockSpec((rows, keys), lambda i: (0, 0)),
                  pl.BlockSpec((rows, 1), lambda i: (0, 0))],
        out_specs=pl.BlockSpec((ns, rows, HEAD_DIM), lambda i: (i, 0, 0)),
        out_shape=jax.ShapeDtypeStruct((n, rows, HEAD_DIM), F32),
        compiler_params=_params("arbitrary"),
        name="swa_sample",
    )(q_r, kp, vp, bias, sink_r)
    return o.reshape(n, N_HEADS, t_new, HEAD_DIM).transpose(0, 2, 1, 3).reshape(n * t_new, N_HEADS * HEAD_DIM)


def kernel(x_prompt, x_sample, c_prompt, c_sample, cache_a_k, cache_a_v, page_table, cache_b_conv, cache_c_k, cache_c_v, rel_bias, norm_a, mod_w_a, mod_b_a, w_in_a, conv_w_b, conv_b_b, ln_g_b, ln_b_b, w_out_a, norm_c, mod_w_c, mod_b_c, w_in_c, sinks_c, w_out_c, final_norm):
    batch, seq, d = x_prompt.shape
    n, t_new, _ = x_sample.shape
    n_pool, page = cache_a_k.shape[1], cache_a_k.shape[2]
    hq = N_HEADS * HEAD_DIM
    cb = conv_w_b.shape[2]
    assert norm_a.shape[0] == 1 and norm_c.shape[0] == 1, "one A/B layer followed by one C layer"

    tm_p = 512
    tm_s = min(256, n * t_new)
    xp = x_prompt.reshape(batch * seq, d)
    xs = x_sample.reshape(n * t_new, d)

    c_all = jnp.concatenate([c_prompt, c_sample], axis=0)
    c_rows = -(-c_all.shape[0] // 8) * 8
    c_all = jnp.pad(c_all, ((0, c_rows - c_all.shape[0]), (0, 0)))

    def split_mod(m):
        mp = m[:batch].reshape(batch, 1, 3, d)
        ms = jnp.repeat(m[batch:batch + n], t_new, axis=0).reshape(n * t_new // tm_s, tm_s, 3, d)
        return [(mp[:, :, j], ms[:, :, j]) for j in range(3)]

    (sh_a, sc_a, gt_a) = split_mod(modulation(c_all, mod_w_a[0], mod_b_a[0]))
    (sh_c, sc_c, gt_c) = split_mod(modulation(c_all, mod_w_c[0], mod_b_c[0]))

    kv0, kv1 = hq, hq + 2 * KV_W
    za0 = kv1
    ga0 = za0 + hq
    gb0 = ga0 + cb
    zb0 = gb0 + cb
    segs_a = (("raw", 0, hq), ("raw", kv0, kv0 + KV_W), ("raw", kv0 + KV_W, kv1),
              ("silu", za0, ga0), ("glu", ga0, gb0, gb0, zb0), ("silu", zb0, zb0 + cb))
    dts_a = (F32, F32, F32, BF16, F32, BF16)
    w_in_a16 = w_in_a[0].astype(BF16)
    w_out_a16 = w_out_a[0].astype(BF16)
    k_pool = cache_a_k[0].transpose(0, 2, 3, 1).reshape(n_pool, KV_W, page)
    v_pool = cache_a_v[0].transpose(0, 2, 3, 1).reshape(n_pool, KV_W, page)

    q, k, v, sza, u, szb = ln_inproj(xp, sh_a[0], sc_a[0], norm_a[0], w_in_a16, segs_a, dts_a, tm_p)
    oa = moba_prompt(q, k.reshape(batch, seq, KV_W), v.reshape(batch, seq, KV_W), rel_bias, batch, seq)
    ob = conv_prompt(u, szb, conv_w_b[0], conv_b_b[0], ln_g_b[0], ln_b_b[0], batch, seq)
    xp1 = out_proj([(oa, sza, w_out_a16[:hq])], [(ob, w_out_a16[hq:])], xp, gt_a[0], None, tm_p)
    ak_p, av_p = k, v
    bc_p = u.reshape(batch, seq, cb)[:, seq - (CONV_W - 1):]

    q, k, v, sza, u, szb = ln_inproj(xs, sh_a[1], sc_a[1], norm_a[0], w_in_a16, segs_a, dts_a, tm_s)
    oa = moba_sample(q, k, v, k_pool, v_pool, page_table, rel_bias, t_new)
    xcat = jnp.concatenate([cache_b_conv[0], u.reshape(n, t_new, cb)], axis=1)
    ob = conv_sample(xcat, szb.reshape(n, t_new, cb), conv_w_b[0], conv_b_b[0], ln_g_b[0], ln_b_b[0], t_new)
    xs1 = out_proj([(oa, sza, w_out_a16[:hq])], [(ob.reshape(n * t_new, cb), w_out_a16[hq:])],
                   xs, gt_a[1], None, tm_s)
    ak_s, av_s = k, v
    bc_s = xcat[:, t_new:]

    segs_c = (("raw", 0, hq), ("raw", hq, hq + KV_W), ("raw", hq + KV_W, hq + 2 * KV_W),
              ("silu", hq + 2 * KV_W, 2 * hq + 2 * KV_W))
    dts_c = (F32, F32, F32, BF16)
    w_in_c16 = w_in_c[0].astype(BF16)
    w_out_c16 = w_out_c[0].astype(BF16)

    q, k, v, sz = ln_inproj(xp1, sh_c[0], sc_c[0], norm_c[0], w_in_c16, segs_c, dts_c, tm_p)
    o = swa_prompt(q, k, v, sinks_c[0], rel_bias, batch, seq)
    y_prompt = out_proj([(o, sz, w_out_c16)], [], xp1, gt_c[0], final_norm, tm_p)
    wb_p = min(SWA_WINDOW, seq)
    ck_p = k.reshape(batch, seq, KV_W)[:, seq - wb_p:]
    cv_p = v.reshape(batch, seq, KV_W)[:, seq - wb_p:]

    q, k, v, sz = ln_inproj(xs1, sh_c[1], sc_c[1], norm_c[0], w_in_c16, segs_c, dts_c, tm_s)
    wb_s = cache_c_k.shape[2]
    kcat = jnp.concatenate([cache_c_k[0].reshape(n, wb_s, KV_W), k.reshape(n, t_new, KV_W)], axis=1)
    vcat = jnp.concatenate([cache_c_v[0].reshape(n, wb_s, KV_W), v.reshape(n, t_new, KV_W)], axis=1)
    o = swa_sample(q, kcat, vcat, sinks_c[0], rel_bias, t_new)
    y_sample = out_proj([(o, sz, w_out_c16)], [], xs1, gt_c[1], final_norm, tm_s)
    ck_s, cv_s = kcat[:, t_new:], vcat[:, t_new:]

    def kv5(a, lead):
        return a.reshape((1,) + lead + (N_KV, HEAD_DIM))

    return (y_prompt.reshape(batch, seq, d), y_sample.reshape(n, t_new, d),
            kv5(ak_p, (batch, seq)), kv5(av_p, (batch, seq)),
            kv5(ak_s, (n, t_new)), kv5(av_s, (n, t_new)),
            bc_p[None], bc_s[None],
            kv5(ck_p, (batch, wb_p)), kv5(cv_p, (batch, wb_p)),
            kv5(ck_s, (n, wb_s)), kv5(cv_s, (n, wb_s)))
```

```python
import functools
import math

import jax
import jax.numpy as jnp
import numpy as np
from jax import lax
from jax.experimental import pallas as pl
from jax.experimental.pallas import tpu as pltpu

F32 = jnp.float32
BF16 = jnp.bfloat16
NEG_INF = float("-inf")
MASK_NEG = -1e30

HEAD_DIM = 64
N_HEADS = 16
N_KV = 2
GROUP = N_HEADS // N_KV
KV_W = N_KV * HEAD_DIM
MOBA_BLOCK = 256
MOBA_TOPK = 3
CONV_W = 31
SWA_WINDOW = 128
N_BUCKETS = 32
MAX_DISTANCE = 128
EPS = 1e-6
SCALE = HEAD_DIM ** -0.5
LOG2E = math.log2(math.e)

VMEM_LIMIT = 56 * 2**20


def _params(*sem):
    return pltpu.CompilerParams(dimension_semantics=sem, vmem_limit_bytes=VMEM_LIMIT)


def _silu(z):
    return z * jax.nn.sigmoid(z)


def _dot(a, b):
    return jnp.dot(a, b, preferred_element_type=F32)


def _dot_nt(a, b):
    return lax.dot_general(a, b, (((1,), (1,)), ((), ())), preferred_element_type=F32)


def _dot_f32(a, b):
    return jnp.dot(a, b, preferred_element_type=F32, precision=lax.Precision.HIGHEST)


def _t5_bucket_np(n):
    n = np.maximum(n, 0)
    max_exact = N_BUCKETS // 2
    nf = np.maximum(n, 1).astype(np.float32)
    large = max_exact + (np.log(nf / np.float32(max_exact)) / np.float32(math.log(MAX_DISTANCE / max_exact))
                         * np.float32(N_BUCKETS - max_exact)).astype(np.int32)
    large = np.minimum(large, N_BUCKETS - 1)
    return np.where(n < max_exact, n, large).astype(np.int32)


def _bucket_matrix(dist, valid):
    return np.where(valid, _t5_bucket_np(dist), -1).astype(np.int32)


def _bias_kernel(bm_ref, rb_ref, sub_ref, o_ref, *, mul):
    bm = bm_ref[...]
    acc = jnp.zeros(bm.shape, F32)
    for b in range(N_BUCKETS):
        acc = jnp.where(bm == b, rb_ref[b:b + 1, :], acc)
    o_ref[...] = jnp.where(bm < 0, NEG_INF, (acc - sub_ref[...]) * mul)


def bias_table(bm, rbx, sub, mul=1.0):
    g, _, c = rbx.shape
    r = bm.shape[0]
    return pl.pallas_call(
        functools.partial(_bias_kernel, mul=mul),
        grid=(g,),
        in_specs=[pl.BlockSpec((r, c), lambda i: (0, 0)),
                  pl.BlockSpec((None, N_BUCKETS, c), lambda i: (i, 0, 0)),
                  pl.BlockSpec((None, 1, c), lambda i: (i, 0, 0))],
        out_specs=pl.BlockSpec((None, r, c), lambda i: (i, 0, 0)),
        out_shape=jax.ShapeDtypeStruct((g, r, c), F32),
        compiler_params=_params("arbitrary"),
        name="bias_table",
    )(jnp.asarray(bm), rbx, sub)


def _mod_kernel(c_ref, w_ref, b_ref, o_ref):
    o_ref[...] = _dot_f32(_silu(c_ref[...]), w_ref[...]) + b_ref[...]


def modulation(c, w, b):
    n, d = c.shape
    m = w.shape[1]
    tn = 512
    return pl.pallas_call(
        _mod_kernel,
        grid=(m // tn,),
        in_specs=[pl.BlockSpec((n, d), lambda j: (0, 0)),
                  pl.BlockSpec((d, tn), lambda j: (0, j)),
                  pl.BlockSpec((1, tn), lambda j: (0, j))],
        out_specs=pl.BlockSpec((n, tn), lambda j: (0, j)),
        out_shape=jax.ShapeDtypeStruct((n, m), F32),
        compiler_params=_params("arbitrary"),
        name="modulation",
    )(c, w, b.reshape(1, m))


def _ln_inproj_kernel(x_ref, shift_ref, scale_ref, g_ref, w_ref, *out_refs, segs):
    x = x_ref[...]
    y = x * lax.rsqrt(jnp.mean(x * x, axis=-1, keepdims=True) + EPS)
    h = (y * g_ref[...]) * (1.0 + scale_ref[...]) + shift_ref[...]
    h16 = h.reshape(-1, h.shape[-1]).astype(BF16)
    for o_ref, seg in zip(out_refs, segs, strict=True):
        kind, lo, hi = seg[0], seg[1], seg[2]
        z = _dot(h16, w_ref[:, lo:hi])
        if kind == "silu":
            z = _silu(z)
        elif kind == "glu":
            z = z * jax.nn.sigmoid(_dot(h16, w_ref[:, seg[3]:seg[4]]))
        o_ref[...] = z.astype(o_ref.dtype)


def ln_inproj(x, shift, scale, norm_g, w16, segs, out_dtypes, tm):
    d = x.shape[-1]
    if x.ndim == 2:
        r = x.shape[0]
        nt = r // tm
        per_group = nt // shift.shape[0]
        x_spec = pl.BlockSpec((tm, d), lambda i: (i, 0))
        mod_spec = pl.BlockSpec((None, 1, d), lambda i: (i // per_group, 0, 0))
    else:
        n, t_new, _ = x.shape
        r = n * t_new
        nt = r // tm
        x_spec = pl.BlockSpec((tm // t_new, t_new, d), lambda i: (i, 0, 0))
        mod_spec = pl.BlockSpec((tm // t_new, 1, d), lambda i: (i, 0, 0))
    out_shape = [jax.ShapeDtypeStruct((r, s[2] - s[1]), dt) for s, dt in zip(segs, out_dtypes, strict=True)]
    out_specs = [pl.BlockSpec((tm, s[2] - s[1]), lambda i: (i, 0)) for s in segs]
    return pl.pallas_call(
        functools.partial(_ln_inproj_kernel, segs=segs),
        grid=(nt,),
        in_specs=[x_spec, mod_spec, mod_spec,
                  pl.BlockSpec((1, d), lambda i: (0, 0)),
                  pl.BlockSpec(w16.shape, lambda i: (0, 0))],
        out_specs=out_specs,
        out_shape=out_shape,
        compiler_params=_params("arbitrary"),
        name="ln_inproj",
    )(x, shift, scale, norm_g.reshape(1, d), w16)


def _out_proj_kernel(*refs, n_gated, n_plain, final_norm):
    it = iter(refs)
    y = None
    for _ in range(n_gated):
        a_ref, m_ref, w_ref = next(it), next(it), next(it)
        t = _dot((a_ref[...] * m_ref[...].astype(F32)).astype(BF16), w_ref[...])
        y = t if y is None else y + t
    for _ in range(n_plain):
        a_ref, w_ref = next(it), next(it)
        t = _dot(a_ref[...], w_ref[...])
        y = t if y is None else y + t
    x_ref, gate_ref = next(it), next(it)
    xn = x_ref[...] + gate_ref[...] * y.reshape(x_ref.shape)
    if final_norm:
        fg_ref = next(it)
        xn = xn * lax.rsqrt(jnp.mean(xn * xn, axis=-1, keepdims=True) + EPS) * fg_ref[...]
    o_ref = next(it)
    o_ref[...] = xn


def out_proj(gated, plain, x, gate, final_g, tm):
    d = x.shape[-1]
    if x.ndim == 2:
        nt = x.shape[0] // tm
        per_group = nt // gate.shape[0]
        x_spec = pl.BlockSpec((tm, d), lambda i: (i, 0))
        gate_spec = pl.BlockSpec((None, 1, d), lambda i: (i // per_group, 0, 0))
    else:
        n, t_new, _ = x.shape
        nt = n * t_new // tm
        x_spec = pl.BlockSpec((tm // t_new, t_new, d), lambda i: (i, 0, 0))
        gate_spec = pl.BlockSpec((tm // t_new, 1, d), lambda i: (i, 0, 0))
    args, specs = [], []
    for a, m, w in gated:
        args += [a, m, w]
        specs += [pl.BlockSpec((tm, a.shape[1]), lambda i: (i, 0)),
                  pl.BlockSpec((tm, m.shape[1]), lambda i: (i, 0)),
                  pl.BlockSpec(w.shape, lambda i: (0, 0))]
    for a, w in plain:
        args += [a, w]
        specs += [pl.BlockSpec((tm, a.shape[1]), lambda i: (i, 0)),
                  pl.BlockSpec(w.shape, lambda i: (0, 0))]
    args += [x, gate]
    specs += [x_spec, gate_spec]
    if final_g is not None:
        args.append(final_g.reshape(1, d))
        specs.append(pl.BlockSpec((1, d), lambda i: (0, 0)))
    return pl.pallas_call(
        functools.partial(_out_proj_kernel, n_gated=len(gated), n_plain=len(plain),
                          final_norm=final_g is not None),
        grid=(nt,),
        in_specs=specs,
        out_specs=x_spec,
        out_shape=jax.ShapeDtypeStruct(x.shape, F32),
        compiler_params=_params("arbitrary"),
        name="out_proj",
    )(*args)


def _top3_mask(scores, n_valid):
    nb = scores.shape[0]
    blk = lax.broadcasted_iota(jnp.int32, scores.shape, 0)
    s = jnp.where(blk < n_valid, scores, NEG_INF)
    picked = jnp.zeros(scores.shape, F32)
    for _ in range(MOBA_TOPK):
        mx = jnp.max(s, axis=0, keepdims=True)
        first = jnp.min(jnp.where(s == mx, blk, nb), axis=0, keepdims=True)
        hit = blk == first
        picked = jnp.where(hit, 1.0, picked)
        s = jnp.where(hit, NEG_INF, s)
    return jnp.where(blk < n_valid, picked, 0.0)


V_AUG = HEAD_DIM + 16


def _moba_prompt_kernel(q_ref, k_ref, v_ref, bown_ref, badj_ref, o_ref,
                        k16_s, vT_s, kbar_s, qT_s, qT16_s, sel_s, m_s, acc_s,
                        s0_s, s1_s, p0_s, p1_s, a0_s, a1_s, *, nblk, chunk):
    g = pl.program_id(1)
    i = pl.program_id(2)
    rows = GROUP * MOBA_BLOCK
    nch = rows // chunk
    sub = chunk // 2

    @pl.when((g == 0) & (i == 0))
    def _per_batch():
        ones = jnp.ones((V_AUG - HEAD_DIM, MOBA_BLOCK), BF16)
        for j in range(nblk):
            kb = k_ref[j * MOBA_BLOCK:(j + 1) * MOBA_BLOCK, :]
            k16_s[j * MOBA_BLOCK:(j + 1) * MOBA_BLOCK, :] = kb.astype(BF16)
            kbar_s[j:j + 1, :] = jnp.mean(kb, axis=0, keepdims=True)
            v_t = v_ref[j * MOBA_BLOCK:(j + 1) * MOBA_BLOCK, :].T.astype(BF16)
            for kv in range(N_KV):
                vT_s[kv, j] = jnp.concatenate([v_t[kv * HEAD_DIM:(kv + 1) * HEAD_DIM], ones], axis=0)

    qT = (q_ref[...] * (SCALE * LOG2E)).T
    qs = jnp.concatenate([qT[h * HEAD_DIM:(h + 1) * HEAD_DIM, :] for h in range(GROUP)], axis=1)
    zeros = jnp.zeros_like(qs)
    qpad = jnp.where(g == 0, jnp.concatenate([qs, zeros], axis=0), jnp.concatenate([zeros, qs], axis=0))
    qT_s[...] = qpad
    qT16_s[...] = qpad.astype(BF16)
    sel_s[...] = _top3_mask(_dot_f32(kbar_s[...], qT_s[...]), i)
    m_s[...] = jnp.full(m_s.shape, NEG_INF, F32)
    acc_s[...] = jnp.zeros(acc_s.shape, F32)

    n_far = i - 1
    sbuf, pbuf, abuf = (s0_s, s1_s), (p0_s, p1_s), (a0_s, a1_s)

    def blk_of(t):
        far_j = jnp.clip(t - 2, 0, jnp.maximum(n_far - 1, 0))
        return jnp.where(t == 0, i, jnp.where(t == 1, jnp.maximum(i - 1, 0), far_j))

    def stage_qk(t, slot, c):
        ln = slice(c * chunk, (c + 1) * chunk)
        kb = k16_s[pl.ds(pl.multiple_of(blk_of(t) * MOBA_BLOCK, MOBA_BLOCK), MOBA_BLOCK), :]
        sbuf[slot][:, ln] = _dot(kb, qT16_s[:, ln])

    def stage_softmax(slot, c, on, bias_ref):
        ln = slice(c * sub, (c + 1) * sub)
        s_t = sbuf[slot][:, ln]
        if bias_ref is not None:
            s_t = s_t + bias_ref[:, ln]
        m_old = m_s[:, ln]
        m_new = jnp.maximum(m_old, jnp.where(on, jnp.max(s_t, axis=0, keepdims=True), NEG_INF))
        pbuf[slot][:, ln] = jnp.exp2(s_t - jnp.where(on, m_new, jnp.inf)).astype(BF16)
        abuf[slot][:, ln] = jnp.exp2(m_old - m_new)
        m_s[:, ln] = m_new

    def stage_pv(t, slot, c):
        ln = slice(c * chunk, (c + 1) * chunk)
        acc_s[:, ln] = acc_s[:, ln] * abuf[slot][:, ln] + _dot(vT_s[g, blk_of(t)], pbuf[slot][:, ln])

    def pipe_half(t, slot, bias_ref=None, pv=True):
        own = bias_ref is bown_ref
        valid = None if own else jnp.where(t == 1, i >= 1, t - 2 < n_far)
        j = blk_of(t)
        for c in range(nch):
            stage_qk(t + 1, 1 - slot, c)
            if own:
                on = jnp.full((1, chunk), True)
            else:
                on = (sel_s[pl.ds(j, 1), c * chunk:(c + 1) * chunk] > 0.0) & valid
            stage_softmax(slot, 2 * c, on[:, :sub], bias_ref)
            stage_softmax(slot, 2 * c + 1, on[:, sub:], bias_ref)
            if pv:
                stage_pv(t - 1, 1 - slot, c)

    for c in range(nch):
        stage_qk(0, 0, c)
    pipe_half(0, 0, bown_ref, pv=False)
    pipe_half(1, 1, badj_ref)

    def pair(u, carry):
        pipe_half(2 + 2 * u, 0)
        pipe_half(3 + 2 * u, 1)
        return carry

    n_pairs = (jnp.maximum(n_far, 0) + 1) // 2
    lax.fori_loop(0, n_pairs, pair, 0)
    for c in range(nch):
        stage_pv(1 + 2 * n_pairs, 1, c)

    acc = acc_s[...]
    o_t = acc[:HEAD_DIM] / acc[HEAD_DIM:HEAD_DIM + 1]
    o_cat = jnp.concatenate([o_t[:, h * MOBA_BLOCK:(h + 1) * MOBA_BLOCK] for h in range(GROUP)], axis=0)
    o_ref[...] = o_cat.T


def moba_prompt(q, k, v, rel_bias, batch, seq):
    nblk = seq // MOBA_BLOCK
    rows = GROUP * MOBA_BLOCK
    kk = np.arange(MOBA_BLOCK)[:, None]
    qq = (np.arange(rows) % MOBA_BLOCK)[None, :]
    bm_own = _bucket_matrix(qq - kk, qq >= kk)
    bm_adj = _bucket_matrix(MOBA_BLOCK + qq - kk, np.ones((MOBA_BLOCK, rows), bool))
    rbx = jnp.repeat(rel_bias.reshape(N_BUCKETS, N_KV, GROUP).transpose(1, 0, 2), MOBA_BLOCK, axis=2)
    far = rbx[:, N_BUCKETS - 1:N_BUCKETS, :]
    b_own = bias_table(bm_own, rbx, far, LOG2E)
    b_adj = bias_table(bm_adj, rbx, far, LOG2E)
    qw = GROUP * HEAD_DIM
    return pl.pallas_call(
        functools.partial(_moba_prompt_kernel, nblk=nblk, chunk=256),
        grid=(batch, N_KV, nblk),
        in_specs=[pl.BlockSpec((MOBA_BLOCK, qw), lambda b, g, i: (b * nblk + i, g)),
                  pl.BlockSpec((None, seq, KV_W), lambda b, g, i: (b, 0, 0)),
                  pl.BlockSpec((None, seq, KV_W), lambda b, g, i: (b, 0, 0)),
                  pl.BlockSpec((None, MOBA_BLOCK, rows), lambda b, g, i: (g, 0, 0)),
                  pl.BlockSpec((None, MOBA_BLOCK, rows), lambda b, g, i: (g, 0, 0))],
        out_specs=pl.BlockSpec((MOBA_BLOCK, qw), lambda b, g, i: (b * nblk + i, g)),
        out_shape=jax.ShapeDtypeStruct(q.shape, F32),
        scratch_shapes=[pltpu.VMEM((seq, KV_W), BF16),
                        pltpu.VMEM((N_KV, nblk, V_AUG, MOBA_BLOCK), BF16),
                        pltpu.VMEM((nblk, KV_W), F32),
                        pltpu.VMEM((KV_W, rows), F32),
                        pltpu.VMEM((KV_W, rows), BF16),
                        pltpu.VMEM((nblk, rows), F32),
                        pltpu.VMEM((1, rows), F32),
                        pltpu.VMEM((V_AUG, rows), F32),
                        pltpu.VMEM((MOBA_BLOCK, rows), F32), pltpu.VMEM((MOBA_BLOCK, rows), F32),
                        pltpu.VMEM((MOBA_BLOCK, rows), BF16), pltpu.VMEM((MOBA_BLOCK, rows), BF16),
                        pltpu.VMEM((1, rows), F32), pltpu.VMEM((1, rows), F32)],
        compiler_params=_params("arbitrary", "arbitrary", "arbitrary"),
        name="moba_prompt",
    )(q, k, v, b_own, b_adj)


def _moba_sample_kernel(pt_ref, q_ref, kn_ref, vn_ref, blast_ref, bown_ref, e_ref, kpool, vpool, o_ref,
                        kbuf, vbuf, ksem, vsem, s_s, *, n_pages, page, n_samples, unroll):
    s = pl.program_id(0)
    slot = s % 2
    ppb = MOBA_BLOCK // page
    nblk = n_pages // ppb
    rows = q_ref.shape[0]

    def k_copy(smp, sl, p):
        return pltpu.make_async_copy(kpool.at[pt_ref[smp, p]], kbuf.at[sl, p], ksem.at[sl])

    def v_copy(smp, sl, p):
        return pltpu.make_async_copy(vpool.at[pt_ref[smp, p]], vbuf.at[sl, p], vsem.at[sl])

    def start_all(smp, sl):
        def body(p, c):
            k_copy(smp, sl, p).start()
            v_copy(smp, sl, p).start()
            return c
        lax.fori_loop(0, n_pages, body, 0, unroll=8)

    @pl.when(s == 0)
    def _first():
        start_all(0, 0)

    @pl.when(s + 1 < n_samples)
    def _prefetch():
        start_all(s + 1, 1 - slot)

    def wait_k(p, c):
        k_copy(s, slot, p).wait()
        return c

    lax.fori_loop(0, n_pages, wait_k, 0, unroll=8)

    def block_t(buf, j):
        return jnp.concatenate([buf[slot, j * ppb + t] for t in range(ppb)], axis=1)

    blk_lane = lax.broadcasted_iota(jnp.int32, (KV_W, nblk), 1)

    def kbar_body(j, kbar_t):
        pages = kbuf[slot, j * ppb]
        for t in range(1, ppb):
            pages = pages + kbuf[slot, j * ppb + t]
        col = jnp.sum(pages, axis=1, keepdims=True) * (1.0 / MOBA_BLOCK)
        return jnp.where(blk_lane == j, col, kbar_t)

    kbar_t = lax.fori_loop(0, nblk, kbar_body, jnp.zeros((KV_W, nblk), F32), unroll=2 * unroll)

    q = q_ref[...] * (SCALE * LOG2E)
    lower = lax.broadcasted_iota(jnp.int32, (rows, HEAD_DIM), 0) < rows // N_KV
    zq = jnp.zeros_like(q)
    qbd = jnp.concatenate([jnp.where(lower, q, zq), jnp.where(lower, zq, q)], axis=1)
    qbd16 = qbd.astype(BF16)

    scores = _dot_f32(qbd, kbar_t)
    sel_t = _top3_mask(scores.T, nblk)
    negm = jnp.where(sel_t.T > 0.0, 0.0, MASK_NEG)
    lhs16 = jnp.concatenate([qbd, negm, jnp.zeros((rows, e_ref.shape[0] - nblk), F32)], axis=1).astype(BF16)

    def blk_slice(j):
        return pl.ds(pl.multiple_of(j * MOBA_BLOCK, MOBA_BLOCK), MOBA_BLOCK)

    def logits(j):
        rhs16 = jnp.concatenate([block_t(kbuf, j).astype(BF16), e_ref[:, blk_slice(j)]], axis=0)
        return _dot(lhs16, rhs16)

    def fold(x):
        return x[:, :LANES], x[:, LANES:]

    def far(j, mrun):
        st = logits(j)
        s_s[:, blk_slice(j)] = st
        lo, hi = fold(st)
        return jnp.maximum(mrun, jnp.maximum(lo, hi))

    mrun = lax.fori_loop(0, nblk - 1, far, jnp.full((rows, LANES), NEG_INF, F32), unroll=unroll - 1)
    s_last = logits(nblk - 1) + blast_ref[...]
    s_s[:, (nblk - 1) * MOBA_BLOCK:nblk * MOBA_BLOCK] = s_last
    kn16 = kn_ref[...].astype(BF16)
    s_own = _dot_nt(qbd16, kn16) + bown_ref[...]
    for part in fold(s_last) + fold(s_own):
        mrun = jnp.maximum(mrun, part)
    m = jnp.max(mrun, axis=1, keepdims=True)

    def wait_v(p, c):
        v_copy(s, slot, p).wait()
        return c

    lax.fori_loop(0, n_pages, wait_v, 0, unroll=8)

    def pv(j, carry):
        acc, lsum = carry
        p = jnp.exp2(s_s[:, blk_slice(j)] - m)
        lo, hi = fold(p)
        return acc + _dot_nt(p.astype(BF16), block_t(vbuf, j).astype(BF16)), lsum + (lo + hi)

    p_own = jnp.exp2(s_own - m)
    acc0 = _dot(p_own.astype(BF16), vn_ref[...].astype(BF16))
    lo, hi = fold(p_own)
    acc, lsum = lax.fori_loop(0, nblk, pv, (acc0, lo + hi), unroll=unroll)
    den = jnp.sum(lsum, axis=1, keepdims=True)
    o_ref[...] = jnp.where(lower, acc[:, :HEAD_DIM], acc[:, HEAD_DIM:]) / den


def moba_sample(q, k_new, v_new, k_pool, v_pool, page_table, rel_bias, t_new):
    n, n_pages = page_table.shape
    page = k_pool.shape[2]
    past = n_pages * page
    nblk = past // MOBA_BLOCK
    rows = N_HEADS * t_new
    q_r = q.reshape(n, t_new, N_HEADS, HEAD_DIM).transpose(0, 2, 1, 3).reshape(n, rows, HEAD_DIM)
    pad = ((0, 0), (0, MOBA_BLOCK - t_new), (0, 0))
    kn = jnp.pad(k_new.reshape(n, t_new, KV_W), pad)
    vn = jnp.pad(v_new.reshape(n, t_new, KV_W), pad)
    tok = (np.arange(rows) % t_new)[None, :]
    kk = np.arange(MOBA_BLOCK)[:, None]
    bm_last = _bucket_matrix(MOBA_BLOCK + tok - kk, np.ones((MOBA_BLOCK, rows), bool))
    bm_own = _bucket_matrix(tok - kk, (kk <= tok) & (kk < t_new))
    rbx = jnp.repeat(rel_bias, t_new, axis=1)[None]
    far = rbx[:, N_BUCKETS - 1:N_BUCKETS, :]
    b_last = bias_table(bm_last, rbx, far, LOG2E)[0].T
    b_own = bias_table(bm_own, rbx, far, LOG2E)[0].T
    onehot_np = np.zeros((KV_W, past), np.float32)
    onehot_np[:nblk] = np.repeat(np.eye(nblk, dtype=np.float32), MOBA_BLOCK, axis=1)
    onehot = jnp.asarray(onehot_np, BF16)

    grid_spec = pltpu.PrefetchScalarGridSpec(
        num_scalar_prefetch=1,
        grid=(n,),
        in_specs=[pl.BlockSpec((None, rows, HEAD_DIM), lambda s, pt: (s, 0, 0)),
                  pl.BlockSpec((None, MOBA_BLOCK, KV_W), lambda s, pt: (s, 0, 0)),
                  pl.BlockSpec((None, MOBA_BLOCK, KV_W), lambda s, pt: (s, 0, 0)),
                  pl.BlockSpec((rows, MOBA_BLOCK), lambda s, pt: (0, 0)),
                  pl.BlockSpec((rows, MOBA_BLOCK), lambda s, pt: (0, 0)),
                  pl.BlockSpec((KV_W, past), lambda s, pt: (0, 0)),
                  pl.BlockSpec(memory_space=pl.ANY),
                  pl.BlockSpec(memory_space=pl.ANY)],
        out_specs=pl.BlockSpec((None, rows, HEAD_DIM), lambda s, pt: (s, 0, 0)),
        scratch_shapes=[pltpu.VMEM((2, n_pages, KV_W, page), F32),
                        pltpu.VMEM((2, n_pages, KV_W, page), F32),
                        pltpu.SemaphoreType.DMA((2,)),
                        pltpu.SemaphoreType.DMA((2,)),
                        pltpu.VMEM((rows, past), F32)],
    )
    o = pl.pallas_call(
        functools.partial(_moba_sample_kernel, n_pages=n_pages, page=page, n_samples=n, unroll=8),
        grid_spec=grid_spec,
        out_shape=jax.ShapeDtypeStruct((n, rows, HEAD_DIM), F32),
        compiler_params=_params("arbitrary"),
        name="moba_sample",
    )(page_table, q_r, kn, vn, b_last, b_own, onehot, k_pool, v_pool)
    return o.reshape(n, N_HEADS, t_new, HEAD_DIM).transpose(0, 2, 1, 3).reshape(n * t_new, N_HEADS * HEAD_DIM)


def _conv_tail(y, cb_ref, lg_ref, lb_ref, gate):
    y = y + cb_ref[...]
    mu = jnp.mean(y, axis=-1, keepdims=True)
    var = jnp.mean(jnp.square(y - mu), axis=-1, keepdims=True)
    yn = (y - mu) * lax.rsqrt(var + EPS) * lg_ref[...] + lb_ref[...]
    return _silu(yn) * gate


HALO = 32


SUBLANES = 8
LANES = 128
CONV_ROWS = 64


def _conv_prompt_kernel(u_ref, prev_ref, g_ref, w_ref, cb_ref, lg_ref, lb_ref, o_ref, xs, xr, ys, *, tl):
    t = pl.program_id(1)
    c = u_ref.shape[1]
    xs[0:HALO, :] = jnp.where(t > 0, prev_ref[...], 0.0)
    xs[HALO:, :] = u_ref[...]
    span = tl + HALO - SUBLANES
    for r in range(1, SUBLANES):
        xr[r - 1] = xs[pl.ds(r, span), :]
    off = HALO - (CONV_W - 1)

    def taps(base, lt):
        ln = slice(lt * LANES, (lt + 1) * LANES)
        acc = jnp.zeros((CONV_ROWS, LANES), F32)
        for k in range(CONV_W):
            a, r = divmod(off + k, SUBLANES)
            src = xs if r == 0 else xr.at[r - 1]
            start = base + a * SUBLANES
            acc = acc + src[start:start + CONV_ROWS, ln] * w_ref[k:k + 1, ln]
        ys[base:base + CONV_ROWS, ln] = acc

    for base in range(0, tl, CONV_ROWS):
        for lt in range(0, c // LANES, 2):
            @pl.when(t >= 0)
            def _():
                taps(base, lt)
                taps(base, lt + 1)
    o_ref[...] = _conv_tail(ys[...], cb_ref, lg_ref, lb_ref, g_ref[...].astype(F32)).astype(o_ref.dtype)


def conv_prompt(u, szb, conv_w, conv_b, ln_g, ln_b, batch, seq, tl=256):
    c = u.shape[1]
    nt = seq // tl
    vec = pl.BlockSpec((1, c), lambda b, t: (0, 0))
    return pl.pallas_call(
        functools.partial(_conv_prompt_kernel, tl=tl),
        grid=(batch, nt),
        in_specs=[pl.BlockSpec((tl, c), lambda b, t: (b * nt + t, 0)),
                  pl.BlockSpec((HALO, c), lambda b, t: (jnp.maximum((b * nt + t) * (tl // HALO) - 1, 0), 0)),
                  pl.BlockSpec((tl, c), lambda b, t: (b * nt + t, 0)),
                  pl.BlockSpec((CONV_W, c), lambda b, t: (0, 0)), vec, vec, vec],
        out_specs=pl.BlockSpec((tl, c), lambda b, t: (b * nt + t, 0)),
        out_shape=jax.ShapeDtypeStruct(u.shape, BF16),
        scratch_shapes=[pltpu.VMEM((HALO + tl, c), F32),
                        pltpu.VMEM((SUBLANES - 1, HALO + tl - SUBLANES, c), F32),
                        pltpu.VMEM((tl, c), F32)],
        compiler_params=_params("arbitrary", "arbitrary"),
        name="conv_prompt",
    )(u, u, szb, conv_w, conv_b.reshape(1, c), ln_g.reshape(1, c), ln_b.reshape(1, c))


def _conv_sample_kernel(xp_ref, g_ref, w_ref, cb_ref, lg_ref, lb_ref, o_ref, *, t_new):
    ns, _, c = xp_ref.shape
    y = jnp.zeros((ns, t_new, c), F32)
    for k in range(CONV_W):
        y = y + xp_ref[:, pl.ds(k, t_new), :] * w_ref[k:k + 1, :]
    o_ref[...] = _conv_tail(y, cb_ref, lg_ref, lb_ref, g_ref[...].astype(F32)).astype(o_ref.dtype)


def conv_sample(xp, szb, conv_w, conv_b, ln_g, ln_b, t_new, ns=8):
    n, rows, c = xp.shape
    vec = pl.BlockSpec((1, c), lambda i: (0, 0))
    return pl.pallas_call(
        functools.partial(_conv_sample_kernel, t_new=t_new),
        grid=(n // ns,),
        in_specs=[pl.BlockSpec((ns, rows, c), lambda i: (i, 0, 0)),
                  pl.BlockSpec((ns, t_new, c), lambda i: (i, 0, 0)),
                  pl.BlockSpec((CONV_W, c), lambda i: (0, 0)), vec, vec, vec],
        out_specs=pl.BlockSpec((ns, t_new, c), lambda i: (i, 0, 0)),
        out_shape=jax.ShapeDtypeStruct((n, t_new, c), BF16),
        compiler_params=_params("arbitrary"),
        name="conv_sample",
    )(xp, szb, conv_w, conv_b.reshape(1, c), ln_g.reshape(1, c), ln_b.reshape(1, c))


def _swa_prompt_kernel(q_ref, kc_ref, kp_ref, vc_ref, vp_ref, bias_ref, sink_ref, o_ref):
    n = pl.program_id(1)
    w = SWA_WINDOW
    rows = GROUP * w
    kcat = jnp.concatenate([kp_ref[...], kc_ref[...]], axis=0).astype(BF16)
    vcat_t = jnp.concatenate([vp_ref[...], vc_ref[...]], axis=0).T.astype(BF16)
    key_id = lax.broadcasted_iota(jnp.int32, (2 * w, rows), 0)
    outs = []
    for g in range(N_KV):
        q_t = (q_ref[:, g * GROUP * HEAD_DIM:(g + 1) * GROUP * HEAD_DIM] * SCALE).T
        qs = jnp.concatenate([q_t[h * HEAD_DIM:(h + 1) * HEAD_DIM, :] for h in range(GROUP)], axis=1)
        zeros = jnp.zeros_like(qs)
        qpad = jnp.concatenate([qs, zeros] if g == 0 else [zeros, qs], axis=0).astype(BF16)
        s_t = _dot(kcat, qpad) + bias_ref[g]
        s_t = jnp.where((key_id >= w) | (n > 0), s_t, NEG_INF)
        sink = sink_ref[g]
        m = jnp.maximum(jnp.max(s_t, axis=0, keepdims=True), sink)
        p = jnp.exp(s_t - m)
        den = jnp.sum(p, axis=0, keepdims=True) + jnp.exp(sink - m)
        o_t = _dot(vcat_t, p.astype(BF16))[g * HEAD_DIM:(g + 1) * HEAD_DIM, :] / den
        outs.append(jnp.concatenate([o_t[:, h * w:(h + 1) * w] for h in range(GROUP)], axis=0).T)
    o_ref[...] = jnp.concatenate(outs, axis=1)


def swa_prompt(q, k, v, sinks, rel_bias, batch, seq):
    w = SWA_WINDOW
    nb = seq // w
    rows = GROUP * w
    kidx = np.arange(2 * w)[:, None]
    qq = (np.arange(rows) % w)[None, :]
    dist = w + qq - kidx
    bm = _bucket_matrix(dist, (dist >= 0) & (dist < w))
    rbx = jnp.repeat(rel_bias.reshape(N_BUCKETS, N_KV, GROUP).transpose(1, 0, 2), w, axis=2)
    bias = bias_table(bm, rbx, jnp.zeros((N_KV, 1, rows), F32))
    sink_x = jnp.repeat(sinks.reshape(N_KV, 1, GROUP), w, axis=2)
    d = q.shape[1]
    cur = lambda b, n: (b * nb + n, 0)
    prev = lambda b, n: (jnp.maximum(b * nb + n - 1, 0), 0)
    return pl.pallas_call(
        _swa_prompt_kernel,
        grid=(batch, nb),
        in_specs=[pl.BlockSpec((w, d), cur),
                  pl.BlockSpec((w, KV_W), cur), pl.BlockSpec((w, KV_W), prev),
                  pl.BlockSpec((w, KV_W), cur), pl.BlockSpec((w, KV_W), prev),
                  pl.BlockSpec((N_KV, 2 * w, rows), lambda b, n: (0, 0, 0)),
                  pl.BlockSpec((N_KV, 1, rows), lambda b, n: (0, 0, 0))],
        out_specs=pl.BlockSpec((w, d), cur),
        out_shape=jax.ShapeDtypeStruct(q.shape, F32),
        compiler_params=_params("arbitrary", "arbitrary"),
        name="swa_prompt",
    )(q, k, k, v, v, bias, sink_x)


def _swa_sample_kernel(q_ref, k_ref, v_ref, bias_ref, sink_ref, o_ref):
    ns, rows, _ = q_ref.shape
    r_id = lax.broadcasted_iota(jnp.int32, (rows, HEAD_DIM), 0)
    lower = r_id < rows // N_KV
    for i in range(ns):
        q = q_ref[i] * SCALE
        zq = jnp.zeros_like(q)
        qbd = jnp.concatenate([jnp.where(lower, q, zq), jnp.where(lower, zq, q)], axis=1).astype(BF16)
        s = _dot_nt(qbd, k_ref[i].astype(BF16)) + bias_ref[...]
        sink = sink_ref[...]
        m = jnp.maximum(jnp.max(s, axis=1, keepdims=True), sink)
        p = jnp.exp(s - m)
        den = jnp.sum(p, axis=1, keepdims=True) + jnp.exp(sink - m)
        acc = _dot(p.astype(BF16), v_ref[i].astype(BF16))
        o_ref[i] = jnp.where(lower, acc[:, :HEAD_DIM], acc[:, HEAD_DIM:]) / den


def swa_sample(q, kcat, vcat, sinks, rel_bias, t_new, ns=4):
    n, nk, _ = kcat.shape
    wb = nk - t_new
    rows = N_HEADS * t_new
    keys = -(-nk // 128) * 128
    q_r = q.reshape(n, t_new, N_HEADS, HEAD_DIM).transpose(0, 2, 1, 3).reshape(n, rows, HEAD_DIM)
    pad = ((0, 0), (0, keys - nk), (0, 0))
    kp, vp = jnp.pad(kcat, pad), jnp.pad(vcat, pad)
    tok = (np.arange(rows) % t_new)[None, :]
    idx = np.arange(keys)[:, None]
    dist = tok + wb - idx
    bm = _bucket_matrix(dist, (dist >= 0) & (dist < SWA_WINDOW) & (idx < nk))
    rbx = jnp.repeat(rel_bias, t_new, axis=1)[None]
    bias = bias_table(bm, rbx, jnp.zeros((1, 1, rows), F32))[0].T
    sink_r = jnp.repeat(sinks, t_new).reshape(rows, 1)
    o = pl.pallas_call(
        _swa_sample_kernel,
        grid=(n // ns,),
        in_specs=[pl.BlockSpec((ns, rows, HEAD_DIM), lambda i: (i, 0, 0)),
                  pl.BlockSpec((ns, keys, KV_W), lambda i: (i, 0, 0)),
                  pl.BlockSpec((ns, keys, KV_W), lambda i: (i, 0, 0)),
                  pl.BlockSpec((rows, keys), lambda i: (0, 0)),
                  pl.BlockSpec((rows, 1), lambda i: (0, 0))],
        out_specs=pl.BlockSpec((ns, rows, HEAD_DIM), lambda i: (i, 0, 0)),
        out_shape=jax.ShapeDtypeStruct((n, rows, HEAD_DIM), F32),
        compiler_params=_params("arbitrary"),
        name="swa_sample",
    )(q_r, kp, vp, bias, sink_r)
    return o.reshape(n, N_HEADS, t_new, HEAD_DIM).transpose(0, 2, 1, 3).reshape(n * t_new, N_HEADS * HEAD_DIM)


def kernel(x_prompt, x_sample, c_prompt, c_sample, cache_a_k, cache_a_v, page_table, cache_b_conv, cache_c_k, cache_c_v, rel_bias, norm_a, mod_w_a, mod_b_a, w_in_a, conv_w_b, conv_b_b, ln_g_b, ln_b_b, w_out_a, norm_c, mod_w_c, mod_b_c, w_in_c, sinks_c, w_out_c, final_norm):
    batch, seq, d = x_prompt.shape
    n, t_new, _ = x_sample.shape
    n_pool, page = cache_a_k.shape[1], cache_a_k.shape[2]
    hq = N_HEADS * HEAD_DIM
    cb = conv_w_b.shape[2]
    assert norm_a.shape[0] == 1 and norm_c.shape[0] == 1, "one A/B layer followed by one C layer"

    tm_p = 512
    tm_s = min(256, n * t_new)
    xp = x_prompt.reshape(batch * seq, d)
    xs = x_sample

    c_all = jnp.concatenate([c_prompt, c_sample], axis=0)
    c_rows = -(-c_all.shape[0] // 8) * 8
    c_all = jnp.pad(c_all, ((0, c_rows - c_all.shape[0]), (0, 0)))

    def split_mod(m):
        mp = m[:batch].reshape(batch, 1, 3, d)
        ms = m[batch:batch + n].reshape(n, 1, 3, d)
        return [(mp[:, :, j], ms[:, :, j]) for j in range(3)]

    (sh_a, sc_a, gt_a) = split_mod(modulation(c_all, mod_w_a[0], mod_b_a[0]))
    (sh_c, sc_c, gt_c) = split_mod(modulation(c_all, mod_w_c[0], mod_b_c[0]))

    kv0, kv1 = hq, hq + 2 * KV_W
    za0 = kv1
    ga0 = za0 + hq
    gb0 = ga0 + cb
    zb0 = gb0 + cb
    segs_a = (("raw", 0, hq), ("raw", kv0, kv0 + KV_W), ("raw", kv0 + KV_W, kv1),
              ("silu", za0, ga0), ("glu", ga0, gb0, gb0, zb0), ("silu", zb0, zb0 + cb))
    dts_a = (F32, F32, F32, BF16, F32, BF16)
    w_in_a16 = w_in_a[0].astype(BF16)
    w_out_a16 = w_out_a[0].astype(BF16)
    k_pool = cache_a_k[0].transpose(0, 2, 3, 1).reshape(n_pool, KV_W, page)
    v_pool = cache_a_v[0].transpose(0, 2, 3, 1).reshape(n_pool, KV_W, page)

    q, k, v, sza, u, szb = ln_inproj(xp, sh_a[0], sc_a[0], norm_a[0], w_in_a16, segs_a, dts_a, tm_p)
    oa = moba_prompt(q, k.reshape(batch, seq, KV_W), v.reshape(batch, seq, KV_W), rel_bias, batch, seq)
    ob = conv_prompt(u, szb, conv_w_b[0], conv_b_b[0], ln_g_b[0], ln_b_b[0], batch, seq)
    xp1 = out_proj([(oa, sza, w_out_a16[:hq])], [(ob, w_out_a16[hq:])], xp, gt_a[0], None, tm_p)
    ak_p, av_p = k, v
    bc_p = u.reshape(batch, seq, cb)[:, seq - (CONV_W - 1):]

    q, k, v, sza, u, szb = ln_inproj(xs, sh_a[1], sc_a[1], norm_a[0], w_in_a16, segs_a, dts_a, tm_s)
    oa = moba_sample(q, k, v, k_pool, v_pool, page_table, rel_bias, t_new)
    xcat = jnp.concatenate([cache_b_conv[0], u.reshape(n, t_new, cb)], axis=1)
    ob = conv_sample(xcat, szb.reshape(n, t_new, cb), conv_w_b[0], conv_b_b[0], ln_g_b[0], ln_b_b[0], t_new)
    xs1 = out_proj([(oa, sza, w_out_a16[:hq])], [(ob.reshape(n * t_new, cb), w_out_a16[hq:])],
                   xs, gt_a[1], None, tm_s)
    ak_s, av_s = k, v
    bc_s = xcat[:, t_new:]

    segs_c = (("raw", 0, hq), ("raw", hq, hq + KV_W), ("raw", hq + KV_W, hq + 2 * KV_W),
              ("silu", hq + 2 * KV_W, 2 * hq + 2 * KV_W))
    dts_c = (F32, F32, F32, BF16)
    w_in_c16 = w_in_c[0].astype(BF16)
    w_out_c16 = w_out_c[0].astype(BF16)

    q, k, v, sz = ln_inproj(xp1, sh_c[0], sc_c[0], norm_c[0], w_in_c16, segs_c, dts_c, tm_p)
    o = swa_prompt(q, k, v, sinks_c[0], rel_bias, batch, seq)
    y_prompt = out_proj([(o, sz, w_out_c16)], [], xp1, gt_c[0], final_norm, tm_p)
    wb_p = min(SWA_WINDOW, seq)
    ck_p = k.reshape(batch, seq, KV_W)[:, seq - wb_p:]
    cv_p = v.reshape(batch, seq, KV_W)[:, seq - wb_p:]

    q, k, v, sz = ln_inproj(xs1, sh_c[1], sc_c[1], norm_c[0], w_in_c16, segs_c, dts_c, tm_s)
    wb_s = cache_c_k.shape[2]
    kcat = jnp.concatenate([cache_c_k[0].reshape(n, wb_s, KV_W), k.reshape(n, t_new, KV_W)], axis=1)
    vcat = jnp.concatenate([cache_c_v[0].reshape(n, wb_s, KV_W), v.reshape(n, t_new, KV_W)], axis=1)
    o = swa_sample(q, kcat, vcat, sinks_c[0], rel_bias, t_new)
    y_sample = out_proj([(o, sz, w_out_c16)], [], xs1, gt_c[1], final_norm, tm_s)
    ck_s, cv_s = kcat[:, t_new:], vcat[:, t_new:]

    def kv5(a, lead):
        return a.reshape((1,) + lead + (N_KV, HEAD_DIM))

    return (y_prompt.reshape(batch, seq, d), y_sample,
            kv5(ak_p, (batch, seq)), kv5(av_p, (batch, seq)),
            kv5(ak_s, (n, t_new)), kv5(av_s, (n, t_new)),
            bc_p[None], bc_s[None],
            kv5(ck_p, (batch, wb_p)), kv5(cv_p, (batch, wb_p)),
            kv5(ck_s, (n, wb_s)), kv5(cv_s, (n, wb_s)))
```

```python
import functools
import math

import jax
import jax.numpy as jnp
import numpy as np
from jax import lax
from jax.experimental import pallas as pl
from jax.experimental.pallas import tpu as pltpu

F32 = jnp.float32
BF16 = jnp.bfloat16
NEG_INF = float("-inf")
MASK_NEG = -1e30

HEAD_DIM = 64
N_HEADS = 16
N_KV = 2
GROUP = N_HEADS // N_KV
KV_W = N_KV * HEAD_DIM
MOBA_BLOCK = 256
MOBA_TOPK = 3
CONV_W = 31
SWA_WINDOW = 128
N_BUCKETS = 32
MAX_DISTANCE = 128
EPS = 1e-6
SCALE = HEAD_DIM ** -0.5
LOG2E = math.log2(math.e)

VMEM_LIMIT = 56 * 2**20


def _params(*sem):
    return pltpu.CompilerParams(dimension_semantics=sem, vmem_limit_bytes=VMEM_LIMIT)


def _silu(z):
    return z * jax.nn.sigmoid(z)


def _dot(a, b):
    return jnp.dot(a, b, preferred_element_type=F32)


def _dot_nt(a, b):
    return lax.dot_general(a, b, (((1,), (1,)), ((), ())), preferred_element_type=F32)


def _dot_f32(a, b):
    return jnp.dot(a, b, preferred_element_type=F32, precision=lax.Precision.HIGHEST)


def _t5_bucket_np(n):
    n = np.maximum(n, 0)
    max_exact = N_BUCKETS // 2
    nf = np.maximum(n, 1).astype(np.float32)
    large = max_exact + (np.log(nf / np.float32(max_exact)) / np.float32(math.log(MAX_DISTANCE / max_exact))
                         * np.float32(N_BUCKETS - max_exact)).astype(np.int32)
    large = np.minimum(large, N_BUCKETS - 1)
    return np.where(n < max_exact, n, large).astype(np.int32)


def _bucket_matrix(dist, valid):
    return np.where(valid, _t5_bucket_np(dist), -1).astype(np.int32)


def _bias_kernel(bm_ref, rb_ref, sub_ref, o_ref, *, mul):
    bm = bm_ref[...]
    acc = jnp.zeros(bm.shape, F32)
    for b in range(N_BUCKETS):
        acc = jnp.where(bm == b, rb_ref[b:b + 1, :], acc)
    o_ref[...] = jnp.where(bm < 0, NEG_INF, (acc - sub_ref[...]) * mul)


def bias_table(bm, rbx, sub, mul=1.0):
    g, _, c = rbx.shape
    r = bm.shape[0]
    return pl.pallas_call(
        functools.partial(_bias_kernel, mul=mul),
        grid=(g,),
        in_specs=[pl.BlockSpec((r, c), lambda i: (0, 0)),
                  pl.BlockSpec((None, N_BUCKETS, c), lambda i: (i, 0, 0)),
                  pl.BlockSpec((None, 1, c), lambda i: (i, 0, 0))],
        out_specs=pl.BlockSpec((None, r, c), lambda i: (i, 0, 0)),
        out_shape=jax.ShapeDtypeStruct((g, r, c), F32),
        compiler_params=_params("arbitrary"),
        name="bias_table",
    )(jnp.asarray(bm), rbx, sub)


def _mod_kernel(c_ref, w_ref, b_ref, o_ref):
    o_ref[...] = _dot_f32(_silu(c_ref[...]), w_ref[...]) + b_ref[...]


def modulation(c, w, b):
    n, d = c.shape
    m = w.shape[1]
    tn = 512
    return pl.pallas_call(
        _mod_kernel,
        grid=(m // tn,),
        in_specs=[pl.BlockSpec((n, d), lambda j: (0, 0)),
                  pl.BlockSpec((d, tn), lambda j: (0, j)),
                  pl.BlockSpec((1, tn), lambda j: (0, j))],
        out_specs=pl.BlockSpec((n, tn), lambda j: (0, j)),
        out_shape=jax.ShapeDtypeStruct((n, m), F32),
        compiler_params=_params("arbitrary"),
        name="modulation",
    )(c, w, b.reshape(1, m))


def _ln_inproj_kernel(x_ref, shift_ref, scale_ref, g_ref, w_ref, *out_refs, segs):
    x = x_ref[...]
    y = x * lax.rsqrt(jnp.mean(x * x, axis=-1, keepdims=True) + EPS)
    h = (y * g_ref[...]) * (1.0 + scale_ref[...]) + shift_ref[...]
    h16 = h.reshape(-1, h.shape[-1]).astype(BF16)
    for o_ref, seg in zip(out_refs, segs, strict=True):
        kind, lo, hi = seg[0], seg[1], seg[2]
        z = _dot(h16, w_ref[:, lo:hi])
        if kind == "silu":
            z = _silu(z)
        elif kind == "glu":
            z = z * jax.nn.sigmoid(_dot(h16, w_ref[:, seg[3]:seg[4]]))
        o_ref[...] = z.astype(o_ref.dtype)


def ln_inproj(x, shift, scale, norm_g, w16, segs, out_dtypes, tm):
    d = x.shape[-1]
    if x.ndim == 2:
        r = x.shape[0]
        nt = r // tm
        per_group = nt // shift.shape[0]
        x_spec = pl.BlockSpec((tm, d), lambda i: (i, 0))
        mod_spec = pl.BlockSpec((None, 1, d), lambda i: (i // per_group, 0, 0))
    else:
        n, t_new, _ = x.shape
        r = n * t_new
        nt = r // tm
        x_spec = pl.BlockSpec((tm // t_new, t_new, d), lambda i: (i, 0, 0))
        mod_spec = pl.BlockSpec((tm // t_new, 1, d), lambda i: (i, 0, 0))
    out_shape = [jax.ShapeDtypeStruct((r, s[2] - s[1]), dt) for s, dt in zip(segs, out_dtypes, strict=True)]
    out_specs = [pl.BlockSpec((tm, s[2] - s[1]), lambda i: (i, 0)) for s in segs]
    return pl.pallas_call(
        functools.partial(_ln_inproj_kernel, segs=segs),
        grid=(nt,),
        in_specs=[x_spec, mod_spec, mod_spec,
                  pl.BlockSpec((1, d), lambda i: (0, 0)),
                  pl.BlockSpec(w16.shape, lambda i: (0, 0))],
        out_specs=out_specs,
        out_shape=out_shape,
        compiler_params=_params("arbitrary"),
        name="ln_inproj",
    )(x, shift, scale, norm_g.reshape(1, d), w16)


def _out_proj_kernel(*refs, n_gated, n_plain, final_norm):
    it = iter(refs)
    y = None
    for _ in range(n_gated):
        a_ref, m_ref, w_ref = next(it), next(it), next(it)
        t = _dot((a_ref[...] * m_ref[...].astype(F32)).astype(BF16), w_ref[...])
        y = t if y is None else y + t
    for _ in range(n_plain):
        a_ref, w_ref = next(it), next(it)
        t = _dot(a_ref[...], w_ref[...])
        y = t if y is None else y + t
    x_ref, gate_ref = next(it), next(it)
    xn = x_ref[...] + gate_ref[...] * y.reshape(x_ref.shape)
    if final_norm:
        fg_ref = next(it)
        xn = xn * lax.rsqrt(jnp.mean(xn * xn, axis=-1, keepdims=True) + EPS) * fg_ref[...]
    o_ref = next(it)
    o_ref[...] = xn


def out_proj(gated, plain, x, gate, final_g, tm):
    d = x.shape[-1]
    if x.ndim == 2:
        nt = x.shape[0] // tm
        per_group = nt // gate.shape[0]
        x_spec = pl.BlockSpec((tm, d), lambda i: (i, 0))
        gate_spec = pl.BlockSpec((None, 1, d), lambda i: (i // per_group, 0, 0))
    else:
        n, t_new, _ = x.shape
        nt = n * t_new // tm
        x_spec = pl.BlockSpec((tm // t_new, t_new, d), lambda i: (i, 0, 0))
        gate_spec = pl.BlockSpec((tm // t_new, 1, d), lambda i: (i, 0, 0))
    args, specs = [], []
    for a, m, w in gated:
        args += [a, m, w]
        specs += [pl.BlockSpec((tm, a.shape[1]), lambda i: (i, 0)),
                  pl.BlockSpec((tm, m.shape[1]), lambda i: (i, 0)),
                  pl.BlockSpec(w.shape, lambda i: (0, 0))]
    for a, w in plain:
        args += [a, w]
        specs += [pl.BlockSpec((tm, a.shape[1]), lambda i: (i, 0)),
                  pl.BlockSpec(w.shape, lambda i: (0, 0))]
    args += [x, gate]
    specs += [x_spec, gate_spec]
    if final_g is not None:
        args.append(final_g.reshape(1, d))
        specs.append(pl.BlockSpec((1, d), lambda i: (0, 0)))
    return pl.pallas_call(
        functools.partial(_out_proj_kernel, n_gated=len(gated), n_plain=len(plain),
                          final_norm=final_g is not None),
        grid=(nt,),
        in_specs=specs,
        out_specs=x_spec,
        out_shape=jax.ShapeDtypeStruct(x.shape, F32),
        compiler_params=_params("arbitrary"),
        name="out_proj",
    )(*args)


def _top3_mask(scores, n_valid):
    nb = scores.shape[0]
    blk = lax.broadcasted_iota(jnp.int32, scores.shape, 0)
    s = jnp.where(blk < n_valid, scores, NEG_INF)
    picked = jnp.zeros(scores.shape, F32)
    for _ in range(MOBA_TOPK):
        mx = jnp.max(s, axis=0, keepdims=True)
        first = jnp.min(jnp.where(s == mx, blk, nb), axis=0, keepdims=True)
        hit = blk == first
        picked = jnp.where(hit, 1.0, picked)
        s = jnp.where(hit, NEG_INF, s)
    return jnp.where(blk < n_valid, picked, 0.0)


V_AUG = HEAD_DIM + 16


def _moba_prompt_kernel(q_ref, k_ref, v_ref, bown_ref, badj_ref, o_ref,
                        k16_s, vT_s, kbar_s, qT16_s, sel_s, m_s, acc_s,
                        s0_s, s1_s, p0_s, p1_s, a0_s, a1_s, *, nblk, chunk, sub):
    g = pl.program_id(1)
    i = pl.program_id(2)
    rows = GROUP * MOBA_BLOCK
    nch = rows // chunk

    @pl.when((g == 0) & (i == 0))
    def _per_batch():
        ones = jnp.ones((V_AUG - HEAD_DIM, MOBA_BLOCK), BF16)
        for j in range(nblk):
            kb = k_ref[j * MOBA_BLOCK:(j + 1) * MOBA_BLOCK, :]
            k16_s[j * MOBA_BLOCK:(j + 1) * MOBA_BLOCK, :] = kb.astype(BF16)
            kbar_s[j:j + 1, :] = jnp.mean(kb, axis=0, keepdims=True)
            v_t = v_ref[j * MOBA_BLOCK:(j + 1) * MOBA_BLOCK, :].T.astype(BF16)
            for kv in range(N_KV):
                vT_s[kv, j] = jnp.concatenate([v_t[kv * HEAD_DIM:(kv + 1) * HEAD_DIM], ones], axis=0)

    qT = (q_ref[...] * (SCALE * LOG2E)).T
    qs = jnp.concatenate([qT[h * HEAD_DIM:(h + 1) * HEAD_DIM, :] for h in range(GROUP)], axis=1)
    zeros = jnp.zeros_like(qs)
    qpad = jnp.where(g == 0, jnp.concatenate([qs, zeros], axis=0), jnp.concatenate([zeros, qs], axis=0))
    qpad16 = qpad.astype(BF16)
    for c in range(nch):
        qT16_s[c] = qpad16[:, c * chunk:(c + 1) * chunk]
    sel_s[...] = _top3_mask(_dot_f32(kbar_s[...], qpad), i)
    m_s[...] = jnp.full(m_s.shape, NEG_INF, F32)
    acc_s[...] = jnp.zeros(acc_s.shape, F32)

    n_far = i - 1
    sbuf, pbuf, abuf = (s0_s, s1_s), (p0_s, p1_s), (a0_s, a1_s)

    def blk_of(t):
        far_j = jnp.clip(t - 2, 0, jnp.maximum(n_far - 1, 0))
        return jnp.where(t == 0, i, jnp.where(t == 1, jnp.maximum(i - 1, 0), far_j))

    per = chunk // sub

    def stage_qk(t, slot, c):
        kb = k16_s[pl.ds(pl.multiple_of(blk_of(t) * MOBA_BLOCK, MOBA_BLOCK), MOBA_BLOCK), :]
        s_t = _dot(kb, qT16_s[c])
        for k in range(per):
            sbuf[slot][c * per + k] = s_t[:, k * sub:(k + 1) * sub]

    def stage_softmax(slot, c, on, bias_ref):
        ln = slice(c * sub, (c + 1) * sub)
        s_t = sbuf[slot][c]
        if bias_ref is not None:
            s_t = s_t + bias_ref[c]
        m_old = m_s[:, ln]
        m_new = jnp.maximum(m_old, jnp.where(on, jnp.max(s_t, axis=0, keepdims=True), NEG_INF))
        pbuf[slot][c] = jnp.exp2(s_t - jnp.where(on, m_new, jnp.inf)).astype(BF16)
        abuf[slot][:, ln] = jnp.exp2(m_old - m_new)
        m_s[:, ln] = m_new

    def stage_pv(t, slot, c):
        p = jnp.concatenate([pbuf[slot][c * per + k] for k in range(per)], axis=1)
        acc_s[c] = acc_s[c] * abuf[slot][:, c * chunk:(c + 1) * chunk] + _dot(vT_s[g, blk_of(t)], p)

    def pipe_half(t, slot, bias_ref=None, pv=True):
        own = bias_ref is bown_ref
        valid = None if own else jnp.where(t == 1, i >= 1, t - 2 < n_far)
        j = blk_of(t)
        for c in range(nch):
            stage_qk(t + 1, 1 - slot, c)
            if own:
                on = jnp.full((1, chunk), True)
            else:
                on = (sel_s[pl.ds(j, 1), c * chunk:(c + 1) * chunk] > 0.0) & valid
            for k in range(per):
                stage_softmax(slot, c * per + k, on[:, k * sub:(k + 1) * sub], bias_ref)
            if pv:
                stage_pv(t - 1, 1 - slot, c)

    for c in range(nch):
        stage_qk(0, 0, c)
    pipe_half(0, 0, bown_ref, pv=False)
    pipe_half(1, 1, badj_ref)

    def pair(u, carry):
        pipe_half(2 + 2 * u, 0)
        pipe_half(3 + 2 * u, 1)
        return carry

    n_pairs = (jnp.maximum(n_far, 0) + 1) // 2
    lax.fori_loop(0, n_pairs, pair, 0)
    for c in range(nch):
        stage_pv(1 + 2 * n_pairs, 1, c)

    acc = jnp.concatenate([acc_s[c] for c in range(nch)], axis=1)
    o_t = acc[:HEAD_DIM] / acc[HEAD_DIM:HEAD_DIM + 1]
    o_cat = jnp.concatenate([o_t[:, h * MOBA_BLOCK:(h + 1) * MOBA_BLOCK] for h in range(GROUP)], axis=0)
    o_ref[...] = o_cat.T


def moba_prompt(q, k, v, rel_bias, batch, seq):
    nblk = seq // MOBA_BLOCK
    rows = GROUP * MOBA_BLOCK
    kk = np.arange(MOBA_BLOCK)[:, None]
    qq = (np.arange(rows) % MOBA_BLOCK)[None, :]
    bm_own = _bucket_matrix(qq - kk, qq >= kk)
    bm_adj = _bucket_matrix(MOBA_BLOCK + qq - kk, np.ones((MOBA_BLOCK, rows), bool))
    rbx = jnp.repeat(rel_bias.reshape(N_BUCKETS, N_KV, GROUP).transpose(1, 0, 2), MOBA_BLOCK, axis=2)
    far = rbx[:, N_BUCKETS - 1:N_BUCKETS, :]
    chunk, sub = 256, 128
    nsub = rows // sub

    def chunk_major(tbl):
        return tbl.reshape(N_KV, MOBA_BLOCK, nsub, sub).transpose(0, 2, 1, 3)

    b_own = chunk_major(bias_table(bm_own, rbx, far, LOG2E))
    b_adj = chunk_major(bias_table(bm_adj, rbx, far, LOG2E))
    qw = GROUP * HEAD_DIM
    tbl_spec = pl.BlockSpec((None, nsub, MOBA_BLOCK, sub), lambda b, g, i: (g, 0, 0, 0))
    return pl.pallas_call(
        functools.partial(_moba_prompt_kernel, nblk=nblk, chunk=chunk, sub=sub),
        grid=(batch, N_KV, nblk),
        in_specs=[pl.BlockSpec((MOBA_BLOCK, qw), lambda b, g, i: (b * nblk + i, g)),
                  pl.BlockSpec((None, seq, KV_W), lambda b, g, i: (b, 0, 0)),
                  pl.BlockSpec((None, seq, KV_W), lambda b, g, i: (b, 0, 0)),
                  tbl_spec, tbl_spec],
        out_specs=pl.BlockSpec((MOBA_BLOCK, qw), lambda b, g, i: (b * nblk + i, g)),
        out_shape=jax.ShapeDtypeStruct(q.shape, F32),
        scratch_shapes=[pltpu.VMEM((seq, KV_W), BF16),
                        pltpu.VMEM((N_KV, nblk, V_AUG, MOBA_BLOCK), BF16),
                        pltpu.VMEM((nblk, KV_W), F32),
                        pltpu.VMEM((rows // chunk, KV_W, chunk), BF16),
                        pltpu.VMEM((nblk, rows), F32),
                        pltpu.VMEM((1, rows), F32),
                        pltpu.VMEM((rows // chunk, V_AUG, chunk), F32),
                        pltpu.VMEM((nsub, MOBA_BLOCK, sub), F32), pltpu.VMEM((nsub, MOBA_BLOCK, sub), F32),
                        pltpu.VMEM((nsub, MOBA_BLOCK, sub), BF16), pltpu.VMEM((nsub, MOBA_BLOCK, sub), BF16),
                        pltpu.VMEM((1, rows), F32), pltpu.VMEM((1, rows), F32)],
        compiler_params=_params("arbitrary", "arbitrary", "arbitrary"),
        name="moba_prompt",
    )(q, k, v, b_own, b_adj)


def _moba_sample_kernel(pt_ref, q_ref, kn_ref, vn_ref, blast_ref, bown_ref, e_ref, kpool, vpool, o_ref,
                        kbuf, vbuf, ksem, vsem, s_s, *, n_pages, page, n_samples, unroll):
    s = pl.program_id(0)
    slot = s % 2
    ppb = MOBA_BLOCK // page
    nblk = n_pages // ppb
    rows = q_ref.shape[0]

    def k_copy(smp, sl, p):
        return pltpu.make_async_copy(kpool.at[pt_ref[smp, p]], kbuf.at[sl, p], ksem.at[sl])

    def v_copy(smp, sl, p):
        return pltpu.make_async_copy(vpool.at[pt_ref[smp, p]], vbuf.at[sl, p], vsem.at[sl])

    def start_all(smp, sl):
        def body(p, c):
            k_copy(smp, sl, p).start()
            v_copy(smp, sl, p).start()
            return c
        lax.fori_loop(0, n_pages, body, 0, unroll=8)

    @pl.when(s == 0)
    def _first():
        start_all(0, 0)

    @pl.when(s + 1 < n_samples)
    def _prefetch():
        start_all(s + 1, 1 - slot)

    def wait_k(p, c):
        k_copy(s, slot, p).wait()
        return c

    lax.fori_loop(0, n_pages, wait_k, 0, unroll=8)

    def block_t(buf, j):
        return jnp.concatenate([buf[slot, j * ppb + t] for t in range(ppb)], axis=1)

    blk_lane = lax.broadcasted_iota(jnp.int32, (KV_W, nblk), 1)

    def kbar_body(j, kbar_t):
        pages = kbuf[slot, j * ppb]
        for t in range(1, ppb):
            pages = pages + kbuf[slot, j * ppb + t]
        col = jnp.sum(pages, axis=1, keepdims=True) * (1.0 / MOBA_BLOCK)
        return jnp.where(blk_lane == j, col, kbar_t)

    kbar_t = lax.fori_loop(0, nblk, kbar_body, jnp.zeros((KV_W, nblk), F32), unroll=2 * unroll)

    q = q_ref[...] * (SCALE * LOG2E)
    lower = lax.broadcasted_iota(jnp.int32, (rows, HEAD_DIM), 0) < rows // N_KV
    zq = jnp.zeros_like(q)
    qbd = jnp.concatenate([jnp.where(lower, q, zq), jnp.where(lower, zq, q)], axis=1)
    qbd16 = qbd.astype(BF16)

    scores = _dot_f32(qbd, kbar_t)
    sel_t = _top3_mask(scores.T, nblk)
    negm = jnp.where(sel_t.T > 0.0, 0.0, MASK_NEG)
    lhs16 = jnp.concatenate([qbd, negm, jnp.zeros((rows, e_ref.shape[1] - nblk), F32)], axis=1).astype(BF16)

    def logits(j):
        rhs16 = jnp.concatenate([block_t(kbuf, j).astype(BF16), e_ref[j]], axis=0)
        return _dot(lhs16, rhs16)

    def fold(x):
        return x[:, :LANES], x[:, LANES:]

    def far(j, mrun):
        st = logits(j)
        s_s[j] = st
        lo, hi = fold(st)
        return jnp.maximum(mrun, jnp.maximum(lo, hi))

    mrun = lax.fori_loop(0, nblk - 1, far, jnp.full((rows, LANES), NEG_INF, F32), unroll=unroll - 1)
    s_last = logits(nblk - 1) + blast_ref[...]
    s_s[nblk - 1] = s_last
    kn16 = kn_ref[...].astype(BF16)
    s_own = _dot_nt(qbd16, kn16) + bown_ref[...]
    for part in fold(s_last) + fold(s_own):
        mrun = jnp.maximum(mrun, part)
    m = jnp.max(mrun, axis=1, keepdims=True)

    def wait_v(p, c):
        v_copy(s, slot, p).wait()
        return c

    lax.fori_loop(0, n_pages, wait_v, 0, unroll=8)

    def pv(j, carry):
        acc, lsum = carry
        p = jnp.exp2(s_s[j] - m)
        lo, hi = fold(p)
        return acc + _dot_nt(p.astype(BF16), block_t(vbuf, j).astype(BF16)), lsum + (lo + hi)

    p_own = jnp.exp2(s_own - m)
    acc0 = _dot(p_own.astype(BF16), vn_ref[...].astype(BF16))
    lo, hi = fold(p_own)
    acc, lsum = lax.fori_loop(0, nblk, pv, (acc0, lo + hi), unroll=unroll)
    den = jnp.sum(lsum, axis=1, keepdims=True)
    o_ref[...] = jnp.where(lower, acc[:, :HEAD_DIM], acc[:, HEAD_DIM:]) / den


def moba_sample(q, k_new, v_new, k_pool, v_pool, page_table, rel_bias, t_new):
    n, n_pages = page_table.shape
    page = k_pool.shape[2]
    past = n_pages * page
    nblk = past // MOBA_BLOCK
    rows = N_HEADS * t_new
    q_r = q.reshape(n, t_new, N_HEADS, HEAD_DIM).transpose(0, 2, 1, 3).reshape(n, rows, HEAD_DIM)
    pad = ((0, 0), (0, MOBA_BLOCK - t_new), (0, 0))
    kn = jnp.pad(k_new.reshape(n, t_new, KV_W), pad)
    vn = jnp.pad(v_new.reshape(n, t_new, KV_W), pad)
    tok = (np.arange(rows) % t_new)[None, :]
    kk = np.arange(MOBA_BLOCK)[:, None]
    bm_last = _bucket_matrix(MOBA_BLOCK + tok - kk, np.ones((MOBA_BLOCK, rows), bool))
    bm_own = _bucket_matrix(tok - kk, (kk <= tok) & (kk < t_new))
    rbx = jnp.repeat(rel_bias, t_new, axis=1)[None]
    far = rbx[:, N_BUCKETS - 1:N_BUCKETS, :]
    b_last = bias_table(bm_last, rbx, far, LOG2E)[0].T
    b_own = bias_table(bm_own, rbx, far, LOG2E)[0].T
    onehot_np = np.zeros((nblk, KV_W, MOBA_BLOCK), np.float32)
    onehot_np[np.arange(nblk), np.arange(nblk)] = 1.0
    onehot = jnp.asarray(onehot_np, BF16)

    grid_spec = pltpu.PrefetchScalarGridSpec(
        num_scalar_prefetch=1,
        grid=(n,),
        in_specs=[pl.BlockSpec((None, rows, HEAD_DIM), lambda s, pt: (s, 0, 0)),
                  pl.BlockSpec((None, MOBA_BLOCK, KV_W), lambda s, pt: (s, 0, 0)),
                  pl.BlockSpec((None, MOBA_BLOCK, KV_W), lambda s, pt: (s, 0, 0)),
                  pl.BlockSpec((rows, MOBA_BLOCK), lambda s, pt: (0, 0)),
                  pl.BlockSpec((rows, MOBA_BLOCK), lambda s, pt: (0, 0)),
                  pl.BlockSpec((nblk, KV_W, MOBA_BLOCK), lambda s, pt: (0, 0, 0)),
                  pl.BlockSpec(memory_space=pl.ANY),
                  pl.BlockSpec(memory_space=pl.ANY)],
        out_specs=pl.BlockSpec((None, rows, HEAD_DIM), lambda s, pt: (s, 0, 0)),
        scratch_shapes=[pltpu.VMEM((2, n_pages, KV_W, page), F32),
                        pltpu.VMEM((2, n_pages, KV_W, page), F32),
                        pltpu.SemaphoreType.DMA((2,)),
                        pltpu.SemaphoreType.DMA((2,)),
                        pltpu.VMEM((nblk, rows, MOBA_BLOCK), F32)],
    )
    o = pl.pallas_call(
        functools.partial(_moba_sample_kernel, n_pages=n_pages, page=page, n_samples=n, unroll=8),
        grid_spec=grid_spec,
        out_shape=jax.ShapeDtypeStruct((n, rows, HEAD_DIM), F32),
        compiler_params=_params("arbitrary"),
        name="moba_sample",
    )(page_table, q_r, kn, vn, b_last, b_own, onehot, k_pool, v_pool)
    return o.reshape(n, N_HEADS, t_new, HEAD_DIM).transpose(0, 2, 1, 3).reshape(n * t_new, N_HEADS * HEAD_DIM)


def _conv_tail(y, cb_ref, lg_ref, lb_ref, gate):
    y = y + cb_ref[...]
    mu = jnp.mean(y, axis=-1, keepdims=True)
    var = jnp.mean(jnp.square(y - mu), axis=-1, keepdims=True)
    yn = (y - mu) * lax.rsqrt(var + EPS) * lg_ref[...] + lb_ref[...]
    return _silu(yn) * gate


HALO = 32


SUBLANES = 8
LANES = 128
CONV_ROWS = 64


def _conv_prompt_kernel(u_ref, prev_ref, g_ref, w_ref, cb_ref, lg_ref, lb_ref, o_ref, xs, xr, ys, *, tl):
    t = pl.program_id(1)
    c = u_ref.shape[1]
    xs[0:HALO, :] = jnp.where(t > 0, prev_ref[...], 0.0)
    xs[HALO:, :] = u_ref[...]
    span = tl + HALO - SUBLANES
    for r in range(1, SUBLANES):
        xr[r - 1] = xs[pl.ds(r, span), :]
    off = HALO - (CONV_W - 1)

    def taps(base, lt):
        ln = slice(lt * LANES, (lt + 1) * LANES)
        acc = jnp.zeros((CONV_ROWS, LANES), F32)
        for k in range(CONV_W):
            a, r = divmod(off + k, SUBLANES)
            src = xs if r == 0 else xr.at[r - 1]
            start = base + a * SUBLANES
            acc = acc + src[start:start + CONV_ROWS, ln] * w_ref[k:k + 1, ln]
        ys[base:base + CONV_ROWS, ln] = acc

    for base in range(0, tl, CONV_ROWS):
        for lt in range(0, c // LANES, 2):
            @pl.when(t >= 0)
            def _():
                taps(base, lt)
                taps(base, lt + 1)
    o_ref[...] = _conv_tail(ys[...], cb_ref, lg_ref, lb_ref, g_ref[...].astype(F32)).astype(o_ref.dtype)


def conv_prompt(u, szb, conv_w, conv_b, ln_g, ln_b, batch, seq, tl=256):
    c = u.shape[1]
    nt = seq // tl
    vec = pl.BlockSpec((1, c), lambda b, t: (0, 0))
    return pl.pallas_call(
        functools.partial(_conv_prompt_kernel, tl=tl),
        grid=(batch, nt),
        in_specs=[pl.BlockSpec((tl, c), lambda b, t: (b * nt + t, 0)),
                  pl.BlockSpec((HALO, c), lambda b, t: (jnp.maximum((b * nt + t) * (tl // HALO) - 1, 0), 0)),
                  pl.BlockSpec((tl, c), lambda b, t: (b * nt + t, 0)),
                  pl.BlockSpec((CONV_W, c), lambda b, t: (0, 0)), vec, vec, vec],
        out_specs=pl.BlockSpec((tl, c), lambda b, t: (b * nt + t, 0)),
        out_shape=jax.ShapeDtypeStruct(u.shape, BF16),
        scratch_shapes=[pltpu.VMEM((HALO + tl, c), F32),
                        pltpu.VMEM((SUBLANES - 1, HALO + tl - SUBLANES, c), F32),
                        pltpu.VMEM((tl, c), F32)],
        compiler_params=_params("arbitrary", "arbitrary"),
        name="conv_prompt",
    )(u, u, szb, conv_w, conv_b.reshape(1, c), ln_g.reshape(1, c), ln_b.reshape(1, c))


def _conv_sample_kernel(xp_ref, g_ref, w_ref, cb_ref, lg_ref, lb_ref, o_ref, *, t_new):
    ns, _, c = xp_ref.shape
    y = jnp.zeros((ns, t_new, c), F32)
    for k in range(CONV_W):
        y = y + xp_ref[:, pl.ds(k, t_new), :] * w_ref[k:k + 1, :]
    o_ref[...] = _conv_tail(y, cb_ref, lg_ref, lb_ref, g_ref[...].astype(F32)).astype(o_ref.dtype)


def conv_sample(xp, szb, conv_w, conv_b, ln_g, ln_b, t_new, ns=8):
    n, rows, c = xp.shape
    vec = pl.BlockSpec((1, c), lambda i: (0, 0))
    return pl.pallas_call(
        functools.partial(_conv_sample_kernel, t_new=t_new),
        grid=(n // ns,),
        in_specs=[pl.BlockSpec((ns, rows, c), lambda i: (i, 0, 0)),
                  pl.BlockSpec((ns, t_new, c), lambda i: (i, 0, 0)),
                  pl.BlockSpec((CONV_W, c), lambda i: (0, 0)), vec, vec, vec],
        out_specs=pl.BlockSpec((ns, t_new, c), lambda i: (i, 0, 0)),
        out_shape=jax.ShapeDtypeStruct((n, t_new, c), BF16),
        compiler_params=_params("arbitrary"),
        name="conv_sample",
    )(xp, szb, conv_w, conv_b.reshape(1, c), ln_g.reshape(1, c), ln_b.reshape(1, c))


def _swa_prompt_kernel(q_ref, kc_ref, kp_ref, vc_ref, vp_ref, bias_ref, sink_ref, o_ref):
    n = pl.program_id(1)
    w = SWA_WINDOW
    rows = GROUP * w
    kcat = jnp.concatenate([kp_ref[...], kc_ref[...]], axis=0).astype(BF16)
    vcat_t = jnp.concatenate([vp_ref[...], vc_ref[...]], axis=0).T.astype(BF16)
    key_id = lax.broadcasted_iota(jnp.int32, (2 * w, rows), 0)
    outs = []
    for g in range(N_KV):
        q_t = (q_ref[:, g * GROUP * HEAD_DIM:(g + 1) * GROUP * HEAD_DIM] * SCALE).T
        qs = jnp.concatenate([q_t[h * HEAD_DIM:(h + 1) * HEAD_DIM, :] for h in range(GROUP)], axis=1)
        zeros = jnp.zeros_like(qs)
        qpad = jnp.concatenate([qs, zeros] if g == 0 else [zeros, qs], axis=0).astype(BF16)
        s_t = _dot(kcat, qpad) + bias_ref[g]
        s_t = jnp.where((key_id >= w) | (n > 0), s_t, NEG_INF)
        sink = sink_ref[g]
        m = jnp.maximum(jnp.max(s_t, axis=0, keepdims=True), sink)
        p = jnp.exp(s_t - m)
        den = jnp.sum(p, axis=0, keepdims=True) + jnp.exp(sink - m)
        o_t = _dot(vcat_t, p.astype(BF16))[g * HEAD_DIM:(g + 1) * HEAD_DIM, :] / den
        outs.append(jnp.concatenate([o_t[:, h * w:(h + 1) * w] for h in range(GROUP)], axis=0).T)
    o_ref[...] = jnp.concatenate(outs, axis=1)


def swa_prompt(q, k, v, sinks, rel_bias, batch, seq):
    w = SWA_WINDOW
    nb = seq // w
    rows = GROUP * w
    kidx = np.arange(2 * w)[:, None]
    qq = (np.arange(rows) % w)[None, :]
    dist = w + qq - kidx
    bm = _bucket_matrix(dist, (dist >= 0) & (dist < w))
    rbx = jnp.repeat(rel_bias.reshape(N_BUCKETS, N_KV, GROUP).transpose(1, 0, 2), w, axis=2)
    bias = bias_table(bm, rbx, jnp.zeros((N_KV, 1, rows), F32))
    sink_x = jnp.repeat(sinks.reshape(N_KV, 1, GROUP), w, axis=2)
    d = q.shape[1]
    cur = lambda b, n: (b * nb + n, 0)
    prev = lambda b, n: (jnp.maximum(b * nb + n - 1, 0), 0)
    return pl.pallas_call(
        _swa_prompt_kernel,
        grid=(batch, nb),
        in_specs=[pl.BlockSpec((w, d), cur),
                  pl.BlockSpec((w, KV_W), cur), pl.BlockSpec((w, KV_W), prev),
                  pl.BlockSpec((w, KV_W), cur), pl.BlockSpec((w, KV_W), prev),
                  pl.BlockSpec((N_KV, 2 * w, rows), lambda b, n: (0, 0, 0)),
                  pl.BlockSpec((N_KV, 1, rows), lambda b, n: (0, 0, 0))],
        out_specs=pl.BlockSpec((w, d), cur),
        out_shape=jax.ShapeDtypeStruct(q.shape, F32),
        compiler_params=_params("arbitrary", "arbitrary"),
        name="swa_prompt",
    )(q, k, k, v, v, bias, sink_x)


def _swa_sample_kernel(q_ref, k_ref, v_ref, bias_ref, sink_ref, o_ref):
    ns, rows, _ = q_ref.shape
    r_id = lax.broadcasted_iota(jnp.int32, (rows, HEAD_DIM), 0)
    lower = r_id < rows // N_KV
    for i in range(ns):
        q = q_ref[i] * SCALE
        zq = jnp.zeros_like(q)
        qbd = jnp.concatenate([jnp.where(lower, q, zq), jnp.where(lower, zq, q)], axis=1).astype(BF16)
        s = _dot_nt(qbd, k_ref[i].astype(BF16)) + bias_ref[...]
        sink = sink_ref[...]
        m = jnp.maximum(jnp.max(s, axis=1, keepdims=True), sink)
        p = jnp.exp(s - m)
        den = jnp.sum(p, axis=1, keepdims=True) + jnp.exp(sink - m)
        acc = _dot(p.astype(BF16), v_ref[i].astype(BF16))
        o_ref[i] = jnp.where(lower, acc[:, :HEAD_DIM], acc[:, HEAD_DIM:]) / den


def swa_sample(q, kcat, vcat, sinks, rel_bias, t_new, ns=4):
    n, nk, _ = kcat.shape
    wb = nk - t_new
    rows = N_HEADS * t_new
    keys = -(-nk // 128) * 128
    q_r = q.reshape(n, t_new, N_HEADS, HEAD_DIM).transpose(0, 2, 1, 3).reshape(n, rows, HEAD_DIM)
    pad = ((0, 0), (0, keys - nk), (0, 0))
    kp, vp = jnp.pad(kcat, pad), jnp.pad(vcat, pad)
    tok = (np.arange(rows) % t_new)[None, :]
    idx = np.arange(keys)[:, None]
    dist = tok + wb - idx
    bm = _bucket_matrix(dist, (dist >= 0) & (dist < SWA_WINDOW) & (idx < nk))
    rbx = jnp.repeat(rel_bias, t_new, axis=1)[None]
    bias = bias_table(bm, rbx, jnp.zeros((1, 1, rows), F32))[0].T
    sink_r = jnp.repeat(sinks, t_new).reshape(rows, 1)
    o = pl.pallas_call(
        _swa_sample_kernel,
        grid=(n // ns,),
        in_specs=[pl.BlockSpec((ns, rows, HEAD_DIM), lambda i: (i, 0, 0)),
                  pl.BlockSpec((ns, keys, KV_W), lambda i: (i, 0, 0)),
                  pl.BlockSpec((ns, keys, KV_W), lambda i: (i, 0, 0)),
                  pl.BlockSpec((rows, keys), lambda i: (0, 0)),
                  pl.BlockSpec((rows, 1), lambda i: (0, 0))],
        out_specs=pl.BlockSpec((ns, rows, HEAD_DIM), lambda i: (i, 0, 0)),
        out_shape=jax.ShapeDtypeStruct((n, rows, HEAD_DIM), F32),
        compiler_params=_params("arbitrary"),
        name="swa_sample",
    )(q_r, kp, vp, bias, sink_r)
    return o.reshape(n, N_HEADS, t_new, HEAD_DIM).transpose(0, 2, 1, 3).reshape(n * t_new, N_HEADS * HEAD_DIM)


def kernel(x_prompt, x_sample, c_prompt, c_sample, cache_a_k, cache_a_v, page_table, cache_b_conv, cache_c_k, cache_c_v, rel_bias, norm_a, mod_w_a, mod_b_a, w_in_a, conv_w_b, conv_b_b, ln_g_b, ln_b_b, w_out_a, norm_c, mod_w_c, mod_b_c, w_in_c, sinks_c, w_out_c, final_norm):
    batch, seq, d = x_prompt.shape
    n, t_new, _ = x_sample.shape
    n_pool, page = cache_a_k.shape[1], cache_a_k.shape[2]
    hq = N_HEADS * HEAD_DIM
    cb = conv_w_b.shape[2]
    assert norm_a.shape[0] == 1 and norm_c.shape[0] == 1, "one A/B layer followed by one C layer"

    tm_p = 512
    tm_s = min(256, n * t_new)
    xp = x_prompt.reshape(batch * seq, d)
    xs = x_sample

    c_all = jnp.concatenate([c_prompt, c_sample], axis=0)
    c_rows = -(-c_all.shape[0] // 8) * 8
    c_all = jnp.pad(c_all, ((0, c_rows - c_all.shape[0]), (0, 0)))

    def split_mod(m):
        mp = m[:batch].reshape(batch, 1, 3, d)
        ms = m[batch:batch + n].reshape(n, 1, 3, d)
        return [(mp[:, :, j], ms[:, :, j]) for j in range(3)]

    (sh_a, sc_a, gt_a) = split_mod(modulation(c_all, mod_w_a[0], mod_b_a[0]))
    (sh_c, sc_c, gt_c) = split_mod(modulation(c_all, mod_w_c[0], mod_b_c[0]))

    kv0, kv1 = hq, hq + 2 * KV_W
    za0 = kv1
    ga0 = za0 + hq
    gb0 = ga0 + cb
    zb0 = gb0 + cb
    segs_a = (("raw", 0, hq), ("raw", kv0, kv0 + KV_W), ("raw", kv0 + KV_W, kv1),
              ("silu", za0, ga0), ("glu", ga0, gb0, gb0, zb0), ("silu", zb0, zb0 + cb))
    dts_a = (F32, F32, F32, BF16, F32, BF16)
    w_in_a16 = w_in_a[0].astype(BF16)
    w_out_a16 = w_out_a[0].astype(BF16)
    k_pool = cache_a_k[0].transpose(0, 2, 3, 1).reshape(n_pool, KV_W, page)
    v_pool = cache_a_v[0].transpose(0, 2, 3, 1).reshape(n_pool, KV_W, page)

    q, k, v, sza, u, szb = ln_inproj(xp, sh_a[0], sc_a[0], norm_a[0], w_in_a16, segs_a, dts_a, tm_p)
    oa = moba_prompt(q, k.reshape(batch, seq, KV_W), v.reshape(batch, seq, KV_W), rel_bias, batch, seq)
    ob = conv_prompt(u, szb, conv_w_b[0], conv_b_b[0], ln_g_b[0], ln_b_b[0], batch, seq)
    xp1 = out_proj([(oa, sza, w_out_a16[:hq])], [(ob, w_out_a16[hq:])], xp, gt_a[0], None, tm_p)
    ak_p, av_p = k, v
    bc_p = u.reshape(batch, seq, cb)[:, seq - (CONV_W - 1):]

    q, k, v, sza, u, szb = ln_inproj(xs, sh_a[1], sc_a[1], norm_a[0], w_in_a16, segs_a, dts_a, tm_s)
    oa = moba_sample(q, k, v, k_pool, v_pool, page_table, rel_bias, t_new)
    xcat = jnp.concatenate([cache_b_conv[0], u.reshape(n, t_new, cb)], axis=1)
    ob = conv_sample(xcat, szb.reshape(n, t_new, cb), conv_w_b[0], conv_b_b[0], ln_g_b[0], ln_b_b[0], t_new)
    xs1 = out_proj([(oa, sza, w_out_a16[:hq])], [(ob.reshape(n * t_new, cb), w_out_a16[hq:])],
                   xs, gt_a[1], None, tm_s)
    ak_s, av_s = k, v
    bc_s = xcat[:, t_new:]

    segs_c = (("raw", 0, hq), ("raw", hq, hq + KV_W), ("raw", hq + KV_W, hq + 2 * KV_W),
              ("silu", hq + 2 * KV_W, 2 * hq + 2 * KV_W))
    dts_c = (F32, F32, F32, BF16)
    w_in_c16 = w_in_c[0].astype(BF16)
    w_out_c16 = w_out_c[0].astype(BF16)

    q, k, v, sz = ln_inproj(xp1, sh_c[0], sc_c[0], norm_c[0], w_in_c16, segs_c, dts_c, tm_p)
    o = swa_prompt(q, k, v, sinks_c[0], rel_bias, batch, seq)
    y_prompt = out_proj([(o, sz, w_out_c16)], [], xp1, gt_c[0], final_norm, tm_p)
    wb_p = min(SWA_WINDOW, seq)
    ck_p = k.reshape(batch, seq, KV_W)[:, seq - wb_p:]
    cv_p = v.reshape(batch, seq, KV_W)[:, seq - wb_p:]

    q, k, v, sz = ln_inproj(xs1, sh_c[1], sc_c[1], norm_c[0], w_in_c16, segs_c, dts_c, tm_s)
    wb_s = cache_c_k.shape[2]
    kcat = jnp.concatenate([cache_c_k[0].reshape(n, wb_s, KV_W), k.reshape(n, t_new, KV_W)], axis=1)
    vcat = jnp.concatenate([cache_c_v[0].reshape(n, wb_s, KV_W), v.reshape(n, t_new, KV_W)], axis=1)
    o = swa_sample(q, kcat, vcat, sinks_c[0], rel_bias, t_new)
    y_sample = out_proj([(o, sz, w_out_c16)], [], xs1, gt_c[1], final_norm, tm_s)
    ck_s, cv_s = kcat[:, t_new:], vcat[:, t_new:]

    def kv5(a, lead):
        return a.reshape((1,) + lead + (N_KV, HEAD_DIM))

    return (y_prompt.reshape(batch, seq, d), y_sample,
            kv5(ak_p, (batch, seq)), kv5(av_p, (batch, seq)),
            kv5(ak_s, (n, t_new)), kv5(av_s, (n, t_new)),
            bc_p[None], bc_s[None],
            kv5(ck_p, (batch, wb_p)), kv5(cv_p, (batch, wb_p)),
            kv5(ck_s, (n, wb_s)), kv5(cv_s, (n, wb_s)))
```

```python
import functools
import math

import jax
import jax.numpy as jnp
import numpy as np
from jax import lax
from jax.experimental import pallas as pl
from jax.experimental.pallas import tpu as pltpu

F32 = jnp.float32
BF16 = jnp.bfloat16
NEG_INF = float("-inf")
MASK_NEG = -1e30

HEAD_DIM = 64
N_HEADS = 16
N_KV = 2
GROUP = N_HEADS // N_KV
KV_W = N_KV * HEAD_DIM
MOBA_BLOCK = 256
MOBA_TOPK = 3
CONV_W = 31
SWA_WINDOW = 128
N_BUCKETS = 32
MAX_DISTANCE = 128
EPS = 1e-6
SCALE = HEAD_DIM ** -0.5
LOG2E = math.log2(math.e)

VMEM_LIMIT = 56 * 2**20


def _params(*sem):
    return pltpu.CompilerParams(dimension_semantics=sem, vmem_limit_bytes=VMEM_LIMIT)


def _silu(z):
    return z * jax.nn.sigmoid(z)


def _dot(a, b):
    return jnp.dot(a, b, preferred_element_type=F32)


def _dot_nt(a, b):
    return lax.dot_general(a, b, (((1,), (1,)), ((), ())), preferred_element_type=F32)


def _dot_f32(a, b):
    return jnp.dot(a, b, preferred_element_type=F32, precision=lax.Precision.HIGHEST)


def _t5_bucket_np(n):
    n = np.maximum(n, 0)
    max_exact = N_BUCKETS // 2
    nf = np.maximum(n, 1).astype(np.float32)
    large = max_exact + (np.log(nf / np.float32(max_exact)) / np.float32(math.log(MAX_DISTANCE / max_exact))
                         * np.float32(N_BUCKETS - max_exact)).astype(np.int32)
    large = np.minimum(large, N_BUCKETS - 1)
    return np.where(n < max_exact, n, large).astype(np.int32)


def _bucket_matrix(dist, valid):
    return np.where(valid, _t5_bucket_np(dist), -1).astype(np.int32)


def _bias_kernel(bm_ref, rb_ref, sub_ref, o_ref, *, mul):
    bm = bm_ref[...]
    acc = jnp.zeros(bm.shape, F32)
    for b in range(N_BUCKETS):
        acc = jnp.where(bm == b, rb_ref[b:b + 1, :], acc)
    o_ref[...] = jnp.where(bm < 0, NEG_INF, (acc - sub_ref[...]) * mul)


def bias_table(bm, rbx, sub, mul=1.0):
    g, _, c = rbx.shape
    r = bm.shape[0]
    return pl.pallas_call(
        functools.partial(_bias_kernel, mul=mul),
        grid=(g,),
        in_specs=[pl.BlockSpec((r, c), lambda i: (0, 0)),
                  pl.BlockSpec((None, N_BUCKETS, c), lambda i: (i, 0, 0)),
                  pl.BlockSpec((None, 1, c), lambda i: (i, 0, 0))],
        out_specs=pl.BlockSpec((None, r, c), lambda i: (i, 0, 0)),
        out_shape=jax.ShapeDtypeStruct((g, r, c), F32),
        compiler_params=_params("arbitrary"),
        name="bias_table",
    )(jnp.asarray(bm), rbx, sub)


def _mod_kernel(c_ref, w_ref, b_ref, o_ref):
    o_ref[...] = _dot_f32(_silu(c_ref[...]), w_ref[...]) + b_ref[...]


def modulation(c, w, b):
    n, d = c.shape
    m = w.shape[1]
    tn = 512
    return pl.pallas_call(
        _mod_kernel,
        grid=(m // tn,),
        in_specs=[pl.BlockSpec((n, d), lambda j: (0, 0)),
                  pl.BlockSpec((d, tn), lambda j: (0, j)),
                  pl.BlockSpec((1, tn), lambda j: (0, j))],
        out_specs=pl.BlockSpec((n, tn), lambda j: (0, j)),
        out_shape=jax.ShapeDtypeStruct((n, m), F32),
        compiler_params=_params("arbitrary"),
        name="modulation",
    )(c, w, b.reshape(1, m))


def _ln_inproj_kernel(x_ref, shift_ref, scale_ref, g_ref, w_ref, *out_refs, segs):
    x = x_ref[...]
    y = x * lax.rsqrt(jnp.mean(x * x, axis=-1, keepdims=True) + EPS)
    h = (y * g_ref[...]) * (1.0 + scale_ref[...]) + shift_ref[...]
    h16 = h.reshape(-1, h.shape[-1]).astype(BF16)
    for o_ref, seg in zip(out_refs, segs, strict=True):
        kind, lo, hi = seg[0], seg[1], seg[2]
        z = _dot(h16, w_ref[:, lo:hi])
        if kind == "silu":
            z = _silu(z)
        elif kind == "glu":
            z = z * jax.nn.sigmoid(_dot(h16, w_ref[:, seg[3]:seg[4]]))
        o_ref[...] = z.astype(o_ref.dtype)


def ln_inproj(x, shift, scale, norm_g, w16, segs, out_dtypes, tm):
    d = x.shape[-1]
    if x.ndim == 2:
        r = x.shape[0]
        nt = r // tm
        per_group = nt // shift.shape[0]
        x_spec = pl.BlockSpec((tm, d), lambda i: (i, 0))
        mod_spec = pl.BlockSpec((None, 1, d), lambda i: (i // per_group, 0, 0))
    else:
        n, t_new, _ = x.shape
        r = n * t_new
        nt = r // tm
        x_spec = pl.BlockSpec((tm // t_new, t_new, d), lambda i: (i, 0, 0))
        mod_spec = pl.BlockSpec((tm // t_new, 1, d), lambda i: (i, 0, 0))
    out_shape = [jax.ShapeDtypeStruct((r, s[2] - s[1]), dt) for s, dt in zip(segs, out_dtypes, strict=True)]
    out_specs = [pl.BlockSpec((tm, s[2] - s[1]), lambda i: (i, 0)) for s in segs]
    return pl.pallas_call(
        functools.partial(_ln_inproj_kernel, segs=segs),
        grid=(nt,),
        in_specs=[x_spec, mod_spec, mod_spec,
                  pl.BlockSpec((1, d), lambda i: (0, 0)),
                  pl.BlockSpec(w16.shape, lambda i: (0, 0), pipeline_mode=pl.Buffered(1))],
        out_specs=out_specs,
        out_shape=out_shape,
        compiler_params=_params("arbitrary"),
        name="ln_inproj",
    )(x, shift, scale, norm_g.reshape(1, d), w16)


def _out_proj_kernel(*refs, n_gated, n_plain, final_norm):
    it = iter(refs)
    y = None
    for _ in range(n_gated):
        a_ref, m_ref, w_ref = next(it), next(it), next(it)
        t = _dot((a_ref[...] * m_ref[...].astype(F32)).astype(BF16), w_ref[...])
        y = t if y is None else y + t
    for _ in range(n_plain):
        a_ref, w_ref = next(it), next(it)
        t = _dot(a_ref[...], w_ref[...])
        y = t if y is None else y + t
    x_ref, gate_ref = next(it), next(it)
    xn = x_ref[...] + gate_ref[...] * y.reshape(x_ref.shape)
    if final_norm:
        fg_ref = next(it)
        xn = xn * lax.rsqrt(jnp.mean(xn * xn, axis=-1, keepdims=True) + EPS) * fg_ref[...]
    o_ref = next(it)
    o_ref[...] = xn


def out_proj(gated, plain, x, gate, final_g, tm):
    d = x.shape[-1]
    if x.ndim == 2:
        nt = x.shape[0] // tm
        per_group = nt // gate.shape[0]
        x_spec = pl.BlockSpec((tm, d), lambda i: (i, 0))
        gate_spec = pl.BlockSpec((None, 1, d), lambda i: (i // per_group, 0, 0))
    else:
        n, t_new, _ = x.shape
        nt = n * t_new // tm
        x_spec = pl.BlockSpec((tm // t_new, t_new, d), lambda i: (i, 0, 0))
        gate_spec = pl.BlockSpec((tm // t_new, 1, d), lambda i: (i, 0, 0))
    args, specs = [], []
    for a, m, w in gated:
        args += [a, m, w]
        specs += [pl.BlockSpec((tm, a.shape[1]), lambda i: (i, 0)),
                  pl.BlockSpec((tm, m.shape[1]), lambda i: (i, 0)),
                  pl.BlockSpec(w.shape, lambda i: (0, 0))]
    for a, w in plain:
        args += [a, w]
        specs += [pl.BlockSpec((tm, a.shape[1]), lambda i: (i, 0)),
                  pl.BlockSpec(w.shape, lambda i: (0, 0))]
    args += [x, gate]
    specs += [x_spec, gate_spec]
    if final_g is not None:
        args.append(final_g.reshape(1, d))
        specs.append(pl.BlockSpec((1, d), lambda i: (0, 0)))
    return pl.pallas_call(
        functools.partial(_out_proj_kernel, n_gated=len(gated), n_plain=len(plain),
                          final_norm=final_g is not None),
        grid=(nt,),
        in_specs=specs,
        out_specs=x_spec,
        out_shape=jax.ShapeDtypeStruct(x.shape, F32),
        compiler_params=_params("arbitrary"),
        name="out_proj",
    )(*args)


def _top3_mask(scores, n_valid):
    nb = scores.shape[0]
    blk = lax.broadcasted_iota(jnp.int32, scores.shape, 0)
    s = jnp.where(blk < n_valid, scores, NEG_INF)
    picked = jnp.zeros(scores.shape, F32)
    for _ in range(MOBA_TOPK):
        mx = jnp.max(s, axis=0, keepdims=True)
        first = jnp.min(jnp.where(s == mx, blk, nb), axis=0, keepdims=True)
        hit = blk == first
        picked = jnp.where(hit, 1.0, picked)
        s = jnp.where(hit, NEG_INF, s)
    return jnp.where(blk < n_valid, picked, 0.0)


V_AUG = HEAD_DIM + 16


def _moba_prompt_kernel(q_ref, k_ref, v_ref, bown_ref, badj_ref, o_ref,
                        k16_s, vT_s, kbar_s, qT16_s, sel_s, m_s, acc_s,
                        s0_s, s1_s, p0_s, p1_s, a0_s, a1_s, *, nblk, chunk, sub):
    g = pl.program_id(1)
    i = pl.program_id(2)
    rows = GROUP * MOBA_BLOCK
    nch = rows // chunk

    @pl.when((g == 0) & (i == 0))
    def _per_batch():
        ones = jnp.ones((V_AUG - HEAD_DIM, MOBA_BLOCK), BF16)
        for j in range(nblk):
            kb = k_ref[j * MOBA_BLOCK:(j + 1) * MOBA_BLOCK, :]
            k16_s[j * MOBA_BLOCK:(j + 1) * MOBA_BLOCK, :] = kb.astype(BF16)
            kbar_s[j:j + 1, :] = jnp.mean(kb, axis=0, keepdims=True)
            v_t = v_ref[j * MOBA_BLOCK:(j + 1) * MOBA_BLOCK, :].T.astype(BF16)
            for kv in range(N_KV):
                vT_s[kv, j] = jnp.concatenate([v_t[kv * HEAD_DIM:(kv + 1) * HEAD_DIM], ones], axis=0)

    qT = (q_ref[...] * (SCALE * LOG2E)).T
    qs = jnp.concatenate([qT[h * HEAD_DIM:(h + 1) * HEAD_DIM, :] for h in range(GROUP)], axis=1)
    zeros = jnp.zeros_like(qs)
    qpad = jnp.where(g == 0, jnp.concatenate([qs, zeros], axis=0), jnp.concatenate([zeros, qs], axis=0))
    qpad16 = qpad.astype(BF16)
    for c in range(nch):
        qT16_s[c] = qpad16[:, c * chunk:(c + 1) * chunk]
    sel_s[...] = _top3_mask(_dot_f32(kbar_s[...], qpad), i)
    m_s[...] = jnp.full(m_s.shape, NEG_INF, F32)
    acc_s[...] = jnp.zeros(acc_s.shape, F32)

    n_far = i - 1
    sbuf, pbuf, abuf = (s0_s, s1_s), (p0_s, p1_s), (a0_s, a1_s)

    def blk_of(t):
        far_j = jnp.clip(t - 2, 0, jnp.maximum(n_far - 1, 0))
        return jnp.where(t == 0, i, jnp.where(t == 1, jnp.maximum(i - 1, 0), far_j))

    per = chunk // sub

    def stage_qk(t, slot, c):
        kb = k16_s[pl.ds(pl.multiple_of(blk_of(t) * MOBA_BLOCK, MOBA_BLOCK), MOBA_BLOCK), :]
        s_t = _dot(kb, qT16_s[c])
        for k in range(per):
            sbuf[slot][c * per + k] = s_t[:, k * sub:(k + 1) * sub]

    def stage_softmax(slot, c, on, bias_ref):
        ln = slice(c * sub, (c + 1) * sub)
        s_t = sbuf[slot][c]
        if bias_ref is not None:
            s_t = s_t + bias_ref[c]
        m_old = m_s[:, ln]
        m_new = jnp.maximum(m_old, jnp.where(on, jnp.max(s_t, axis=0, keepdims=True), NEG_INF))
        pbuf[slot][c] = jnp.exp2(s_t - jnp.where(on, m_new, jnp.inf)).astype(BF16)
        abuf[slot][:, ln] = jnp.exp2(m_old - m_new)
        m_s[:, ln] = m_new

    def stage_pv(t, slot, c):
        p = jnp.concatenate([pbuf[slot][c * per + k] for k in range(per)], axis=1)
        acc_s[c] = acc_s[c] * abuf[slot][:, c * chunk:(c + 1) * chunk] + _dot(vT_s[g, blk_of(t)], p)

    def pipe_half(t, slot, bias_ref=None, pv=True):
        own = bias_ref is bown_ref
        valid = None if own else jnp.where(t == 1, i >= 1, t - 2 < n_far)
        j = blk_of(t)
        for c in range(nch):
            stage_qk(t + 1, 1 - slot, c)
            if own:
                on = jnp.full((1, chunk), True)
            else:
                on = (sel_s[pl.ds(j, 1), c * chunk:(c + 1) * chunk] > 0.0) & valid
            for k in range(per):
                stage_softmax(slot, c * per + k, on[:, k * sub:(k + 1) * sub], bias_ref)
            if pv:
                stage_pv(t - 1, 1 - slot, c)

    for c in range(nch):
        stage_qk(0, 0, c)
    pipe_half(0, 0, bown_ref, pv=False)
    pipe_half(1, 1, badj_ref)

    def pair(u, carry):
        pipe_half(2 + 2 * u, 0)
        pipe_half(3 + 2 * u, 1)
        return carry

    n_pairs = (jnp.maximum(n_far, 0) + 1) // 2
    lax.fori_loop(0, n_pairs, pair, 0)
    for c in range(nch):
        stage_pv(1 + 2 * n_pairs, 1, c)

    acc = jnp.concatenate([acc_s[c] for c in range(nch)], axis=1)
    o_t = acc[:HEAD_DIM] / acc[HEAD_DIM:HEAD_DIM + 1]
    o_cat = jnp.concatenate([o_t[:, h * MOBA_BLOCK:(h + 1) * MOBA_BLOCK] for h in range(GROUP)], axis=0)
    o_ref[...] = o_cat.T.astype(o_ref.dtype)


def moba_prompt(q, k, v, rel_bias, batch, seq):
    nblk = seq // MOBA_BLOCK
    rows = GROUP * MOBA_BLOCK
    kk = np.arange(MOBA_BLOCK)[:, None]
    qq = (np.arange(rows) % MOBA_BLOCK)[None, :]
    bm_own = _bucket_matrix(qq - kk, qq >= kk)
    bm_adj = _bucket_matrix(MOBA_BLOCK + qq - kk, np.ones((MOBA_BLOCK, rows), bool))
    rbx = jnp.repeat(rel_bias.reshape(N_BUCKETS, N_KV, GROUP).transpose(1, 0, 2), MOBA_BLOCK, axis=2)
    far = rbx[:, N_BUCKETS - 1:N_BUCKETS, :]
    chunk, sub = 256, 128
    nsub = rows // sub

    def chunk_major(tbl):
        return tbl.reshape(N_KV, MOBA_BLOCK, nsub, sub).transpose(0, 2, 1, 3)

    b_own = chunk_major(bias_table(bm_own, rbx, far, LOG2E))
    b_adj = chunk_major(bias_table(bm_adj, rbx, far, LOG2E))
    qw = GROUP * HEAD_DIM
    tbl_spec = pl.BlockSpec((None, nsub, MOBA_BLOCK, sub), lambda b, g, i: (g, 0, 0, 0))
    return pl.pallas_call(
        functools.partial(_moba_prompt_kernel, nblk=nblk, chunk=chunk, sub=sub),
        grid=(batch, N_KV, nblk),
        in_specs=[pl.BlockSpec((MOBA_BLOCK, qw), lambda b, g, i: (b * nblk + i, g)),
                  pl.BlockSpec((None, seq, KV_W), lambda b, g, i: (b, 0, 0)),
                  pl.BlockSpec((None, seq, KV_W), lambda b, g, i: (b, 0, 0)),
                  tbl_spec, tbl_spec],
        out_specs=pl.BlockSpec((MOBA_BLOCK, qw), lambda b, g, i: (b * nblk + i, g)),
        out_shape=jax.ShapeDtypeStruct(q.shape, BF16),
        scratch_shapes=[pltpu.VMEM((seq, KV_W), BF16),
                        pltpu.VMEM((N_KV, nblk, V_AUG, MOBA_BLOCK), BF16),
                        pltpu.VMEM((nblk, KV_W), F32),
                        pltpu.VMEM((rows // chunk, KV_W, chunk), BF16),
                        pltpu.VMEM((nblk, rows), F32),
                        pltpu.VMEM((1, rows), F32),
                        pltpu.VMEM((rows // chunk, V_AUG, chunk), F32),
                        pltpu.VMEM((nsub, MOBA_BLOCK, sub), F32), pltpu.VMEM((nsub, MOBA_BLOCK, sub), F32),
                        pltpu.VMEM((nsub, MOBA_BLOCK, sub), BF16), pltpu.VMEM((nsub, MOBA_BLOCK, sub), BF16),
                        pltpu.VMEM((1, rows), F32), pltpu.VMEM((1, rows), F32)],
        compiler_params=_params("arbitrary", "arbitrary", "arbitrary"),
        name="moba_prompt",
    )(q, k, v, b_own, b_adj)


def _moba_sample_kernel(pt_ref, q_ref, kn_ref, vn_ref, blast_ref, bown_ref, e_ref, kpool, vpool, o_ref,
                        kbuf, vbuf, ksem, vsem, s_s, *, n_pages, page, n_samples, unroll):
    s = pl.program_id(0)
    slot = s % 2
    ppb = MOBA_BLOCK // page
    nblk = n_pages // ppb
    rows = q_ref.shape[0]

    def k_copy(smp, sl, p):
        return pltpu.make_async_copy(kpool.at[pt_ref[smp, p]], kbuf.at[sl, p], ksem.at[sl])

    def v_copy(smp, sl, p):
        return pltpu.make_async_copy(vpool.at[pt_ref[smp, p]], vbuf.at[sl, p], vsem.at[sl])

    def start_all(smp, sl):
        def body(p, c):
            k_copy(smp, sl, p).start()
            v_copy(smp, sl, p).start()
            return c
        lax.fori_loop(0, n_pages, body, 0, unroll=8)

    @pl.when(s == 0)
    def _first():
        start_all(0, 0)

    def wait_all(copy, smp, sl):
        def body(p, c):
            copy(smp, sl, p).wait()
            return c
        lax.fori_loop(0, n_pages, body, 0, unroll=8)

    wait_all(k_copy, s, slot)
    nxt = jnp.minimum(s + 1, n_samples - 1)

    def block_t(buf, j):
        return jnp.concatenate([buf[slot, j * ppb + t] for t in range(ppb)], axis=1)

    blk_lane = lax.broadcasted_iota(jnp.int32, (KV_W, nblk), 1)

    def kbar_body(j, kbar_t):
        for t in range(ppb):
            k_copy(nxt, 1 - slot, j * ppb + t).start()
            v_copy(nxt, 1 - slot, j * ppb + t).start()
        pages = kbuf[slot, j * ppb]
        for t in range(1, ppb):
            pages = pages + kbuf[slot, j * ppb + t]
        col = jnp.sum(pages, axis=1, keepdims=True) * (1.0 / MOBA_BLOCK)
        return jnp.where(blk_lane == j, col, kbar_t)

    kbar_t = lax.fori_loop(0, nblk, kbar_body, jnp.zeros((KV_W, nblk), F32), unroll=2 * unroll)

    q = q_ref[...] * (SCALE * LOG2E)
    lower = lax.broadcasted_iota(jnp.int32, (rows, HEAD_DIM), 0) < rows // N_KV
    zq = jnp.zeros_like(q)
    qbd = jnp.concatenate([jnp.where(lower, q, zq), jnp.where(lower, zq, q)], axis=1)
    qbd16 = qbd.astype(BF16)

    scores = _dot_f32(qbd, kbar_t)
    sel_t = _top3_mask(scores.T, nblk)
    negm = jnp.where(sel_t.T > 0.0, 0.0, MASK_NEG)
    lhs16 = jnp.concatenate([qbd, negm, jnp.zeros((rows, e_ref.shape[1] - nblk), F32)], axis=1).astype(BF16)

    def logits(j):
        rhs16 = jnp.concatenate([block_t(kbuf, j).astype(BF16), e_ref[j]], axis=0)
        return _dot(lhs16, rhs16)

    def fold(x):
        return x[:, :LANES], x[:, LANES:]

    def far(j, mrun):
        st = logits(j)
        s_s[j] = st
        lo, hi = fold(st)
        return jnp.maximum(mrun, jnp.maximum(lo, hi))

    mrun = lax.fori_loop(0, nblk - 1, far, jnp.full((rows, LANES), NEG_INF, F32), unroll=unroll - 1)
    s_last = logits(nblk - 1) + blast_ref[...]
    s_s[nblk - 1] = s_last
    kn16 = kn_ref[...].astype(BF16)
    s_own = _dot_nt(qbd16, kn16) + bown_ref[...]
    for part in fold(s_last) + fold(s_own):
        mrun = jnp.maximum(mrun, part)
    m = jnp.max(mrun, axis=1, keepdims=True)

    wait_all(v_copy, s, slot)

    def pv(j, carry):
        acc, lsum = carry
        p = jnp.exp2(s_s[j] - m)
        lo, hi = fold(p)
        return acc + _dot_nt(p.astype(BF16), block_t(vbuf, j).astype(BF16)), lsum + (lo + hi)

    p_own = jnp.exp2(s_own - m)
    acc0 = _dot(p_own.astype(BF16), vn_ref[...].astype(BF16))
    lo, hi = fold(p_own)
    acc, lsum = lax.fori_loop(0, nblk, pv, (acc0, lo + hi), unroll=unroll)
    den = jnp.sum(lsum, axis=1, keepdims=True)
    o_ref[...] = jnp.where(lower, acc[:, :HEAD_DIM], acc[:, HEAD_DIM:]) / den

    @pl.when(s == n_samples - 1)
    def _drain():
        wait_all(k_copy, nxt, 1 - slot)
        wait_all(v_copy, nxt, 1 - slot)


def moba_sample(q, k_new, v_new, k_pool, v_pool, page_table, rel_bias, t_new):
    n, n_pages = page_table.shape
    page = k_pool.shape[2]
    past = n_pages * page
    nblk = past // MOBA_BLOCK
    rows = N_HEADS * t_new
    q_r = q.reshape(n, t_new, N_HEADS, HEAD_DIM).transpose(0, 2, 1, 3).reshape(n, rows, HEAD_DIM)
    pad = ((0, 0), (0, MOBA_BLOCK - t_new), (0, 0))
    kn = jnp.pad(k_new.reshape(n, t_new, KV_W), pad)
    vn = jnp.pad(v_new.reshape(n, t_new, KV_W), pad)
    tok = (np.arange(rows) % t_new)[None, :]
    kk = np.arange(MOBA_BLOCK)[:, None]
    bm_last = _bucket_matrix(MOBA_BLOCK + tok - kk, np.ones((MOBA_BLOCK, rows), bool))
    bm_own = _bucket_matrix(tok - kk, (kk <= tok) & (kk < t_new))
    rbx = jnp.repeat(rel_bias, t_new, axis=1)[None]
    far = rbx[:, N_BUCKETS - 1:N_BUCKETS, :]
    b_last = bias_table(bm_last, rbx, far, LOG2E)[0].T
    b_own = bias_table(bm_own, rbx, far, LOG2E)[0].T
    onehot_np = np.zeros((nblk, KV_W, MOBA_BLOCK), np.float32)
    onehot_np[np.arange(nblk), np.arange(nblk)] = 1.0
    onehot = jnp.asarray(onehot_np, BF16)

    grid_spec = pltpu.PrefetchScalarGridSpec(
        num_scalar_prefetch=1,
        grid=(n,),
        in_specs=[pl.BlockSpec((None, rows, HEAD_DIM), lambda s, pt: (s, 0, 0)),
                  pl.BlockSpec((None, MOBA_BLOCK, KV_W), lambda s, pt: (s, 0, 0)),
                  pl.BlockSpec((None, MOBA_BLOCK, KV_W), lambda s, pt: (s, 0, 0)),
                  pl.BlockSpec((rows, MOBA_BLOCK), lambda s, pt: (0, 0)),
                  pl.BlockSpec((rows, MOBA_BLOCK), lambda s, pt: (0, 0)),
                  pl.BlockSpec((nblk, KV_W, MOBA_BLOCK), lambda s, pt: (0, 0, 0)),
                  pl.BlockSpec(memory_space=pl.ANY),
                  pl.BlockSpec(memory_space=pl.ANY)],
        out_specs=pl.BlockSpec((None, rows, HEAD_DIM), lambda s, pt: (s, 0, 0)),
        scratch_shapes=[pltpu.VMEM((2, n_pages, KV_W, page), F32),
                        pltpu.VMEM((2, n_pages, KV_W, page), F32),
                        pltpu.SemaphoreType.DMA((2,)),
                        pltpu.SemaphoreType.DMA((2,)),
                        pltpu.VMEM((nblk, rows, MOBA_BLOCK), F32)],
    )
    o = pl.pallas_call(
        functools.partial(_moba_sample_kernel, n_pages=n_pages, page=page, n_samples=n, unroll=8),
        grid_spec=grid_spec,
        out_shape=jax.ShapeDtypeStruct((n, rows, HEAD_DIM), F32),
        compiler_params=_params("arbitrary"),
        name="moba_sample",
    )(page_table, q_r, kn, vn, b_last, b_own, onehot, k_pool, v_pool)
    return o.reshape(n, N_HEADS, t_new, HEAD_DIM).transpose(0, 2, 1, 3).reshape(n * t_new, N_HEADS * HEAD_DIM)


def _conv_tail(y, cb_ref, lg_ref, lb_ref, gate):
    y = y + cb_ref[...]
    mu = jnp.mean(y, axis=-1, keepdims=True)
    var = jnp.mean(jnp.square(y - mu), axis=-1, keepdims=True)
    yn = (y - mu) * lax.rsqrt(var + EPS) * lg_ref[...] + lb_ref[...]
    return _silu(yn) * gate


HALO = 32


SUBLANES = 8
LANES = 128
CONV_ROWS = 64


def _conv_prompt_kernel(u_ref, prev_ref, g_ref, w_ref, cb_ref, lg_ref, lb_ref, o_ref, xs, xr, ys, *, tl):
    t = pl.program_id(1)
    c = u_ref.shape[1]
    xs[0:HALO, :] = jnp.where(t > 0, prev_ref[...], 0.0)
    xs[HALO:, :] = u_ref[...]
    span = tl + HALO - SUBLANES
    for r in range(1, SUBLANES):
        xr[r - 1] = xs[pl.ds(r, span), :]
    off = HALO - (CONV_W - 1)

    def taps(base, lt):
        ln = slice(lt * LANES, (lt + 1) * LANES)
        acc = jnp.zeros((CONV_ROWS, LANES), F32)
        for k in range(CONV_W):
            a, r = divmod(off + k, SUBLANES)
            src = xs if r == 0 else xr.at[r - 1]
            start = base + a * SUBLANES
            acc = acc + src[start:start + CONV_ROWS, ln] * w_ref[k:k + 1, ln]
        ys[base:base + CONV_ROWS, ln] = acc

    for base in range(0, tl, CONV_ROWS):
        for lt in range(0, c // LANES, 2):
            @pl.when(t >= 0)
            def _():
                taps(base, lt)
                taps(base, lt + 1)
    o_ref[...] = _conv_tail(ys[...], cb_ref, lg_ref, lb_ref, g_ref[...].astype(F32)).astype(o_ref.dtype)


def conv_prompt(u, szb, conv_w, conv_b, ln_g, ln_b, batch, seq, tl=256):
    c = u.shape[1]
    nt = seq // tl
    vec = pl.BlockSpec((1, c), lambda b, t: (0, 0))
    return pl.pallas_call(
        functools.partial(_conv_prompt_kernel, tl=tl),
        grid=(batch, nt),
        in_specs=[pl.BlockSpec((tl, c), lambda b, t: (b * nt + t, 0)),
                  pl.BlockSpec((HALO, c), lambda b, t: (jnp.maximum((b * nt + t) * (tl // HALO) - 1, 0), 0)),
                  pl.BlockSpec((tl, c), lambda b, t: (b * nt + t, 0)),
                  pl.BlockSpec((CONV_W, c), lambda b, t: (0, 0)), vec, vec, vec],
        out_specs=pl.BlockSpec((tl, c), lambda b, t: (b * nt + t, 0)),
        out_shape=jax.ShapeDtypeStruct(u.shape, BF16),
        scratch_shapes=[pltpu.VMEM((HALO + tl, c), F32),
                        pltpu.VMEM((SUBLANES - 1, HALO + tl - SUBLANES, c), F32),
                        pltpu.VMEM((tl, c), F32)],
        compiler_params=_params("arbitrary", "arbitrary"),
        name="conv_prompt",
    )(u, u, szb, conv_w, conv_b.reshape(1, c), ln_g.reshape(1, c), ln_b.reshape(1, c))


def _conv_sample_kernel(xp_ref, g_ref, w_ref, cb_ref, lg_ref, lb_ref, o_ref, *, t_new):
    ns, _, c = xp_ref.shape
    y = jnp.zeros((ns, t_new, c), F32)
    for k in range(CONV_W):
        y = y + xp_ref[:, pl.ds(k, t_new), :] * w_ref[k:k + 1, :]
    o_ref[...] = _conv_tail(y, cb_ref, lg_ref, lb_ref, g_ref[...].astype(F32)).astype(o_ref.dtype)


def conv_sample(xp, szb, conv_w, conv_b, ln_g, ln_b, t_new, ns=8):
    n, rows, c = xp.shape
    vec = pl.BlockSpec((1, c), lambda i: (0, 0))
    return pl.pallas_call(
        functools.partial(_conv_sample_kernel, t_new=t_new),
        grid=(n // ns,),
        in_specs=[pl.BlockSpec((ns, rows, c), lambda i: (i, 0, 0)),
                  pl.BlockSpec((ns, t_new, c), lambda i: (i, 0, 0)),
                  pl.BlockSpec((CONV_W, c), lambda i: (0, 0)), vec, vec, vec],
        out_specs=pl.BlockSpec((ns, t_new, c), lambda i: (i, 0, 0)),
        out_shape=jax.ShapeDtypeStruct((n, t_new, c), BF16),
        compiler_params=_params("arbitrary"),
        name="conv_sample",
    )(xp, szb, conv_w, conv_b.reshape(1, c), ln_g.reshape(1, c), ln_b.reshape(1, c))


def _swa_prompt_kernel(q_ref, kc_ref, kp_ref, vc_ref, vp_ref, bias_ref, sink_ref, o_ref):
    n = pl.program_id(1)
    w = SWA_WINDOW
    rows = GROUP * w
    kcat = jnp.concatenate([kp_ref[...], kc_ref[...]], axis=0).astype(BF16)
    vcat_t = jnp.concatenate([vp_ref[...], vc_ref[...]], axis=0).T.astype(BF16)
    key_id = lax.broadcasted_iota(jnp.int32, (2 * w, rows), 0)
    outs = []
    for g in range(N_KV):
        q_t = (q_ref[:, g * GROUP * HEAD_DIM:(g + 1) * GROUP * HEAD_DIM] * SCALE).T
        qs = jnp.concatenate([q_t[h * HEAD_DIM:(h + 1) * HEAD_DIM, :] for h in range(GROUP)], axis=1)
        zeros = jnp.zeros_like(qs)
        qpad = jnp.concatenate([qs, zeros] if g == 0 else [zeros, qs], axis=0).astype(BF16)
        s_t = _dot(kcat, qpad) + bias_ref[g]
        s_t = jnp.where((key_id >= w) | (n > 0), s_t, NEG_INF)
        sink = sink_ref[g]
        m = jnp.maximum(jnp.max(s_t, axis=0, keepdims=True), sink)
        p = jnp.exp(s_t - m)
        den = jnp.sum(p, axis=0, keepdims=True) + jnp.exp(sink - m)
        o_t = _dot(vcat_t, p.astype(BF16))[g * HEAD_DIM:(g + 1) * HEAD_DIM, :] / den
        outs.append(jnp.concatenate([o_t[:, h * w:(h + 1) * w] for h in range(GROUP)], axis=0).T)
    o_ref[...] = jnp.concatenate(outs, axis=1).astype(o_ref.dtype)


def swa_prompt(q, k, v, sinks, rel_bias, batch, seq):
    w = SWA_WINDOW
    nb = seq // w
    rows = GROUP * w
    kidx = np.arange(2 * w)[:, None]
    qq = (np.arange(rows) % w)[None, :]
    dist = w + qq - kidx
    bm = _bucket_matrix(dist, (dist >= 0) & (dist < w))
    rbx = jnp.repeat(rel_bias.reshape(N_BUCKETS, N_KV, GROUP).transpose(1, 0, 2), w, axis=2)
    bias = bias_table(bm, rbx, jnp.zeros((N_KV, 1, rows), F32))
    sink_x = jnp.repeat(sinks.reshape(N_KV, 1, GROUP), w, axis=2)
    d = q.shape[1]
    cur = lambda b, n: (b * nb + n, 0)
    prev = lambda b, n: (jnp.maximum(b * nb + n - 1, 0), 0)
    return pl.pallas_call(
        _swa_prompt_kernel,
        grid=(batch, nb),
        in_specs=[pl.BlockSpec((w, d), cur),
                  pl.BlockSpec((w, KV_W), cur), pl.BlockSpec((w, KV_W), prev),
                  pl.BlockSpec((w, KV_W), cur), pl.BlockSpec((w, KV_W), prev),
                  pl.BlockSpec((N_KV, 2 * w, rows), lambda b, n: (0, 0, 0)),
                  pl.BlockSpec((N_KV, 1, rows), lambda b, n: (0, 0, 0))],
        out_specs=pl.BlockSpec((w, d), cur),
        out_shape=jax.ShapeDtypeStruct(q.shape, BF16),
        compiler_params=_params("arbitrary", "arbitrary"),
        name="swa_prompt",
    )(q, k, k, v, v, bias, sink_x)


def _swa_sample_kernel(q_ref, k_ref, v_ref, bias_ref, sink_ref, o_ref):
    ns, rows, _ = q_ref.shape
    r_id = lax.broadcasted_iota(jnp.int32, (rows, HEAD_DIM), 0)
    lower = r_id < rows // N_KV
    for i in range(ns):
        q = q_ref[i] * SCALE
        zq = jnp.zeros_like(q)
        qbd = jnp.concatenate([jnp.where(lower, q, zq), jnp.where(lower, zq, q)], axis=1).astype(BF16)
        s = _dot_nt(qbd, k_ref[i].astype(BF16)) + bias_ref[...]
        sink = sink_ref[...]
        m = jnp.maximum(jnp.max(s, axis=1, keepdims=True), sink)
        p = jnp.exp(s - m)
        den = jnp.sum(p, axis=1, keepdims=True) + jnp.exp(sink - m)
        acc = _dot(p.astype(BF16), v_ref[i].astype(BF16))
        o_ref[i] = jnp.where(lower, acc[:, :HEAD_DIM], acc[:, HEAD_DIM:]) / den


def swa_sample(q, kcat, vcat, sinks, rel_bias, t_new, ns=4):
    n, nk, _ = kcat.shape
    wb = nk - t_new
    rows = N_HEADS * t_new
    keys = -(-nk // 128) * 128
    q_r = q.reshape(n, t_new, N_HEADS, HEAD_DIM).transpose(0, 2, 1, 3).reshape(n, rows, HEAD_DIM)
    pad = ((0, 0), (0, keys - nk), (0, 0))
    kp, vp = jnp.pad(kcat, pad), jnp.pad(vcat, pad)
    tok = (np.arange(rows) % t_new)[None, :]
    idx = np.arange(keys)[:, None]
    dist = tok + wb - idx
    bm = _bucket_matrix(dist, (dist >= 0) & (dist < SWA_WINDOW) & (idx < nk))
    rbx = jnp.repeat(rel_bias, t_new, axis=1)[None]
    bias = bias_table(bm, rbx, jnp.zeros((1, 1, rows), F32))[0].T
    sink_r = jnp.repeat(sinks, t_new).reshape(rows, 1)
    o = pl.pallas_call(
        _swa_sample_kernel,
        grid=(n // ns,),
        in_specs=[pl.BlockSpec((ns, rows, HEAD_DIM), lambda i: (i, 0, 0)),
                  pl.BlockSpec((ns, keys, KV_W), lambda i: (i, 0, 0)),
                  pl.BlockSpec((ns, keys, KV_W), lambda i: (i, 0, 0)),
                  pl.BlockSpec((rows, keys), lambda i: (0, 0)),
                  pl.BlockSpec((rows, 1), lambda i: (0, 0))],
        out_specs=pl.BlockSpec((ns, rows, HEAD_DIM), lambda i: (i, 0, 0)),
        out_shape=jax.ShapeDtypeStruct((n, rows, HEAD_DIM), F32),
        compiler_params=_params("arbitrary"),
        name="swa_sample",
    )(q_r, kp, vp, bias, sink_r)
    return o.reshape(n, N_HEADS, t_new, HEAD_DIM).transpose(0, 2, 1, 3).reshape(n * t_new, N_HEADS * HEAD_DIM)


def kernel(x_prompt, x_sample, c_prompt, c_sample, cache_a_k, cache_a_v, page_table, cache_b_conv, cache_c_k, cache_c_v, rel_bias, norm_a, mod_w_a, mod_b_a, w_in_a, conv_w_b, conv_b_b, ln_g_b, ln_b_b, w_out_a, norm_c, mod_w_c, mod_b_c, w_in_c, sinks_c, w_out_c, final_norm):
    batch, seq, d = x_prompt.shape
    n, t_new, _ = x_sample.shape
    n_pool, page = cache_a_k.shape[1], cache_a_k.shape[2]
    hq = N_HEADS * HEAD_DIM
    cb = conv_w_b.shape[2]
    assert norm_a.shape[0] == 1 and norm_c.shape[0] == 1, "one A/B layer followed by one C layer"

    tm_p = 1024
    tm_s = min(256, n * t_new)
    xp = x_prompt.reshape(batch * seq, d)
    xs = x_sample

    c_all = jnp.concatenate([c_prompt, c_sample], axis=0)
    c_rows = -(-c_all.shape[0] // 8) * 8
    c_all = jnp.pad(c_all, ((0, c_rows - c_all.shape[0]), (0, 0)))

    def split_mod(m):
        mp = m[:batch].reshape(batch, 1, 3, d)
        ms = m[batch:batch + n].reshape(n, 1, 3, d)
        return [(mp[:, :, j], ms[:, :, j]) for j in range(3)]

    (sh_a, sc_a, gt_a) = split_mod(modulation(c_all, mod_w_a[0], mod_b_a[0]))
    (sh_c, sc_c, gt_c) = split_mod(modulation(c_all, mod_w_c[0], mod_b_c[0]))

    kv0, kv1 = hq, hq + 2 * KV_W
    za0 = kv1
    ga0 = za0 + hq
    gb0 = ga0 + cb
    zb0 = gb0 + cb
    segs_a = (("raw", 0, hq), ("raw", kv0, kv0 + KV_W), ("raw", kv0 + KV_W, kv1),
              ("silu", za0, ga0), ("glu", ga0, gb0, gb0, zb0), ("silu", zb0, zb0 + cb))
    dts_a = (F32, F32, F32, BF16, F32, BF16)
    w_in_a16 = w_in_a[0].astype(BF16)
    w_out_a16 = w_out_a[0].astype(BF16)
    k_pool = cache_a_k[0].transpose(0, 2, 3, 1).reshape(n_pool, KV_W, page)
    v_pool = cache_a_v[0].transpose(0, 2, 3, 1).reshape(n_pool, KV_W, page)

    q, k, v, sza, u, szb = ln_inproj(xp, sh_a[0], sc_a[0], norm_a[0], w_in_a16, segs_a, dts_a, tm_p)
    oa = moba_prompt(q, k.reshape(batch, seq, KV_W), v.reshape(batch, seq, KV_W), rel_bias, batch, seq)
    ob = conv_prompt(u, szb, conv_w_b[0], conv_b_b[0], ln_g_b[0], ln_b_b[0], batch, seq)
    xp1 = out_proj([(oa, sza, w_out_a16[:hq])], [(ob, w_out_a16[hq:])], xp, gt_a[0], None, tm_p)
    ak_p, av_p = k, v
    bc_p = u.reshape(batch, seq, cb)[:, seq - (CONV_W - 1):]

    q, k, v, sza, u, szb = ln_inproj(xs, sh_a[1], sc_a[1], norm_a[0], w_in_a16, segs_a, dts_a, tm_s)
    oa = moba_sample(q, k, v, k_pool, v_pool, page_table, rel_bias, t_new)
    xcat = jnp.concatenate([cache_b_conv[0], u.reshape(n, t_new, cb)], axis=1)
    ob = conv_sample(xcat, szb.reshape(n, t_new, cb), conv_w_b[0], conv_b_b[0], ln_g_b[0], ln_b_b[0], t_new)
    xs1 = out_proj([(oa, sza, w_out_a16[:hq])], [(ob.reshape(n * t_new, cb), w_out_a16[hq:])],
                   xs, gt_a[1], None, tm_s)
    ak_s, av_s = k, v
    bc_s = xcat[:, t_new:]

    segs_c = (("raw", 0, hq), ("raw", hq, hq + KV_W), ("raw", hq + KV_W, hq + 2 * KV_W),
              ("silu", hq + 2 * KV_W, 2 * hq + 2 * KV_W))
    dts_c = (F32, F32, F32, BF16)
    w_in_c16 = w_in_c[0].astype(BF16)
    w_out_c16 = w_out_c[0].astype(BF16)

    q, k, v, sz = ln_inproj(xp1, sh_c[0], sc_c[0], norm_c[0], w_in_c16, segs_c, dts_c, tm_p)
    o = swa_prompt(q, k, v, sinks_c[0], rel_bias, batch, seq)
    y_prompt = out_proj([(o, sz, w_out_c16)], [], xp1, gt_c[0], final_norm, tm_p)
    wb_p = min(SWA_WINDOW, seq)
    ck_p = k.reshape(batch, seq, KV_W)[:, seq - wb_p:]
    cv_p = v.reshape(batch, seq, KV_W)[:, seq - wb_p:]

    q, k, v, sz = ln_inproj(xs1, sh_c[1], sc_c[1], norm_c[0], w_in_c16, segs_c, dts_c, tm_s)
    wb_s = cache_c_k.shape[2]
    kcat = jnp.concatenate([cache_c_k[0].reshape(n, wb_s, KV_W), k.reshape(n, t_new, KV_W)], axis=1)
    vcat = jnp.concatenate([cache_c_v[0].reshape(n, wb_s, KV_W), v.reshape(n, t_new, KV_W)], axis=1)
    o = swa_sample(q, kcat, vcat, sinks_c[0], rel_bias, t_new)
    y_sample = out_proj([(o, sz, w_out_c16)], [], xs1, gt_c[1], final_norm, tm_s)
    ck_s, cv_s = kcat[:, t_new:], vcat[:, t_new:]

    def kv5(a, lead):
        return a.reshape((1,) + lead + (N_KV, HEAD_DIM))

    return (y_prompt.reshape(batch, seq, d), y_sample,
            kv5(ak_p, (batch, seq)), kv5(av_p, (batch, seq)),
            kv5(ak_s, (n, t_new)), kv5(av_s, (n, t_new)),
            bc_p[None], bc_s[None],
            kv5(ck_p, (batch, wb_p)), kv5(cv_p, (batch, wb_p)),
            kv5(ck_s, (n, wb_s)), kv5(cv_s, (n, wb_s)))
```

```python
import functools
import math

import jax
import jax.numpy as jnp
import numpy as np
from jax import lax
from jax.experimental import pallas as pl
from jax.experimental.pallas import tpu as pltpu

F32 = jnp.float32
BF16 = jnp.bfloat16
NEG_INF = float("-inf")
MASK_NEG = -1e30

HEAD_DIM = 64
N_HEADS = 16
N_KV = 2
GROUP = N_HEADS // N_KV
KV_W = N_KV * HEAD_DIM
MOBA_BLOCK = 256
MOBA_TOPK = 3
CONV_W = 31
SWA_WINDOW = 128
N_BUCKETS = 32
MAX_DISTANCE = 128
EPS = 1e-6
SCALE = HEAD_DIM ** -0.5
LOG2E = math.log2(math.e)

VMEM_LIMIT = 56 * 2**20


def _params(*sem):
    return pltpu.CompilerParams(dimension_semantics=sem, vmem_limit_bytes=VMEM_LIMIT)


def _silu(z):
    return z * jax.nn.sigmoid(z)


def _dot(a, b):
    return jnp.dot(a, b, preferred_element_type=F32)


def _dot_nt(a, b):
    return lax.dot_general(a, b, (((1,), (1,)), ((), ())), preferred_element_type=F32)


def _dot_f32(a, b):
    return jnp.dot(a, b, preferred_element_type=F32, precision=lax.Precision.HIGHEST)


def _t5_bucket_np(n):
    n = np.maximum(n, 0)
    max_exact = N_BUCKETS // 2
    nf = np.maximum(n, 1).astype(np.float32)
    large = max_exact + (np.log(nf / np.float32(max_exact)) / np.float32(math.log(MAX_DISTANCE / max_exact))
                         * np.float32(N_BUCKETS - max_exact)).astype(np.int32)
    large = np.minimum(large, N_BUCKETS - 1)
    return np.where(n < max_exact, n, large).astype(np.int32)


def _bucket_matrix(dist, valid):
    return np.where(valid, _t5_bucket_np(dist), -1).astype(np.int32)


def _bias_kernel(bm_ref, rb_ref, sub_ref, o_ref, *, mul):
    bm = bm_ref[...]
    acc = jnp.zeros(bm.shape, F32)
    for b in range(N_BUCKETS):
        acc = jnp.where(bm == b, rb_ref[b:b + 1, :], acc)
    o_ref[...] = jnp.where(bm < 0, NEG_INF, (acc - sub_ref[...]) * mul)


def bias_table(bm, rbx, sub, mul=1.0):
    g, _, c = rbx.shape
    r = bm.shape[0]
    return pl.pallas_call(
        functools.partial(_bias_kernel, mul=mul),
        grid=(g,),
        in_specs=[pl.BlockSpec((r, c), lambda i: (0, 0)),
                  pl.BlockSpec((None, N_BUCKETS, c), lambda i: (i, 0, 0)),
                  pl.BlockSpec((None, 1, c), lambda i: (i, 0, 0))],
        out_specs=pl.BlockSpec((None, r, c), lambda i: (i, 0, 0)),
        out_shape=jax.ShapeDtypeStruct((g, r, c), F32),
        compiler_params=_params("arbitrary"),
        name="bias_table",
    )(jnp.asarray(bm), rbx, sub)


def _mod_kernel(c_ref, w_ref, b_ref, o_ref):
    o_ref[...] = _dot_f32(_silu(c_ref[...]), w_ref[...]) + b_ref[...]


def modulation(c, w, b):
    n, d = c.shape
    m = w.shape[1]
    tn = 512
    return pl.pallas_call(
        _mod_kernel,
        grid=(m // tn,),
        in_specs=[pl.BlockSpec((n, d), lambda j: (0, 0)),
                  pl.BlockSpec((d, tn), lambda j: (0, j)),
                  pl.BlockSpec((1, tn), lambda j: (0, j))],
        out_specs=pl.BlockSpec((n, tn), lambda j: (0, j)),
        out_shape=jax.ShapeDtypeStruct((n, m), F32),
        compiler_params=_params("arbitrary"),
        name="modulation",
    )(c, w, b.reshape(1, m))


def _ln_inproj_kernel(x_ref, shift_ref, scale_ref, g_ref, w_ref, *out_refs, segs):
    x = x_ref[...]
    y = x * lax.rsqrt(jnp.mean(x * x, axis=-1, keepdims=True) + EPS)
    h = (y * g_ref[...]) * (1.0 + scale_ref[...]) + shift_ref[...]
    h16 = h.reshape(-1, h.shape[-1]).astype(BF16)
    for o_ref, seg in zip(out_refs, segs, strict=True):
        kind, lo, hi = seg[0], seg[1], seg[2]
        z = _dot(h16, w_ref[:, lo:hi])
        if kind == "silu":
            z = _silu(z)
        elif kind == "glu":
            z = z * jax.nn.sigmoid(_dot(h16, w_ref[:, seg[3]:seg[4]]))
        o_ref[...] = z.astype(o_ref.dtype)


def ln_inproj(x, shift, scale, norm_g, w16, segs, out_dtypes, tm):
    d = x.shape[-1]
    if x.ndim == 2:
        r = x.shape[0]
        nt = r // tm
        per_group = nt // shift.shape[0]
        x_spec = pl.BlockSpec((tm, d), lambda i: (i, 0))
        mod_spec = pl.BlockSpec((None, 1, d), lambda i: (i // per_group, 0, 0))
    else:
        n, t_new, _ = x.shape
        r = n * t_new
        nt = r // tm
        x_spec = pl.BlockSpec((tm // t_new, t_new, d), lambda i: (i, 0, 0))
        mod_spec = pl.BlockSpec((tm // t_new, 1, d), lambda i: (i, 0, 0))
    out_shape = [jax.ShapeDtypeStruct((r, s[2] - s[1]), dt) for s, dt in zip(segs, out_dtypes, strict=True)]
    out_specs = [pl.BlockSpec((tm, s[2] - s[1]), lambda i: (i, 0)) for s in segs]
    return pl.pallas_call(
        functools.partial(_ln_inproj_kernel, segs=segs),
        grid=(nt,),
        in_specs=[x_spec, mod_spec, mod_spec,
                  pl.BlockSpec((1, d), lambda i: (0, 0)),
                  pl.BlockSpec(w16.shape, lambda i: (0, 0), pipeline_mode=pl.Buffered(1))],
        out_specs=out_specs,
        out_shape=out_shape,
        compiler_params=_params("arbitrary"),
        name="ln_inproj",
    )(x, shift, scale, norm_g.reshape(1, d), w16)


def _out_proj_kernel(*refs, n_gated, n_plain, final_norm):
    it = iter(refs)
    y = None
    for _ in range(n_gated):
        a_ref, m_ref, w_ref = next(it), next(it), next(it)
        t = _dot((a_ref[...] * m_ref[...].astype(F32)).astype(BF16), w_ref[...])
        y = t if y is None else y + t
    for _ in range(n_plain):
        a_ref, w_ref = next(it), next(it)
        t = _dot(a_ref[...], w_ref[...])
        y = t if y is None else y + t
    x_ref, gate_ref = next(it), next(it)
    xn = x_ref[...] + gate_ref[...] * y.reshape(x_ref.shape)
    if final_norm:
        fg_ref = next(it)
        xn = xn * lax.rsqrt(jnp.mean(xn * xn, axis=-1, keepdims=True) + EPS) * fg_ref[...]
    o_ref = next(it)
    o_ref[...] = xn


def out_proj(gated, plain, x, gate, final_g, tm):
    d = x.shape[-1]
    if x.ndim == 2:
        nt = x.shape[0] // tm
        per_group = nt // gate.shape[0]
        x_spec = pl.BlockSpec((tm, d), lambda i: (i, 0))
        gate_spec = pl.BlockSpec((None, 1, d), lambda i: (i // per_group, 0, 0))
    else:
        n, t_new, _ = x.shape
        nt = n * t_new // tm
        x_spec = pl.BlockSpec((tm // t_new, t_new, d), lambda i: (i, 0, 0))
        gate_spec = pl.BlockSpec((tm // t_new, 1, d), lambda i: (i, 0, 0))
    args, specs = [], []
    for a, m, w in gated:
        args += [a, m, w]
        specs += [pl.BlockSpec((tm, a.shape[1]), lambda i: (i, 0)),
                  pl.BlockSpec((tm, m.shape[1]), lambda i: (i, 0)),
                  pl.BlockSpec(w.shape, lambda i: (0, 0))]
    for a, w in plain:
        args += [a, w]
        specs += [pl.BlockSpec((tm, a.shape[1]), lambda i: (i, 0)),
                  pl.BlockSpec(w.shape, lambda i: (0, 0))]
    args += [x, gate]
    specs += [x_spec, gate_spec]
    if final_g is not None:
        args.append(final_g.reshape(1, d))
        specs.append(pl.BlockSpec((1, d), lambda i: (0, 0)))
    return pl.pallas_call(
        functools.partial(_out_proj_kernel, n_gated=len(gated), n_plain=len(plain),
                          final_norm=final_g is not None),
        grid=(nt,),
        in_specs=specs,
        out_specs=x_spec,
        out_shape=jax.ShapeDtypeStruct(x.shape, F32),
        compiler_params=_params("arbitrary"),
        name="out_proj",
    )(*args)


def _top3_mask(scores, n_valid):
    nb = scores.shape[0]
    blk = lax.broadcasted_iota(jnp.int32, scores.shape, 0)
    s = jnp.where(blk < n_valid, scores, NEG_INF)
    picked = jnp.zeros(scores.shape, F32)
    for _ in range(MOBA_TOPK):
        mx = jnp.max(s, axis=0, keepdims=True)
        first = jnp.min(jnp.where(s == mx, blk, nb), axis=0, keepdims=True)
        hit = blk == first
        picked = jnp.where(hit, 1.0, picked)
        s = jnp.where(hit, NEG_INF, s)
    return jnp.where(blk < n_valid, picked, 0.0)


V_AUG = HEAD_DIM + 16


def _moba_prompt_kernel(q_ref, k_ref, v_ref, bown_ref, badj_ref, o_ref,
                        k16_s, vT_s, kbar_s, qT16_s, sel_s, m_s, acc_s,
                        s0_s, s1_s, p0_s, p1_s, a0_s, a1_s, *, nblk, chunk, sub):
    g = pl.program_id(1)
    i = pl.program_id(2)
    rows = GROUP * MOBA_BLOCK
    nch = rows // chunk

    @pl.when((g == 0) & (i == 0))
    def _per_batch():
        ones = jnp.ones((V_AUG - HEAD_DIM, MOBA_BLOCK), BF16)
        for j in range(nblk):
            kb = k_ref[j * MOBA_BLOCK:(j + 1) * MOBA_BLOCK, :]
            k16_s[j * MOBA_BLOCK:(j + 1) * MOBA_BLOCK, :] = kb.astype(BF16)
            kbar_s[j:j + 1, :] = jnp.mean(kb, axis=0, keepdims=True)
            v_t = v_ref[j * MOBA_BLOCK:(j + 1) * MOBA_BLOCK, :].T.astype(BF16)
            for kv in range(N_KV):
                vT_s[kv, j] = jnp.concatenate([v_t[kv * HEAD_DIM:(kv + 1) * HEAD_DIM], ones], axis=0)

    qT = (q_ref[...] * (SCALE * LOG2E)).T
    qs = jnp.concatenate([qT[h * HEAD_DIM:(h + 1) * HEAD_DIM, :] for h in range(GROUP)], axis=1)
    zeros = jnp.zeros_like(qs)
    qpad = jnp.where(g == 0, jnp.concatenate([qs, zeros], axis=0), jnp.concatenate([zeros, qs], axis=0))
    qpad16 = qpad.astype(BF16)
    for c in range(nch):
        qT16_s[c] = qpad16[:, c * chunk:(c + 1) * chunk]
    sel_s[...] = _top3_mask(_dot_f32(kbar_s[...], qpad), i)
    m_s[...] = jnp.full(m_s.shape, NEG_INF, F32)
    acc_s[...] = jnp.zeros(acc_s.shape, F32)

    n_far = i - 1
    sbuf, pbuf, abuf = (s0_s, s1_s), (p0_s, p1_s), (a0_s, a1_s)

    def blk_of(t):
        far_j = jnp.clip(t - 2, 0, jnp.maximum(n_far - 1, 0))
        return jnp.where(t == 0, i, jnp.where(t == 1, jnp.maximum(i - 1, 0), far_j))

    per = chunk // sub

    def stage_qk(t, slot, c):
        kb = k16_s[pl.ds(pl.multiple_of(blk_of(t) * MOBA_BLOCK, MOBA_BLOCK), MOBA_BLOCK), :]
        s_t = _dot(kb, qT16_s[c])
        for k in range(per):
            sbuf[slot][c * per + k] = s_t[:, k * sub:(k + 1) * sub]

    def stage_softmax(slot, c, on, bias_ref):
        ln = slice(c * sub, (c + 1) * sub)
        s_t = sbuf[slot][c]
        if bias_ref is not None:
            s_t = s_t + bias_ref[c]
        m_old = m_s[:, ln]
        m_new = jnp.maximum(m_old, jnp.where(on, jnp.max(s_t, axis=0, keepdims=True), NEG_INF))
        pbuf[slot][c] = jnp.exp2(s_t - jnp.where(on, m_new, jnp.inf)).astype(BF16)
        abuf[slot][:, ln] = jnp.exp2(m_old - m_new)
        m_s[:, ln] = m_new

    def stage_pv(t, slot, c):
        p = jnp.concatenate([pbuf[slot][c * per + k] for k in range(per)], axis=1)
        acc_s[c] = acc_s[c] * abuf[slot][:, c * chunk:(c + 1) * chunk] + _dot(vT_s[g, blk_of(t)], p)

    def pipe_half(t, slot, bias_ref=None, pv=True):
        own = bias_ref is bown_ref
        valid = None if own else jnp.where(t == 1, i >= 1, t - 2 < n_far)
        j = blk_of(t)
        for c in range(nch):
            stage_qk(t + 1, 1 - slot, c)
            if own:
                on = jnp.full((1, chunk), True)
            else:
                on = (sel_s[pl.ds(j, 1), c * chunk:(c + 1) * chunk] > 0.0) & valid
            for k in range(per):
                stage_softmax(slot, c * per + k, on[:, k * sub:(k + 1) * sub], bias_ref)
            if pv:
                stage_pv(t - 1, 1 - slot, c)

    for c in range(nch):
        stage_qk(0, 0, c)
    pipe_half(0, 0, bown_ref, pv=False)
    pipe_half(1, 1, badj_ref)

    def pair(u, carry):
        pipe_half(2 + 2 * u, 0)
        pipe_half(3 + 2 * u, 1)
        return carry

    n_pairs = (jnp.maximum(n_far, 0) + 1) // 2
    lax.fori_loop(0, n_pairs, pair, 0)
    for c in range(nch):
        stage_pv(1 + 2 * n_pairs, 1, c)

    acc = jnp.concatenate([acc_s[c] for c in range(nch)], axis=1)
    o_t = acc[:HEAD_DIM] / acc[HEAD_DIM:HEAD_DIM + 1]
    o_cat = jnp.concatenate([o_t[:, h * MOBA_BLOCK:(h + 1) * MOBA_BLOCK] for h in range(GROUP)], axis=0)
    o_ref[...] = o_cat.T.astype(o_ref.dtype)


def moba_prompt(q, k, v, rel_bias, batch, seq):
    nblk = seq // MOBA_BLOCK
    rows = GROUP * MOBA_BLOCK
    kk = np.arange(MOBA_BLOCK)[:, None]
    qq = (np.arange(rows) % MOBA_BLOCK)[None, :]
    bm_own = _bucket_matrix(qq - kk, qq >= kk)
    bm_adj = _bucket_matrix(MOBA_BLOCK + qq - kk, np.ones((MOBA_BLOCK, rows), bool))
    rbx = jnp.repeat(rel_bias.reshape(N_BUCKETS, N_KV, GROUP).transpose(1, 0, 2), MOBA_BLOCK, axis=2)
    far = rbx[:, N_BUCKETS - 1:N_BUCKETS, :]
    chunk, sub = 256, 128
    nsub = rows // sub

    def chunk_major(tbl):
        return tbl.reshape(N_KV, MOBA_BLOCK, nsub, sub).transpose(0, 2, 1, 3)

    b_own = chunk_major(bias_table(bm_own, rbx, far, LOG2E))
    b_adj = chunk_major(bias_table(bm_adj, rbx, far, LOG2E))
    qw = GROUP * HEAD_DIM
    tbl_spec = pl.BlockSpec((None, nsub, MOBA_BLOCK, sub), lambda b, g, i: (g, 0, 0, 0))
    return pl.pallas_call(
        functools.partial(_moba_prompt_kernel, nblk=nblk, chunk=chunk, sub=sub),
        grid=(batch, N_KV, nblk),
        in_specs=[pl.BlockSpec((MOBA_BLOCK, qw), lambda b, g, i: (b * nblk + i, g)),
                  pl.BlockSpec((None, seq, KV_W), lambda b, g, i: (b, 0, 0)),
                  pl.BlockSpec((None, seq, KV_W), lambda b, g, i: (b, 0, 0)),
                  tbl_spec, tbl_spec],
        out_specs=pl.BlockSpec((MOBA_BLOCK, qw), lambda b, g, i: (b * nblk + i, g)),
        out_shape=jax.ShapeDtypeStruct(q.shape, BF16),
        scratch_shapes=[pltpu.VMEM((seq, KV_W), BF16),
                        pltpu.VMEM((N_KV, nblk, V_AUG, MOBA_BLOCK), BF16),
                        pltpu.VMEM((nblk, KV_W), F32),
                        pltpu.VMEM((rows // chunk, KV_W, chunk), BF16),
                        pltpu.VMEM((nblk, rows), F32),
                        pltpu.VMEM((1, rows), F32),
                        pltpu.VMEM((rows // chunk, V_AUG, chunk), F32),
                        pltpu.VMEM((nsub, MOBA_BLOCK, sub), F32), pltpu.VMEM((nsub, MOBA_BLOCK, sub), F32),
                        pltpu.VMEM((nsub, MOBA_BLOCK, sub), BF16), pltpu.VMEM((nsub, MOBA_BLOCK, sub), BF16),
                        pltpu.VMEM((1, rows), F32), pltpu.VMEM((1, rows), F32)],
        compiler_params=_params("arbitrary", "arbitrary", "arbitrary"),
        name="moba_prompt",
    )(q, k, v, b_own, b_adj)


def _heads_to_rows(q):
    t = q.shape[0]
    lane = lax.broadcasted_iota(jnp.int32, (t, LANES), 1)
    pieces = []
    for h in range(N_HEADS):
        src = q[:, (h // 2) * LANES:(h // 2 + 1) * LANES]
        g = h // GROUP
        if h % 2 != g:
            src = pltpu.roll(src, HEAD_DIM, axis=1)
        keep = lane < HEAD_DIM if g == 0 else lane >= HEAD_DIM
        pieces.append(jnp.where(keep, src, 0.0))
    return jnp.concatenate(pieces, axis=0)


def _rows_to_heads(acc, t):
    lane = lax.broadcasted_iota(jnp.int32, (t, LANES), 1)
    cols = []
    for k in range(N_HEADS // 2):
        a = acc[2 * k * t:(2 * k + 1) * t, :]
        b = acc[(2 * k + 1) * t:(2 * k + 2) * t, :]
        if (2 * k) // GROUP == 1:
            a = pltpu.roll(a, HEAD_DIM, axis=1)
        else:
            b = pltpu.roll(b, HEAD_DIM, axis=1)
        cols.append(jnp.where(lane < HEAD_DIM, a, b))
    return jnp.concatenate(cols, axis=1)


def _pad_rows(x, rows):
    return jnp.concatenate([x, jnp.zeros((rows - x.shape[0], x.shape[1]), x.dtype)], axis=0)


def _moba_sample_kernel(pt_ref, q_ref, kn_ref, vn_ref, blast_ref, bown_ref, e_ref, kpool, vpool, o_ref,
                        kbuf, vbuf, ksem, vsem, s_s, *, n_pages, page, n_samples, unroll):
    s = pl.program_id(0)
    slot = s % 2
    ppb = MOBA_BLOCK // page
    nblk = n_pages // ppb
    t_new = q_ref.shape[0]
    rows = N_HEADS * t_new

    def k_copy(smp, sl, p):
        return pltpu.make_async_copy(kpool.at[pt_ref[smp, p]], kbuf.at[sl, p], ksem.at[sl])

    def v_copy(smp, sl, p):
        return pltpu.make_async_copy(vpool.at[pt_ref[smp, p]], vbuf.at[sl, p], vsem.at[sl])

    def start_all(smp, sl):
        def body(p, c):
            k_copy(smp, sl, p).start()
            v_copy(smp, sl, p).start()
            return c
        lax.fori_loop(0, n_pages, body, 0, unroll=8)

    def wait_all(copy, smp, sl):
        def body(p, c):
            copy(smp, sl, p).wait()
            return c
        lax.fori_loop(0, n_pages, body, 0, unroll=8)

    def block_t(buf, j):
        return jnp.concatenate([buf[slot, j * ppb + t] for t in range(ppb)], axis=1)

    blk_lane = lax.broadcasted_iota(jnp.int32, (KV_W, nblk), 1)

    @pl.when(s == 0)
    def _first():
        start_all(0, 0)

    wait_all(k_copy, s, slot)
    nxt = jnp.minimum(s + 1, n_samples - 1)

    def kbar_body(j, kbar_t):
        for t in range(ppb):
            k_copy(nxt, 1 - slot, j * ppb + t).start()
            v_copy(nxt, 1 - slot, j * ppb + t).start()
        pages = kbuf[slot, j * ppb]
        for t in range(1, ppb):
            pages = pages + kbuf[slot, j * ppb + t]
        col = jnp.sum(pages, axis=1, keepdims=True) * (1.0 / MOBA_BLOCK)
        return jnp.where(blk_lane == j, col, kbar_t)

    kbar_t = lax.fori_loop(0, nblk, kbar_body, jnp.zeros((KV_W, nblk), F32), unroll=2 * unroll)

    qbd = _heads_to_rows(q_ref[...] * (SCALE * LOG2E))
    qbd16 = qbd.astype(BF16)

    scores = _dot_f32(qbd, kbar_t)
    sel_t = _top3_mask(scores.T, nblk)
    negm = jnp.where(sel_t.T > 0.0, 0.0, MASK_NEG)
    lhs16 = jnp.concatenate([qbd, negm, jnp.zeros((rows, e_ref.shape[1] - nblk), F32)], axis=1).astype(BF16)

    def logits(j):
        rhs16 = jnp.concatenate([block_t(kbuf, j).astype(BF16), e_ref[j]], axis=0)
        return _dot(lhs16, rhs16)

    def fold(x):
        return x[:, :LANES], x[:, LANES:]

    def far(j, mrun):
        st = logits(j)
        s_s[j] = st
        lo, hi = fold(st)
        return jnp.maximum(mrun, jnp.maximum(lo, hi))

    mrun = lax.fori_loop(0, nblk - 1, far, jnp.full((rows, LANES), NEG_INF, F32), unroll=unroll - 1)
    s_last = logits(nblk - 1) + blast_ref[...]
    s_s[nblk - 1] = s_last
    kn16 = _pad_rows(kn_ref[...], LANES).astype(BF16)
    s_own = _dot_nt(qbd16, kn16) + bown_ref[...]
    for part in fold(s_last) + (s_own,):
        mrun = jnp.maximum(mrun, part)
    m = jnp.max(mrun, axis=1, keepdims=True)

    wait_all(v_copy, s, slot)

    def pv(j, carry):
        acc, lsum = carry
        p = jnp.exp2(s_s[j] - m)
        lo, hi = fold(p)
        return acc + _dot_nt(p.astype(BF16), block_t(vbuf, j).astype(BF16)), lsum + (lo + hi)

    p_own = jnp.exp2(s_own - m)
    acc0 = _dot(p_own.astype(BF16), _pad_rows(vn_ref[...], LANES).astype(BF16))
    acc, lsum = lax.fori_loop(0, nblk, pv, (acc0, p_own), unroll=unroll)
    den = jnp.sum(lsum, axis=1, keepdims=True)
    o_ref[...] = _rows_to_heads(acc / den, t_new)

    @pl.when(s == n_samples - 1)
    def _drain():
        wait_all(k_copy, nxt, 1 - slot)
        wait_all(v_copy, nxt, 1 - slot)


def moba_sample(q, k_new, v_new, k_pool, v_pool, page_table, rel_bias, t_new):
    n, n_pages = page_table.shape
    page = k_pool.shape[2]
    past = n_pages * page
    nblk = past // MOBA_BLOCK
    rows = N_HEADS * t_new
    tok = (np.arange(rows) % t_new)[None, :]
    kk = np.arange(MOBA_BLOCK)[:, None]
    bm_last = _bucket_matrix(MOBA_BLOCK + tok - kk, np.ones((MOBA_BLOCK, rows), bool))
    ko = np.arange(LANES)[:, None]
    bm_own = _bucket_matrix(tok - ko, (ko <= tok) & (ko < t_new))
    rbx = jnp.repeat(rel_bias, t_new, axis=1)[None]
    far = rbx[:, N_BUCKETS - 1:N_BUCKETS, :]
    b_last = bias_table(bm_last, rbx, far, LOG2E)[0].T
    b_own = bias_table(bm_own, rbx, far, LOG2E)[0].T
    onehot_np = np.zeros((nblk, KV_W, MOBA_BLOCK), np.float32)
    onehot_np[np.arange(nblk), np.arange(nblk)] = 1.0
    onehot = jnp.asarray(onehot_np, BF16)

    grid_spec = pltpu.PrefetchScalarGridSpec(
        num_scalar_prefetch=1,
        grid=(n,),
        in_specs=[pl.BlockSpec((t_new, q.shape[1]), lambda s, pt: (s, 0)),
                  pl.BlockSpec((t_new, KV_W), lambda s, pt: (s, 0)),
                  pl.BlockSpec((t_new, KV_W), lambda s, pt: (s, 0)),
                  pl.BlockSpec((rows, MOBA_BLOCK), lambda s, pt: (0, 0)),
                  pl.BlockSpec((rows, LANES), lambda s, pt: (0, 0)),
                  pl.BlockSpec((nblk, KV_W, MOBA_BLOCK), lambda s, pt: (0, 0, 0)),
                  pl.BlockSpec(memory_space=pl.ANY),
                  pl.BlockSpec(memory_space=pl.ANY)],
        out_specs=pl.BlockSpec((t_new, q.shape[1]), lambda s, pt: (s, 0)),
        scratch_shapes=[pltpu.VMEM((2, n_pages, KV_W, page), F32),
                        pltpu.VMEM((2, n_pages, KV_W, page), F32),
                        pltpu.SemaphoreType.DMA((2,)),
                        pltpu.SemaphoreType.DMA((2,)),
                        pltpu.VMEM((nblk, rows, MOBA_BLOCK), F32)],
    )
    return pl.pallas_call(
        functools.partial(_moba_sample_kernel, n_pages=n_pages, page=page, n_samples=n, unroll=8),
        grid_spec=grid_spec,
        out_shape=jax.ShapeDtypeStruct(q.shape, F32),
        compiler_params=_params("arbitrary"),
        name="moba_sample",
    )(page_table, q, k_new, v_new, b_last, b_own, onehot, k_pool, v_pool)


def _conv_tail(y, cb_ref, lg_ref, lb_ref, gate):
    y = y + cb_ref[...]
    mu = jnp.mean(y, axis=-1, keepdims=True)
    var = jnp.mean(jnp.square(y - mu), axis=-1, keepdims=True)
    yn = (y - mu) * lax.rsqrt(var + EPS) * lg_ref[...] + lb_ref[...]
    return _silu(yn) * gate


HALO = 32


SUBLANES = 8
LANES = 128
CONV_ROWS = 64


def _conv_prompt_kernel(u_ref, prev_ref, g_ref, w_ref, cb_ref, lg_ref, lb_ref, o_ref, xs, xr, ys, *, tl):
    t = pl.program_id(1)
    c = u_ref.shape[1]
    xs[0:HALO, :] = jnp.where(t > 0, prev_ref[...], 0.0)
    xs[HALO:, :] = u_ref[...]
    span = tl + HALO - SUBLANES
    for r in range(1, SUBLANES):
        xr[r - 1] = xs[pl.ds(r, span), :]
    off = HALO - (CONV_W - 1)

    def taps(base, lt):
        ln = slice(lt * LANES, (lt + 1) * LANES)
        acc = jnp.zeros((CONV_ROWS, LANES), F32)
        for k in range(CONV_W):
            a, r = divmod(off + k, SUBLANES)
            src = xs if r == 0 else xr.at[r - 1]
            start = base + a * SUBLANES
            acc = acc + src[start:start + CONV_ROWS, ln] * w_ref[k:k + 1, ln]
        ys[base:base + CONV_ROWS, ln] = acc

    for base in range(0, tl, CONV_ROWS):
        for lt in range(0, c // LANES, 2):
            @pl.when(t >= 0)
            def _():
                taps(base, lt)
                taps(base, lt + 1)
    o_ref[...] = _conv_tail(ys[...], cb_ref, lg_ref, lb_ref, g_ref[...].astype(F32)).astype(o_ref.dtype)


def conv_prompt(u, szb, conv_w, conv_b, ln_g, ln_b, batch, seq, tl=256):
    c = u.shape[1]
    nt = seq // tl
    vec = pl.BlockSpec((1, c), lambda b, t: (0, 0))
    return pl.pallas_call(
        functools.partial(_conv_prompt_kernel, tl=tl),
        grid=(batch, nt),
        in_specs=[pl.BlockSpec((tl, c), lambda b, t: (b * nt + t, 0)),
                  pl.BlockSpec((HALO, c), lambda b, t: (jnp.maximum((b * nt + t) * (tl // HALO) - 1, 0), 0)),
                  pl.BlockSpec((tl, c), lambda b, t: (b * nt + t, 0)),
                  pl.BlockSpec((CONV_W, c), lambda b, t: (0, 0)), vec, vec, vec],
        out_specs=pl.BlockSpec((tl, c), lambda b, t: (b * nt + t, 0)),
        out_shape=jax.ShapeDtypeStruct(u.shape, BF16),
        scratch_shapes=[pltpu.VMEM((HALO + tl, c), F32),
                        pltpu.VMEM((SUBLANES - 1, HALO + tl - SUBLANES, c), F32),
                        pltpu.VMEM((tl, c), F32)],
        compiler_params=_params("arbitrary", "arbitrary"),
        name="conv_prompt",
    )(u, u, szb, conv_w, conv_b.reshape(1, c), ln_g.reshape(1, c), ln_b.reshape(1, c))


def _conv_sample_kernel(xp_ref, g_ref, w_ref, cb_ref, lg_ref, lb_ref, o_ref, *, t_new):
    ns, _, c = xp_ref.shape
    y = jnp.zeros((ns, t_new, c), F32)
    for k in range(CONV_W):
        y = y + xp_ref[:, pl.ds(k, t_new), :] * w_ref[k:k + 1, :]
    o_ref[...] = _conv_tail(y, cb_ref, lg_ref, lb_ref, g_ref[...].astype(F32)).astype(o_ref.dtype)


def conv_sample(xp, szb, conv_w, conv_b, ln_g, ln_b, t_new, ns=8):
    n, rows, c = xp.shape
    vec = pl.BlockSpec((1, c), lambda i: (0, 0))
    return pl.pallas_call(
        functools.partial(_conv_sample_kernel, t_new=t_new),
        grid=(n // ns,),
        in_specs=[pl.BlockSpec((ns, rows, c), lambda i: (i, 0, 0)),
                  pl.BlockSpec((ns, t_new, c), lambda i: (i, 0, 0)),
                  pl.BlockSpec((CONV_W, c), lambda i: (0, 0)), vec, vec, vec],
        out_specs=pl.BlockSpec((ns, t_new, c), lambda i: (i, 0, 0)),
        out_shape=jax.ShapeDtypeStruct((n, t_new, c), BF16),
        compiler_params=_params("arbitrary"),
        name="conv_sample",
    )(xp, szb, conv_w, conv_b.reshape(1, c), ln_g.reshape(1, c), ln_b.reshape(1, c))


def _swa_prompt_kernel(q_ref, kc_ref, kp_ref, vc_ref, vp_ref, bias_ref, sink_ref, o_ref, s_s, p_s, *, chunk, sub):
    n = pl.program_id(1)
    w = SWA_WINDOW
    rows = GROUP * w
    nch, per = rows // chunk, chunk // sub
    gw = GROUP * HEAD_DIM
    kcat = jnp.concatenate([kp_ref[...], kc_ref[...]], axis=0).astype(BF16)
    v_t = jnp.concatenate([vp_ref[...], vc_ref[...]], axis=0).T.astype(BF16)
    ones = jnp.ones((V_AUG - HEAD_DIM, 2 * w), BF16)
    no_prev = (lax.broadcasted_iota(jnp.int32, (2 * w, sub), 0) < w) & (n == 0)
    qpads, v_augs, sinks = [], [], []
    for g in range(N_KV):
        q_t = (q_ref[:, g * gw:(g + 1) * gw] * (SCALE * LOG2E)).T
        qs = jnp.concatenate([q_t[h * HEAD_DIM:(h + 1) * HEAD_DIM, :] for h in range(GROUP)], axis=1)
        zeros = jnp.zeros_like(qs)
        qpads.append(jnp.concatenate([qs, zeros] if g == 0 else [zeros, qs], axis=0).astype(BF16))
        v_augs.append(jnp.concatenate([v_t[g * HEAD_DIM:(g + 1) * HEAD_DIM], ones], axis=0))
        sinks.append(sink_ref[g] * LOG2E)

    def qk(g, c):
        s_t = _dot(kcat, qpads[g][:, c * chunk:(c + 1) * chunk])
        for k in range(per):
            s_s[g, c * per + k] = s_t[:, k * sub:(k + 1) * sub]

    def softmax(g, cs):
        s_t = jnp.where(no_prev, NEG_INF, s_s[g, cs] + bias_ref[g, cs])
        m = jnp.maximum(jnp.max(s_t, axis=0, keepdims=True), sinks[g][:, cs * sub:(cs + 1) * sub])
        p_s[g, cs] = jnp.exp2(s_t - m).astype(BF16)
        return m

    def pv(g, c, ms):
        p = jnp.concatenate([p_s[g, c * per + k] for k in range(per)], axis=1)
        o_t = _dot(v_augs[g], p)
        den = o_t[HEAD_DIM:HEAD_DIM + 1] + jnp.exp2(sinks[g][:, c * chunk:(c + 1) * chunk] - jnp.concatenate(ms, axis=1))
        return o_t[:HEAD_DIM] / den

    for c in range(nch):
        qk(0, c)
    ms = [[], []]
    outs = [[], []]
    for c in range(nch):
        qk(1, c)
        ms[0] += [softmax(0, c * per + k) for k in range(per)]
    for c in range(nch):
        outs[0].append(pv(0, c, ms[0][c * per:(c + 1) * per]))
        ms[1] += [softmax(1, c * per + k) for k in range(per)]
    for c in range(nch):
        outs[1].append(pv(1, c, ms[1][c * per:(c + 1) * per]))
    tiles = []
    for g in range(N_KV):
        o_t = jnp.concatenate(outs[g], axis=1)
        tiles.append(jnp.concatenate([o_t[:, h * w:(h + 1) * w] for h in range(GROUP)], axis=0).T)
    o_ref[...] = jnp.concatenate(tiles, axis=1).astype(o_ref.dtype)


def swa_prompt(q, k, v, sinks, rel_bias, batch, seq):
    w = SWA_WINDOW
    nb = seq // w
    rows = GROUP * w
    kidx = np.arange(2 * w)[:, None]
    qq = (np.arange(rows) % w)[None, :]
    dist = w + qq - kidx
    bm = _bucket_matrix(dist, (dist >= 0) & (dist < w))
    rbx = jnp.repeat(rel_bias.reshape(N_BUCKETS, N_KV, GROUP).transpose(1, 0, 2), w, axis=2)
    chunk, sub = 256, 128
    nsub = rows // sub
    bias = bias_table(bm, rbx, jnp.zeros((N_KV, 1, rows), F32), LOG2E)
    bias = bias.reshape(N_KV, 2 * w, nsub, sub).transpose(0, 2, 1, 3)
    sink_x = jnp.repeat(sinks.reshape(N_KV, 1, GROUP), w, axis=2)
    d = q.shape[1]
    cur = lambda b, n: (b * nb + n, 0)
    prev = lambda b, n: (jnp.maximum(b * nb + n - 1, 0), 0)
    return pl.pallas_call(
        functools.partial(_swa_prompt_kernel, chunk=chunk, sub=sub),
        grid=(batch, nb),
        in_specs=[pl.BlockSpec((w, d), cur),
                  pl.BlockSpec((w, KV_W), cur), pl.BlockSpec((w, KV_W), prev),
                  pl.BlockSpec((w, KV_W), cur), pl.BlockSpec((w, KV_W), prev),
                  pl.BlockSpec((N_KV, nsub, 2 * w, sub), lambda b, n: (0, 0, 0, 0)),
                  pl.BlockSpec((N_KV, 1, rows), lambda b, n: (0, 0, 0))],
        out_specs=pl.BlockSpec((w, d), cur),
        out_shape=jax.ShapeDtypeStruct(q.shape, BF16),
        scratch_shapes=[pltpu.VMEM((N_KV, nsub, 2 * w, sub), F32),
                        pltpu.VMEM((N_KV, nsub, 2 * w, sub), BF16)],
        compiler_params=_params("arbitrary", "arbitrary"),
        name="swa_prompt",
    )(q, k, k, v, v, bias, sink_x)


def _swa_sample_kernel(q_ref, kb_ref, kn_ref, vb_ref, vn_ref, bbuf_ref, bnew_ref, sink_ref, o_ref, *, t_new):
    ns = kb_ref.shape[0]
    sink = sink_ref[...]
    for i in range(ns):
        rs = slice(i * t_new, (i + 1) * t_new)
        qbd = _heads_to_rows(q_ref[rs, :] * SCALE).astype(BF16)
        s_buf = _dot_nt(qbd, kb_ref[i].astype(BF16)) + bbuf_ref[...]
        s_new = _dot_nt(qbd, _pad_rows(kn_ref[rs, :], LANES).astype(BF16)) + bnew_ref[...]
        m = jnp.maximum(jnp.maximum(jnp.max(s_buf, axis=1, keepdims=True), jnp.max(s_new, axis=1, keepdims=True)),
                        sink)
        p_buf, p_new = jnp.exp(s_buf - m), jnp.exp(s_new - m)
        den = jnp.sum(p_buf, axis=1, keepdims=True) + jnp.sum(p_new, axis=1, keepdims=True) + jnp.exp(sink - m)
        acc = (_dot(p_buf.astype(BF16), vb_ref[i].astype(BF16))
               + _dot(p_new.astype(BF16), _pad_rows(vn_ref[rs, :], LANES).astype(BF16)))
        o_ref[rs, :] = _rows_to_heads(acc / den, t_new)


def swa_sample(q, k_buf, v_buf, k_new, v_new, sinks, rel_bias, t_new, ns=4):
    n, wb, _ = k_buf.shape
    rows = N_HEADS * t_new
    tok = (np.arange(rows) % t_new)[None, :]
    dist_buf = tok + wb - np.arange(wb)[:, None]
    j_new = np.arange(LANES)[:, None]
    dist_new = tok - j_new
    bm_buf = _bucket_matrix(dist_buf, (dist_buf >= 0) & (dist_buf < SWA_WINDOW))
    bm_new = _bucket_matrix(dist_new, (dist_new >= 0) & (dist_new < SWA_WINDOW) & (j_new < t_new))
    rbx = jnp.repeat(rel_bias, t_new, axis=1)[None]
    zero = jnp.zeros((1, 1, rows), F32)
    b_buf = bias_table(bm_buf, rbx, zero)[0].T
    b_new = bias_table(bm_new, rbx, zero)[0].T
    sink_r = jnp.repeat(sinks, t_new).reshape(rows, 1)
    d = q.shape[1]
    tile = lambda width: pl.BlockSpec((ns * t_new, width), lambda i: (i, 0))
    buf = pl.BlockSpec((ns, wb, KV_W), lambda i: (i, 0, 0))
    return pl.pallas_call(
        functools.partial(_swa_sample_kernel, t_new=t_new),
        grid=(n // ns,),
        in_specs=[tile(d), buf, tile(KV_W), buf, tile(KV_W),
                  pl.BlockSpec((rows, wb), lambda i: (0, 0)),
                  pl.BlockSpec((rows, LANES), lambda i: (0, 0)),
                  pl.BlockSpec((rows, 1), lambda i: (0, 0))],
        out_specs=tile(d),
        out_shape=jax.ShapeDtypeStruct(q.shape, F32),
        compiler_params=_params("arbitrary"),
        name="swa_sample",
    )(q, k_buf, k_new, v_buf, v_new, b_buf, b_new, sink_r)


def kernel(x_prompt, x_sample, c_prompt, c_sample, cache_a_k, cache_a_v, page_table, cache_b_conv, cache_c_k, cache_c_v, rel_bias, norm_a, mod_w_a, mod_b_a, w_in_a, conv_w_b, conv_b_b, ln_g_b, ln_b_b, w_out_a, norm_c, mod_w_c, mod_b_c, w_in_c, sinks_c, w_out_c, final_norm):
    batch, seq, d = x_prompt.shape
    n, t_new, _ = x_sample.shape
    n_pool, page = cache_a_k.shape[1], cache_a_k.shape[2]
    hq = N_HEADS * HEAD_DIM
    cb = conv_w_b.shape[2]
    assert norm_a.shape[0] == 1 and norm_c.shape[0] == 1, "one A/B layer followed by one C layer"

    tm_p = 1024
    tm_s = min(256, n * t_new)
    xp = x_prompt.reshape(batch * seq, d)
    xs = x_sample

    c_all = jnp.concatenate([c_prompt, c_sample], axis=0)
    c_rows = -(-c_all.shape[0] // 8) * 8
    c_all = jnp.pad(c_all, ((0, c_rows - c_all.shape[0]), (0, 0)))

    def split_mod(m):
        mp = m[:batch].reshape(batch, 1, 3, d)
        ms = m[batch:batch + n].reshape(n, 1, 3, d)
        return [(mp[:, :, j], ms[:, :, j]) for j in range(3)]

    (sh_a, sc_a, gt_a) = split_mod(modulation(c_all, mod_w_a[0], mod_b_a[0]))
    (sh_c, sc_c, gt_c) = split_mod(modulation(c_all, mod_w_c[0], mod_b_c[0]))

    kv0, kv1 = hq, hq + 2 * KV_W
    za0 = kv1
    ga0 = za0 + hq
    gb0 = ga0 + cb
    zb0 = gb0 + cb
    segs_a = (("raw", 0, hq), ("raw", kv0, kv0 + KV_W), ("raw", kv0 + KV_W, kv1),
              ("silu", za0, ga0), ("glu", ga0, gb0, gb0, zb0), ("silu", zb0, zb0 + cb))
    dts_a = (F32, F32, F32, BF16, F32, BF16)
    w_in_a16 = w_in_a[0].astype(BF16)
    w_out_a16 = w_out_a[0].astype(BF16)
    k_pool = cache_a_k[0].transpose(0, 2, 3, 1).reshape(n_pool, KV_W, page)
    v_pool = cache_a_v[0].transpose(0, 2, 3, 1).reshape(n_pool, KV_W, page)

    q, k, v, sza, u, szb = ln_inproj(xp, sh_a[0], sc_a[0], norm_a[0], w_in_a16, segs_a, dts_a, tm_p)
    oa = moba_prompt(q, k.reshape(batch, seq, KV_W), v.reshape(batch, seq, KV_W), rel_bias, batch, seq)
    ob = conv_prompt(u, szb, conv_w_b[0], conv_b_b[0], ln_g_b[0], ln_b_b[0], batch, seq)
    xp1 = out_proj([(oa, sza, w_out_a16[:hq])], [(ob, w_out_a16[hq:])], xp, gt_a[0], None, tm_p)
    ak_p, av_p = k, v
    bc_p = u.reshape(batch, seq, cb)[:, seq - (CONV_W - 1):]

    q, k, v, sza, u, szb = ln_inproj(xs, sh_a[1], sc_a[1], norm_a[0], w_in_a16, segs_a, dts_a, tm_s)
    oa = moba_sample(q, k, v, k_pool, v_pool, page_table, rel_bias, t_new)
    xcat = jnp.concatenate([cache_b_conv[0], u.reshape(n, t_new, cb)], axis=1)
    ob = conv_sample(xcat, szb.reshape(n, t_new, cb), conv_w_b[0], conv_b_b[0], ln_g_b[0], ln_b_b[0], t_new)
    xs1 = out_proj([(oa, sza, w_out_a16[:hq])], [(ob.reshape(n * t_new, cb), w_out_a16[hq:])],
                   xs, gt_a[1], None, tm_s)
    ak_s, av_s = k, v
    bc_s = xcat[:, t_new:]

    segs_c = (("raw", 0, hq), ("raw", hq, hq + KV_W), ("raw", hq + KV_W, hq + 2 * KV_W),
              ("silu", hq + 2 * KV_W, 2 * hq + 2 * KV_W))
    dts_c = (F32, F32, F32, BF16)
    w_in_c16 = w_in_c[0].astype(BF16)
    w_out_c16 = w_out_c[0].astype(BF16)

    q, k, v, sz = ln_inproj(xp1, sh_c[0], sc_c[0], norm_c[0], w_in_c16, segs_c, dts_c, tm_p)
    o = swa_prompt(q, k, v, sinks_c[0], rel_bias, batch, seq)
    y_prompt = out_proj([(o, sz, w_out_c16)], [], xp1, gt_c[0], final_norm, tm_p)
    wb_p = min(SWA_WINDOW, seq)
    ck_p = k.reshape(batch, seq, KV_W)[:, seq - wb_p:]
    cv_p = v.reshape(batch, seq, KV_W)[:, seq - wb_p:]

    q, k, v, sz = ln_inproj(xs1, sh_c[1], sc_c[1], norm_c[0], w_in_c16, segs_c, dts_c, tm_s)
    wb_s = cache_c_k.shape[2]
    kb, vb = cache_c_k[0].reshape(n, wb_s, KV_W), cache_c_v[0].reshape(n, wb_s, KV_W)
    o = swa_sample(q, kb, vb, k, v, sinks_c[0], rel_bias, t_new)
    y_sample = out_proj([(o, sz, w_out_c16)], [], xs1, gt_c[1], final_norm, tm_s)
    ck_s = jnp.concatenate([kb, k.reshape(n, t_new, KV_W)], axis=1)[:, t_new:]
    cv_s = jnp.concatenate([vb, v.reshape(n, t_new, KV_W)], axis=1)[:, t_new:]

    def kv5(a, lead):
        return a.reshape((1,) + lead + (N_KV, HEAD_DIM))

    return (y_prompt.reshape(batch, seq, d), y_sample,
            kv5(ak_p, (batch, seq)), kv5(av_p, (batch, seq)),
            kv5(ak_s, (n, t_new)), kv5(av_s, (n, t_new)),
            bc_p[None], bc_s[None],
            kv5(ck_p, (batch, wb_p)), kv5(cv_p, (batch, wb_p)),
            kv5(ck_s, (n, wb_s)), kv5(cv_s, (n, wb_s)))
```

```python
import functools
import math

import jax
import jax.numpy as jnp
import numpy as np
from jax import lax
from jax.experimental import pallas as pl
from jax.experimental.pallas import tpu as pltpu

F32 = jnp.float32
BF16 = jnp.bfloat16
NEG_INF = float("-inf")
MASK_NEG = -1e30

HEAD_DIM = 64
N_HEADS = 16
N_KV = 2
GROUP = N_HEADS // N_KV
KV_W = N_KV * HEAD_DIM
MOBA_BLOCK = 256
MOBA_TOPK = 3
CONV_W = 31
SWA_WINDOW = 128
N_BUCKETS = 32
MAX_DISTANCE = 128
EPS = 1e-6
SCALE = HEAD_DIM ** -0.5
LOG2E = math.log2(math.e)

VMEM_LIMIT = 56 * 2**20


def _params(*sem):
    return pltpu.CompilerParams(dimension_semantics=sem, vmem_limit_bytes=VMEM_LIMIT)


def _silu(z):
    return z * jax.nn.sigmoid(z)


def _dot(a, b):
    return jnp.dot(a, b, preferred_element_type=F32)


def _dot_nt(a, b):
    return lax.dot_general(a, b, (((1,), (1,)), ((), ())), preferred_element_type=F32)


def _dot_f32(a, b):
    return jnp.dot(a, b, preferred_element_type=F32, precision=lax.Precision.HIGHEST)


def _t5_bucket_np(n):
    n = np.maximum(n, 0)
    max_exact = N_BUCKETS // 2
    nf = np.maximum(n, 1).astype(np.float32)
    large = max_exact + (np.log(nf / np.float32(max_exact)) / np.float32(math.log(MAX_DISTANCE / max_exact))
                         * np.float32(N_BUCKETS - max_exact)).astype(np.int32)
    large = np.minimum(large, N_BUCKETS - 1)
    return np.where(n < max_exact, n, large).astype(np.int32)


def _bucket_matrix(dist, valid):
    return np.where(valid, _t5_bucket_np(dist), -1).astype(np.int32)


def _bias_kernel(bm_ref, rb_ref, sub_ref, o_ref, *, mul):
    bm = bm_ref[...]
    acc = jnp.zeros(bm.shape, F32)
    for b in range(N_BUCKETS):
        acc = jnp.where(bm == b, rb_ref[b:b + 1, :], acc)
    o_ref[...] = jnp.where(bm < 0, NEG_INF, (acc - sub_ref[...]) * mul)


def bias_table(bm, rbx, sub, mul=1.0):
    g, _, c = rbx.shape
    r = bm.shape[0]
    return pl.pallas_call(
        functools.partial(_bias_kernel, mul=mul),
        grid=(g,),
        in_specs=[pl.BlockSpec((r, c), lambda i: (0, 0)),
                  pl.BlockSpec((None, N_BUCKETS, c), lambda i: (i, 0, 0)),
                  pl.BlockSpec((None, 1, c), lambda i: (i, 0, 0))],
        out_specs=pl.BlockSpec((None, r, c), lambda i: (i, 0, 0)),
        out_shape=jax.ShapeDtypeStruct((g, r, c), F32),
        compiler_params=_params("arbitrary"),
        name="bias_table",
    )(jnp.asarray(bm), rbx, sub)


def _mod_kernel(c_ref, w_ref, b_ref, o_ref):
    o_ref[...] = _dot_f32(_silu(c_ref[...]), w_ref[...]) + b_ref[...]


def modulation(c, w, b):
    n, d = c.shape
    m = w.shape[1]
    tn = 512
    return pl.pallas_call(
        _mod_kernel,
        grid=(m // tn,),
        in_specs=[pl.BlockSpec((n, d), lambda j: (0, 0)),
                  pl.BlockSpec((d, tn), lambda j: (0, j)),
                  pl.BlockSpec((1, tn), lambda j: (0, j))],
        out_specs=pl.BlockSpec((n, tn), lambda j: (0, j)),
        out_shape=jax.ShapeDtypeStruct((n, m), F32),
        compiler_params=_params("arbitrary"),
        name="modulation",
    )(c, w, b.reshape(1, m))


def _ln_inproj_kernel(x_ref, shift_ref, scale_ref, g_ref, w_ref, *out_refs, segs):
    x = x_ref[...]
    y = x * lax.rsqrt(jnp.mean(x * x, axis=-1, keepdims=True) + EPS)
    h = (y * g_ref[...]) * (1.0 + scale_ref[...]) + shift_ref[...]
    h16 = h.reshape(-1, h.shape[-1]).astype(BF16)
    for o_ref, seg in zip(out_refs, segs, strict=True):
        kind, lo, hi = seg[0], seg[1], seg[2]
        z = _dot(h16, w_ref[:, lo:hi])
        if kind == "silu":
            z = _silu(z)
        elif kind == "glu":
            z = z * jax.nn.sigmoid(_dot(h16, w_ref[:, seg[3]:seg[4]]))
        o_ref[...] = z.astype(o_ref.dtype)


def ln_inproj(x, shift, scale, norm_g, w16, segs, out_dtypes, tm):
    d = x.shape[-1]
    if x.ndim == 2:
        r = x.shape[0]
        nt = r // tm
        per_group = nt // shift.shape[0]
        x_spec = pl.BlockSpec((tm, d), lambda i: (i, 0))
        mod_spec = pl.BlockSpec((None, 1, d), lambda i: (i // per_group, 0, 0))
    else:
        n, t_new, _ = x.shape
        r = n * t_new
        nt = r // tm
        x_spec = pl.BlockSpec((tm // t_new, t_new, d), lambda i: (i, 0, 0))
        mod_spec = pl.BlockSpec((tm // t_new, 1, d), lambda i: (i, 0, 0))
    out_shape = [jax.ShapeDtypeStruct((r, s[2] - s[1]), dt) for s, dt in zip(segs, out_dtypes, strict=True)]
    out_specs = [pl.BlockSpec((tm, s[2] - s[1]), lambda i: (i, 0)) for s in segs]
    return pl.pallas_call(
        functools.partial(_ln_inproj_kernel, segs=segs),
        grid=(nt,),
        in_specs=[x_spec, mod_spec, mod_spec,
                  pl.BlockSpec((1, d), lambda i: (0, 0)),
                  pl.BlockSpec(w16.shape, lambda i: (0, 0), pipeline_mode=pl.Buffered(1))],
        out_specs=out_specs,
        out_shape=out_shape,
        compiler_params=_params("arbitrary"),
        name="ln_inproj",
    )(x, shift, scale, norm_g.reshape(1, d), w16)


def _out_proj_kernel(*refs, n_gated, n_plain, final_norm):
    it = iter(refs)
    y = None
    for _ in range(n_gated):
        a_ref, m_ref, w_ref = next(it), next(it), next(it)
        t = _dot((a_ref[...] * m_ref[...].astype(F32)).astype(BF16), w_ref[...])
        y = t if y is None else y + t
    for _ in range(n_plain):
        a_ref, w_ref = next(it), next(it)
        t = _dot(a_ref[...], w_ref[...])
        y = t if y is None else y + t
    x_ref, gate_ref = next(it), next(it)
    xn = x_ref[...] + gate_ref[...] * y.reshape(x_ref.shape)
    if final_norm:
        fg_ref = next(it)
        xn = xn * lax.rsqrt(jnp.mean(xn * xn, axis=-1, keepdims=True) + EPS) * fg_ref[...]
    o_ref = next(it)
    o_ref[...] = xn


def out_proj(gated, plain, x, gate, final_g, tm):
    d = x.shape[-1]
    if x.ndim == 2:
        nt = x.shape[0] // tm
        per_group = nt // gate.shape[0]
        x_spec = pl.BlockSpec((tm, d), lambda i: (i, 0))
        gate_spec = pl.BlockSpec((None, 1, d), lambda i: (i // per_group, 0, 0))
    else:
        n, t_new, _ = x.shape
        nt = n * t_new // tm
        x_spec = pl.BlockSpec((tm // t_new, t_new, d), lambda i: (i, 0, 0))
        gate_spec = pl.BlockSpec((tm // t_new, 1, d), lambda i: (i, 0, 0))
    args, specs = [], []
    for a, m, w in gated:
        args += [a, m, w]
        specs += [pl.BlockSpec((tm, a.shape[1]), lambda i: (i, 0)),
                  pl.BlockSpec((tm, m.shape[1]), lambda i: (i, 0)),
                  pl.BlockSpec(w.shape, lambda i: (0, 0))]
    for a, w in plain:
        args += [a, w]
        specs += [pl.BlockSpec((tm, a.shape[1]), lambda i: (i, 0)),
                  pl.BlockSpec(w.shape, lambda i: (0, 0))]
    args += [x, gate]
    specs += [x_spec, gate_spec]
    if final_g is not None:
        args.append(final_g.reshape(1, d))
        specs.append(pl.BlockSpec((1, d), lambda i: (0, 0)))
    return pl.pallas_call(
        functools.partial(_out_proj_kernel, n_gated=len(gated), n_plain=len(plain),
                          final_norm=final_g is not None),
        grid=(nt,),
        in_specs=specs,
        out_specs=x_spec,
        out_shape=jax.ShapeDtypeStruct(x.shape, F32),
        compiler_params=_params("arbitrary"),
        name="out_proj",
    )(*args)


def _top3_mask(scores, n_valid):
    nb = scores.shape[0]
    blk = lax.broadcasted_iota(jnp.int32, scores.shape, 0)
    s = jnp.where(blk < n_valid, scores, NEG_INF)
    picked = jnp.zeros(scores.shape, F32)
    for _ in range(MOBA_TOPK):
        mx = jnp.max(s, axis=0, keepdims=True)
        first = jnp.min(jnp.where(s == mx, blk, nb), axis=0, keepdims=True)
        hit = blk == first
        picked = jnp.where(hit, 1.0, picked)
        s = jnp.where(hit, NEG_INF, s)
    return jnp.where(blk < n_valid, picked, 0.0)


V_AUG = HEAD_DIM + 16


def _moba_prompt_kernel(q_ref, k_ref, v_ref, bown_ref, badj_ref, o_ref,
                        k16_s, vT_s, kbar_s, qT16_s, sel_s, m_s, acc_s,
                        s0_s, s1_s, p0_s, p1_s, a0_s, a1_s, *, nblk, chunk, sub):
    g = pl.program_id(1)
    i = pl.program_id(2)
    rows = GROUP * MOBA_BLOCK
    nch = rows // chunk

    @pl.when((g == 0) & (i == 0))
    def _per_batch():
        ones = jnp.ones((V_AUG - HEAD_DIM, MOBA_BLOCK), BF16)
        for j in range(nblk):
            kb = k_ref[j * MOBA_BLOCK:(j + 1) * MOBA_BLOCK, :]
            k16_s[j * MOBA_BLOCK:(j + 1) * MOBA_BLOCK, :] = kb.astype(BF16)
            kbar_s[j:j + 1, :] = jnp.mean(kb, axis=0, keepdims=True)
            v_t = v_ref[j * MOBA_BLOCK:(j + 1) * MOBA_BLOCK, :].T.astype(BF16)
            for kv in range(N_KV):
                vT_s[kv, j] = jnp.concatenate([v_t[kv * HEAD_DIM:(kv + 1) * HEAD_DIM], ones], axis=0)

    qT = (q_ref[...] * (SCALE * LOG2E)).T
    qs = jnp.concatenate([qT[h * HEAD_DIM:(h + 1) * HEAD_DIM, :] for h in range(GROUP)], axis=1)
    zeros = jnp.zeros_like(qs)
    qpad = jnp.where(g == 0, jnp.concatenate([qs, zeros], axis=0), jnp.concatenate([zeros, qs], axis=0))
    qpad16 = qpad.astype(BF16)
    for c in range(nch):
        qT16_s[c] = qpad16[:, c * chunk:(c + 1) * chunk]
    sel_s[...] = _top3_mask(_dot_f32(kbar_s[...], qpad), i)
    m_s[...] = jnp.full(m_s.shape, NEG_INF, F32)
    acc_s[...] = jnp.zeros(acc_s.shape, F32)

    n_far = i - 1
    sbuf, pbuf, abuf = (s0_s, s1_s), (p0_s, p1_s), (a0_s, a1_s)

    def blk_of(t):
        far_j = jnp.clip(t - 2, 0, jnp.maximum(n_far - 1, 0))
        return jnp.where(t == 0, i, jnp.where(t == 1, jnp.maximum(i - 1, 0), far_j))

    per = chunk // sub

    def stage_qk(t, slot, c):
        kb = k16_s[pl.ds(pl.multiple_of(blk_of(t) * MOBA_BLOCK, MOBA_BLOCK), MOBA_BLOCK), :]
        s_t = _dot(kb, qT16_s[c])
        for k in range(per):
            sbuf[slot][c * per + k] = s_t[:, k * sub:(k + 1) * sub]

    def stage_softmax(slot, c, on, bias_ref):
        ln = slice(c * sub, (c + 1) * sub)
        s_t = sbuf[slot][c]
        if bias_ref is not None:
            s_t = s_t + bias_ref[c]
        m_old = m_s[:, ln]
        m_new = jnp.maximum(m_old, jnp.where(on, jnp.max(s_t, axis=0, keepdims=True), NEG_INF))
        pbuf[slot][c] = jnp.exp2(s_t - jnp.where(on, m_new, jnp.inf)).astype(BF16)
        abuf[slot][:, ln] = jnp.exp2(m_old - m_new)
        m_s[:, ln] = m_new

    def stage_pv(t, slot, c):
        p = jnp.concatenate([pbuf[slot][c * per + k] for k in range(per)], axis=1)
        acc_s[c] = acc_s[c] * abuf[slot][:, c * chunk:(c + 1) * chunk] + _dot(vT_s[g, blk_of(t)], p)

    def pipe_half(t, slot, bias_ref=None, pv=True):
        own = bias_ref is bown_ref
        valid = None if own else jnp.where(t == 1, i >= 1, t - 2 < n_far)
        j = blk_of(t)
        for c in range(nch):
            stage_qk(t + 1, 1 - slot, c)
            if own:
                on = jnp.full((1, chunk), True)
            else:
                on = (sel_s[pl.ds(j, 1), c * chunk:(c + 1) * chunk] > 0.0) & valid
            for k in range(per):
                stage_softmax(slot, c * per + k, on[:, k * sub:(k + 1) * sub], bias_ref)
            if pv:
                stage_pv(t - 1, 1 - slot, c)

    for c in range(nch):
        stage_qk(0, 0, c)
    pipe_half(0, 0, bown_ref, pv=False)
    pipe_half(1, 1, badj_ref)

    def pair(u, carry):
        pipe_half(2 + 2 * u, 0)
        pipe_half(3 + 2 * u, 1)
        return carry

    n_pairs = (jnp.maximum(n_far, 0) + 1) // 2
    lax.fori_loop(0, n_pairs, pair, 0)
    for c in range(nch):
        stage_pv(1 + 2 * n_pairs, 1, c)

    acc = jnp.concatenate([acc_s[c] for c in range(nch)], axis=1)
    o_t = acc[:HEAD_DIM] / acc[HEAD_DIM:HEAD_DIM + 1]
    o_cat = jnp.concatenate([o_t[:, h * MOBA_BLOCK:(h + 1) * MOBA_BLOCK] for h in range(GROUP)], axis=0)
    o_ref[...] = o_cat.T.astype(o_ref.dtype)


def moba_prompt(q, k, v, rel_bias, batch, seq):
    nblk = seq // MOBA_BLOCK
    rows = GROUP * MOBA_BLOCK
    kk = np.arange(MOBA_BLOCK)[:, None]
    qq = (np.arange(rows) % MOBA_BLOCK)[None, :]
    bm_own = _bucket_matrix(qq - kk, qq >= kk)
    bm_adj = _bucket_matrix(MOBA_BLOCK + qq - kk, np.ones((MOBA_BLOCK, rows), bool))
    rbx = jnp.repeat(rel_bias.reshape(N_BUCKETS, N_KV, GROUP).transpose(1, 0, 2), MOBA_BLOCK, axis=2)
    far = rbx[:, N_BUCKETS - 1:N_BUCKETS, :]
    chunk, sub = 256, 128
    nsub = rows // sub

    def chunk_major(tbl):
        return tbl.reshape(N_KV, MOBA_BLOCK, nsub, sub).transpose(0, 2, 1, 3)

    b_own = chunk_major(bias_table(bm_own, rbx, far, LOG2E))
    b_adj = chunk_major(bias_table(bm_adj, rbx, far, LOG2E))
    qw = GROUP * HEAD_DIM
    tbl_spec = pl.BlockSpec((None, nsub, MOBA_BLOCK, sub), lambda b, g, i: (g, 0, 0, 0))
    return pl.pallas_call(
        functools.partial(_moba_prompt_kernel, nblk=nblk, chunk=chunk, sub=sub),
        grid=(batch, N_KV, nblk),
        in_specs=[pl.BlockSpec((MOBA_BLOCK, qw), lambda b, g, i: (b * nblk + i, g)),
                  pl.BlockSpec((None, seq, KV_W), lambda b, g, i: (b, 0, 0)),
                  pl.BlockSpec((None, seq, KV_W), lambda b, g, i: (b, 0, 0)),
                  tbl_spec, tbl_spec],
        out_specs=pl.BlockSpec((MOBA_BLOCK, qw), lambda b, g, i: (b * nblk + i, g)),
        out_shape=jax.ShapeDtypeStruct(q.shape, BF16),
        scratch_shapes=[pltpu.VMEM((seq, KV_W), BF16),
                        pltpu.VMEM((N_KV, nblk, V_AUG, MOBA_BLOCK), BF16),
                        pltpu.VMEM((nblk, KV_W), F32),
                        pltpu.VMEM((rows // chunk, KV_W, chunk), BF16),
                        pltpu.VMEM((nblk, rows), F32),
                        pltpu.VMEM((1, rows), F32),
                        pltpu.VMEM((rows // chunk, V_AUG, chunk), F32),
                        pltpu.VMEM((nsub, MOBA_BLOCK, sub), F32), pltpu.VMEM((nsub, MOBA_BLOCK, sub), F32),
                        pltpu.VMEM((nsub, MOBA_BLOCK, sub), BF16), pltpu.VMEM((nsub, MOBA_BLOCK, sub), BF16),
                        pltpu.VMEM((1, rows), F32), pltpu.VMEM((1, rows), F32)],
        compiler_params=_params("arbitrary", "arbitrary", "arbitrary"),
        name="moba_prompt",
    )(q, k, v, b_own, b_adj)


def _heads_to_rows(q):
    t = q.shape[0]
    lane = lax.broadcasted_iota(jnp.int32, (t, LANES), 1)
    pieces = []
    for h in range(N_HEADS):
        src = q[:, (h // 2) * LANES:(h // 2 + 1) * LANES]
        g = h // GROUP
        if h % 2 != g:
            src = pltpu.roll(src, HEAD_DIM, axis=1)
        keep = lane < HEAD_DIM if g == 0 else lane >= HEAD_DIM
        pieces.append(jnp.where(keep, src, 0.0))
    return jnp.concatenate(pieces, axis=0)


def _rows_to_heads(acc, t):
    lane = lax.broadcasted_iota(jnp.int32, (t, LANES), 1)
    cols = []
    for k in range(N_HEADS // 2):
        a = acc[2 * k * t:(2 * k + 1) * t, :]
        b = acc[(2 * k + 1) * t:(2 * k + 2) * t, :]
        if (2 * k) // GROUP == 1:
            a = pltpu.roll(a, HEAD_DIM, axis=1)
        else:
            b = pltpu.roll(b, HEAD_DIM, axis=1)
        cols.append(jnp.where(lane < HEAD_DIM, a, b))
    return jnp.concatenate(cols, axis=1)


def _pad_rows(x, rows):
    return jnp.concatenate([x, jnp.zeros((rows - x.shape[0], x.shape[1]), x.dtype)], axis=0)


def _moba_sample_kernel(pt_ref, q_ref, kn_ref, vn_ref, blast_ref, bown_ref, e_ref, kpool, vpool, o_ref,
                        kbuf, vbuf, ksem, vsem, s_s, *, n_pages, page, n_samples, unroll):
    s = pl.program_id(0)
    slot = s % 2
    ppb = MOBA_BLOCK // page
    nblk = n_pages // ppb
    t_new = q_ref.shape[0]
    rows = N_HEADS * t_new

    def k_copy(smp, sl, p):
        return pltpu.make_async_copy(kpool.at[pt_ref[smp, p]], kbuf.at[sl, p], ksem.at[sl])

    def v_copy(smp, sl, p):
        return pltpu.make_async_copy(vpool.at[pt_ref[smp, p]], vbuf.at[sl, p], vsem.at[sl])

    def start_all(smp, sl):
        def body(p, c):
            k_copy(smp, sl, p).start()
            v_copy(smp, sl, p).start()
            return c
        lax.fori_loop(0, n_pages, body, 0, unroll=8)

    def wait_all(copy, smp, sl):
        def body(p, c):
            copy(smp, sl, p).wait()
            return c
        lax.fori_loop(0, n_pages, body, 0, unroll=8)

    def block_t(buf, j):
        return jnp.concatenate([buf[slot, j * ppb + t] for t in range(ppb)], axis=1)

    blk_lane = lax.broadcasted_iota(jnp.int32, (KV_W, nblk), 1)

    @pl.when(s == 0)
    def _first():
        start_all(0, 0)

    wait_all(k_copy, s, slot)
    nxt = jnp.minimum(s + 1, n_samples - 1)

    def kbar_body(j, kbar_t):
        for t in range(ppb):
            k_copy(nxt, 1 - slot, j * ppb + t).start()
            v_copy(nxt, 1 - slot, j * ppb + t).start()
        pages = kbuf[slot, j * ppb]
        for t in range(1, ppb):
            pages = pages + kbuf[slot, j * ppb + t]
        col = jnp.sum(pages, axis=1, keepdims=True) * (1.0 / MOBA_BLOCK)
        return jnp.where(blk_lane == j, col, kbar_t)

    kbar_t = lax.fori_loop(0, nblk, kbar_body, jnp.zeros((KV_W, nblk), F32), unroll=unroll[0])

    qbd = _heads_to_rows(q_ref[...] * (SCALE * LOG2E))
    qbd16 = qbd.astype(BF16)

    scores = _dot_f32(qbd, kbar_t)
    sel_t = _top3_mask(scores.T, nblk)
    negm = jnp.where(sel_t.T > 0.0, 0.0, MASK_NEG)
    lhs16 = jnp.concatenate([qbd, negm, jnp.zeros((rows, e_ref.shape[1] - nblk), F32)], axis=1).astype(BF16)

    def logits(j):
        rhs16 = jnp.concatenate([block_t(kbuf, j).astype(BF16), e_ref[j]], axis=0)
        return _dot(lhs16, rhs16)

    def fold(x):
        return x[:, :LANES], x[:, LANES:]

    def far(j, mrun):
        st = logits(j)
        s_s[j] = st
        lo, hi = fold(st)
        return jnp.maximum(mrun, jnp.maximum(lo, hi))

    mrun = lax.fori_loop(0, nblk - 1, far, jnp.full((rows, LANES), NEG_INF, F32), unroll=unroll[1])
    s_last = logits(nblk - 1) + blast_ref[...]
    s_s[nblk - 1] = s_last
    kn16 = _pad_rows(kn_ref[...], LANES).astype(BF16)
    s_own = _dot_nt(qbd16, kn16) + bown_ref[...]
    for part in fold(s_last) + (s_own,):
        mrun = jnp.maximum(mrun, part)
    m = jnp.max(mrun, axis=1, keepdims=True)

    wait_all(v_copy, s, slot)

    def pv(j, carry):
        acc, lsum = carry
        p = jnp.exp2(s_s[j] - m)
        lo, hi = fold(p)
        return acc + _dot_nt(p.astype(BF16), block_t(vbuf, j).astype(BF16)), lsum + (lo + hi)

    p_own = jnp.exp2(s_own - m)
    acc0 = _dot(p_own.astype(BF16), _pad_rows(vn_ref[...], LANES).astype(BF16))
    acc, lsum = lax.fori_loop(0, nblk, pv, (acc0, p_own), unroll=unroll[2])
    den = jnp.sum(lsum, axis=1, keepdims=True)
    o_ref[...] = _rows_to_heads(acc / den, t_new)

    @pl.when(s == n_samples - 1)
    def _drain():
        wait_all(k_copy, nxt, 1 - slot)
        wait_all(v_copy, nxt, 1 - slot)


def moba_sample(q, k_new, v_new, k_pool, v_pool, page_table, rel_bias, t_new):
    n, n_pages = page_table.shape
    page = k_pool.shape[2]
    past = n_pages * page
    nblk = past // MOBA_BLOCK
    rows = N_HEADS * t_new
    tok = (np.arange(rows) % t_new)[None, :]
    kk = np.arange(MOBA_BLOCK)[:, None]
    bm_last = _bucket_matrix(MOBA_BLOCK + tok - kk, np.ones((MOBA_BLOCK, rows), bool))
    ko = np.arange(LANES)[:, None]
    bm_own = _bucket_matrix(tok - ko, (ko <= tok) & (ko < t_new))
    rbx = jnp.repeat(rel_bias, t_new, axis=1)[None]
    far = rbx[:, N_BUCKETS - 1:N_BUCKETS, :]
    b_last = bias_table(bm_last, rbx, far, LOG2E)[0].T
    b_own = bias_table(bm_own, rbx, far, LOG2E)[0].T
    onehot_np = np.zeros((nblk, KV_W, MOBA_BLOCK), np.float32)
    onehot_np[np.arange(nblk), np.arange(nblk)] = 1.0
    onehot = jnp.asarray(onehot_np, BF16)

    grid_spec = pltpu.PrefetchScalarGridSpec(
        num_scalar_prefetch=1,
        grid=(n,),
        in_specs=[pl.BlockSpec((t_new, q.shape[1]), lambda s, pt: (s, 0)),
                  pl.BlockSpec((t_new, KV_W), lambda s, pt: (s, 0)),
                  pl.BlockSpec((t_new, KV_W), lambda s, pt: (s, 0)),
                  pl.BlockSpec((rows, MOBA_BLOCK), lambda s, pt: (0, 0)),
                  pl.BlockSpec((rows, LANES), lambda s, pt: (0, 0)),
                  pl.BlockSpec((nblk, KV_W, MOBA_BLOCK), lambda s, pt: (0, 0, 0)),
                  pl.BlockSpec(memory_space=pl.ANY),
                  pl.BlockSpec(memory_space=pl.ANY)],
        out_specs=pl.BlockSpec((t_new, q.shape[1]), lambda s, pt: (s, 0)),
        scratch_shapes=[pltpu.VMEM((2, n_pages, KV_W, page), F32),
                        pltpu.VMEM((2, n_pages, KV_W, page), F32),
                        pltpu.SemaphoreType.DMA((2,)),
                        pltpu.SemaphoreType.DMA((2,)),
                        pltpu.VMEM((nblk, rows, MOBA_BLOCK), F32)],
    )
    return pl.pallas_call(
        functools.partial(_moba_sample_kernel, n_pages=n_pages, page=page, n_samples=n, unroll=(32, 63, 32)),
        grid_spec=grid_spec,
        out_shape=jax.ShapeDtypeStruct(q.shape, F32),
        compiler_params=_params("arbitrary"),
        name="moba_sample",
    )(page_table, q, k_new, v_new, b_last, b_own, onehot, k_pool, v_pool)


def _conv_tail(y, cb_ref, lg_ref, lb_ref, gate):
    y = y + cb_ref[...]
    mu = jnp.mean(y, axis=-1, keepdims=True)
    var = jnp.mean(jnp.square(y - mu), axis=-1, keepdims=True)
    yn = (y - mu) * lax.rsqrt(var + EPS) * lg_ref[...] + lb_ref[...]
    return _silu(yn) * gate


HALO = 32


SUBLANES = 8
LANES = 128
CONV_ROWS = 64


def _conv_prompt_kernel(u_ref, prev_ref, g_ref, w_ref, cb_ref, lg_ref, lb_ref, o_ref, xs, xr, ys, *, tl):
    t = pl.program_id(1)
    c = u_ref.shape[1]
    xs[0:HALO, :] = jnp.where(t > 0, prev_ref[...], 0.0)
    xs[HALO:, :] = u_ref[...]
    span = tl + HALO - SUBLANES
    for r in range(1, SUBLANES):
        xr[r - 1] = xs[pl.ds(r, span), :]
    off = HALO - (CONV_W - 1)

    def taps(base, lt):
        ln = slice(lt * LANES, (lt + 1) * LANES)
        acc = jnp.zeros((CONV_ROWS, LANES), F32)
        for k in range(CONV_W):
            a, r = divmod(off + k, SUBLANES)
            src = xs if r == 0 else xr.at[r - 1]
            start = base + a * SUBLANES
            acc = acc + src[start:start + CONV_ROWS, ln] * w_ref[k:k + 1, ln]
        ys[base:base + CONV_ROWS, ln] = acc

    for base in range(0, tl, CONV_ROWS):
        for lt in range(0, c // LANES, 2):
            @pl.when(t >= 0)
            def _():
                taps(base, lt)
                taps(base, lt + 1)
    o_ref[...] = _conv_tail(ys[...], cb_ref, lg_ref, lb_ref, g_ref[...].astype(F32)).astype(o_ref.dtype)


def conv_prompt(u, szb, conv_w, conv_b, ln_g, ln_b, batch, seq, tl=256):
    c = u.shape[1]
    nt = seq // tl
    vec = pl.BlockSpec((1, c), lambda b, t: (0, 0))
    return pl.pallas_call(
        functools.partial(_conv_prompt_kernel, tl=tl),
        grid=(batch, nt),
        in_specs=[pl.BlockSpec((tl, c), lambda b, t: (b * nt + t, 0)),
                  pl.BlockSpec((HALO, c), lambda b, t: (jnp.maximum((b * nt + t) * (tl // HALO) - 1, 0), 0)),
                  pl.BlockSpec((tl, c), lambda b, t: (b * nt + t, 0)),
                  pl.BlockSpec((CONV_W, c), lambda b, t: (0, 0)), vec, vec, vec],
        out_specs=pl.BlockSpec((tl, c), lambda b, t: (b * nt + t, 0)),
        out_shape=jax.ShapeDtypeStruct(u.shape, BF16),
        scratch_shapes=[pltpu.VMEM((HALO + tl, c), F32),
                        pltpu.VMEM((SUBLANES - 1, HALO + tl - SUBLANES, c), F32),
                        pltpu.VMEM((tl, c), F32)],
        compiler_params=_params("arbitrary", "arbitrary"),
        name="conv_prompt",
    )(u, u, szb, conv_w, conv_b.reshape(1, c), ln_g.reshape(1, c), ln_b.reshape(1, c))


def _conv_sample_kernel(xp_ref, g_ref, w_ref, cb_ref, lg_ref, lb_ref, o_ref, *, t_new):
    ns, _, c = xp_ref.shape
    y = jnp.zeros((ns, t_new, c), F32)
    for k in range(CONV_W):
        y = y + xp_ref[:, pl.ds(k, t_new), :] * w_ref[k:k + 1, :]
    o_ref[...] = _conv_tail(y, cb_ref, lg_ref, lb_ref, g_ref[...].astype(F32)).astype(o_ref.dtype)


def conv_sample(xp, szb, conv_w, conv_b, ln_g, ln_b, t_new, ns=8):
    n, rows, c = xp.shape
    vec = pl.BlockSpec((1, c), lambda i: (0, 0))
    return pl.pallas_call(
        functools.partial(_conv_sample_kernel, t_new=t_new),
        grid=(n // ns,),
        in_specs=[pl.BlockSpec((ns, rows, c), lambda i: (i, 0, 0)),
                  pl.BlockSpec((ns, t_new, c), lambda i: (i, 0, 0)),
                  pl.BlockSpec((CONV_W, c), lambda i: (0, 0)), vec, vec, vec],
        out_specs=pl.BlockSpec((ns, t_new, c), lambda i: (i, 0, 0)),
        out_shape=jax.ShapeDtypeStruct((n, t_new, c), BF16),
        compiler_params=_params("arbitrary"),
        name="conv_sample",
    )(xp, szb, conv_w, conv_b.reshape(1, c), ln_g.reshape(1, c), ln_b.reshape(1, c))


def _swa_prompt_kernel(q_ref, kc_ref, kp_ref, vc_ref, vp_ref, bias_ref, sink_ref, o_ref, s_s, p_s, *, chunk, sub):
    n = pl.program_id(1)
    w = SWA_WINDOW
    rows = GROUP * w
    nch, per = rows // chunk, chunk // sub
    gw = GROUP * HEAD_DIM
    kcat = jnp.concatenate([kp_ref[...], kc_ref[...]], axis=0).astype(BF16)
    v_t = jnp.concatenate([vp_ref[...], vc_ref[...]], axis=0).T.astype(BF16)
    ones = jnp.ones((V_AUG - HEAD_DIM, 2 * w), BF16)
    no_prev = (lax.broadcasted_iota(jnp.int32, (2 * w, sub), 0) < w) & (n == 0)
    qpads, v_augs, sinks = [], [], []
    for g in range(N_KV):
        q_t = (q_ref[:, g * gw:(g + 1) * gw] * (SCALE * LOG2E)).T
        qs = jnp.concatenate([q_t[h * HEAD_DIM:(h + 1) * HEAD_DIM, :] for h in range(GROUP)], axis=1)
        zeros = jnp.zeros_like(qs)
        qpads.append(jnp.concatenate([qs, zeros] if g == 0 else [zeros, qs], axis=0).astype(BF16))
        v_augs.append(jnp.concatenate([v_t[g * HEAD_DIM:(g + 1) * HEAD_DIM], ones], axis=0))
        sinks.append(sink_ref[g] * LOG2E)

    def qk(g, c):
        s_t = _dot(kcat, qpads[g][:, c * chunk:(c + 1) * chunk])
        for k in range(per):
            s_s[g, c * per + k] = s_t[:, k * sub:(k + 1) * sub]

    def softmax(g, cs):
        s_t = jnp.where(no_prev, NEG_INF, s_s[g, cs] + bias_ref[g, cs])
        m = jnp.maximum(jnp.max(s_t, axis=0, keepdims=True), sinks[g][:, cs * sub:(cs + 1) * sub])
        p_s[g, cs] = jnp.exp2(s_t - m).astype(BF16)
        return m

    def pv(g, c, ms):
        p = jnp.concatenate([p_s[g, c * per + k] for k in range(per)], axis=1)
        o_t = _dot(v_augs[g], p)
        den = o_t[HEAD_DIM:HEAD_DIM + 1] + jnp.exp2(sinks[g][:, c * chunk:(c + 1) * chunk] - jnp.concatenate(ms, axis=1))
        return o_t[:HEAD_DIM] / den

    for c in range(nch):
        qk(0, c)
    ms = [[], []]
    outs = [[], []]
    for c in range(nch):
        qk(1, c)
        ms[0] += [softmax(0, c * per + k) for k in range(per)]
    for c in range(nch):
        outs[0].append(pv(0, c, ms[0][c * per:(c + 1) * per]))
        ms[1] += [softmax(1, c * per + k) for k in range(per)]
    for c in range(nch):
        outs[1].append(pv(1, c, ms[1][c * per:(c + 1) * per]))
    tiles = []
    for g in range(N_KV):
        o_t = jnp.concatenate(outs[g], axis=1)
        tiles.append(jnp.concatenate([o_t[:, h * w:(h + 1) * w] for h in range(GROUP)], axis=0).T)
    o_ref[...] = jnp.concatenate(tiles, axis=1).astype(o_ref.dtype)


def swa_prompt(q, k, v, sinks, rel_bias, batch, seq):
    w = SWA_WINDOW
    nb = seq // w
    rows = GROUP * w
    kidx = np.arange(2 * w)[:, None]
    qq = (np.arange(rows) % w)[None, :]
    dist = w + qq - kidx
    bm = _bucket_matrix(dist, (dist >= 0) & (dist < w))
    rbx = jnp.repeat(rel_bias.reshape(N_BUCKETS, N_KV, GROUP).transpose(1, 0, 2), w, axis=2)
    chunk, sub = 256, 128
    nsub = rows // sub
    bias = bias_table(bm, rbx, jnp.zeros((N_KV, 1, rows), F32), LOG2E)
    bias = bias.reshape(N_KV, 2 * w, nsub, sub).transpose(0, 2, 1, 3)
    sink_x = jnp.repeat(sinks.reshape(N_KV, 1, GROUP), w, axis=2)
    d = q.shape[1]
    cur = lambda b, n: (b * nb + n, 0)
    prev = lambda b, n: (jnp.maximum(b * nb + n - 1, 0), 0)
    return pl.pallas_call(
        functools.partial(_swa_prompt_kernel, chunk=chunk, sub=sub),
        grid=(batch, nb),
        in_specs=[pl.BlockSpec((w, d), cur),
                  pl.BlockSpec((w, KV_W), cur), pl.BlockSpec((w, KV_W), prev),
                  pl.BlockSpec((w, KV_W), cur), pl.BlockSpec((w, KV_W), prev),
                  pl.BlockSpec((N_KV, nsub, 2 * w, sub), lambda b, n: (0, 0, 0, 0)),
                  pl.BlockSpec((N_KV, 1, rows), lambda b, n: (0, 0, 0))],
        out_specs=pl.BlockSpec((w, d), cur),
        out_shape=jax.ShapeDtypeStruct(q.shape, BF16),
        scratch_shapes=[pltpu.VMEM((N_KV, nsub, 2 * w, sub), F32),
                        pltpu.VMEM((N_KV, nsub, 2 * w, sub), BF16)],
        compiler_params=_params("arbitrary", "arbitrary"),
        name="swa_prompt",
    )(q, k, k, v, v, bias, sink_x)


def _swa_sample_kernel(q_ref, kb_ref, kn_ref, vb_ref, vn_ref, bbuf_ref, bnew_ref, sink_ref, o_ref, *, t_new):
    ns = kb_ref.shape[0]
    sink = sink_ref[...]
    for i in range(ns):
        rs = slice(i * t_new, (i + 1) * t_new)
        qbd = _heads_to_rows(q_ref[rs, :] * SCALE).astype(BF16)
        s_buf = _dot_nt(qbd, kb_ref[i].astype(BF16)) + bbuf_ref[...]
        s_new = _dot_nt(qbd, _pad_rows(kn_ref[rs, :], LANES).astype(BF16)) + bnew_ref[...]
        m = jnp.maximum(jnp.maximum(jnp.max(s_buf, axis=1, keepdims=True), jnp.max(s_new, axis=1, keepdims=True)),
                        sink)
        p_buf, p_new = jnp.exp(s_buf - m), jnp.exp(s_new - m)
        den = jnp.sum(p_buf, axis=1, keepdims=True) + jnp.sum(p_new, axis=1, keepdims=True) + jnp.exp(sink - m)
        acc = (_dot(p_buf.astype(BF16), vb_ref[i].astype(BF16))
               + _dot(p_new.astype(BF16), _pad_rows(vn_ref[rs, :], LANES).astype(BF16)))
        o_ref[rs, :] = _rows_to_heads(acc / den, t_new)


def swa_sample(q, k_buf, v_buf, k_new, v_new, sinks, rel_bias, t_new, ns=4):
    n, wb, _ = k_buf.shape
    rows = N_HEADS * t_new
    tok = (np.arange(rows) % t_new)[None, :]
    dist_buf = tok + wb - np.arange(wb)[:, None]
    j_new = np.arange(LANES)[:, None]
    dist_new = tok - j_new
    bm_buf = _bucket_matrix(dist_buf, (dist_buf >= 0) & (dist_buf < SWA_WINDOW))
    bm_new = _bucket_matrix(dist_new, (dist_new >= 0) & (dist_new < SWA_WINDOW) & (j_new < t_new))
    rbx = jnp.repeat(rel_bias, t_new, axis=1)[None]
    zero = jnp.zeros((1, 1, rows), F32)
    b_buf = bias_table(bm_buf, rbx, zero)[0].T
    b_new = bias_table(bm_new, rbx, zero)[0].T
    sink_r = jnp.repeat(sinks, t_new).reshape(rows, 1)
    d = q.shape[1]
    tile = lambda width: pl.BlockSpec((ns * t_new, width), lambda i: (i, 0))
    buf = pl.BlockSpec((ns, wb, KV_W), lambda i: (i, 0, 0))
    return pl.pallas_call(
        functools.partial(_swa_sample_kernel, t_new=t_new),
        grid=(n // ns,),
        in_specs=[tile(d), buf, tile(KV_W), buf, tile(KV_W),
                  pl.BlockSpec((rows, wb), lambda i: (0, 0)),
                  pl.BlockSpec((rows, LANES), lambda i: (0, 0)),
                  pl.BlockSpec((rows, 1), lambda i: (0, 0))],
        out_specs=tile(d),
        out_shape=jax.ShapeDtypeStruct(q.shape, F32),
        compiler_params=_params("arbitrary"),
        name="swa_sample",
    )(q, k_buf, k_new, v_buf, v_new, b_buf, b_new, sink_r)


def kernel(x_prompt, x_sample, c_prompt, c_sample, cache_a_k, cache_a_v, page_table, cache_b_conv, cache_c_k, cache_c_v, rel_bias, norm_a, mod_w_a, mod_b_a, w_in_a, conv_w_b, conv_b_b, ln_g_b, ln_b_b, w_out_a, norm_c, mod_w_c, mod_b_c, w_in_c, sinks_c, w_out_c, final_norm):
    batch, seq, d = x_prompt.shape
    n, t_new, _ = x_sample.shape
    n_pool, page = cache_a_k.shape[1], cache_a_k.shape[2]
    hq = N_HEADS * HEAD_DIM
    cb = conv_w_b.shape[2]
    assert norm_a.shape[0] == 1 and norm_c.shape[0] == 1, "one A/B layer followed by one C layer"

    tm_p = 1024
    tm_s = min(256, n * t_new)
    xp = x_prompt.reshape(batch * seq, d)
    xs = x_sample

    c_all = jnp.concatenate([c_prompt, c_sample], axis=0)
    c_rows = -(-c_all.shape[0] // 8) * 8
    c_all = jnp.pad(c_all, ((0, c_rows - c_all.shape[0]), (0, 0)))

    def split_mod(m):
        mp = m[:batch].reshape(batch, 1, 3, d)
        ms = m[batch:batch + n].reshape(n, 1, 3, d)
        return [(mp[:, :, j], ms[:, :, j]) for j in range(3)]

    (sh_a, sc_a, gt_a) = split_mod(modulation(c_all, mod_w_a[0], mod_b_a[0]))
    (sh_c, sc_c, gt_c) = split_mod(modulation(c_all, mod_w_c[0], mod_b_c[0]))

    kv0, kv1 = hq, hq + 2 * KV_W
    za0 = kv1
    ga0 = za0 + hq
    gb0 = ga0 + cb
    zb0 = gb0 + cb
    segs_a = (("raw", 0, hq), ("raw", kv0, kv0 + KV_W), ("raw", kv0 + KV_W, kv1),
              ("silu", za0, ga0), ("glu", ga0, gb0, gb0, zb0), ("silu", zb0, zb0 + cb))
    dts_a = (F32, F32, F32, BF16, F32, BF16)
    w_in_a16 = w_in_a[0].astype(BF16)
    w_out_a16 = w_out_a[0].astype(BF16)
    k_pool = cache_a_k[0].transpose(0, 2, 3, 1).reshape(n_pool, KV_W, page)
    v_pool = cache_a_v[0].transpose(0, 2, 3, 1).reshape(n_pool, KV_W, page)

    q, k, v, sza, u, szb = ln_inproj(xp, sh_a[0], sc_a[0], norm_a[0], w_in_a16, segs_a, dts_a, tm_p)
    oa = moba_prompt(q, k.reshape(batch, seq, KV_W), v.reshape(batch, seq, KV_W), rel_bias, batch, seq)
    ob = conv_prompt(u, szb, conv_w_b[0], conv_b_b[0], ln_g_b[0], ln_b_b[0], batch, seq)
    xp1 = out_proj([(oa, sza, w_out_a16[:hq])], [(ob, w_out_a16[hq:])], xp, gt_a[0], None, tm_p)
    ak_p, av_p = k, v
    bc_p = u.reshape(batch, seq, cb)[:, seq - (CONV_W - 1):]

    q, k, v, sza, u, szb = ln_inproj(xs, sh_a[1], sc_a[1], norm_a[0], w_in_a16, segs_a, dts_a, tm_s)
    oa = moba_sample(q, k, v, k_pool, v_pool, page_table, rel_bias, t_new)
    xcat = jnp.concatenate([cache_b_conv[0], u.reshape(n, t_new, cb)], axis=1)
    ob = conv_sample(xcat, szb.reshape(n, t_new, cb), conv_w_b[0], conv_b_b[0], ln_g_b[0], ln_b_b[0], t_new)
    xs1 = out_proj([(oa, sza, w_out_a16[:hq])], [(ob.reshape(n * t_new, cb), w_out_a16[hq:])],
                   xs, gt_a[1], None, tm_s)
    ak_s, av_s = k, v
    bc_s = xcat[:, t_new:]

    segs_c = (("raw", 0, hq), ("raw", hq, hq + KV_W), ("raw", hq + KV_W, hq + 2 * KV_W),
              ("silu", hq + 2 * KV_W, 2 * hq + 2 * KV_W))
    dts_c = (F32, F32, F32, BF16)
    w_in_c16 = w_in_c[0].astype(BF16)
    w_out_c16 = w_out_c[0].astype(BF16)

    q, k, v, sz = ln_inproj(xp1, sh_c[0], sc_c[0], norm_c[0], w_in_c16, segs_c, dts_c, tm_p)
    o = swa_prompt(q, k, v, sinks_c[0], rel_bias, batch, seq)
    y_prompt = out_proj([(o, sz, w_out_c16)], [], xp1, gt_c[0], final_norm, tm_p)
    wb_p = min(SWA_WINDOW, seq)
    ck_p = k.reshape(batch, seq, KV_W)[:, seq - wb_p:]
    cv_p = v.reshape(batch, seq, KV_W)[:, seq - wb_p:]

    q, k, v, sz = ln_inproj(xs1, sh_c[1], sc_c[1], norm_c[0], w_in_c16, segs_c, dts_c, tm_s)
    wb_s = cache_c_k.shape[2]
    kb, vb = cache_c_k[0].reshape(n, wb_s, KV_W), cache_c_v[0].reshape(n, wb_s, KV_W)
    o = swa_sample(q, kb, vb, k, v, sinks_c[0], rel_bias, t_new)
    y_sample = out_proj([(o, sz, w_out_c16)], [], xs1, gt_c[1], final_norm, tm_s)
    ck_s = jnp.concatenate([kb, k.reshape(n, t_new, KV_W)], axis=1)[:, t_new:]
    cv_s = jnp.concatenate([vb, v.reshape(n, t_new, KV_W)], axis=1)[:, t_new:]

    def kv5(a, lead):
        return a.reshape((1,) + lead + (N_KV, HEAD_DIM))

    return (y_prompt.reshape(batch, seq, d), y_sample,
            kv5(ak_p, (batch, seq)), kv5(av_p, (batch, seq)),
            kv5(ak_s, (n, t_new)), kv5(av_s, (n, t_new)),
            bc_p[None], bc_s[None],
            kv5(ck_p, (batch, wb_p)), kv5(cv_p, (batch, wb_p)),
            kv5(ck_s, (n, wb_s)), kv5(cv_s, (n, wb_s)))
```

```python
import functools
import math

import jax
import jax.numpy as jnp
import numpy as np
from jax import lax
from jax.experimental import pallas as pl
from jax.experimental.pallas import tpu as pltpu

F32 = jnp.float32
BF16 = jnp.bfloat16
NEG_INF = float("-inf")
MASK_NEG = -1e30

HEAD_DIM = 64
N_HEADS = 16
N_KV = 2
GROUP = N_HEADS // N_KV
KV_W = N_KV * HEAD_DIM
MOBA_BLOCK = 256
MOBA_TOPK = 3
CONV_W = 31
SWA_WINDOW = 128
N_BUCKETS = 32
MAX_DISTANCE = 128
EPS = 1e-6
SCALE = HEAD_DIM ** -0.5
LOG2E = math.log2(math.e)

VMEM_LIMIT = 56 * 2**20


def _params(*sem):
    return pltpu.CompilerParams(dimension_semantics=sem, vmem_limit_bytes=VMEM_LIMIT)


def _silu(z):
    return z * jax.nn.sigmoid(z)


def _dot(a, b):
    return jnp.dot(a, b, preferred_element_type=F32)


def _dot_nt(a, b):
    return lax.dot_general(a, b, (((1,), (1,)), ((), ())), preferred_element_type=F32)


def _dot_f32(a, b):
    return jnp.dot(a, b, preferred_element_type=F32, precision=lax.Precision.HIGHEST)


def _t5_bucket_np(n):
    n = np.maximum(n, 0)
    max_exact = N_BUCKETS // 2
    nf = np.maximum(n, 1).astype(np.float32)
    large = max_exact + (np.log(nf / np.float32(max_exact)) / np.float32(math.log(MAX_DISTANCE / max_exact))
                         * np.float32(N_BUCKETS - max_exact)).astype(np.int32)
    large = np.minimum(large, N_BUCKETS - 1)
    return np.where(n < max_exact, n, large).astype(np.int32)


def _bucket_matrix(dist, valid):
    return np.where(valid, _t5_bucket_np(dist), -1).astype(np.int32)


def _bias_kernel(bm_ref, rb_ref, sub_ref, o_ref, *, mul):
    bm = bm_ref[...]
    acc = jnp.zeros(bm.shape, F32)
    for b in range(N_BUCKETS):
        acc = jnp.where(bm == b, rb_ref[b:b + 1, :], acc)
    o_ref[...] = jnp.where(bm < 0, NEG_INF, (acc - sub_ref[...]) * mul)


def bias_table(bm, rbx, sub, mul=1.0):
    g, _, c = rbx.shape
    r = bm.shape[0]
    return pl.pallas_call(
        functools.partial(_bias_kernel, mul=mul),
        grid=(g,),
        in_specs=[pl.BlockSpec((r, c), lambda i: (0, 0)),
                  pl.BlockSpec((None, N_BUCKETS, c), lambda i: (i, 0, 0)),
                  pl.BlockSpec((None, 1, c), lambda i: (i, 0, 0))],
        out_specs=pl.BlockSpec((None, r, c), lambda i: (i, 0, 0)),
        out_shape=jax.ShapeDtypeStruct((g, r, c), F32),
        compiler_params=_params("arbitrary"),
        name="bias_table",
    )(jnp.asarray(bm), rbx, sub)


def _mod_kernel(c_ref, w_ref, b_ref, o_ref):
    o_ref[...] = _dot_f32(_silu(c_ref[...]), w_ref[...]) + b_ref[...]


def modulation(c, w, b):
    n, d = c.shape
    m = w.shape[1]
    tn = 512
    return pl.pallas_call(
        _mod_kernel,
        grid=(m // tn,),
        in_specs=[pl.BlockSpec((n, d), lambda j: (0, 0)),
                  pl.BlockSpec((d, tn), lambda j: (0, j)),
                  pl.BlockSpec((1, tn), lambda j: (0, j))],
        out_specs=pl.BlockSpec((n, tn), lambda j: (0, j)),
        out_shape=jax.ShapeDtypeStruct((n, m), F32),
        compiler_params=_params("arbitrary"),
        name="modulation",
    )(c, w, b.reshape(1, m))


def _ln_inproj_kernel(x_ref, shift_ref, scale_ref, g_ref, w_ref, *out_refs, segs):
    x = x_ref[...]
    y = x * lax.rsqrt(jnp.mean(x * x, axis=-1, keepdims=True) + EPS)
    h = (y * g_ref[...]) * (1.0 + scale_ref[...]) + shift_ref[...]
    h16 = h.reshape(-1, h.shape[-1]).astype(BF16)
    for o_ref, seg in zip(out_refs, segs, strict=True):
        kind, lo, hi = seg[0], seg[1], seg[2]
        z = _dot(h16, w_ref[:, lo:hi])
        if kind == "silu":
            z = _silu(z)
        elif kind == "glu":
            z = z * jax.nn.sigmoid(_dot(h16, w_ref[:, seg[3]:seg[4]]))
        o_ref[...] = z.astype(o_ref.dtype)


def ln_inproj(x, shift, scale, norm_g, w16, segs, out_dtypes, tm):
    d = x.shape[-1]
    if x.ndim == 2:
        r = x.shape[0]
        nt = r // tm
        per_group = nt // shift.shape[0]
        x_spec = pl.BlockSpec((tm, d), lambda i: (i, 0))
        mod_spec = pl.BlockSpec((None, 1, d), lambda i: (i // per_group, 0, 0))
    else:
        n, t_new, _ = x.shape
        r = n * t_new
        nt = r // tm
        x_spec = pl.BlockSpec((tm // t_new, t_new, d), lambda i: (i, 0, 0))
        mod_spec = pl.BlockSpec((tm // t_new, 1, d), lambda i: (i, 0, 0))
    out_shape = [jax.ShapeDtypeStruct((r, s[2] - s[1]), dt) for s, dt in zip(segs, out_dtypes, strict=True)]
    out_specs = [pl.BlockSpec((tm, s[2] - s[1]), lambda i: (i, 0)) for s in segs]
    return pl.pallas_call(
        functools.partial(_ln_inproj_kernel, segs=segs),
        grid=(nt,),
        in_specs=[x_spec, mod_spec, mod_spec,
                  pl.BlockSpec((1, d), lambda i: (0, 0)),
                  pl.BlockSpec(w16.shape, lambda i: (0, 0), pipeline_mode=pl.Buffered(1))],
        out_specs=out_specs,
        out_shape=out_shape,
        compiler_params=_params("arbitrary"),
        name="ln_inproj",
    )(x, shift, scale, norm_g.reshape(1, d), w16)


def _out_proj_kernel(*refs, n_gated, n_plain, final_norm):
    it = iter(refs)
    y = None
    for _ in range(n_gated):
        a_ref, m_ref, w_ref = next(it), next(it), next(it)
        t = _dot((a_ref[...] * m_ref[...].astype(F32)).astype(BF16), w_ref[...])
        y = t if y is None else y + t
    for _ in range(n_plain):
        a_ref, w_ref = next(it), next(it)
        t = _dot(a_ref[...], w_ref[...])
        y = t if y is None else y + t
    x_ref, gate_ref = next(it), next(it)
    xn = x_ref[...] + gate_ref[...] * y.reshape(x_ref.shape)
    if final_norm:
        fg_ref = next(it)
        xn = xn * lax.rsqrt(jnp.mean(xn * xn, axis=-1, keepdims=True) + EPS) * fg_ref[...]
    o_ref = next(it)
    o_ref[...] = xn


def out_proj(gated, plain, x, gate, final_g, tm):
    d = x.shape[-1]
    if x.ndim == 2:
        nt = x.shape[0] // tm
        per_group = nt // gate.shape[0]
        x_spec = pl.BlockSpec((tm, d), lambda i: (i, 0))
        gate_spec = pl.BlockSpec((None, 1, d), lambda i: (i // per_group, 0, 0))
    else:
        n, t_new, _ = x.shape
        nt = n * t_new // tm
        x_spec = pl.BlockSpec((tm // t_new, t_new, d), lambda i: (i, 0, 0))
        gate_spec = pl.BlockSpec((tm // t_new, 1, d), lambda i: (i, 0, 0))
    args, specs = [], []
    for a, m, w in gated:
        args += [a, m, w]
        specs += [pl.BlockSpec((tm, a.shape[1]), lambda i: (i, 0)),
                  pl.BlockSpec((tm, m.shape[1]), lambda i: (i, 0)),
                  pl.BlockSpec(w.shape, lambda i: (0, 0))]
    for a, w in plain:
        args += [a, w]
        specs += [pl.BlockSpec((tm, a.shape[1]), lambda i: (i, 0)),
                  pl.BlockSpec(w.shape, lambda i: (0, 0))]
    args += [x, gate]
    specs += [x_spec, gate_spec]
    if final_g is not None:
        args.append(final_g.reshape(1, d))
        specs.append(pl.BlockSpec((1, d), lambda i: (0, 0)))
    return pl.pallas_call(
        functools.partial(_out_proj_kernel, n_gated=len(gated), n_plain=len(plain),
                          final_norm=final_g is not None),
        grid=(nt,),
        in_specs=specs,
        out_specs=x_spec,
        out_shape=jax.ShapeDtypeStruct(x.shape, F32),
        compiler_params=_params("arbitrary"),
        name="out_proj",
    )(*args)


def _top3_mask(scores, n_valid):
    nb = scores.shape[0]
    blk = lax.broadcasted_iota(jnp.int32, scores.shape, 0)
    s = jnp.where(blk < n_valid, scores, NEG_INF)
    picked = jnp.zeros(scores.shape, F32)
    for _ in range(MOBA_TOPK):
        mx = jnp.max(s, axis=0, keepdims=True)
        first = jnp.min(jnp.where(s == mx, blk, nb), axis=0, keepdims=True)
        hit = blk == first
        picked = jnp.where(hit, 1.0, picked)
        s = jnp.where(hit, NEG_INF, s)
    return jnp.where(blk < n_valid, picked, 0.0)


V_AUG = HEAD_DIM + 16


def _moba_prompt_kernel(q_ref, k_ref, v_ref, bown_ref, badj_ref, o_ref,
                        k16_s, vT_s, kbar_s, qT16_s, sel_s, m_s, acc_s,
                        s0_s, s1_s, p0_s, p1_s, a0_s, a1_s, *, nblk, chunk, sub):
    g = pl.program_id(1)
    i = pl.program_id(2)
    rows = GROUP * MOBA_BLOCK
    nch = rows // chunk

    @pl.when((g == 0) & (i == 0))
    def _per_batch():
        ones = jnp.ones((V_AUG - HEAD_DIM, MOBA_BLOCK), BF16)
        for j in range(nblk):
            kb = k_ref[j * MOBA_BLOCK:(j + 1) * MOBA_BLOCK, :]
            k16_s[j * MOBA_BLOCK:(j + 1) * MOBA_BLOCK, :] = kb.astype(BF16)
            kbar_s[j:j + 1, :] = jnp.mean(kb, axis=0, keepdims=True)
            v_t = v_ref[j * MOBA_BLOCK:(j + 1) * MOBA_BLOCK, :].T.astype(BF16)
            for kv in range(N_KV):
                vT_s[kv, j] = jnp.concatenate([v_t[kv * HEAD_DIM:(kv + 1) * HEAD_DIM], ones], axis=0)

    qT = (q_ref[...] * (SCALE * LOG2E)).T
    qs = jnp.concatenate([qT[h * HEAD_DIM:(h + 1) * HEAD_DIM, :] for h in range(GROUP)], axis=1)
    zeros = jnp.zeros_like(qs)
    qpad = jnp.where(g == 0, jnp.concatenate([qs, zeros], axis=0), jnp.concatenate([zeros, qs], axis=0))
    qpad16 = qpad.astype(BF16)
    for c in range(nch):
        qT16_s[c] = qpad16[:, c * chunk:(c + 1) * chunk]
    sel_s[...] = _top3_mask(_dot_f32(kbar_s[...], qpad), i)
    m_s[...] = jnp.full(m_s.shape, NEG_INF, F32)
    acc_s[...] = jnp.zeros(acc_s.shape, F32)

    n_far = i - 1
    sbuf, pbuf, abuf = (s0_s, s1_s), (p0_s, p1_s), (a0_s, a1_s)

    def blk_of(t):
        far_j = jnp.clip(t - 2, 0, jnp.maximum(n_far - 1, 0))
        return jnp.where(t == 0, i, jnp.where(t == 1, jnp.maximum(i - 1, 0), far_j))

    per = chunk // sub

    def stage_qk(t, slot, c):
        kb = k16_s[pl.ds(pl.multiple_of(blk_of(t) * MOBA_BLOCK, MOBA_BLOCK), MOBA_BLOCK), :]
        s_t = _dot(kb, qT16_s[c])
        for k in range(per):
            sbuf[slot][c * per + k] = s_t[:, k * sub:(k + 1) * sub]

    def stage_softmax(slot, c, on, bias_ref):
        ln = slice(c * sub, (c + 1) * sub)
        s_t = sbuf[slot][c]
        if bias_ref is not None:
            s_t = s_t + bias_ref[c]
        m_old = m_s[:, ln]
        m_new = jnp.maximum(m_old, jnp.where(on, jnp.max(s_t, axis=0, keepdims=True), NEG_INF))
        pbuf[slot][c] = jnp.exp2(s_t - jnp.where(on, m_new, jnp.inf)).astype(BF16)
        abuf[slot][:, ln] = jnp.exp2(m_old - m_new)
        m_s[:, ln] = m_new

    def stage_pv(t, slot, c):
        p = jnp.concatenate([pbuf[slot][c * per + k] for k in range(per)], axis=1)
        acc_s[c] = acc_s[c] * abuf[slot][:, c * chunk:(c + 1) * chunk] + _dot(vT_s[g, blk_of(t)], p)

    def pipe_half(t, slot, bias_ref=None, pv=True, qk=True):
        own = bias_ref is bown_ref
        valid = None if own else jnp.where(t == 1, i >= 1, t - 2 < n_far)
        j = blk_of(t)
        for c in range(nch):
            if qk:
                stage_qk(t + 1, 1 - slot, c)
            if own:
                on = jnp.full((1, chunk), True)
            else:
                on = (sel_s[pl.ds(j, 1), c * chunk:(c + 1) * chunk] > 0.0) & valid
            for k in range(per):
                stage_softmax(slot, c * per + k, on[:, k * sub:(k + 1) * sub], bias_ref)
            if pv:
                stage_pv(t - 1, 1 - slot, c)

    for c in range(nch):
        stage_qk(0, 0, c)
    pipe_half(0, 0, bown_ref, pv=False)
    pipe_half(1, 1, badj_ref)

    def pair(u, carry):
        pipe_half(2 + 2 * u, 0)
        pipe_half(3 + 2 * u, 1)
        return carry

    n_pairs = jnp.maximum(n_far, 0) // 2
    lax.fori_loop(0, n_pairs, pair, 0)
    t_tail = 2 + 2 * n_pairs
    odd = jnp.maximum(n_far, 0) % 2 == 1

    @pl.when(odd)
    def _tail():
        pipe_half(t_tail, 0, qk=False)
        for c in range(nch):
            stage_pv(t_tail, 0, c)

    @pl.when(jnp.logical_not(odd))
    def _drain():
        for c in range(nch):
            stage_pv(t_tail - 1, 1, c)

    acc = jnp.concatenate([acc_s[c] for c in range(nch)], axis=1)
    o_t = acc[:HEAD_DIM] / acc[HEAD_DIM:HEAD_DIM + 1]
    o_cat = jnp.concatenate([o_t[:, h * MOBA_BLOCK:(h + 1) * MOBA_BLOCK] for h in range(GROUP)], axis=0)
    o_ref[...] = o_cat.T.astype(o_ref.dtype)


def moba_prompt(q, k, v, rel_bias, batch, seq):
    nblk = seq // MOBA_BLOCK
    rows = GROUP * MOBA_BLOCK
    kk = np.arange(MOBA_BLOCK)[:, None]
    qq = (np.arange(rows) % MOBA_BLOCK)[None, :]
    bm_own = _bucket_matrix(qq - kk, qq >= kk)
    bm_adj = _bucket_matrix(MOBA_BLOCK + qq - kk, np.ones((MOBA_BLOCK, rows), bool))
    rbx = jnp.repeat(rel_bias.reshape(N_BUCKETS, N_KV, GROUP).transpose(1, 0, 2), MOBA_BLOCK, axis=2)
    far = rbx[:, N_BUCKETS - 1:N_BUCKETS, :]
    chunk, sub = 256, 128
    nsub = rows // sub

    def chunk_major(tbl):
        return tbl.reshape(N_KV, MOBA_BLOCK, nsub, sub).transpose(0, 2, 1, 3)

    b_own = chunk_major(bias_table(bm_own, rbx, far, LOG2E))
    b_adj = chunk_major(bias_table(bm_adj, rbx, far, LOG2E))
    qw = GROUP * HEAD_DIM
    tbl_spec = pl.BlockSpec((None, nsub, MOBA_BLOCK, sub), lambda b, g, i: (g, 0, 0, 0))
    return pl.pallas_call(
        functools.partial(_moba_prompt_kernel, nblk=nblk, chunk=chunk, sub=sub),
        grid=(batch, N_KV, nblk),
        in_specs=[pl.BlockSpec((MOBA_BLOCK, qw), lambda b, g, i: (b * nblk + i, g)),
                  pl.BlockSpec((None, seq, KV_W), lambda b, g, i: (b, 0, 0)),
                  pl.BlockSpec((None, seq, KV_W), lambda b, g, i: (b, 0, 0)),
                  tbl_spec, tbl_spec],
        out_specs=pl.BlockSpec((MOBA_BLOCK, qw), lambda b, g, i: (b * nblk + i, g)),
        out_shape=jax.ShapeDtypeStruct(q.shape, BF16),
        scratch_shapes=[pltpu.VMEM((seq, KV_W), BF16),
                        pltpu.VMEM((N_KV, nblk, V_AUG, MOBA_BLOCK), BF16),
                        pltpu.VMEM((nblk, KV_W), F32),
                        pltpu.VMEM((rows // chunk, KV_W, chunk), BF16),
                        pltpu.VMEM((nblk, rows), F32),
                        pltpu.VMEM((1, rows), F32),
                        pltpu.VMEM((rows // chunk, V_AUG, chunk), F32),
                        pltpu.VMEM((nsub, MOBA_BLOCK, sub), F32), pltpu.VMEM((nsub, MOBA_BLOCK, sub), F32),
                        pltpu.VMEM((nsub, MOBA_BLOCK, sub), BF16), pltpu.VMEM((nsub, MOBA_BLOCK, sub), BF16),
                        pltpu.VMEM((1, rows), F32), pltpu.VMEM((1, rows), F32)],
        compiler_params=_params("arbitrary", "arbitrary", "arbitrary"),
        name="moba_prompt",
    )(q, k, v, b_own, b_adj)


def _heads_to_rows(q):
    t = q.shape[0]
    lane = lax.broadcasted_iota(jnp.int32, (t, LANES), 1)
    pieces = []
    for h in range(N_HEADS):
        src = q[:, (h // 2) * LANES:(h // 2 + 1) * LANES]
        g = h // GROUP
        if h % 2 != g:
            src = pltpu.roll(src, HEAD_DIM, axis=1)
        keep = lane < HEAD_DIM if g == 0 else lane >= HEAD_DIM
        pieces.append(jnp.where(keep, src, 0.0))
    return jnp.concatenate(pieces, axis=0)


def _rows_to_heads(acc, t):
    lane = lax.broadcasted_iota(jnp.int32, (t, LANES), 1)
    cols = []
    for k in range(N_HEADS // 2):
        a = acc[2 * k * t:(2 * k + 1) * t, :]
        b = acc[(2 * k + 1) * t:(2 * k + 2) * t, :]
        if (2 * k) // GROUP == 1:
            a = pltpu.roll(a, HEAD_DIM, axis=1)
        else:
            b = pltpu.roll(b, HEAD_DIM, axis=1)
        cols.append(jnp.where(lane < HEAD_DIM, a, b))
    return jnp.concatenate(cols, axis=1)


def _pad_rows(x, rows):
    return jnp.concatenate([x, jnp.zeros((rows - x.shape[0], x.shape[1]), x.dtype)], axis=0)


def _moba_sample_kernel(pt_ref, q_ref, kn_ref, vn_ref, blast_ref, bown_ref, e_ref, kpool, vpool, o_ref,
                        kbuf, vbuf, ksem, vsem, s_s, *, n_pages, page, n_samples, unroll):
    s = pl.program_id(0)
    slot = s % 2
    ppb = MOBA_BLOCK // page
    nblk = n_pages // ppb
    t_new = q_ref.shape[0]
    rows = N_HEADS * t_new

    def k_copy(smp, sl, p):
        return pltpu.make_async_copy(kpool.at[pt_ref[smp, p]], kbuf.at[sl, p], ksem.at[sl])

    def v_copy(smp, sl, p):
        return pltpu.make_async_copy(vpool.at[pt_ref[smp, p]], vbuf.at[sl, p], vsem.at[sl])

    def start_all(smp, sl):
        def body(p, c):
            k_copy(smp, sl, p).start()
            v_copy(smp, sl, p).start()
            return c
        lax.fori_loop(0, n_pages, body, 0, unroll=8)

    def wait_all(copy, smp, sl):
        def body(p, c):
            copy(smp, sl, p).wait()
            return c
        lax.fori_loop(0, n_pages, body, 0, unroll=8)

    def block_t(buf, j):
        return jnp.concatenate([buf[slot, j * ppb + t] for t in range(ppb)], axis=1)

    blk_lane = lax.broadcasted_iota(jnp.int32, (KV_W, nblk), 1)

    @pl.when(s == 0)
    def _first():
        start_all(0, 0)

    wait_all(k_copy, s, slot)
    nxt = jnp.minimum(s + 1, n_samples - 1)

    def kbar_body(j, kbar_t):
        for t in range(ppb):
            k_copy(nxt, 1 - slot, j * ppb + t).start()
            v_copy(nxt, 1 - slot, j * ppb + t).start()
        pages = kbuf[slot, j * ppb]
        for t in range(1, ppb):
            pages = pages + kbuf[slot, j * ppb + t]
        col = jnp.sum(pages, axis=1, keepdims=True) * (1.0 / MOBA_BLOCK)
        return jnp.where(blk_lane == j, col, kbar_t)

    kbar_t = lax.fori_loop(0, nblk, kbar_body, jnp.zeros((KV_W, nblk), F32), unroll=unroll[0])

    qbd = _heads_to_rows(q_ref[...] * (SCALE * LOG2E))
    qbd16 = qbd.astype(BF16)

    scores = _dot_f32(qbd, kbar_t)
    sel_t = _top3_mask(scores.T, nblk)
    negm = jnp.where(sel_t.T > 0.0, 0.0, MASK_NEG)
    lhs16 = jnp.concatenate([qbd, negm, jnp.zeros((rows, e_ref.shape[1] - nblk), F32)], axis=1).astype(BF16)

    def logits(j):
        rhs16 = jnp.concatenate([block_t(kbuf, j).astype(BF16), e_ref[j]], axis=0)
        return _dot(lhs16, rhs16)

    def fold(x):
        return x[:, :LANES], x[:, LANES:]

    def far(j, mrun):
        st = logits(j)
        s_s[j] = st
        lo, hi = fold(st)
        return jnp.maximum(mrun, jnp.maximum(lo, hi))

    mrun = lax.fori_loop(0, nblk - 1, far, jnp.full((rows, LANES), NEG_INF, F32), unroll=unroll[1])
    s_last = logits(nblk - 1) + blast_ref[...]
    s_s[nblk - 1] = s_last
    kn16 = _pad_rows(kn_ref[...], LANES).astype(BF16)
    s_own = _dot_nt(qbd16, kn16) + bown_ref[...]
    for part in fold(s_last) + (s_own,):
        mrun = jnp.maximum(mrun, part)
    m = jnp.max(mrun, axis=1, keepdims=True)

    wait_all(v_copy, s, slot)

    def pv(j, carry):
        acc, lsum = carry
        p = jnp.exp2(s_s[j] - m)
        lo, hi = fold(p)
        return acc + _dot_nt(p.astype(BF16), block_t(vbuf, j).astype(BF16)), lsum + (lo + hi)

    p_own = jnp.exp2(s_own - m)
    acc0 = _dot(p_own.astype(BF16), _pad_rows(vn_ref[...], LANES).astype(BF16))
    acc, lsum = lax.fori_loop(0, nblk, pv, (acc0, p_own), unroll=unroll[2])
    den = jnp.sum(lsum, axis=1, keepdims=True)
    o_ref[...] = _rows_to_heads(acc / den, t_new)

    @pl.when(s == n_samples - 1)
    def _drain():
        wait_all(k_copy, nxt, 1 - slot)
        wait_all(v_copy, nxt, 1 - slot)


def moba_sample(q, k_new, v_new, k_pool, v_pool, page_table, rel_bias, t_new):
    n, n_pages = page_table.shape
    page = k_pool.shape[2]
    past = n_pages * page
    nblk = past // MOBA_BLOCK
    rows = N_HEADS * t_new
    tok = (np.arange(rows) % t_new)[None, :]
    kk = np.arange(MOBA_BLOCK)[:, None]
    bm_last = _bucket_matrix(MOBA_BLOCK + tok - kk, np.ones((MOBA_BLOCK, rows), bool))
    ko = np.arange(LANES)[:, None]
    bm_own = _bucket_matrix(tok - ko, (ko <= tok) & (ko < t_new))
    rbx = jnp.repeat(rel_bias, t_new, axis=1)[None]
    far = rbx[:, N_BUCKETS - 1:N_BUCKETS, :]
    b_last = bias_table(bm_last, rbx, far, LOG2E)[0].T
    b_own = bias_table(bm_own, rbx, far, LOG2E)[0].T
    onehot_np = np.zeros((nblk, KV_W, MOBA_BLOCK), np.float32)
    onehot_np[np.arange(nblk), np.arange(nblk)] = 1.0
    onehot = jnp.asarray(onehot_np, BF16)

    grid_spec = pltpu.PrefetchScalarGridSpec(
        num_scalar_prefetch=1,
        grid=(n,),
        in_specs=[pl.BlockSpec((t_new, q.shape[1]), lambda s, pt: (s, 0)),
                  pl.BlockSpec((t_new, KV_W), lambda s, pt: (s, 0)),
                  pl.BlockSpec((t_new, KV_W), lambda s, pt: (s, 0)),
                  pl.BlockSpec((rows, MOBA_BLOCK), lambda s, pt: (0, 0)),
                  pl.BlockSpec((rows, LANES), lambda s, pt: (0, 0)),
                  pl.BlockSpec((nblk, KV_W, MOBA_BLOCK), lambda s, pt: (0, 0, 0)),
                  pl.BlockSpec(memory_space=pl.ANY),
                  pl.BlockSpec(memory_space=pl.ANY)],
        out_specs=pl.BlockSpec((t_new, q.shape[1]), lambda s, pt: (s, 0)),
        scratch_shapes=[pltpu.VMEM((2, n_pages, KV_W, page), F32),
                        pltpu.VMEM((2, n_pages, KV_W, page), F32),
                        pltpu.SemaphoreType.DMA((2,)),
                        pltpu.SemaphoreType.DMA((2,)),
                        pltpu.VMEM((nblk, rows, MOBA_BLOCK), F32)],
    )
    return pl.pallas_call(
        functools.partial(_moba_sample_kernel, n_pages=n_pages, page=page, n_samples=n, unroll=(32, 63, 32)),
        grid_spec=grid_spec,
        out_shape=jax.ShapeDtypeStruct(q.shape, F32),
        compiler_params=_params("arbitrary"),
        name="moba_sample",
    )(page_table, q, k_new, v_new, b_last, b_own, onehot, k_pool, v_pool)


def _conv_tail(y, cb_ref, lg_ref, lb_ref, gate):
    y = y + cb_ref[...]
    mu = jnp.mean(y, axis=-1, keepdims=True)
    var = jnp.mean(jnp.square(y - mu), axis=-1, keepdims=True)
    yn = (y - mu) * lax.rsqrt(var + EPS) * lg_ref[...] + lb_ref[...]
    return _silu(yn) * gate


HALO = 32


SUBLANES = 8
LANES = 128
CONV_ROWS = 64


def _conv_prompt_kernel(u_ref, prev_ref, g_ref, w_ref, cb_ref, lg_ref, lb_ref, o_ref, xs, xr, ys, *, tl):
    t = pl.program_id(1)
    c = u_ref.shape[1]
    xs[0:HALO, :] = jnp.where(t > 0, prev_ref[...], 0.0)
    xs[HALO:, :] = u_ref[...]
    span = tl + HALO - SUBLANES
    for r in range(1, SUBLANES):
        xr[r - 1] = xs[pl.ds(r, span), :]
    off = HALO - (CONV_W - 1)

    def taps(base, lt):
        ln = slice(lt * LANES, (lt + 1) * LANES)
        acc = jnp.zeros((CONV_ROWS, LANES), F32)
        for k in range(CONV_W):
            a, r = divmod(off + k, SUBLANES)
            src = xs if r == 0 else xr.at[r - 1]
            start = base + a * SUBLANES
            acc = acc + src[start:start + CONV_ROWS, ln] * w_ref[k:k + 1, ln]
        ys[base:base + CONV_ROWS, ln] = acc

    for base in range(0, tl, CONV_ROWS):
        for lt in range(0, c // LANES, 2):
            @pl.when(t >= 0)
            def _():
                taps(base, lt)
                taps(base, lt + 1)
    o_ref[...] = _conv_tail(ys[...], cb_ref, lg_ref, lb_ref, g_ref[...].astype(F32)).astype(o_ref.dtype)


def conv_prompt(u, szb, conv_w, conv_b, ln_g, ln_b, batch, seq, tl=256):
    c = u.shape[1]
    nt = seq // tl
    vec = pl.BlockSpec((1, c), lambda b, t: (0, 0))
    return pl.pallas_call(
        functools.partial(_conv_prompt_kernel, tl=tl),
        grid=(batch, nt),
        in_specs=[pl.BlockSpec((tl, c), lambda b, t: (b * nt + t, 0)),
                  pl.BlockSpec((HALO, c), lambda b, t: (jnp.maximum((b * nt + t) * (tl // HALO) - 1, 0), 0)),
                  pl.BlockSpec((tl, c), lambda b, t: (b * nt + t, 0)),
                  pl.BlockSpec((CONV_W, c), lambda b, t: (0, 0)), vec, vec, vec],
        out_specs=pl.BlockSpec((tl, c), lambda b, t: (b * nt + t, 0)),
        out_shape=jax.ShapeDtypeStruct(u.shape, BF16),
        scratch_shapes=[pltpu.VMEM((HALO + tl, c), F32),
                        pltpu.VMEM((SUBLANES - 1, HALO + tl - SUBLANES, c), F32),
                        pltpu.VMEM((tl, c), F32)],
        compiler_params=_params("arbitrary", "arbitrary"),
        name="conv_prompt",
    )(u, u, szb, conv_w, conv_b.reshape(1, c), ln_g.reshape(1, c), ln_b.reshape(1, c))


def _conv_sample_kernel(xp_ref, g_ref, w_ref, cb_ref, lg_ref, lb_ref, o_ref, *, t_new):
    ns, _, c = xp_ref.shape
    y = jnp.zeros((ns, t_new, c), F32)
    for k in range(CONV_W):
        y = y + xp_ref[:, pl.ds(k, t_new), :] * w_ref[k:k + 1, :]
    o_ref[...] = _conv_tail(y, cb_ref, lg_ref, lb_ref, g_ref[...].astype(F32)).astype(o_ref.dtype)


def conv_sample(xp, szb, conv_w, conv_b, ln_g, ln_b, t_new, ns=8):
    n, rows, c = xp.shape
    vec = pl.BlockSpec((1, c), lambda i: (0, 0))
    return pl.pallas_call(
        functools.partial(_conv_sample_kernel, t_new=t_new),
        grid=(n // ns,),
        in_specs=[pl.BlockSpec((ns, rows, c), lambda i: (i, 0, 0)),
                  pl.BlockSpec((ns, t_new, c), lambda i: (i, 0, 0)),
                  pl.BlockSpec((CONV_W, c), lambda i: (0, 0)), vec, vec, vec],
        out_specs=pl.BlockSpec((ns, t_new, c), lambda i: (i, 0, 0)),
        out_shape=jax.ShapeDtypeStruct((n, t_new, c), BF16),
        compiler_params=_params("arbitrary"),
        name="conv_sample",
    )(xp, szb, conv_w, conv_b.reshape(1, c), ln_g.reshape(1, c), ln_b.reshape(1, c))


def _swa_prompt_kernel(q_ref, kc_ref, kp_ref, vc_ref, vp_ref, bias_ref, sink_ref, o_ref, s_s, p_s, *, chunk, sub):
    n = pl.program_id(1)
    w = SWA_WINDOW
    rows = GROUP * w
    nch, per = rows // chunk, chunk // sub
    gw = GROUP * HEAD_DIM
    kcat = jnp.concatenate([kp_ref[...], kc_ref[...]], axis=0).astype(BF16)
    v_t = jnp.concatenate([vp_ref[...], vc_ref[...]], axis=0).T.astype(BF16)
    ones = jnp.ones((V_AUG - HEAD_DIM, 2 * w), BF16)
    no_prev = (lax.broadcasted_iota(jnp.int32, (2 * w, sub), 0) < w) & (n == 0)
    qpads, v_augs, sinks = [], [], []
    for g in range(N_KV):
        q_t = (q_ref[:, g * gw:(g + 1) * gw] * (SCALE * LOG2E)).T
        qs = jnp.concatenate([q_t[h * HEAD_DIM:(h + 1) * HEAD_DIM, :] for h in range(GROUP)], axis=1)
        zeros = jnp.zeros_like(qs)
        qpads.append(jnp.concatenate([qs, zeros] if g == 0 else [zeros, qs], axis=0).astype(BF16))
        v_augs.append(jnp.concatenate([v_t[g * HEAD_DIM:(g + 1) * HEAD_DIM], ones], axis=0))
        sinks.append(sink_ref[g] * LOG2E)

    def qk(g, c):
        s_t = _dot(kcat, qpads[g][:, c * chunk:(c + 1) * chunk])
        for k in range(per):
            s_s[g, c * per + k] = s_t[:, k * sub:(k + 1) * sub]

    def softmax(g, cs):
        s_t = jnp.where(no_prev, NEG_INF, s_s[g, cs] + bias_ref[g, cs])
        m = jnp.maximum(jnp.max(s_t, axis=0, keepdims=True), sinks[g][:, cs * sub:(cs + 1) * sub])
        p_s[g, cs] = jnp.exp2(s_t - m).astype(BF16)
        return m

    def pv(g, c, ms):
        p = jnp.concatenate([p_s[g, c * per + k] for k in range(per)], axis=1)
        o_t = _dot(v_augs[g], p)
        den = o_t[HEAD_DIM:HEAD_DIM + 1] + jnp.exp2(sinks[g][:, c * chunk:(c + 1) * chunk] - jnp.concatenate(ms, axis=1))
        return o_t[:HEAD_DIM] / den

    for c in range(nch):
        qk(0, c)
    ms = [[], []]
    outs = [[], []]
    for c in range(nch):
        qk(1, c)
        ms[0] += [softmax(0, c * per + k) for k in range(per)]
    for c in range(nch):
        outs[0].append(pv(0, c, ms[0][c * per:(c + 1) * per]))
        ms[1] += [softmax(1, c * per + k) for k in range(per)]
    for c in range(nch):
        outs[1].append(pv(1, c, ms[1][c * per:(c + 1) * per]))
    tiles = []
    for g in range(N_KV):
        o_t = jnp.concatenate(outs[g], axis=1)
        tiles.append(jnp.concatenate([o_t[:, h * w:(h + 1) * w] for h in range(GROUP)], axis=0).T)
    o_ref[...] = jnp.concatenate(tiles, axis=1).astype(o_ref.dtype)


def swa_prompt(q, k, v, sinks, rel_bias, batch, seq):
    w = SWA_WINDOW
    nb = seq // w
    rows = GROUP * w
    kidx = np.arange(2 * w)[:, None]
    qq = (np.arange(rows) % w)[None, :]
    dist = w + qq - kidx
    bm = _bucket_matrix(dist, (dist >= 0) & (dist < w))
    rbx = jnp.repeat(rel_bias.reshape(N_BUCKETS, N_KV, GROUP).transpose(1, 0, 2), w, axis=2)
    chunk, sub = 256, 128
    nsub = rows // sub
    bias = bias_table(bm, rbx, jnp.zeros((N_KV, 1, rows), F32), LOG2E)
    bias = bias.reshape(N_KV, 2 * w, nsub, sub).transpose(0, 2, 1, 3)
    sink_x = jnp.repeat(sinks.reshape(N_KV, 1, GROUP), w, axis=2)
    d = q.shape[1]
    cur = lambda b, n: (b * nb + n, 0)
    prev = lambda b, n: (jnp.maximum(b * nb + n - 1, 0), 0)
    return pl.pallas_call(
        functools.partial(_swa_prompt_kernel, chunk=chunk, sub=sub),
        grid=(batch, nb),
        in_specs=[pl.BlockSpec((w, d), cur),
                  pl.BlockSpec((w, KV_W), cur), pl.BlockSpec((w, KV_W), prev),
                  pl.BlockSpec((w, KV_W), cur), pl.BlockSpec((w, KV_W), prev),
                  pl.BlockSpec((N_KV, nsub, 2 * w, sub), lambda b, n: (0, 0, 0, 0)),
                  pl.BlockSpec((N_KV, 1, rows), lambda b, n: (0, 0, 0))],
        out_specs=pl.BlockSpec((w, d), cur),
        out_shape=jax.ShapeDtypeStruct(q.shape, BF16),
        scratch_shapes=[pltpu.VMEM((N_KV, nsub, 2 * w, sub), F32),
                        pltpu.VMEM((N_KV, nsub, 2 * w, sub), BF16)],
        compiler_params=_params("arbitrary", "arbitrary"),
        name="swa_prompt",
    )(q, k, k, v, v, bias, sink_x)


def _swa_sample_kernel(q_ref, kb_ref, kn_ref, vb_ref, vn_ref, bbuf_ref, bnew_ref, sink_ref, o_ref, *, t_new):
    ns = kb_ref.shape[0]
    sink = sink_ref[...]
    for i in range(ns):
        rs = slice(i * t_new, (i + 1) * t_new)
        qbd = _heads_to_rows(q_ref[rs, :] * SCALE).astype(BF16)
        s_buf = _dot_nt(qbd, kb_ref[i].astype(BF16)) + bbuf_ref[...]
        s_new = _dot_nt(qbd, _pad_rows(kn_ref[rs, :], LANES).astype(BF16)) + bnew_ref[...]
        m = jnp.maximum(jnp.maximum(jnp.max(s_buf, axis=1, keepdims=True), jnp.max(s_new, axis=1, keepdims=True)),
                        sink)
        p_buf, p_new = jnp.exp(s_buf - m), jnp.exp(s_new - m)
        den = jnp.sum(p_buf, axis=1, keepdims=True) + jnp.sum(p_new, axis=1, keepdims=True) + jnp.exp(sink - m)
        acc = (_dot(p_buf.astype(BF16), vb_ref[i].astype(BF16))
               + _dot(p_new.astype(BF16), _pad_rows(vn_ref[rs, :], LANES).astype(BF16)))
        o_ref[rs, :] = _rows_to_heads(acc / den, t_new)


def swa_sample(q, k_buf, v_buf, k_new, v_new, sinks, rel_bias, t_new, ns=8):
    n, wb, _ = k_buf.shape
    rows = N_HEADS * t_new
    tok = (np.arange(rows) % t_new)[None, :]
    dist_buf = tok + wb - np.arange(wb)[:, None]
    j_new = np.arange(LANES)[:, None]
    dist_new = tok - j_new
    bm_buf = _bucket_matrix(dist_buf, (dist_buf >= 0) & (dist_buf < SWA_WINDOW))
    bm_new = _bucket_matrix(dist_new, (dist_new >= 0) & (dist_new < SWA_WINDOW) & (j_new < t_new))
    rbx = jnp.repeat(rel_bias, t_new, axis=1)[None]
    zero = jnp.zeros((1, 1, rows), F32)
    b_buf = bias_table(bm_buf, rbx, zero)[0].T
    b_new = bias_table(bm_new, rbx, zero)[0].T
    sink_r = jnp.repeat(sinks, t_new).reshape(rows, 1)
    d = q.shape[1]
    tile = lambda width: pl.BlockSpec((ns * t_new, width), lambda i: (i, 0))
    buf = pl.BlockSpec((ns, wb, KV_W), lambda i: (i, 0, 0))
    return pl.pallas_call(
        functools.partial(_swa_sample_kernel, t_new=t_new),
        grid=(n // ns,),
        in_specs=[tile(d), buf, tile(KV_W), buf, tile(KV_W),
                  pl.BlockSpec((rows, wb), lambda i: (0, 0)),
                  pl.BlockSpec((rows, LANES), lambda i: (0, 0)),
                  pl.BlockSpec((rows, 1), lambda i: (0, 0))],
        out_specs=tile(d),
        out_shape=jax.ShapeDtypeStruct(q.shape, F32),
        compiler_params=_params("arbitrary"),
        name="swa_sample",
    )(q, k_buf, k_new, v_buf, v_new, b_buf, b_new, sink_r)


def kernel(x_prompt, x_sample, c_prompt, c_sample, cache_a_k, cache_a_v, page_table, cache_b_conv, cache_c_k, cache_c_v, rel_bias, norm_a, mod_w_a, mod_b_a, w_in_a, conv_w_b, conv_b_b, ln_g_b, ln_b_b, w_out_a, norm_c, mod_w_c, mod_b_c, w_in_c, sinks_c, w_out_c, final_norm):
    batch, seq, d = x_prompt.shape
    n, t_new, _ = x_sample.shape
    n_pool, page = cache_a_k.shape[1], cache_a_k.shape[2]
    hq = N_HEADS * HEAD_DIM
    cb = conv_w_b.shape[2]
    assert norm_a.shape[0] == 1 and norm_c.shape[0] == 1, "one A/B layer followed by one C layer"

    tm_p = 1024
    tm_s = min(256, n * t_new)
    xp = x_prompt.reshape(batch * seq, d)
    xs = x_sample

    c_all = jnp.concatenate([c_prompt, c_sample], axis=0)
    c_rows = -(-c_all.shape[0] // 8) * 8
    c_all = jnp.pad(c_all, ((0, c_rows - c_all.shape[0]), (0, 0)))

    def split_mod(m):
        mp = m[:batch].reshape(batch, 1, 3, d)
        ms = m[batch:batch + n].reshape(n, 1, 3, d)
        return [(mp[:, :, j], ms[:, :, j]) for j in range(3)]

    (sh_a, sc_a, gt_a) = split_mod(modulation(c_all, mod_w_a[0], mod_b_a[0]))
    (sh_c, sc_c, gt_c) = split_mod(modulation(c_all, mod_w_c[0], mod_b_c[0]))

    kv0, kv1 = hq, hq + 2 * KV_W
    za0 = kv1
    ga0 = za0 + hq
    gb0 = ga0 + cb
    zb0 = gb0 + cb
    segs_a = (("raw", 0, hq), ("raw", kv0, kv0 + KV_W), ("raw", kv0 + KV_W, kv1),
              ("silu", za0, ga0), ("glu", ga0, gb0, gb0, zb0), ("silu", zb0, zb0 + cb))
    dts_a = (F32, F32, F32, BF16, F32, BF16)
    w_in_a16 = w_in_a[0].astype(BF16)
    w_out_a16 = w_out_a[0].astype(BF16)
    k_pool = cache_a_k[0].transpose(0, 2, 3, 1).reshape(n_pool, KV_W, page)
    v_pool = cache_a_v[0].transpose(0, 2, 3, 1).reshape(n_pool, KV_W, page)

    q, k, v, sza, u, szb = ln_inproj(xp, sh_a[0], sc_a[0], norm_a[0], w_in_a16, segs_a, dts_a, tm_p)
    oa = moba_prompt(q, k.reshape(batch, seq, KV_W), v.reshape(batch, seq, KV_W), rel_bias, batch, seq)
    ob = conv_prompt(u, szb, conv_w_b[0], conv_b_b[0], ln_g_b[0], ln_b_b[0], batch, seq)
    xp1 = out_proj([(oa, sza, w_out_a16[:hq])], [(ob, w_out_a16[hq:])], xp, gt_a[0], None, tm_p)
    ak_p, av_p = k, v
    bc_p = u.reshape(batch, seq, cb)[:, seq - (CONV_W - 1):]

    q, k, v, sza, u, szb = ln_inproj(xs, sh_a[1], sc_a[1], norm_a[0], w_in_a16, segs_a, dts_a, tm_s)
    oa = moba_sample(q, k, v, k_pool, v_pool, page_table, rel_bias, t_new)
    xcat = jnp.concatenate([cache_b_conv[0], u.reshape(n, t_new, cb)], axis=1)
    ob = conv_sample(xcat, szb.reshape(n, t_new, cb), conv_w_b[0], conv_b_b[0], ln_g_b[0], ln_b_b[0], t_new)
    xs1 = out_proj([(oa, sza, w_out_a16[:hq])], [(ob.reshape(n * t_new, cb), w_out_a16[hq:])],
                   xs, gt_a[1], None, tm_s)
    ak_s, av_s = k, v
    bc_s = xcat[:, t_new:]

    segs_c = (("raw", 0, hq), ("raw", hq, hq + KV_W), ("raw", hq + KV_W, hq + 2 * KV_W),
              ("silu", hq + 2 * KV_W, 2 * hq + 2 * KV_W))
    dts_c = (F32, F32, F32, BF16)
    w_in_c16 = w_in_c[0].astype(BF16)
    w_out_c16 = w_out_c[0].astype(BF16)

    q, k, v, sz = ln_inproj(xp1, sh_c[0], sc_c[0], norm_c[0], w_in_c16, segs_c, dts_c, tm_p)
    o = swa_prompt(q, k, v, sinks_c[0], rel_bias, batch, seq)
    y_prompt = out_proj([(o, sz, w_out_c16)], [], xp1, gt_c[0], final_norm, tm_p)
    wb_p = min(SWA_WINDOW, seq)
    ck_p = k.reshape(batch, seq, KV_W)[:, seq - wb_p:]
    cv_p = v.reshape(batch, seq, KV_W)[:, seq - wb_p:]

    q, k, v, sz = ln_inproj(xs1, sh_c[1], sc_c[1], norm_c[0], w_in_c16, segs_c, dts_c, tm_s)
    wb_s = cache_c_k.shape[2]
    kb, vb = cache_c_k[0].reshape(n, wb_s, KV_W), cache_c_v[0].reshape(n, wb_s, KV_W)
    o = swa_sample(q, kb, vb, k, v, sinks_c[0], rel_bias, t_new)
    y_sample = out_proj([(o, sz, w_out_c16)], [], xs1, gt_c[1], final_norm, tm_s)
    ck_s = jnp.concatenate([kb, k.reshape(n, t_new, KV_W)], axis=1)[:, t_new:]
    cv_s = jnp.concatenate([vb, v.reshape(n, t_new, KV_W)], axis=1)[:, t_new:]

    def kv5(a, lead):
        return a.reshape((1,) + lead + (N_KV, HEAD_DIM))

    return (y_prompt.reshape(batch, seq, d), y_sample,
            kv5(ak_p, (batch, seq)), kv5(av_p, (batch, seq)),
            kv5(ak_s, (n, t_new)), kv5(av_s, (n, t_new)),
            bc_p[None], bc_s[None],
            kv5(ck_p, (batch, wb_p)), kv5(cv_p, (batch, wb_p)),
            kv5(ck_s, (n, wb_s)), kv5(cv_s, (n, wb_s)))
```

```python
import functools
import math

import jax
import jax.numpy as jnp
import numpy as np
from jax import lax
from jax.experimental import pallas as pl
from jax.experimental.pallas import tpu as pltpu

F32 = jnp.float32
BF16 = jnp.bfloat16
NEG_INF = float("-inf")
MASK_NEG = -1e30

HEAD_DIM = 64
N_HEADS = 16
N_KV = 2
GROUP = N_HEADS // N_KV
KV_W = N_KV * HEAD_DIM
MOBA_BLOCK = 256
MOBA_TOPK = 3
CONV_W = 31
SWA_WINDOW = 128
N_BUCKETS = 32
MAX_DISTANCE = 128
EPS = 1e-6
SCALE = HEAD_DIM ** -0.5
LOG2E = math.log2(math.e)

VMEM_LIMIT = 56 * 2**20


def _params(*sem):
    return pltpu.CompilerParams(dimension_semantics=sem, vmem_limit_bytes=VMEM_LIMIT)


def _silu(z):
    return z * jax.nn.sigmoid(z)


def _dot(a, b):
    return jnp.dot(a, b, preferred_element_type=F32)


def _dot_nt(a, b):
    return lax.dot_general(a, b, (((1,), (1,)), ((), ())), preferred_element_type=F32)


def _dot_f32(a, b):
    return jnp.dot(a, b, preferred_element_type=F32, precision=lax.Precision.HIGHEST)


def _t5_bucket_np(n):
    n = np.maximum(n, 0)
    max_exact = N_BUCKETS // 2
    nf = np.maximum(n, 1).astype(np.float32)
    large = max_exact + (np.log(nf / np.float32(max_exact)) / np.float32(math.log(MAX_DISTANCE / max_exact))
                         * np.float32(N_BUCKETS - max_exact)).astype(np.int32)
    large = np.minimum(large, N_BUCKETS - 1)
    return np.where(n < max_exact, n, large).astype(np.int32)


def _bucket_matrix(dist, valid):
    return np.where(valid, _t5_bucket_np(dist), -1).astype(np.int32)


def _bias_kernel(bm_ref, rb_ref, sub_ref, o_ref, *, mul):
    bm = bm_ref[...]
    acc = jnp.zeros(bm.shape, F32)
    for b in range(N_BUCKETS):
        acc = jnp.where(bm == b, rb_ref[b:b + 1, :], acc)
    o_ref[...] = jnp.where(bm < 0, NEG_INF, (acc - sub_ref[...]) * mul)


def bias_table(bm, rbx, sub, mul=1.0):
    g, _, c = rbx.shape
    r = bm.shape[0]
    return pl.pallas_call(
        functools.partial(_bias_kernel, mul=mul),
        grid=(g,),
        in_specs=[pl.BlockSpec((r, c), lambda i: (0, 0)),
                  pl.BlockSpec((None, N_BUCKETS, c), lambda i: (i, 0, 0)),
                  pl.BlockSpec((None, 1, c), lambda i: (i, 0, 0))],
        out_specs=pl.BlockSpec((None, r, c), lambda i: (i, 0, 0)),
        out_shape=jax.ShapeDtypeStruct((g, r, c), F32),
        compiler_params=_params("arbitrary"),
        name="bias_table",
    )(jnp.asarray(bm), rbx, sub)


def _mod_kernel(c_ref, w_ref, b_ref, o_ref):
    o_ref[...] = _dot_f32(_silu(c_ref[...]), w_ref[...]) + b_ref[...]


def modulation(c, w, b):
    n, d = c.shape
    m = w.shape[1]
    tn = 512
    return pl.pallas_call(
        _mod_kernel,
        grid=(m // tn,),
        in_specs=[pl.BlockSpec((n, d), lambda j: (0, 0)),
                  pl.BlockSpec((d, tn), lambda j: (0, j)),
                  pl.BlockSpec((1, tn), lambda j: (0, j))],
        out_specs=pl.BlockSpec((n, tn), lambda j: (0, j)),
        out_shape=jax.ShapeDtypeStruct((n, m), F32),
        compiler_params=_params("arbitrary"),
        name="modulation",
    )(c, w, b.reshape(1, m))


def _ln_inproj_kernel(x_ref, shift_ref, scale_ref, g_ref, w_ref, *out_refs, segs):
    x = x_ref[...]
    y = x * lax.rsqrt(jnp.mean(x * x, axis=-1, keepdims=True) + EPS)
    h = (y * g_ref[...]) * (1.0 + scale_ref[...]) + shift_ref[...]
    h16 = h.reshape(-1, h.shape[-1]).astype(BF16)
    for o_ref, seg in zip(out_refs, segs, strict=True):
        kind, lo, hi = seg[0], seg[1], seg[2]
        z = _dot(h16, w_ref[:, lo:hi])
        if kind == "silu":
            z = _silu(z)
        elif kind == "glu":
            z = z * jax.nn.sigmoid(_dot(h16, w_ref[:, seg[3]:seg[4]]))
        o_ref[...] = z.astype(o_ref.dtype)


def ln_inproj(x, shift, scale, norm_g, w16, segs, out_dtypes, tm):
    d = x.shape[-1]
    if x.ndim == 2:
        r = x.shape[0]
        nt = r // tm
        per_group = nt // shift.shape[0]
        x_spec = pl.BlockSpec((tm, d), lambda i: (i, 0))
        mod_spec = pl.BlockSpec((None, 1, d), lambda i: (i // per_group, 0, 0))
    else:
        n, t_new, _ = x.shape
        r = n * t_new
        nt = r // tm
        x_spec = pl.BlockSpec((tm // t_new, t_new, d), lambda i: (i, 0, 0))
        mod_spec = pl.BlockSpec((tm // t_new, 1, d), lambda i: (i, 0, 0))
    out_shape = [jax.ShapeDtypeStruct((r, s[2] - s[1]), dt) for s, dt in zip(segs, out_dtypes, strict=True)]
    out_specs = [pl.BlockSpec((tm, s[2] - s[1]), lambda i: (i, 0)) for s in segs]
    return pl.pallas_call(
        functools.partial(_ln_inproj_kernel, segs=segs),
        grid=(nt,),
        in_specs=[x_spec, mod_spec, mod_spec,
                  pl.BlockSpec((1, d), lambda i: (0, 0)),
                  pl.BlockSpec(w16.shape, lambda i: (0, 0), pipeline_mode=pl.Buffered(1))],
        out_specs=out_specs,
        out_shape=out_shape,
        compiler_params=_params("arbitrary"),
        name="ln_inproj",
    )(x, shift, scale, norm_g.reshape(1, d), w16)


def _out_proj_kernel(*refs, n_gated, n_plain, final_norm):
    it = iter(refs)
    y = None
    for _ in range(n_gated):
        a_ref, m_ref, w_ref = next(it), next(it), next(it)
        t = _dot((a_ref[...] * m_ref[...].astype(F32)).astype(BF16), w_ref[...])
        y = t if y is None else y + t
    for _ in range(n_plain):
        a_ref, w_ref = next(it), next(it)
        t = _dot(a_ref[...], w_ref[...])
        y = t if y is None else y + t
    x_ref, gate_ref = next(it), next(it)
    xn = x_ref[...] + gate_ref[...] * y.reshape(x_ref.shape)
    if final_norm:
        fg_ref = next(it)
        xn = xn * lax.rsqrt(jnp.mean(xn * xn, axis=-1, keepdims=True) + EPS) * fg_ref[...]
    o_ref = next(it)
    o_ref[...] = xn


def out_proj(gated, plain, x, gate, final_g, tm):
    d = x.shape[-1]
    if x.ndim == 2:
        nt = x.shape[0] // tm
        per_group = nt // gate.shape[0]
        x_spec = pl.BlockSpec((tm, d), lambda i: (i, 0))
        gate_spec = pl.BlockSpec((None, 1, d), lambda i: (i // per_group, 0, 0))
    else:
        n, t_new, _ = x.shape
        nt = n * t_new // tm
        x_spec = pl.BlockSpec((tm // t_new, t_new, d), lambda i: (i, 0, 0))
        gate_spec = pl.BlockSpec((tm // t_new, 1, d), lambda i: (i, 0, 0))
    args, specs = [], []
    for a, m, w in gated:
        args += [a, m, w]
        specs += [pl.BlockSpec((tm, a.shape[1]), lambda i: (i, 0)),
                  pl.BlockSpec((tm, m.shape[1]), lambda i: (i, 0)),
                  pl.BlockSpec(w.shape, lambda i: (0, 0))]
    for a, w in plain:
        args += [a, w]
        specs += [pl.BlockSpec((tm, a.shape[1]), lambda i: (i, 0)),
                  pl.BlockSpec(w.shape, lambda i: (0, 0))]
    args += [x, gate]
    specs += [x_spec, gate_spec]
    if final_g is not None:
        args.append(final_g.reshape(1, d))
        specs.append(pl.BlockSpec((1, d), lambda i: (0, 0)))
    return pl.pallas_call(
        functools.partial(_out_proj_kernel, n_gated=len(gated), n_plain=len(plain),
                          final_norm=final_g is not None),
        grid=(nt,),
        in_specs=specs,
        out_specs=x_spec,
        out_shape=jax.ShapeDtypeStruct(x.shape, F32),
        compiler_params=_params("arbitrary"),
        name="out_proj",
    )(*args)


def _top3_mask(scores, n_valid):
    nb = scores.shape[0]
    blk = lax.broadcasted_iota(jnp.int32, scores.shape, 0)
    s = jnp.where(blk < n_valid, scores, NEG_INF)
    picked = jnp.zeros(scores.shape, F32)
    for _ in range(MOBA_TOPK):
        mx = jnp.max(s, axis=0, keepdims=True)
        first = jnp.min(jnp.where(s == mx, blk, nb), axis=0, keepdims=True)
        hit = blk == first
        picked = jnp.where(hit, 1.0, picked)
        s = jnp.where(hit, NEG_INF, s)
    return jnp.where(blk < n_valid, picked, 0.0)


V_AUG = HEAD_DIM + 16


def _moba_prompt_kernel(q_ref, k_ref, v_ref, bown_ref, badj_ref, o_ref,
                        k16_s, vT_s, kbar_s, qT16_s, sel_s, m_s, acc_s,
                        s0_s, s1_s, p0_s, p1_s, a0_s, a1_s, *, nblk, chunk, sub):
    g = pl.program_id(1)
    i = pl.program_id(2)
    rows = GROUP * MOBA_BLOCK
    nch = rows // chunk

    @pl.when((g == 0) & (i == 0))
    def _per_batch():
        ones = jnp.ones((V_AUG - HEAD_DIM, MOBA_BLOCK), BF16)
        for j in range(nblk):
            kb = k_ref[j * MOBA_BLOCK:(j + 1) * MOBA_BLOCK, :]
            k16_s[j * MOBA_BLOCK:(j + 1) * MOBA_BLOCK, :] = kb.astype(BF16)
            kbar_s[j:j + 1, :] = jnp.mean(kb, axis=0, keepdims=True)
            v_t = v_ref[j * MOBA_BLOCK:(j + 1) * MOBA_BLOCK, :].T.astype(BF16)
            for kv in range(N_KV):
                vT_s[kv, j] = jnp.concatenate([v_t[kv * HEAD_DIM:(kv + 1) * HEAD_DIM], ones], axis=0)

    qT = (q_ref[...] * (SCALE * LOG2E)).T
    qs = jnp.concatenate([qT[h * HEAD_DIM:(h + 1) * HEAD_DIM, :] for h in range(GROUP)], axis=1)
    qs16 = qs.astype(BF16)
    zeros = jnp.zeros_like(qs16)
    qpad16 = jnp.where(g == 0, jnp.concatenate([qs16, zeros], axis=0), jnp.concatenate([zeros, qs16], axis=0))
    for c in range(nch):
        qT16_s[c] = qpad16[:, c * chunk:(c + 1) * chunk]
    kbar = kbar_s[...]
    kbar_g = jnp.where(g == 0, kbar[:, :HEAD_DIM], kbar[:, HEAD_DIM:])
    sel_s[...] = _top3_mask(_dot_f32(kbar_g, qs), i)
    m_s[...] = jnp.full(m_s.shape, NEG_INF, F32)
    acc_s[...] = jnp.zeros(acc_s.shape, F32)

    n_far = i - 1
    sbuf, pbuf, abuf = (s0_s, s1_s), (p0_s, p1_s), (a0_s, a1_s)

    def blk_of(t):
        far_j = jnp.clip(t - 2, 0, jnp.maximum(n_far - 1, 0))
        return jnp.where(t == 0, i, jnp.where(t == 1, jnp.maximum(i - 1, 0), far_j))

    per = chunk // sub

    def stage_qk(t, slot, c):
        kb = k16_s[pl.ds(pl.multiple_of(blk_of(t) * MOBA_BLOCK, MOBA_BLOCK), MOBA_BLOCK), :]
        s_t = _dot(kb, qT16_s[c])
        for k in range(per):
            sbuf[slot][c * per + k] = s_t[:, k * sub:(k + 1) * sub]

    def stage_softmax(slot, c, on, bias_ref):
        ln = slice(c * sub, (c + 1) * sub)
        s_t = sbuf[slot][c]
        if bias_ref is not None:
            s_t = s_t + bias_ref[c]
        m_old = m_s[:, ln]
        m_new = jnp.maximum(m_old, jnp.where(on, jnp.max(s_t, axis=0, keepdims=True), NEG_INF))
        pbuf[slot][c] = jnp.exp2(s_t - jnp.where(on, m_new, jnp.inf)).astype(BF16)
        abuf[slot][:, ln] = jnp.exp2(m_old - m_new)
        m_s[:, ln] = m_new

    def stage_pv(t, slot, c):
        p = jnp.concatenate([pbuf[slot][c * per + k] for k in range(per)], axis=1)
        acc_s[c] = acc_s[c] * abuf[slot][:, c * chunk:(c + 1) * chunk] + _dot(vT_s[g, blk_of(t)], p)

    def pipe_half(t, slot, bias_ref=None, pv=True, qk=True):
        own = bias_ref is bown_ref
        valid = None if own else jnp.where(t == 1, i >= 1, t - 2 < n_far)
        j = blk_of(t)
        for c in range(nch):
            if qk:
                stage_qk(t + 1, 1 - slot, c)
            if own:
                on = jnp.full((1, chunk), True)
            else:
                on = (sel_s[pl.ds(j, 1), c * chunk:(c + 1) * chunk] > 0.0) & valid
            for k in range(per):
                stage_softmax(slot, c * per + k, on[:, k * sub:(k + 1) * sub], bias_ref)
            if pv:
                stage_pv(t - 1, 1 - slot, c)

    for c in range(nch):
        stage_qk(0, 0, c)
    pipe_half(0, 0, bown_ref, pv=False)
    pipe_half(1, 1, badj_ref)

    def pair(u, carry):
        pipe_half(2 + 2 * u, 0)
        pipe_half(3 + 2 * u, 1)
        return carry

    n_pairs = jnp.maximum(n_far, 0) // 2
    lax.fori_loop(0, n_pairs, pair, 0)
    t_tail = 2 + 2 * n_pairs
    odd = jnp.maximum(n_far, 0) % 2 == 1

    @pl.when(odd)
    def _tail():
        pipe_half(t_tail, 0, qk=False)
        for c in range(nch):
            stage_pv(t_tail, 0, c)

    @pl.when(jnp.logical_not(odd))
    def _drain():
        for c in range(nch):
            stage_pv(t_tail - 1, 1, c)

    acc = jnp.concatenate([acc_s[c] for c in range(nch)], axis=1)
    o_t = acc[:HEAD_DIM] / acc[HEAD_DIM:HEAD_DIM + 1]
    o_cat = jnp.concatenate([o_t[:, h * MOBA_BLOCK:(h + 1) * MOBA_BLOCK] for h in range(GROUP)], axis=0)
    o_ref[...] = o_cat.T.astype(o_ref.dtype)


def moba_prompt(q, k, v, rel_bias, batch, seq):
    nblk = seq // MOBA_BLOCK
    rows = GROUP * MOBA_BLOCK
    kk = np.arange(MOBA_BLOCK)[:, None]
    qq = (np.arange(rows) % MOBA_BLOCK)[None, :]
    bm_own = _bucket_matrix(qq - kk, qq >= kk)
    bm_adj = _bucket_matrix(MOBA_BLOCK + qq - kk, np.ones((MOBA_BLOCK, rows), bool))
    rbx = jnp.repeat(rel_bias.reshape(N_BUCKETS, N_KV, GROUP).transpose(1, 0, 2), MOBA_BLOCK, axis=2)
    far = rbx[:, N_BUCKETS - 1:N_BUCKETS, :]
    chunk, sub = 256, 128
    nsub = rows // sub

    def chunk_major(tbl):
        return tbl.reshape(N_KV, MOBA_BLOCK, nsub, sub).transpose(0, 2, 1, 3)

    b_own = chunk_major(bias_table(bm_own, rbx, far, LOG2E))
    b_adj = chunk_major(bias_table(bm_adj, rbx, far, LOG2E))
    qw = GROUP * HEAD_DIM
    tbl_spec = pl.BlockSpec((None, nsub, MOBA_BLOCK, sub), lambda b, g, i: (g, 0, 0, 0))
    return pl.pallas_call(
        functools.partial(_moba_prompt_kernel, nblk=nblk, chunk=chunk, sub=sub),
        grid=(batch, N_KV, nblk),
        in_specs=[pl.BlockSpec((MOBA_BLOCK, qw), lambda b, g, i: (b * nblk + i, g)),
                  pl.BlockSpec((None, seq, KV_W), lambda b, g, i: (b, 0, 0)),
                  pl.BlockSpec((None, seq, KV_W), lambda b, g, i: (b, 0, 0)),
                  tbl_spec, tbl_spec],
        out_specs=pl.BlockSpec((MOBA_BLOCK, qw), lambda b, g, i: (b * nblk + i, g)),
        out_shape=jax.ShapeDtypeStruct(q.shape, BF16),
        scratch_shapes=[pltpu.VMEM((seq, KV_W), BF16),
                        pltpu.VMEM((N_KV, nblk, V_AUG, MOBA_BLOCK), BF16),
                        pltpu.VMEM((nblk, KV_W), F32),
                        pltpu.VMEM((rows // chunk, KV_W, chunk), BF16),
                        pltpu.VMEM((nblk, rows), F32),
                        pltpu.VMEM((1, rows), F32),
                        pltpu.VMEM((rows // chunk, V_AUG, chunk), F32),
                        pltpu.VMEM((nsub, MOBA_BLOCK, sub), F32), pltpu.VMEM((nsub, MOBA_BLOCK, sub), F32),
                        pltpu.VMEM((nsub, MOBA_BLOCK, sub), BF16), pltpu.VMEM((nsub, MOBA_BLOCK, sub), BF16),
                        pltpu.VMEM((1, rows), F32), pltpu.VMEM((1, rows), F32)],
        compiler_params=_params("arbitrary", "arbitrary", "arbitrary"),
        name="moba_prompt",
    )(q, k, v, b_own, b_adj)


def _heads_to_rows(q):
    t = q.shape[0]
    lane = lax.broadcasted_iota(jnp.int32, (t, LANES), 1)
    pieces = []
    for h in range(N_HEADS):
        src = q[:, (h // 2) * LANES:(h // 2 + 1) * LANES]
        g = h // GROUP
        if h % 2 != g:
            src = pltpu.roll(src, HEAD_DIM, axis=1)
        keep = lane < HEAD_DIM if g == 0 else lane >= HEAD_DIM
        pieces.append(jnp.where(keep, src, 0.0))
    return jnp.concatenate(pieces, axis=0)


def _rows_to_heads(acc, t):
    lane = lax.broadcasted_iota(jnp.int32, (t, LANES), 1)
    cols = []
    for k in range(N_HEADS // 2):
        a = acc[2 * k * t:(2 * k + 1) * t, :]
        b = acc[(2 * k + 1) * t:(2 * k + 2) * t, :]
        if (2 * k) // GROUP == 1:
            a = pltpu.roll(a, HEAD_DIM, axis=1)
        else:
            b = pltpu.roll(b, HEAD_DIM, axis=1)
        cols.append(jnp.where(lane < HEAD_DIM, a, b))
    return jnp.concatenate(cols, axis=1)


def _pad_rows(x, rows):
    return jnp.concatenate([x, jnp.zeros((rows - x.shape[0], x.shape[1]), x.dtype)], axis=0)


def _moba_sample_kernel(pt_ref, q_ref, kn_ref, vn_ref, blast_ref, bown_ref, e_ref, kpool, vpool, o_ref,
                        kbuf, vbuf, ksem, vsem, s_s, *, n_pages, page, n_samples, unroll):
    s = pl.program_id(0)
    slot = s % 2
    ppb = MOBA_BLOCK // page
    nblk = n_pages // ppb
    t_new = q_ref.shape[0]
    rows = N_HEADS * t_new

    def k_copy(smp, sl, p):
        return pltpu.make_async_copy(kpool.at[pt_ref[smp, p]], kbuf.at[sl, p], ksem.at[sl])

    def v_copy(smp, sl, p):
        return pltpu.make_async_copy(vpool.at[pt_ref[smp, p]], vbuf.at[sl, p], vsem.at[sl])

    def start_all(smp, sl):
        def body(p, c):
            k_copy(smp, sl, p).start()
            v_copy(smp, sl, p).start()
            return c
        lax.fori_loop(0, n_pages, body, 0, unroll=8)

    def wait_all(copy, smp, sl):
        def body(p, c):
            copy(smp, sl, p).wait()
            return c
        lax.fori_loop(0, n_pages, body, 0, unroll=8)

    def block_t(buf, j):
        return jnp.concatenate([buf[slot, j * ppb + t] for t in range(ppb)], axis=1)

    blk_lane = lax.broadcasted_iota(jnp.int32, (KV_W, nblk), 1)

    @pl.when(s == 0)
    def _first():
        start_all(0, 0)

    wait_all(k_copy, s, slot)
    nxt = jnp.minimum(s + 1, n_samples - 1)

    def kbar_body(j, kbar_t):
        for t in range(ppb):
            k_copy(nxt, 1 - slot, j * ppb + t).start()
            v_copy(nxt, 1 - slot, j * ppb + t).start()
        pages = kbuf[slot, j * ppb]
        for t in range(1, ppb):
            pages = pages + kbuf[slot, j * ppb + t]
        col = jnp.sum(pages, axis=1, keepdims=True) * (1.0 / MOBA_BLOCK)
        return jnp.where(blk_lane == j, col, kbar_t)

    kbar_t = lax.fori_loop(0, nblk, kbar_body, jnp.zeros((KV_W, nblk), F32), unroll=unroll[0])

    qbd = _heads_to_rows(q_ref[...] * (SCALE * LOG2E))
    qbd16 = qbd.astype(BF16)

    scores = _dot_f32(qbd, kbar_t)
    sel_t = _top3_mask(scores.T, nblk)
    negm = jnp.where(sel_t.T > 0.0, 0.0, MASK_NEG)
    lhs16 = jnp.concatenate([qbd, negm, jnp.zeros((rows, e_ref.shape[1] - nblk), F32)], axis=1).astype(BF16)

    def logits(j):
        rhs16 = jnp.concatenate([block_t(kbuf, j).astype(BF16), e_ref[j]], axis=0)
        return _dot(lhs16, rhs16)

    def fold(x):
        return x[:, :LANES], x[:, LANES:]

    def far(j, mrun):
        st = logits(j)
        s_s[j] = st
        lo, hi = fold(st)
        return jnp.maximum(mrun, jnp.maximum(lo, hi))

    mrun = lax.fori_loop(0, nblk - 1, far, jnp.full((rows, LANES), NEG_INF, F32), unroll=unroll[1])
    s_last = logits(nblk - 1) + blast_ref[...]
    s_s[nblk - 1] = s_last
    kn16 = _pad_rows(kn_ref[...], LANES).astype(BF16)
    s_own = _dot_nt(qbd16, kn16) + bown_ref[...]
    for part in fold(s_last) + (s_own,):
        mrun = jnp.maximum(mrun, part)
    m = jnp.max(mrun, axis=1, keepdims=True)

    wait_all(v_copy, s, slot)

    def pv(j, carry):
        acc, lsum = carry
        p = jnp.exp2(s_s[j] - m)
        lo, hi = fold(p)
        return acc + _dot_nt(p.astype(BF16), block_t(vbuf, j).astype(BF16)), lsum + (lo + hi)

    p_own = jnp.exp2(s_own - m)
    acc0 = _dot(p_own.astype(BF16), _pad_rows(vn_ref[...], LANES).astype(BF16))
    acc, lsum = lax.fori_loop(0, nblk, pv, (acc0, p_own), unroll=unroll[2])
    den = jnp.sum(lsum, axis=1, keepdims=True)
    o_ref[...] = _rows_to_heads(acc / den, t_new)

    @pl.when(s == n_samples - 1)
    def _drain():
        wait_all(k_copy, nxt, 1 - slot)
        wait_all(v_copy, nxt, 1 - slot)


def moba_sample(q, k_new, v_new, k_pool, v_pool, page_table, rel_bias, t_new):
    n, n_pages = page_table.shape
    page = k_pool.shape[2]
    past = n_pages * page
    nblk = past // MOBA_BLOCK
    rows = N_HEADS * t_new
    tok = (np.arange(rows) % t_new)[None, :]
    kk = np.arange(MOBA_BLOCK)[:, None]
    bm_last = _bucket_matrix(MOBA_BLOCK + tok - kk, np.ones((MOBA_BLOCK, rows), bool))
    ko = np.arange(LANES)[:, None]
    bm_own = _bucket_matrix(tok - ko, (ko <= tok) & (ko < t_new))
    rbx = jnp.repeat(rel_bias, t_new, axis=1)[None]
    far = rbx[:, N_BUCKETS - 1:N_BUCKETS, :]
    b_last = bias_table(bm_last, rbx, far, LOG2E)[0].T
    b_own = bias_table(bm_own, rbx, far, LOG2E)[0].T
    onehot_np = np.zeros((nblk, KV_W, MOBA_BLOCK), np.float32)
    onehot_np[np.arange(nblk), np.arange(nblk)] = 1.0
    onehot = jnp.asarray(onehot_np, BF16)

    grid_spec = pltpu.PrefetchScalarGridSpec(
        num_scalar_prefetch=1,
        grid=(n,),
        in_specs=[pl.BlockSpec((t_new, q.shape[1]), lambda s, pt: (s, 0)),
                  pl.BlockSpec((t_new, KV_W), lambda s, pt: (s, 0)),
                  pl.BlockSpec((t_new, KV_W), lambda s, pt: (s, 0)),
                  pl.BlockSpec((rows, MOBA_BLOCK), lambda s, pt: (0, 0)),
                  pl.BlockSpec((rows, LANES), lambda s, pt: (0, 0)),
                  pl.BlockSpec((nblk, KV_W, MOBA_BLOCK), lambda s, pt: (0, 0, 0)),
                  pl.BlockSpec(memory_space=pl.ANY),
                  pl.BlockSpec(memory_space=pl.ANY)],
        out_specs=pl.BlockSpec((t_new, q.shape[1]), lambda s, pt: (s, 0)),
        scratch_shapes=[pltpu.VMEM((2, n_pages, KV_W, page), F32),
                        pltpu.VMEM((2, n_pages, KV_W, page), F32),
                        pltpu.SemaphoreType.DMA((2,)),
                        pltpu.SemaphoreType.DMA((2,)),
                        pltpu.VMEM((nblk, rows, MOBA_BLOCK), F32)],
    )
    return pl.pallas_call(
        functools.partial(_moba_sample_kernel, n_pages=n_pages, page=page, n_samples=n, unroll=(32, 63, 32)),
        grid_spec=grid_spec,
        out_shape=jax.ShapeDtypeStruct(q.shape, F32),
        compiler_params=_params("arbitrary"),
        name="moba_sample",
    )(page_table, q, k_new, v_new, b_last, b_own, onehot, k_pool, v_pool)


def _conv_tail(y, cb_ref, lg_ref, lb_ref, gate):
    y = y + cb_ref[...]
    mu = jnp.mean(y, axis=-1, keepdims=True)
    var = jnp.mean(jnp.square(y - mu), axis=-1, keepdims=True)
    yn = (y - mu) * lax.rsqrt(var + EPS) * lg_ref[...] + lb_ref[...]
    return _silu(yn) * gate


HALO = 32


SUBLANES = 8
LANES = 128
CONV_ROWS = 64


def _conv_prompt_kernel(u_ref, prev_ref, g_ref, w_ref, cb_ref, lg_ref, lb_ref, o_ref, xs, xr, ys, *, tl):
    t = pl.program_id(1)
    c = u_ref.shape[1]
    xs[0:HALO, :] = jnp.where(t > 0, prev_ref[...], 0.0)
    xs[HALO:, :] = u_ref[...]
    span = tl + HALO - SUBLANES
    for r in range(1, SUBLANES):
        xr[r - 1] = xs[pl.ds(r, span), :]
    off = HALO - (CONV_W - 1)

    def taps(base, lt):
        ln = slice(lt * LANES, (lt + 1) * LANES)
        acc = jnp.zeros((CONV_ROWS, LANES), F32)
        for k in range(CONV_W):
            a, r = divmod(off + k, SUBLANES)
            src = xs if r == 0 else xr.at[r - 1]
            start = base + a * SUBLANES
            acc = acc + src[start:start + CONV_ROWS, ln] * w_ref[k:k + 1, ln]
        ys[base:base + CONV_ROWS, ln] = acc

    for base in range(0, tl, CONV_ROWS):
        for lt in range(0, c // LANES, 2):
            @pl.when(t >= 0)
            def _():
                taps(base, lt)
                taps(base, lt + 1)
    o_ref[...] = _conv_tail(ys[...], cb_ref, lg_ref, lb_ref, g_ref[...].astype(F32)).astype(o_ref.dtype)


def conv_prompt(u, szb, conv_w, conv_b, ln_g, ln_b, batch, seq, tl=256):
    c = u.shape[1]
    nt = seq // tl
    vec = pl.BlockSpec((1, c), lambda b, t: (0, 0))
    return pl.pallas_call(
        functools.partial(_conv_prompt_kernel, tl=tl),
        grid=(batch, nt),
        in_specs=[pl.BlockSpec((tl, c), lambda b, t: (b * nt + t, 0)),
                  pl.BlockSpec((HALO, c), lambda b, t: (jnp.maximum((b * nt + t) * (tl // HALO) - 1, 0), 0)),
                  pl.BlockSpec((tl, c), lambda b, t: (b * nt + t, 0)),
                  pl.BlockSpec((CONV_W, c), lambda b, t: (0, 0)), vec, vec, vec],
        out_specs=pl.BlockSpec((tl, c), lambda b, t: (b * nt + t, 0)),
        out_shape=jax.ShapeDtypeStruct(u.shape, BF16),
        scratch_shapes=[pltpu.VMEM((HALO + tl, c), F32),
                        pltpu.VMEM((SUBLANES - 1, HALO + tl - SUBLANES, c), F32),
                        pltpu.VMEM((tl, c), F32)],
        compiler_params=_params("arbitrary", "arbitrary"),
        name="conv_prompt",
    )(u, u, szb, conv_w, conv_b.reshape(1, c), ln_g.reshape(1, c), ln_b.reshape(1, c))


def _conv_sample_kernel(xp_ref, g_ref, w_ref, cb_ref, lg_ref, lb_ref, o_ref, *, t_new):
    ns, _, c = xp_ref.shape
    y = jnp.zeros((ns, t_new, c), F32)
    for k in range(CONV_W):
        y = y + xp_ref[:, pl.ds(k, t_new), :] * w_ref[k:k + 1, :]
    o_ref[...] = _conv_tail(y, cb_ref, lg_ref, lb_ref, g_ref[...].astype(F32)).astype(o_ref.dtype)


def conv_sample(xp, szb, conv_w, conv_b, ln_g, ln_b, t_new, ns=8):
    n, rows, c = xp.shape
    vec = pl.BlockSpec((1, c), lambda i: (0, 0))
    return pl.pallas_call(
        functools.partial(_conv_sample_kernel, t_new=t_new),
        grid=(n // ns,),
        in_specs=[pl.BlockSpec((ns, rows, c), lambda i: (i, 0, 0)),
                  pl.BlockSpec((ns, t_new, c), lambda i: (i, 0, 0)),
                  pl.BlockSpec((CONV_W, c), lambda i: (0, 0)), vec, vec, vec],
        out_specs=pl.BlockSpec((ns, t_new, c), lambda i: (i, 0, 0)),
        out_shape=jax.ShapeDtypeStruct((n, t_new, c), BF16),
        compiler_params=_params("arbitrary"),
        name="conv_sample",
    )(xp, szb, conv_w, conv_b.reshape(1, c), ln_g.reshape(1, c), ln_b.reshape(1, c))


def _swa_prompt_kernel(q_ref, kc_ref, kp_ref, vc_ref, vp_ref, bias_ref, sink_ref, o_ref, s_s, p_s, *, chunk, sub):
    n = pl.program_id(1)
    w = SWA_WINDOW
    rows = GROUP * w
    nch, per = rows // chunk, chunk // sub
    gw = GROUP * HEAD_DIM
    kcat = jnp.concatenate([kp_ref[...], kc_ref[...]], axis=0).astype(BF16)
    v_t = jnp.concatenate([vp_ref[...], vc_ref[...]], axis=0).T.astype(BF16)
    ones = jnp.ones((V_AUG - HEAD_DIM, 2 * w), BF16)
    no_prev = (lax.broadcasted_iota(jnp.int32, (2 * w, sub), 0) < w) & (n == 0)
    qpads, v_augs, sinks = [], [], []
    for g in range(N_KV):
        q_t = (q_ref[:, g * gw:(g + 1) * gw] * (SCALE * LOG2E)).T
        qs = jnp.concatenate([q_t[h * HEAD_DIM:(h + 1) * HEAD_DIM, :] for h in range(GROUP)], axis=1)
        zeros = jnp.zeros_like(qs)
        qpads.append(jnp.concatenate([qs, zeros] if g == 0 else [zeros, qs], axis=0).astype(BF16))
        v_augs.append(jnp.concatenate([v_t[g * HEAD_DIM:(g + 1) * HEAD_DIM], ones], axis=0))
        sinks.append(sink_ref[g] * LOG2E)

    def qk(g, c):
        s_t = _dot(kcat, qpads[g][:, c * chunk:(c + 1) * chunk])
        for k in range(per):
            s_s[g, c * per + k] = s_t[:, k * sub:(k + 1) * sub]

    def softmax(g, cs):
        s_t = jnp.where(no_prev, NEG_INF, s_s[g, cs] + bias_ref[g, cs])
        m = jnp.maximum(jnp.max(s_t, axis=0, keepdims=True), sinks[g][:, cs * sub:(cs + 1) * sub])
        p_s[g, cs] = jnp.exp2(s_t - m).astype(BF16)
        return m

    def pv(g, c, ms):
        p = jnp.concatenate([p_s[g, c * per + k] for k in range(per)], axis=1)
        o_t = _dot(v_augs[g], p)
        den = o_t[HEAD_DIM:HEAD_DIM + 1] + jnp.exp2(sinks[g][:, c * chunk:(c + 1) * chunk] - jnp.concatenate(ms, axis=1))
        return o_t[:HEAD_DIM] / den

    for c in range(nch):
        qk(0, c)
    ms = [[], []]
    outs = [[], []]
    for c in range(nch):
        qk(1, c)
        ms[0] += [softmax(0, c * per + k) for k in range(per)]
    for c in range(nch):
        outs[0].append(pv(0, c, ms[0][c * per:(c + 1) * per]))
        ms[1] += [softmax(1, c * per + k) for k in range(per)]
    for c in range(nch):
        outs[1].append(pv(1, c, ms[1][c * per:(c + 1) * per]))
    tiles = []
    for g in range(N_KV):
        o_t = jnp.concatenate(outs[g], axis=1)
        tiles.append(jnp.concatenate([o_t[:, h * w:(h + 1) * w] for h in range(GROUP)], axis=0).T)
    o_ref[...] = jnp.concatenate(tiles, axis=1).astype(o_ref.dtype)


def swa_prompt(q, k, v, sinks, rel_bias, batch, seq):
    w = SWA_WINDOW
    nb = seq // w
    rows = GROUP * w
    kidx = np.arange(2 * w)[:, None]
    qq = (np.arange(rows) % w)[None, :]
    dist = w + qq - kidx
    bm = _bucket_matrix(dist, (dist >= 0) & (dist < w))
    rbx = jnp.repeat(rel_bias.reshape(N_BUCKETS, N_KV, GROUP).transpose(1, 0, 2), w, axis=2)
    chunk, sub = 256, 128
    nsub = rows // sub
    bias = bias_table(bm, rbx, jnp.zeros((N_KV, 1, rows), F32), LOG2E)
    bias = bias.reshape(N_KV, 2 * w, nsub, sub).transpose(0, 2, 1, 3)
    sink_x = jnp.repeat(sinks.reshape(N_KV, 1, GROUP), w, axis=2)
    d = q.shape[1]
    cur = lambda b, n: (b * nb + n, 0)
    prev = lambda b, n: (jnp.maximum(b * nb + n - 1, 0), 0)
    return pl.pallas_call(
        functools.partial(_swa_prompt_kernel, chunk=chunk, sub=sub),
        grid=(batch, nb),
        in_specs=[pl.BlockSpec((w, d), cur),
                  pl.BlockSpec((w, KV_W), cur), pl.BlockSpec((w, KV_W), prev),
                  pl.BlockSpec((w, KV_W), cur), pl.BlockSpec((w, KV_W), prev),
                  pl.BlockSpec((N_KV, nsub, 2 * w, sub), lambda b, n: (0, 0, 0, 0)),
                  pl.BlockSpec((N_KV, 1, rows), lambda b, n: (0, 0, 0))],
        out_specs=pl.BlockSpec((w, d), cur),
        out_shape=jax.ShapeDtypeStruct(q.shape, BF16),
        scratch_shapes=[pltpu.VMEM((N_KV, nsub, 2 * w, sub), F32),
                        pltpu.VMEM((N_KV, nsub, 2 * w, sub), BF16)],
        compiler_params=_params("arbitrary", "arbitrary"),
        name="swa_prompt",
    )(q, k, k, v, v, bias, sink_x)


def _swa_sample_kernel(q_ref, kb_ref, kn_ref, vb_ref, vn_ref, bbuf_ref, bnew_ref, sink_ref, o_ref, *, t_new):
    ns = kb_ref.shape[0]
    sink = sink_ref[...]
    for i in range(ns):
        rs = slice(i * t_new, (i + 1) * t_new)
        qbd = _heads_to_rows(q_ref[rs, :] * SCALE).astype(BF16)
        s_buf = _dot_nt(qbd, kb_ref[i].astype(BF16)) + bbuf_ref[...]
        s_new = _dot_nt(qbd, _pad_rows(kn_ref[rs, :], LANES).astype(BF16)) + bnew_ref[...]
        m = jnp.maximum(jnp.maximum(jnp.max(s_buf, axis=1, keepdims=True), jnp.max(s_new, axis=1, keepdims=True)),
                        sink)
        p_buf, p_new = jnp.exp(s_buf - m), jnp.exp(s_new - m)
        den = jnp.sum(p_buf, axis=1, keepdims=True) + jnp.sum(p_new, axis=1, keepdims=True) + jnp.exp(sink - m)
        acc = (_dot(p_buf.astype(BF16), vb_ref[i].astype(BF16))
               + _dot(p_new.astype(BF16), _pad_rows(vn_ref[rs, :], LANES).astype(BF16)))
        o_ref[rs, :] = _rows_to_heads(acc / den, t_new)


def swa_sample(q, k_buf, v_buf, k_new, v_new, sinks, rel_bias, t_new, ns=8):
    n, wb, _ = k_buf.shape
    rows = N_HEADS * t_new
    tok = (np.arange(rows) % t_new)[None, :]
    dist_buf = tok + wb - np.arange(wb)[:, None]
    j_new = np.arange(LANES)[:, None]
    dist_new = tok - j_new
    bm_buf = _bucket_matrix(dist_buf, (dist_buf >= 0) & (dist_buf < SWA_WINDOW))
    bm_new = _bucket_matrix(dist_new, (dist_new >= 0) & (dist_new < SWA_WINDOW) & (j_new < t_new))
    rbx = jnp.repeat(rel_bias, t_new, axis=1)[None]
    zero = jnp.zeros((1, 1, rows), F32)
    b_buf = bias_table(bm_buf, rbx, zero)[0].T
    b_new = bias_table(bm_new, rbx, zero)[0].T
    sink_r = jnp.repeat(sinks, t_new).reshape(rows, 1)
    d = q.shape[1]
    tile = lambda width: pl.BlockSpec((ns * t_new, width), lambda i: (i, 0))
    buf = pl.BlockSpec((ns, wb, KV_W), lambda i: (i, 0, 0))
    return pl.pallas_call(
        functools.partial(_swa_sample_kernel, t_new=t_new),
        grid=(n // ns,),
        in_specs=[tile(d), buf, tile(KV_W), buf, tile(KV_W),
                  pl.BlockSpec((rows, wb), lambda i: (0, 0)),
                  pl.BlockSpec((rows, LANES), lambda i: (0, 0)),
                  pl.BlockSpec((rows, 1), lambda i: (0, 0))],
        out_specs=tile(d),
        out_shape=jax.ShapeDtypeStruct(q.shape, F32),
        compiler_params=_params("arbitrary"),
        name="swa_sample",
    )(q, k_buf, k_new, v_buf, v_new, b_buf, b_new, sink_r)


def kernel(x_prompt, x_sample, c_prompt, c_sample, cache_a_k, cache_a_v, page_table, cache_b_conv, cache_c_k, cache_c_v, rel_bias, norm_a, mod_w_a, mod_b_a, w_in_a, conv_w_b, conv_b_b, ln_g_b, ln_b_b, w_out_a, norm_c, mod_w_c, mod_b_c, w_in_c, sinks_c, w_out_c, final_norm):
    batch, seq, d = x_prompt.shape
    n, t_new, _ = x_sample.shape
    n_pool, page = cache_a_k.shape[1], cache_a_k.shape[2]
    hq = N_HEADS * HEAD_DIM
    cb = conv_w_b.shape[2]
    assert norm_a.shape[0] == 1 and norm_c.shape[0] == 1, "one A/B layer followed by one C layer"

    tm_p = 1024
    tm_s = min(256, n * t_new)
    xp = x_prompt.reshape(batch * seq, d)
    xs = x_sample

    c_all = jnp.concatenate([c_prompt, c_sample], axis=0)
    c_rows = -(-c_all.shape[0] // 8) * 8
    c_all = jnp.pad(c_all, ((0, c_rows - c_all.shape[0]), (0, 0)))

    def split_mod(m):
        mp = m[:batch].reshape(batch, 1, 3, d)
        ms = m[batch:batch + n].reshape(n, 1, 3, d)
        return [(mp[:, :, j], ms[:, :, j]) for j in range(3)]

    (sh_a, sc_a, gt_a) = split_mod(modulation(c_all, mod_w_a[0], mod_b_a[0]))
    (sh_c, sc_c, gt_c) = split_mod(modulation(c_all, mod_w_c[0], mod_b_c[0]))

    kv0, kv1 = hq, hq + 2 * KV_W
    za0 = kv1
    ga0 = za0 + hq
    gb0 = ga0 + cb
    zb0 = gb0 + cb
    segs_a = (("raw", 0, hq), ("raw", kv0, kv0 + KV_W), ("raw", kv0 + KV_W, kv1),
              ("silu", za0, ga0), ("glu", ga0, gb0, gb0, zb0), ("silu", zb0, zb0 + cb))
    dts_a = (F32, F32, F32, BF16, F32, BF16)
    w_in_a16 = w_in_a[0].astype(BF16)
    w_out_a16 = w_out_a[0].astype(BF16)
    k_pool = cache_a_k[0].transpose(0, 2, 3, 1).reshape(n_pool, KV_W, page)
    v_pool = cache_a_v[0].transpose(0, 2, 3, 1).reshape(n_pool, KV_W, page)

    q, k, v, sza, u, szb = ln_inproj(xp, sh_a[0], sc_a[0], norm_a[0], w_in_a16, segs_a, dts_a, tm_p)
    oa = moba_prompt(q, k.reshape(batch, seq, KV_W), v.reshape(batch, seq, KV_W), rel_bias, batch, seq)
    ob = conv_prompt(u, szb, conv_w_b[0], conv_b_b[0], ln_g_b[0], ln_b_b[0], batch, seq)
    xp1 = out_proj([(oa, sza, w_out_a16[:hq])], [(ob, w_out_a16[hq:])], xp, gt_a[0], None, tm_p)
    ak_p, av_p = k, v
    bc_p = u.reshape(batch, seq, cb)[:, seq - (CONV_W - 1):]

    q, k, v, sza, u, szb = ln_inproj(xs, sh_a[1], sc_a[1], norm_a[0], w_in_a16, segs_a, dts_a, tm_s)
    oa = moba_sample(q, k, v, k_pool, v_pool, page_table, rel_bias, t_new)
    xcat = jnp.concatenate([cache_b_conv[0], u.reshape(n, t_new, cb)], axis=1)
    ob = conv_sample(xcat, szb.reshape(n, t_new, cb), conv_w_b[0], conv_b_b[0], ln_g_b[0], ln_b_b[0], t_new)
    xs1 = out_proj([(oa, sza, w_out_a16[:hq])], [(ob.reshape(n * t_new, cb), w_out_a16[hq:])],
                   xs, gt_a[1], None, tm_s)
    ak_s, av_s = k, v
    bc_s = xcat[:, t_new:]

    segs_c = (("raw", 0, hq), ("raw", hq, hq + KV_W), ("raw", hq + KV_W, hq + 2 * KV_W),
              ("silu", hq + 2 * KV_W, 2 * hq + 2 * KV_W))
    dts_c = (F32, F32, F32, BF16)
    w_in_c16 = w_in_c[0].astype(BF16)
    w_out_c16 = w_out_c[0].astype(BF16)

    q, k, v, sz = ln_inproj(xp1, sh_c[0], sc_c[0], norm_c[0], w_in_c16, segs_c, dts_c, tm_p)
    o = swa_prompt(q, k, v, sinks_c[0], rel_bias, batch, seq)
    y_prompt = out_proj([(o, sz, w_out_c16)], [], xp1, gt_c[0], final_norm, tm_p)
    wb_p = min(SWA_WINDOW, seq)
    ck_p = k.reshape(batch, seq, KV_W)[:, seq - wb_p:]
    cv_p = v.reshape(batch, seq, KV_W)[:, seq - wb_p:]

    q, k, v, sz = ln_inproj(xs1, sh_c[1], sc_c[1], norm_c[0], w_in_c16, segs_c, dts_c, tm_s)
    wb_s = cache_c_k.shape[2]
    kb, vb = cache_c_k[0].reshape(n, wb_s, KV_W), cache_c_v[0].reshape(n, wb_s, KV_W)
    o = swa_sample(q, kb, vb, k, v, sinks_c[0], rel_bias, t_new)
    y_sample = out_proj([(o, sz, w_out_c16)], [], xs1, gt_c[1], final_norm, tm_s)
    ck_s = jnp.concatenate([kb, k.reshape(n, t_new, KV_W)], axis=1)[:, t_new:]
    cv_s = jnp.concatenate([vb, v.reshape(n, t_new, KV_W)], axis=1)[:, t_new:]

    def kv5(a, lead):
        return a.reshape((1,) + lead + (N_KV, HEAD_DIM))

    return (y_prompt.reshape(batch, seq, d), y_sample,
            kv5(ak_p, (batch, seq)), kv5(av_p, (batch, seq)),
            kv5(ak_s, (n, t_new)), kv5(av_s, (n, t_new)),
            bc_p[None], bc_s[None],
            kv5(ck_p, (batch, wb_p)), kv5(cv_p, (batch, wb_p)),
            kv5(ck_s, (n, wb_s)), kv5(cv_s, (n, wb_s)))
```

```python
import functools
import math

import jax
import jax.numpy as jnp
import numpy as np
from jax import lax
from jax.experimental import pallas as pl
from jax.experimental.pallas import tpu as pltpu

F32 = jnp.float32
BF16 = jnp.bfloat16
NEG_INF = float("-inf")
MASK_NEG = -1e30

HEAD_DIM = 64
N_HEADS = 16
N_KV = 2
GROUP = N_HEADS // N_KV
KV_W = N_KV * HEAD_DIM
MOBA_BLOCK = 256
MOBA_TOPK = 3
CONV_W = 31
SWA_WINDOW = 128
N_BUCKETS = 32
MAX_DISTANCE = 128
EPS = 1e-6
SCALE = HEAD_DIM ** -0.5
LOG2E = math.log2(math.e)

SUBLANES = 8
LANES = 128
VMEM_LIMIT = 56 * 2**20


def _params(*sem):
    return pltpu.CompilerParams(dimension_semantics=sem, vmem_limit_bytes=VMEM_LIMIT)


def _silu(z):
    return z * jax.nn.sigmoid(z)


def _dot(a, b):
    return jnp.dot(a, b, preferred_element_type=F32)


def _dot_nt(a, b):
    return lax.dot_general(a, b, (((1,), (1,)), ((), ())), preferred_element_type=F32)


def _dot_f32(a, b):
    return jnp.dot(a, b, preferred_element_type=F32, precision=lax.Precision.HIGHEST)


def _t5_bucket_np(n):
    n = np.maximum(n, 0)
    max_exact = N_BUCKETS // 2
    nf = np.maximum(n, 1).astype(np.float32)
    large = max_exact + (np.log(nf / np.float32(max_exact)) / np.float32(math.log(MAX_DISTANCE / max_exact))
                         * np.float32(N_BUCKETS - max_exact)).astype(np.int32)
    large = np.minimum(large, N_BUCKETS - 1)
    return np.where(n < max_exact, n, large).astype(np.int32)


def _bucket_matrix(dist, valid):
    return np.where(valid, _t5_bucket_np(dist), -1).astype(np.int32)


def _bias_kernel(bm_ref, rb_ref, sub_ref, o_ref, *, mul):
    bm = bm_ref[...]
    acc = jnp.zeros(bm.shape, F32)
    for b in range(N_BUCKETS):
        acc = jnp.where(bm == b, rb_ref[b:b + 1, :], acc)
    o_ref[...] = jnp.where(bm < 0, NEG_INF, (acc - sub_ref[...]) * mul)


def bias_table(bm, rbx, sub, mul=1.0):
    g, _, c = rbx.shape
    r = bm.shape[0]
    return pl.pallas_call(
        functools.partial(_bias_kernel, mul=mul),
        grid=(g,),
        in_specs=[pl.BlockSpec((r, c), lambda i: (0, 0)),
                  pl.BlockSpec((None, N_BUCKETS, c), lambda i: (i, 0, 0)),
                  pl.BlockSpec((None, 1, c), lambda i: (i, 0, 0))],
        out_specs=pl.BlockSpec((None, r, c), lambda i: (i, 0, 0)),
        out_shape=jax.ShapeDtypeStruct((g, r, c), F32),
        compiler_params=_params("arbitrary"),
        name="bias_table",
    )(jnp.asarray(bm), rbx, sub)


def _mod_kernel(c_ref, w_ref, b_ref, o_ref):
    o_ref[...] = _dot_f32(_silu(c_ref[...]), w_ref[...]) + b_ref[...]


def modulation(c, w, b):
    n, d = c.shape
    m = w.shape[1]
    tn = 512
    return pl.pallas_call(
        _mod_kernel,
        grid=(m // tn,),
        in_specs=[pl.BlockSpec((n, d), lambda j: (0, 0)),
                  pl.BlockSpec((d, tn), lambda j: (0, j)),
                  pl.BlockSpec((1, tn), lambda j: (0, j))],
        out_specs=pl.BlockSpec((n, tn), lambda j: (0, j)),
        out_shape=jax.ShapeDtypeStruct((n, m), F32),
        compiler_params=_params("arbitrary"),
        name="modulation",
    )(c, w, b.reshape(1, m))


def _ln_inproj_kernel(x_ref, shift_ref, scale_ref, g_ref, w_ref, *out_refs, segs):
    x = x_ref[...]
    y = x * lax.rsqrt(jnp.mean(x * x, axis=-1, keepdims=True) + EPS)
    h = (y * g_ref[...]) * (1.0 + scale_ref[...]) + shift_ref[...]
    h16 = h.reshape(-1, h.shape[-1]).astype(BF16)
    raw = {}
    for o_ref, seg in zip(out_refs, segs, strict=True):
        kind, lo, hi = seg[0], seg[1], seg[2]
        z = raw[lo, hi] if (lo, hi) in raw else _dot(h16, w_ref[:, lo:hi])
        raw[lo, hi] = z
        if kind == "kv_rows":
            tm = z.shape[0]
            o_ref[pl.ds(0, tm, stride=N_KV), :] = z
            o_ref[pl.ds(1, tm, stride=N_KV), :] = pltpu.roll(z, HEAD_DIM, axis=1)
            continue
        if kind == "silu":
            z = _silu(z)
        elif kind == "glu":
            z = z * jax.nn.sigmoid(_dot(h16, w_ref[:, seg[3]:seg[4]]))
        o_ref[...] = z.astype(o_ref.dtype)


def ln_inproj(x, shift, scale, norm_g, w16, segs, out_dtypes, tm):
    d = x.shape[-1]
    if x.ndim == 2:
        r = x.shape[0]
        nt = r // tm
        per_group = nt // shift.shape[0]
        x_spec = pl.BlockSpec((tm, d), lambda i: (i, 0))
        mod_spec = pl.BlockSpec((None, 1, d), lambda i: (i // per_group, 0, 0))
    else:
        n, t_new, _ = x.shape
        r = n * t_new
        nt = r // tm
        x_spec = pl.BlockSpec((tm // t_new, t_new, d), lambda i: (i, 0, 0))
        mod_spec = pl.BlockSpec((tm // t_new, 1, d), lambda i: (i, 0, 0))
    mult = [N_KV if s[0] == "kv_rows" else 1 for s in segs]
    out_shape = [jax.ShapeDtypeStruct((m * r, s[2] - s[1]), dt) for m, s, dt in zip(mult, segs, out_dtypes, strict=True)]
    out_specs = [pl.BlockSpec((m * tm, s[2] - s[1]), lambda i: (i, 0)) for m, s in zip(mult, segs)]
    return pl.pallas_call(
        functools.partial(_ln_inproj_kernel, segs=segs),
        grid=(nt,),
        in_specs=[x_spec, mod_spec, mod_spec,
                  pl.BlockSpec((1, d), lambda i: (0, 0)),
                  pl.BlockSpec(w16.shape, lambda i: (0, 0), pipeline_mode=pl.Buffered(1))],
        out_specs=out_specs,
        out_shape=out_shape,
        compiler_params=_params("arbitrary"),
        name="ln_inproj",
    )(x, shift, scale, norm_g.reshape(1, d), w16)


def _out_proj_kernel(*refs, n_gated, n_plain, final_norm):
    it = iter(refs)
    y = None
    for _ in range(n_gated):
        a_ref, m_ref, w_ref = next(it), next(it), next(it)
        t = _dot((a_ref[...] * m_ref[...].astype(F32)).astype(BF16), w_ref[...])
        y = t if y is None else y + t
    for _ in range(n_plain):
        a_ref, w_ref = next(it), next(it)
        t = _dot(a_ref[...], w_ref[...])
        y = t if y is None else y + t
    x_ref, gate_ref = next(it), next(it)
    xn = x_ref[...] + gate_ref[...] * y.reshape(x_ref.shape)
    if final_norm:
        fg_ref = next(it)
        xn = xn * lax.rsqrt(jnp.mean(xn * xn, axis=-1, keepdims=True) + EPS) * fg_ref[...]
    o_ref = next(it)
    o_ref[...] = xn


def out_proj(gated, plain, x, gate, final_g, tm):
    d = x.shape[-1]
    if x.ndim == 2:
        nt = x.shape[0] // tm
        per_group = nt // gate.shape[0]
        x_spec = pl.BlockSpec((tm, d), lambda i: (i, 0))
        gate_spec = pl.BlockSpec((None, 1, d), lambda i: (i // per_group, 0, 0))
    else:
        n, t_new, _ = x.shape
        nt = n * t_new // tm
        x_spec = pl.BlockSpec((tm // t_new, t_new, d), lambda i: (i, 0, 0))
        gate_spec = pl.BlockSpec((tm // t_new, 1, d), lambda i: (i, 0, 0))
    args, specs = [], []
    for a, m, w in gated:
        args += [a, m, w]
        specs += [pl.BlockSpec((tm, a.shape[1]), lambda i: (i, 0)),
                  pl.BlockSpec((tm, m.shape[1]), lambda i: (i, 0)),
                  pl.BlockSpec(w.shape, lambda i: (0, 0))]
    for a, w in plain:
        args += [a, w]
        specs += [pl.BlockSpec((tm, a.shape[1]), lambda i: (i, 0)),
                  pl.BlockSpec(w.shape, lambda i: (0, 0))]
    args += [x, gate]
    specs += [x_spec, gate_spec]
    if final_g is not None:
        args.append(final_g.reshape(1, d))
        specs.append(pl.BlockSpec((1, d), lambda i: (0, 0)))
    return pl.pallas_call(
        functools.partial(_out_proj_kernel, n_gated=len(gated), n_plain=len(plain),
                          final_norm=final_g is not None),
        grid=(nt,),
        in_specs=specs,
        out_specs=x_spec,
        out_shape=jax.ShapeDtypeStruct(x.shape, F32),
        compiler_params=_params("arbitrary"),
        name="out_proj",
    )(*args)


def _top3_mask(scores, n_valid):
    nb = scores.shape[0]
    blk = lax.broadcasted_iota(jnp.int32, scores.shape, 0)
    s = jnp.where(blk < n_valid, scores, NEG_INF)
    picked = jnp.zeros(scores.shape, F32)
    for _ in range(MOBA_TOPK):
        mx = jnp.max(s, axis=0, keepdims=True)
        first = jnp.min(jnp.where(s == mx, blk, nb), axis=0, keepdims=True)
        hit = blk == first
        picked = jnp.where(hit, 1.0, picked)
        s = jnp.where(hit, NEG_INF, s)
    return jnp.where(blk < n_valid, picked, 0.0)


V_AUG = HEAD_DIM + 16


def _moba_prompt_kernel(q_ref, k_ref, v_ref, bown_ref, badj_ref, o_ref,
                        k16_s, vT_s, kbar_s, qT16_s, sel_s, m_s, acc_s,
                        s0_s, s1_s, p0_s, p1_s, a0_s, a1_s, *, nblk, chunk, sub):
    g = pl.program_id(1)
    i = pl.program_id(2)
    rows = GROUP * MOBA_BLOCK
    nch = rows // chunk

    @pl.when((g == 0) & (i == 0))
    def _per_batch():
        ones = jnp.ones((V_AUG - HEAD_DIM, MOBA_BLOCK), BF16)
        for j in range(nblk):
            kb = k_ref[j * MOBA_BLOCK:(j + 1) * MOBA_BLOCK, :]
            k16_s[j * MOBA_BLOCK:(j + 1) * MOBA_BLOCK, :] = kb.astype(BF16)
            kbar_s[j:j + 1, :] = jnp.mean(kb, axis=0, keepdims=True)
            v_t = v_ref[j * MOBA_BLOCK:(j + 1) * MOBA_BLOCK, :].T.astype(BF16)
            for kv in range(N_KV):
                vT_s[kv, j] = jnp.concatenate([v_t[kv * HEAD_DIM:(kv + 1) * HEAD_DIM], ones], axis=0)

    qT = (q_ref[...] * (SCALE * LOG2E)).T
    qs = jnp.concatenate([qT[h * HEAD_DIM:(h + 1) * HEAD_DIM, :] for h in range(GROUP)], axis=1)
    qs16 = qs.astype(BF16)
    zeros = jnp.zeros_like(qs16)
    qpad16 = jnp.where(g == 0, jnp.concatenate([qs16, zeros], axis=0), jnp.concatenate([zeros, qs16], axis=0))
    for c in range(nch):
        qT16_s[c] = qpad16[:, c * chunk:(c + 1) * chunk]
    kbar = kbar_s[...]
    kbar_g = jnp.where(g == 0, kbar[:, :HEAD_DIM], kbar[:, HEAD_DIM:])
    sel_s[...] = _top3_mask(_dot_f32(kbar_g, qs), i)
    m_s[...] = jnp.full(m_s.shape, NEG_INF, F32)
    acc_s[...] = jnp.zeros(acc_s.shape, F32)

    n_far = i - 1
    sbuf, pbuf, abuf = (s0_s, s1_s), (p0_s, p1_s), (a0_s, a1_s)

    def blk_of(t):
        far_j = jnp.clip(t - 2, 0, jnp.maximum(n_far - 1, 0))
        return jnp.where(t == 0, i, jnp.where(t == 1, jnp.maximum(i - 1, 0), far_j))

    per = chunk // sub

    def stage_qk(t, slot, c):
        kb = k16_s[pl.ds(pl.multiple_of(blk_of(t) * MOBA_BLOCK, MOBA_BLOCK), MOBA_BLOCK), :]
        s_t = _dot(kb, qT16_s[c])
        for k in range(per):
            sbuf[slot][c * per + k] = s_t[:, k * sub:(k + 1) * sub]

    def stage_softmax(slot, c, on, bias_ref):
        ln = slice(c * sub, (c + 1) * sub)
        s_t = sbuf[slot][c]
        if bias_ref is not None:
            s_t = s_t + bias_ref[c]
        m_old = m_s[:, ln]
        m_new = jnp.maximum(m_old, jnp.where(on, jnp.max(s_t, axis=0, keepdims=True), NEG_INF))
        pbuf[slot][c] = jnp.exp2(s_t - jnp.where(on, m_new, jnp.inf)).astype(BF16)
        abuf[slot][:, ln] = jnp.exp2(m_old - m_new)
        m_s[:, ln] = m_new

    def stage_pv(t, slot, c):
        p = jnp.concatenate([pbuf[slot][c * per + k] for k in range(per)], axis=1)
        acc_s[c] = acc_s[c] * abuf[slot][:, c * chunk:(c + 1) * chunk] + _dot(vT_s[g, blk_of(t)], p)

    def pipe_half(t, slot, bias_ref=None, pv=True, qk=True):
        own = bias_ref is bown_ref
        valid = None if own else jnp.where(t == 1, i >= 1, t - 2 < n_far)
        j = blk_of(t)
        for c in range(nch):
            if qk:
                stage_qk(t + 1, 1 - slot, c)
            if own:
                on = jnp.full((1, chunk), True)
            else:
                on = (sel_s[pl.ds(j, 1), c * chunk:(c + 1) * chunk] > 0.0) & valid
            for k in range(per):
                stage_softmax(slot, c * per + k, on[:, k * sub:(k + 1) * sub], bias_ref)
            if pv:
                stage_pv(t - 1, 1 - slot, c)

    for c in range(nch):
        stage_qk(0, 0, c)
    pipe_half(0, 0, bown_ref, pv=False)
    pipe_half(1, 1, badj_ref)

    def pair(u, carry):
        pipe_half(2 + 2 * u, 0)
        pipe_half(3 + 2 * u, 1)
        return carry

    n_pairs = jnp.maximum(n_far, 0) // 2
    lax.fori_loop(0, n_pairs, pair, 0)
    t_tail = 2 + 2 * n_pairs
    odd = jnp.maximum(n_far, 0) % 2 == 1

    @pl.when(odd)
    def _tail():
        pipe_half(t_tail, 0, qk=False)
        for c in range(nch):
            stage_pv(t_tail, 0, c)

    @pl.when(jnp.logical_not(odd))
    def _drain():
        for c in range(nch):
            stage_pv(t_tail - 1, 1, c)

    acc = jnp.concatenate([acc_s[c] for c in range(nch)], axis=1)
    o_t = acc[:HEAD_DIM] / acc[HEAD_DIM:HEAD_DIM + 1]
    o_cat = jnp.concatenate([o_t[:, h * MOBA_BLOCK:(h + 1) * MOBA_BLOCK] for h in range(GROUP)], axis=0)
    o_ref[...] = o_cat.T.astype(o_ref.dtype)


def moba_prompt(q, k, v, rel_bias, batch, seq):
    nblk = seq // MOBA_BLOCK
    rows = GROUP * MOBA_BLOCK
    kk = np.arange(MOBA_BLOCK)[:, None]
    qq = (np.arange(rows) % MOBA_BLOCK)[None, :]
    bm_own = _bucket_matrix(qq - kk, qq >= kk)
    bm_adj = _bucket_matrix(MOBA_BLOCK + qq - kk, np.ones((MOBA_BLOCK, rows), bool))
    rbx = jnp.repeat(rel_bias.reshape(N_BUCKETS, N_KV, GROUP).transpose(1, 0, 2), MOBA_BLOCK, axis=2)
    far = rbx[:, N_BUCKETS - 1:N_BUCKETS, :]
    chunk, sub = 256, 128
    nsub = rows // sub

    def chunk_major(tbl):
        return tbl.reshape(N_KV, MOBA_BLOCK, nsub, sub).transpose(0, 2, 1, 3)

    b_own = chunk_major(bias_table(bm_own, rbx, far, LOG2E))
    b_adj = chunk_major(bias_table(bm_adj, rbx, far, LOG2E))
    qw = GROUP * HEAD_DIM
    tbl_spec = pl.BlockSpec((None, nsub, MOBA_BLOCK, sub), lambda b, g, i: (g, 0, 0, 0))
    return pl.pallas_call(
        functools.partial(_moba_prompt_kernel, nblk=nblk, chunk=chunk, sub=sub),
        grid=(batch, N_KV, nblk),
        in_specs=[pl.BlockSpec((MOBA_BLOCK, qw), lambda b, g, i: (b * nblk + i, g)),
                  pl.BlockSpec((None, seq, KV_W), lambda b, g, i: (b, 0, 0)),
                  pl.BlockSpec((None, seq, KV_W), lambda b, g, i: (b, 0, 0)),
                  tbl_spec, tbl_spec],
        out_specs=pl.BlockSpec((MOBA_BLOCK, qw), lambda b, g, i: (b * nblk + i, g)),
        out_shape=jax.ShapeDtypeStruct(q.shape, BF16),
        scratch_shapes=[pltpu.VMEM((seq, KV_W), BF16),
                        pltpu.VMEM((N_KV, nblk, V_AUG, MOBA_BLOCK), BF16),
                        pltpu.VMEM((nblk, KV_W), F32),
                        pltpu.VMEM((rows // chunk, KV_W, chunk), BF16),
                        pltpu.VMEM((nblk, rows), F32),
                        pltpu.VMEM((1, rows), F32),
                        pltpu.VMEM((rows // chunk, V_AUG, chunk), F32),
                        pltpu.VMEM((nsub, MOBA_BLOCK, sub), F32), pltpu.VMEM((nsub, MOBA_BLOCK, sub), F32),
                        pltpu.VMEM((nsub, MOBA_BLOCK, sub), BF16), pltpu.VMEM((nsub, MOBA_BLOCK, sub), BF16),
                        pltpu.VMEM((1, rows), F32), pltpu.VMEM((1, rows), F32)],
        compiler_params=_params("arbitrary", "arbitrary", "arbitrary"),
        name="moba_prompt",
    )(q, k, v, b_own, b_adj)


def _heads_to_rows(q):
    t = q.shape[0]
    lane = lax.broadcasted_iota(jnp.int32, (t, LANES), 1)
    pieces = []
    for h in range(N_HEADS):
        src = q[:, (h // 2) * LANES:(h // 2 + 1) * LANES]
        g = h // GROUP
        if h % 2 != g:
            src = pltpu.roll(src, HEAD_DIM, axis=1)
        keep = lane < HEAD_DIM if g == 0 else lane >= HEAD_DIM
        pieces.append(jnp.where(keep, src, 0.0))
    return jnp.concatenate(pieces, axis=0)


def _rows_to_heads(acc, t):
    lane = lax.broadcasted_iota(jnp.int32, (t, LANES), 1)
    cols = []
    for k in range(N_HEADS // 2):
        a = acc[2 * k * t:(2 * k + 1) * t, :]
        b = acc[(2 * k + 1) * t:(2 * k + 2) * t, :]
        if (2 * k) // GROUP == 1:
            a = pltpu.roll(a, HEAD_DIM, axis=1)
        else:
            b = pltpu.roll(b, HEAD_DIM, axis=1)
        cols.append(jnp.where(lane < HEAD_DIM, a, b))
    return jnp.concatenate(cols, axis=1)


def _pad_rows(x, rows):
    return jnp.concatenate([x, jnp.zeros((rows - x.shape[0], x.shape[1]), x.dtype)], axis=0)


def _moba_sample_kernel(pt_ref, q_ref, kn_ref, vn_ref, blast_ref, bown_ref, e_ref, kpool, vpool, o_ref,
                        kbuf, vbuf, ksem, vsem, s_s, *, n_pages, page, n_samples, unroll):
    s = pl.program_id(0)
    slot = s % 2
    ppb = MOBA_BLOCK // page
    nblk = n_pages // ppb
    t_new = q_ref.shape[0]
    rows = N_HEADS * t_new

    def k_copy(smp, sl, p):
        return pltpu.make_async_copy(kpool.at[pt_ref[smp, p]], kbuf.at[sl, p], ksem.at[sl])

    def v_copy(smp, sl, p):
        return pltpu.make_async_copy(vpool.at[pt_ref[smp, p]], vbuf.at[sl, p], vsem.at[sl])

    def start_all(smp, sl):
        def body(p, c):
            k_copy(smp, sl, p).start()
            v_copy(smp, sl, p).start()
            return c
        lax.fori_loop(0, n_pages, body, 0, unroll=8)

    def wait_all(copy, smp, sl):
        def body(p, c):
            copy(smp, sl, p).wait()
            return c
        lax.fori_loop(0, n_pages, body, 0, unroll=8)

    def block_t(buf, j):
        return jnp.concatenate([buf[slot, j * ppb + t] for t in range(ppb)], axis=1)

    blk_lane = lax.broadcasted_iota(jnp.int32, (KV_W, nblk), 1)

    @pl.when(s == 0)
    def _first():
        start_all(0, 0)

    wait_all(k_copy, s, slot)
    nxt = jnp.minimum(s + 1, n_samples - 1)

    def kbar_body(j, kbar_t):
        for t in range(ppb):
            k_copy(nxt, 1 - slot, j * ppb + t).start()
            v_copy(nxt, 1 - slot, j * ppb + t).start()
        pages = kbuf[slot, j * ppb]
        for t in range(1, ppb):
            pages = pages + kbuf[slot, j * ppb + t]
        col = jnp.sum(pages, axis=1, keepdims=True) * (1.0 / MOBA_BLOCK)
        return jnp.where(blk_lane == j, col, kbar_t)

    kbar_t = lax.fori_loop(0, nblk, kbar_body, jnp.zeros((KV_W, nblk), F32), unroll=unroll[0])

    qbd = _heads_to_rows(q_ref[...] * (SCALE * LOG2E))
    qbd16 = qbd.astype(BF16)

    scores = _dot_f32(qbd, kbar_t)
    sel_t = _top3_mask(scores.T, nblk)
    negm = jnp.where(sel_t.T > 0.0, 0.0, MASK_NEG)
    lhs16 = jnp.concatenate([qbd, negm, jnp.zeros((rows, e_ref.shape[1] - nblk), F32)], axis=1).astype(BF16)

    def logits(j):
        rhs16 = jnp.concatenate([block_t(kbuf, j).astype(BF16), e_ref[j]], axis=0)
        return _dot(lhs16, rhs16)

    def fold(x):
        return x[:, :LANES], x[:, LANES:]

    def far(j, mrun):
        st = logits(j)
        s_s[j] = st
        lo, hi = fold(st)
        return jnp.maximum(mrun, jnp.maximum(lo, hi))

    mrun = lax.fori_loop(0, nblk - 1, far, jnp.full((rows, LANES), NEG_INF, F32), unroll=unroll[1])
    s_last = logits(nblk - 1) + blast_ref[...]
    s_s[nblk - 1] = s_last
    kn16 = _pad_rows(kn_ref[...], LANES).astype(BF16)
    s_own = _dot_nt(qbd16, kn16) + bown_ref[...]
    for part in fold(s_last) + (s_own,):
        mrun = jnp.maximum(mrun, part)
    m = jnp.max(mrun, axis=1, keepdims=True)

    wait_all(v_copy, s, slot)

    def pv(j, carry):
        acc, lsum = carry
        p = jnp.exp2(s_s[j] - m)
        lo, hi = fold(p)
        return acc + _dot_nt(p.astype(BF16), block_t(vbuf, j).astype(BF16)), lsum + (lo + hi)

    p_own = jnp.exp2(s_own - m)
    acc0 = _dot(p_own.astype(BF16), _pad_rows(vn_ref[...], LANES).astype(BF16))
    acc, lsum = lax.fori_loop(0, nblk, pv, (acc0, p_own), unroll=unroll[2])
    den = jnp.sum(lsum, axis=1, keepdims=True)
    o_ref[...] = _rows_to_heads(acc / den, t_new)

    @pl.when(s == n_samples - 1)
    def _drain():
        wait_all(k_copy, nxt, 1 - slot)
        wait_all(v_copy, nxt, 1 - slot)


def moba_sample(q, k_new, v_new, k_pool, v_pool, page_table, rel_bias, t_new):
    n, n_pages = page_table.shape
    page = k_pool.shape[2]
    past = n_pages * page
    nblk = past // MOBA_BLOCK
    rows = N_HEADS * t_new
    tok = (np.arange(rows) % t_new)[None, :]
    kk = np.arange(MOBA_BLOCK)[:, None]
    bm_last = _bucket_matrix(MOBA_BLOCK + tok - kk, np.ones((MOBA_BLOCK, rows), bool))
    ko = np.arange(LANES)[:, None]
    bm_own = _bucket_matrix(tok - ko, (ko <= tok) & (ko < t_new))
    rbx = jnp.repeat(rel_bias, t_new, axis=1)[None]
    far = rbx[:, N_BUCKETS - 1:N_BUCKETS, :]
    b_last = bias_table(bm_last, rbx, far, LOG2E)[0].T
    b_own = bias_table(bm_own, rbx, far, LOG2E)[0].T
    onehot_np = np.zeros((nblk, KV_W, MOBA_BLOCK), np.float32)
    onehot_np[np.arange(nblk), np.arange(nblk)] = 1.0
    onehot = jnp.asarray(onehot_np, BF16)

    grid_spec = pltpu.PrefetchScalarGridSpec(
        num_scalar_prefetch=1,
        grid=(n,),
        in_specs=[pl.BlockSpec((t_new, q.shape[1]), lambda s, pt: (s, 0)),
                  pl.BlockSpec((t_new, KV_W), lambda s, pt: (s, 0)),
                  pl.BlockSpec((t_new, KV_W), lambda s, pt: (s, 0)),
                  pl.BlockSpec((rows, MOBA_BLOCK), lambda s, pt: (0, 0)),
                  pl.BlockSpec((rows, LANES), lambda s, pt: (0, 0)),
                  pl.BlockSpec((nblk, KV_W, MOBA_BLOCK), lambda s, pt: (0, 0, 0)),
                  pl.BlockSpec(memory_space=pl.ANY),
                  pl.BlockSpec(memory_space=pl.ANY)],
        out_specs=pl.BlockSpec((t_new, q.shape[1]), lambda s, pt: (s, 0)),
        scratch_shapes=[pltpu.VMEM((2, n_pages, KV_W, page), F32),
                        pltpu.VMEM((2, n_pages, KV_W, page), F32),
                        pltpu.SemaphoreType.DMA((2,)),
                        pltpu.SemaphoreType.DMA((2,)),
                        pltpu.VMEM((nblk, rows, MOBA_BLOCK), F32)],
    )
    return pl.pallas_call(
        functools.partial(_moba_sample_kernel, n_pages=n_pages, page=page, n_samples=n, unroll=(32, 63, 32)),
        grid_spec=grid_spec,
        out_shape=jax.ShapeDtypeStruct(q.shape, F32),
        compiler_params=_params("arbitrary"),
        name="moba_sample",
    )(page_table, q, k_new, v_new, b_last, b_own, onehot, k_pool, v_pool)


def _conv_tail(y, cb_ref, lg_ref, lb_ref, gate):
    y = y + cb_ref[...]
    mu = jnp.mean(y, axis=-1, keepdims=True)
    var = jnp.mean(jnp.square(y - mu), axis=-1, keepdims=True)
    yn = (y - mu) * lax.rsqrt(var + EPS) * lg_ref[...] + lb_ref[...]
    return _silu(yn) * gate


HALO = 32


CONV_ROWS = 64


def _conv_prompt_kernel(u_ref, prev_ref, g_ref, w_ref, cb_ref, lg_ref, lb_ref, o_ref, xs, xr, ys, *, tl):
    t = pl.program_id(1)
    c = u_ref.shape[1]
    xs[0:HALO, :] = jnp.where(t > 0, prev_ref[...], 0.0)
    xs[HALO:, :] = u_ref[...]
    span = tl + HALO - SUBLANES
    for r in range(1, SUBLANES):
        xr[r - 1] = xs[pl.ds(r, span), :]
    off = HALO - (CONV_W - 1)

    def taps(base, lt):
        ln = slice(lt * LANES, (lt + 1) * LANES)
        acc = jnp.zeros((CONV_ROWS, LANES), F32)
        for k in range(CONV_W):
            a, r = divmod(off + k, SUBLANES)
            src = xs if r == 0 else xr.at[r - 1]
            start = base + a * SUBLANES
            acc = acc + src[start:start + CONV_ROWS, ln] * w_ref[k:k + 1, ln]
        ys[base:base + CONV_ROWS, ln] = acc

    for base in range(0, tl, CONV_ROWS):
        for lt in range(0, c // LANES, 2):
            @pl.when(t >= 0)
            def _():
                taps(base, lt)
                taps(base, lt + 1)
    o_ref[...] = _conv_tail(ys[...], cb_ref, lg_ref, lb_ref, g_ref[...].astype(F32)).astype(o_ref.dtype)


def conv_prompt(u, szb, conv_w, conv_b, ln_g, ln_b, batch, seq, tl=256):
    c = u.shape[1]
    nt = seq // tl
    vec = pl.BlockSpec((1, c), lambda b, t: (0, 0))
    return pl.pallas_call(
        functools.partial(_conv_prompt_kernel, tl=tl),
        grid=(batch, nt),
        in_specs=[pl.BlockSpec((tl, c), lambda b, t: (b * nt + t, 0)),
                  pl.BlockSpec((HALO, c), lambda b, t: (jnp.maximum((b * nt + t) * (tl // HALO) - 1, 0), 0)),
                  pl.BlockSpec((tl, c), lambda b, t: (b * nt + t, 0)),
                  pl.BlockSpec((CONV_W, c), lambda b, t: (0, 0)), vec, vec, vec],
        out_specs=pl.BlockSpec((tl, c), lambda b, t: (b * nt + t, 0)),
        out_shape=jax.ShapeDtypeStruct(u.shape, BF16),
        scratch_shapes=[pltpu.VMEM((HALO + tl, c), F32),
                        pltpu.VMEM((SUBLANES - 1, HALO + tl - SUBLANES, c), F32),
                        pltpu.VMEM((tl, c), F32)],
        compiler_params=_params("arbitrary", "arbitrary"),
        name="conv_prompt",
    )(u, u, szb, conv_w, conv_b.reshape(1, c), ln_g.reshape(1, c), ln_b.reshape(1, c))


def _conv_sample_kernel(xp_ref, g_ref, w_ref, cb_ref, lg_ref, lb_ref, o_ref, *, t_new):
    ns, _, c = xp_ref.shape
    y = jnp.zeros((ns, t_new, c), F32)
    for k in range(CONV_W):
        y = y + xp_ref[:, pl.ds(k, t_new), :] * w_ref[k:k + 1, :]
    o_ref[...] = _conv_tail(y, cb_ref, lg_ref, lb_ref, g_ref[...].astype(F32)).astype(o_ref.dtype)


def conv_sample(xp, szb, conv_w, conv_b, ln_g, ln_b, t_new, ns=8):
    n, rows, c = xp.shape
    vec = pl.BlockSpec((1, c), lambda i: (0, 0))
    return pl.pallas_call(
        functools.partial(_conv_sample_kernel, t_new=t_new),
        grid=(n // ns,),
        in_specs=[pl.BlockSpec((ns, rows, c), lambda i: (i, 0, 0)),
                  pl.BlockSpec((ns, t_new, c), lambda i: (i, 0, 0)),
                  pl.BlockSpec((CONV_W, c), lambda i: (0, 0)), vec, vec, vec],
        out_specs=pl.BlockSpec((ns, t_new, c), lambda i: (i, 0, 0)),
        out_shape=jax.ShapeDtypeStruct((n, t_new, c), BF16),
        compiler_params=_params("arbitrary"),
        name="conv_sample",
    )(xp, szb, conv_w, conv_b.reshape(1, c), ln_g.reshape(1, c), ln_b.reshape(1, c))


def _swa_prompt_kernel(q_ref, kc_ref, kp_ref, vc_ref, vp_ref, bias_ref, sink_ref, o_ref, s_s, p_s, *, chunk, sub):
    n = pl.program_id(1)
    w = SWA_WINDOW
    rows = GROUP * w
    nch, per = rows // chunk, chunk // sub
    gw = GROUP * HEAD_DIM
    kcat = jnp.concatenate([kp_ref[...], kc_ref[...]], axis=0).astype(BF16)
    v_t = jnp.concatenate([vp_ref[...], vc_ref[...]], axis=0).T.astype(BF16)
    ones = jnp.ones((V_AUG - HEAD_DIM, 2 * w), BF16)
    no_prev = (lax.broadcasted_iota(jnp.int32, (2 * w, sub), 0) < w) & (n == 0)
    qpads, v_augs, sinks = [], [], []
    for g in range(N_KV):
        q_t = (q_ref[:, g * gw:(g + 1) * gw] * (SCALE * LOG2E)).T
        qs = jnp.concatenate([q_t[h * HEAD_DIM:(h + 1) * HEAD_DIM, :] for h in range(GROUP)], axis=1)
        zeros = jnp.zeros_like(qs)
        qpads.append(jnp.concatenate([qs, zeros] if g == 0 else [zeros, qs], axis=0).astype(BF16))
        v_augs.append(jnp.concatenate([v_t[g * HEAD_DIM:(g + 1) * HEAD_DIM], ones], axis=0))
        sinks.append(sink_ref[g] * LOG2E)

    def qk(g, c):
        s_t = _dot(kcat, qpads[g][:, c * chunk:(c + 1) * chunk])
        for k in range(per):
            s_s[g, c * per + k] = s_t[:, k * sub:(k + 1) * sub]

    def softmax(g, cs):
        s_t = jnp.where(no_prev, NEG_INF, s_s[g, cs] + bias_ref[g, cs])
        m = jnp.maximum(jnp.max(s_t, axis=0, keepdims=True), sinks[g][:, cs * sub:(cs + 1) * sub])
        p_s[g, cs] = jnp.exp2(s_t - m).astype(BF16)
        return m

    def pv(g, c, ms):
        p = jnp.concatenate([p_s[g, c * per + k] for k in range(per)], axis=1)
        o_t = _dot(v_augs[g], p)
        den = o_t[HEAD_DIM:HEAD_DIM + 1] + jnp.exp2(sinks[g][:, c * chunk:(c + 1) * chunk] - jnp.concatenate(ms, axis=1))
        return o_t[:HEAD_DIM] / den

    for c in range(nch):
        qk(0, c)
    ms = [[], []]
    outs = [[], []]
    for c in range(nch):
        qk(1, c)
        ms[0] += [softmax(0, c * per + k) for k in range(per)]
    for c in range(nch):
        outs[0].append(pv(0, c, ms[0][c * per:(c + 1) * per]))
        ms[1] += [softmax(1, c * per + k) for k in range(per)]
    for c in range(nch):
        outs[1].append(pv(1, c, ms[1][c * per:(c + 1) * per]))
    tiles = []
    for g in range(N_KV):
        o_t = jnp.concatenate(outs[g], axis=1)
        tiles.append(jnp.concatenate([o_t[:, h * w:(h + 1) * w] for h in range(GROUP)], axis=0).T)
    o_ref[...] = jnp.concatenate(tiles, axis=1).astype(o_ref.dtype)


def swa_prompt(q, k, v, sinks, rel_bias, batch, seq):
    w = SWA_WINDOW
    nb = seq // w
    rows = GROUP * w
    kidx = np.arange(2 * w)[:, None]
    qq = (np.arange(rows) % w)[None, :]
    dist = w + qq - kidx
    bm = _bucket_matrix(dist, (dist >= 0) & (dist < w))
    rbx = jnp.repeat(rel_bias.reshape(N_BUCKETS, N_KV, GROUP).transpose(1, 0, 2), w, axis=2)
    chunk, sub = 256, 128
    nsub = rows // sub
    bias = bias_table(bm, rbx, jnp.zeros((N_KV, 1, rows), F32), LOG2E)
    bias = bias.reshape(N_KV, 2 * w, nsub, sub).transpose(0, 2, 1, 3)
    sink_x = jnp.repeat(sinks.reshape(N_KV, 1, GROUP), w, axis=2)
    d = q.shape[1]
    cur = lambda b, n: (b * nb + n, 0)
    prev = lambda b, n: (jnp.maximum(b * nb + n - 1, 0), 0)
    return pl.pallas_call(
        functools.partial(_swa_prompt_kernel, chunk=chunk, sub=sub),
        grid=(batch, nb),
        in_specs=[pl.BlockSpec((w, d), cur),
                  pl.BlockSpec((w, KV_W), cur), pl.BlockSpec((w, KV_W), prev),
                  pl.BlockSpec((w, KV_W), cur), pl.BlockSpec((w, KV_W), prev),
                  pl.BlockSpec((N_KV, nsub, 2 * w, sub), lambda b, n: (0, 0, 0, 0)),
                  pl.BlockSpec((N_KV, 1, rows), lambda b, n: (0, 0, 0))],
        out_specs=pl.BlockSpec((w, d), cur),
        out_shape=jax.ShapeDtypeStruct(q.shape, BF16),
        scratch_shapes=[pltpu.VMEM((N_KV, nsub, 2 * w, sub), F32),
                        pltpu.VMEM((N_KV, nsub, 2 * w, sub), BF16)],
        compiler_params=_params("arbitrary", "arbitrary"),
        name="swa_prompt",
    )(q, k, k, v, v, bias, sink_x)


def _swa_sample_kernel(q_ref, kb_ref, kn_ref, vb_ref, vn_ref, bbuf_ref, bnew_ref, sink_ref, o_ref, *, t_new):
    ns = kb_ref.shape[0]
    sink = sink_ref[...]
    for i in range(ns):
        rs = slice(i * t_new, (i + 1) * t_new)
        qbd = _heads_to_rows(q_ref[rs, :] * SCALE).astype(BF16)
        s_buf = _dot_nt(qbd, kb_ref[i].astype(BF16)) + bbuf_ref[...]
        s_new = _dot_nt(qbd, _pad_rows(kn_ref[rs, :], LANES).astype(BF16)) + bnew_ref[...]
        m = jnp.maximum(jnp.maximum(jnp.max(s_buf, axis=1, keepdims=True), jnp.max(s_new, axis=1, keepdims=True)),
                        sink)
        p_buf, p_new = jnp.exp(s_buf - m), jnp.exp(s_new - m)
        den = jnp.sum(p_buf, axis=1, keepdims=True) + jnp.sum(p_new, axis=1, keepdims=True) + jnp.exp(sink - m)
        acc = (_dot(p_buf.astype(BF16), vb_ref[i].astype(BF16))
               + _dot(p_new.astype(BF16), _pad_rows(vn_ref[rs, :], LANES).astype(BF16)))
        o_ref[rs, :] = _rows_to_heads(acc / den, t_new)


def swa_sample(q, k_buf, v_buf, k_new, v_new, sinks, rel_bias, t_new, ns=8):
    n, wb, _ = k_buf.shape
    rows = N_HEADS * t_new
    tok = (np.arange(rows) % t_new)[None, :]
    dist_buf = tok + wb - np.arange(wb)[:, None]
    j_new = np.arange(LANES)[:, None]
    dist_new = tok - j_new
    bm_buf = _bucket_matrix(dist_buf, (dist_buf >= 0) & (dist_buf < SWA_WINDOW))
    bm_new = _bucket_matrix(dist_new, (dist_new >= 0) & (dist_new < SWA_WINDOW) & (j_new < t_new))
    rbx = jnp.repeat(rel_bias, t_new, axis=1)[None]
    zero = jnp.zeros((1, 1, rows), F32)
    b_buf = bias_table(bm_buf, rbx, zero)[0].T
    b_new = bias_table(bm_new, rbx, zero)[0].T
    sink_r = jnp.repeat(sinks, t_new).reshape(rows, 1)
    d = q.shape[1]
    tile = lambda width: pl.BlockSpec((ns * t_new, width), lambda i: (i, 0))
    buf = pl.BlockSpec((ns, wb, KV_W), lambda i: (i, 0, 0))
    return pl.pallas_call(
        functools.partial(_swa_sample_kernel, t_new=t_new),
        grid=(n // ns,),
        in_specs=[tile(d), buf, tile(KV_W), buf, tile(KV_W),
                  pl.BlockSpec((rows, wb), lambda i: (0, 0)),
                  pl.BlockSpec((rows, LANES), lambda i: (0, 0)),
                  pl.BlockSpec((rows, 1), lambda i: (0, 0))],
        out_specs=tile(d),
        out_shape=jax.ShapeDtypeStruct(q.shape, F32),
        compiler_params=_params("arbitrary"),
        name="swa_sample",
    )(q, k_buf, k_new, v_buf, v_new, b_buf, b_new, sink_r)


def kernel(x_prompt, x_sample, c_prompt, c_sample, cache_a_k, cache_a_v, page_table, cache_b_conv, cache_c_k, cache_c_v, rel_bias, norm_a, mod_w_a, mod_b_a, w_in_a, conv_w_b, conv_b_b, ln_g_b, ln_b_b, w_out_a, norm_c, mod_w_c, mod_b_c, w_in_c, sinks_c, w_out_c, final_norm):
    batch, seq, d = x_prompt.shape
    n, t_new, _ = x_sample.shape
    n_pool, page = cache_a_k.shape[1], cache_a_k.shape[2]
    hq = N_HEADS * HEAD_DIM
    cb = conv_w_b.shape[2]
    assert norm_a.shape[0] == 1 and norm_c.shape[0] == 1, "one A/B layer followed by one C layer"

    tm_p = 1024
    tm_s = min(256, n * t_new)
    xp = x_prompt.reshape(batch * seq, d)
    xs = x_sample

    c_all = jnp.concatenate([c_prompt, c_sample], axis=0)
    c_rows = -(-c_all.shape[0] // 8) * 8
    c_all = jnp.pad(c_all, ((0, c_rows - c_all.shape[0]), (0, 0)))

    def split_mod(m):
        mp = m[:batch].reshape(batch, 1, 3, d)
        ms = m[batch:batch + n].reshape(n, 1, 3, d)
        return [(mp[:, :, j], ms[:, :, j]) for j in range(3)]

    (sh_a, sc_a, gt_a) = split_mod(modulation(c_all, mod_w_a[0], mod_b_a[0]))
    (sh_c, sc_c, gt_c) = split_mod(modulation(c_all, mod_w_c[0], mod_b_c[0]))

    kv0, kv1 = hq, hq + 2 * KV_W
    za0 = kv1
    ga0 = za0 + hq
    gb0 = ga0 + cb
    zb0 = gb0 + cb
    segs_a = (("raw", 0, hq), ("raw", kv0, kv0 + KV_W), ("raw", kv0 + KV_W, kv1),
              ("silu", za0, ga0), ("glu", ga0, gb0, gb0, zb0), ("silu", zb0, zb0 + cb))
    dts_a = (F32, F32, F32, BF16, F32, BF16)
    w_in_a16 = w_in_a[0].astype(BF16)
    w_out_a16 = w_out_a[0].astype(BF16)
    k_pool = cache_a_k[0].transpose(0, 2, 3, 1).reshape(n_pool, KV_W, page)
    v_pool = cache_a_v[0].transpose(0, 2, 3, 1).reshape(n_pool, KV_W, page)

    segs_ap = segs_a + (("kv_rows", kv0, kv0 + KV_W), ("kv_rows", kv0 + KV_W, kv1))
    q, k, v, sza, u, szb, k_rows, v_rows = ln_inproj(xp, sh_a[0], sc_a[0], norm_a[0], w_in_a16, segs_ap,
                                                     dts_a + (F32, F32), tm_p)
    oa = moba_prompt(q, k.reshape(batch, seq, KV_W), v.reshape(batch, seq, KV_W), rel_bias, batch, seq)
    ob = conv_prompt(u, szb, conv_w_b[0], conv_b_b[0], ln_g_b[0], ln_b_b[0], batch, seq)
    xp1 = out_proj([(oa, sza, w_out_a16[:hq])], [(ob, w_out_a16[hq:])], xp, gt_a[0], None, tm_p)
    ak_p = k_rows.reshape(1, batch, seq, N_KV, KV_W)[..., :HEAD_DIM]
    av_p = v_rows.reshape(1, batch, seq, N_KV, KV_W)[..., :HEAD_DIM]
    bc_p = u.reshape(batch, seq, cb)[:, seq - (CONV_W - 1):]

    q, k, v, sza, u, szb = ln_inproj(xs, sh_a[1], sc_a[1], norm_a[0], w_in_a16, segs_a, dts_a, tm_s)
    oa = moba_sample(q, k, v, k_pool, v_pool, page_table, rel_bias, t_new)
    xcat = jnp.concatenate([cache_b_conv[0], u.reshape(n, t_new, cb)], axis=1)
    ob = conv_sample(xcat, szb.reshape(n, t_new, cb), conv_w_b[0], conv_b_b[0], ln_g_b[0], ln_b_b[0], t_new)
    xs1 = out_proj([(oa, sza, w_out_a16[:hq])], [(ob.reshape(n * t_new, cb), w_out_a16[hq:])],
                   xs, gt_a[1], None, tm_s)
    ak_s, av_s = k, v
    bc_s = xcat[:, t_new:]

    segs_c = (("raw", 0, hq), ("raw", hq, hq + KV_W), ("raw", hq + KV_W, hq + 2 * KV_W),
              ("silu", hq + 2 * KV_W, 2 * hq + 2 * KV_W))
    dts_c = (F32, F32, F32, BF16)
    w_in_c16 = w_in_c[0].astype(BF16)
    w_out_c16 = w_out_c[0].astype(BF16)

    q, k, v, sz = ln_inproj(xp1, sh_c[0], sc_c[0], norm_c[0], w_in_c16, segs_c, dts_c, tm_p)
    o = swa_prompt(q, k, v, sinks_c[0], rel_bias, batch, seq)
    y_prompt = out_proj([(o, sz, w_out_c16)], [], xp1, gt_c[0], final_norm, tm_p)
    wb_p = min(SWA_WINDOW, seq)
    ck_p = k.reshape(batch, seq, KV_W)[:, seq - wb_p:]
    cv_p = v.reshape(batch, seq, KV_W)[:, seq - wb_p:]

    q, k, v, sz = ln_inproj(xs1, sh_c[1], sc_c[1], norm_c[0], w_in_c16, segs_c, dts_c, tm_s)
    wb_s = cache_c_k.shape[2]
    kb, vb = cache_c_k[0].reshape(n, wb_s, KV_W), cache_c_v[0].reshape(n, wb_s, KV_W)
    o = swa_sample(q, kb, vb, k, v, sinks_c[0], rel_bias, t_new)
    y_sample = out_proj([(o, sz, w_out_c16)], [], xs1, gt_c[1], final_norm, tm_s)
    ck_s = jnp.concatenate([kb, k.reshape(n, t_new, KV_W)], axis=1)[:, t_new:]
    cv_s = jnp.concatenate([vb, v.reshape(n, t_new, KV_W)], axis=1)[:, t_new:]

    def kv5(a, lead):
        return a.reshape((1,) + lead + (N_KV, HEAD_DIM))

    return (y_prompt.reshape(batch, seq, d), y_sample,
            ak_p, av_p,
            kv5(ak_s, (n, t_new)), kv5(av_s, (n, t_new)),
            bc_p[None], bc_s[None],
            kv5(ck_p, (batch, wb_p)), kv5(cv_p, (batch, wb_p)),
            kv5(ck_s, (n, wb_s)), kv5(cv_s, (n, wb_s)))
```

```python
import functools
import math

import jax
import jax.numpy as jnp
import numpy as np
from jax import lax
from jax.experimental import pallas as pl
from jax.experimental.pallas import tpu as pltpu

F32 = jnp.float32
BF16 = jnp.bfloat16
NEG_INF = float("-inf")
MASK_NEG = -1e30

HEAD_DIM = 64
N_HEADS = 16
N_KV = 2
GROUP = N_HEADS // N_KV
KV_W = N_KV * HEAD_DIM
MOBA_BLOCK = 256
MOBA_TOPK = 3
CONV_W = 31
SWA_WINDOW = 128
N_BUCKETS = 32
MAX_DISTANCE = 128
EPS = 1e-6
SCALE = HEAD_DIM ** -0.5
LOG2E = math.log2(math.e)

SUBLANES = 8
LANES = 128
VMEM_LIMIT = 56 * 2**20


def _params(*sem):
    return pltpu.CompilerParams(dimension_semantics=sem, vmem_limit_bytes=VMEM_LIMIT)


def _silu(z):
    return z * jax.nn.sigmoid(z)


def _dot(a, b):
    return jnp.dot(a, b, preferred_element_type=F32)


def _dot_nt(a, b):
    return lax.dot_general(a, b, (((1,), (1,)), ((), ())), preferred_element_type=F32)


def _dot_f32(a, b):
    return jnp.dot(a, b, preferred_element_type=F32, precision=lax.Precision.HIGHEST)


def _t5_bucket_np(n):
    n = np.maximum(n, 0)
    max_exact = N_BUCKETS // 2
    nf = np.maximum(n, 1).astype(np.float32)
    large = max_exact + (np.log(nf / np.float32(max_exact)) / np.float32(math.log(MAX_DISTANCE / max_exact))
                         * np.float32(N_BUCKETS - max_exact)).astype(np.int32)
    large = np.minimum(large, N_BUCKETS - 1)
    return np.where(n < max_exact, n, large).astype(np.int32)


def _bucket_matrix(dist, valid):
    return np.where(valid, _t5_bucket_np(dist), -1).astype(np.int32)


def _bias_kernel(bm_ref, rb_ref, sub_ref, o_ref, *, mul):
    bm = bm_ref[...]
    acc = jnp.zeros(bm.shape, F32)
    for b in range(N_BUCKETS):
        acc = jnp.where(bm == b, rb_ref[b:b + 1, :], acc)
    o_ref[...] = jnp.where(bm < 0, NEG_INF, (acc - sub_ref[...]) * mul)


def bias_table(bm, rbx, sub, mul=1.0):
    g, _, c = rbx.shape
    r = bm.shape[0]
    return pl.pallas_call(
        functools.partial(_bias_kernel, mul=mul),
        grid=(g,),
        in_specs=[pl.BlockSpec((r, c), lambda i: (0, 0)),
                  pl.BlockSpec((None, N_BUCKETS, c), lambda i: (i, 0, 0)),
                  pl.BlockSpec((None, 1, c), lambda i: (i, 0, 0))],
        out_specs=pl.BlockSpec((None, r, c), lambda i: (i, 0, 0)),
        out_shape=jax.ShapeDtypeStruct((g, r, c), F32),
        compiler_params=_params("arbitrary"),
        name="bias_table",
    )(jnp.asarray(bm), rbx, sub)


def _mod_kernel(c_ref, w_ref, b_ref, o_ref):
    o_ref[...] = _dot_f32(_silu(c_ref[...]), w_ref[...]) + b_ref[...]


def modulation(c, w, b):
    n, d = c.shape
    m = w.shape[1]
    tn = 512
    return pl.pallas_call(
        _mod_kernel,
        grid=(m // tn,),
        in_specs=[pl.BlockSpec((n, d), lambda j: (0, 0)),
                  pl.BlockSpec((d, tn), lambda j: (0, j)),
                  pl.BlockSpec((1, tn), lambda j: (0, j))],
        out_specs=pl.BlockSpec((n, tn), lambda j: (0, j)),
        out_shape=jax.ShapeDtypeStruct((n, m), F32),
        compiler_params=_params("arbitrary"),
        name="modulation",
    )(c, w, b.reshape(1, m))


def _ln_inproj_kernel(x_ref, shift_ref, scale_ref, g_ref, w_ref, *out_refs, segs):
    x = x_ref[...]
    y = x * lax.rsqrt(jnp.mean(x * x, axis=-1, keepdims=True) + EPS)
    h = (y * g_ref[...]) * (1.0 + scale_ref[...]) + shift_ref[...]
    h16 = h.reshape(-1, h.shape[-1]).astype(BF16)
    raw = {}
    for o_ref, seg in zip(out_refs, segs, strict=True):
        kind, lo, hi = seg[0], seg[1], seg[2]
        z = raw[lo, hi] if (lo, hi) in raw else _dot(h16, w_ref[:, lo:hi])
        raw[lo, hi] = z
        if kind == "kv_rows":
            tm = z.shape[0]
            o_ref[pl.ds(0, tm, stride=N_KV), :] = z
            o_ref[pl.ds(1, tm, stride=N_KV), :] = pltpu.roll(z, HEAD_DIM, axis=1)
            continue
        if kind == "silu":
            z = _silu(z)
        elif kind == "glu":
            z = z * jax.nn.sigmoid(_dot(h16, w_ref[:, seg[3]:seg[4]]))
        o_ref[...] = z.astype(o_ref.dtype)


def ln_inproj(x, shift, scale, norm_g, w16, segs, out_dtypes, tm):
    d = x.shape[-1]
    if x.ndim == 2:
        r = x.shape[0]
        nt = r // tm
        per_group = nt // shift.shape[0]
        x_spec = pl.BlockSpec((tm, d), lambda i: (i, 0))
        mod_spec = pl.BlockSpec((None, 1, d), lambda i: (i // per_group, 0, 0))
    else:
        n, t_new, _ = x.shape
        r = n * t_new
        nt = r // tm
        x_spec = pl.BlockSpec((tm // t_new, t_new, d), lambda i: (i, 0, 0))
        mod_spec = pl.BlockSpec((tm // t_new, 1, d), lambda i: (i, 0, 0))
    mult = [N_KV if s[0] == "kv_rows" else 1 for s in segs]
    out_shape = [jax.ShapeDtypeStruct((m * r, s[2] - s[1]), dt) for m, s, dt in zip(mult, segs, out_dtypes, strict=True)]
    out_specs = [pl.BlockSpec((m * tm, s[2] - s[1]), lambda i: (i, 0)) for m, s in zip(mult, segs)]
    return pl.pallas_call(
        functools.partial(_ln_inproj_kernel, segs=segs),
        grid=(nt,),
        in_specs=[x_spec, mod_spec, mod_spec,
                  pl.BlockSpec((1, d), lambda i: (0, 0)),
                  pl.BlockSpec(w16.shape, lambda i: (0, 0), pipeline_mode=pl.Buffered(1))],
        out_specs=out_specs,
        out_shape=out_shape,
        compiler_params=_params("arbitrary"),
        name="ln_inproj",
    )(x, shift, scale, norm_g.reshape(1, d), w16)


def _out_proj_kernel(*refs, n_gated, n_plain, final_norm):
    it = iter(refs)
    y = None
    for _ in range(n_gated):
        a_ref, m_ref, w_ref = next(it), next(it), next(it)
        t = _dot((a_ref[...] * m_ref[...].astype(F32)).astype(BF16), w_ref[...])
        y = t if y is None else y + t
    for _ in range(n_plain):
        a_ref, w_ref = next(it), next(it)
        t = _dot(a_ref[...], w_ref[...])
        y = t if y is None else y + t
    x_ref, gate_ref = next(it), next(it)
    xn = x_ref[...] + gate_ref[...] * y.reshape(x_ref.shape)
    if final_norm:
        fg_ref = next(it)
        xn = xn * lax.rsqrt(jnp.mean(xn * xn, axis=-1, keepdims=True) + EPS) * fg_ref[...]
    o_ref = next(it)
    o_ref[...] = xn


def out_proj(gated, plain, x, gate, final_g, tm):
    d = x.shape[-1]
    if x.ndim == 2:
        nt = x.shape[0] // tm
        per_group = nt // gate.shape[0]
        x_spec = pl.BlockSpec((tm, d), lambda i: (i, 0))
        gate_spec = pl.BlockSpec((None, 1, d), lambda i: (i // per_group, 0, 0))
    else:
        n, t_new, _ = x.shape
        nt = n * t_new // tm
        x_spec = pl.BlockSpec((tm // t_new, t_new, d), lambda i: (i, 0, 0))
        gate_spec = pl.BlockSpec((tm // t_new, 1, d), lambda i: (i, 0, 0))
    args, specs = [], []
    for a, m, w in gated:
        args += [a, m, w]
        specs += [pl.BlockSpec((tm, a.shape[1]), lambda i: (i, 0)),
                  pl.BlockSpec((tm, m.shape[1]), lambda i: (i, 0)),
                  pl.BlockSpec(w.shape, lambda i: (0, 0))]
    for a, w in plain:
        args += [a, w]
        specs += [pl.BlockSpec((tm, a.shape[1]), lambda i: (i, 0)),
                  pl.BlockSpec(w.shape, lambda i: (0, 0))]
    args += [x, gate]
    specs += [x_spec, gate_spec]
    if final_g is not None:
        args.append(final_g.reshape(1, d))
        specs.append(pl.BlockSpec((1, d), lambda i: (0, 0)))
    return pl.pallas_call(
        functools.partial(_out_proj_kernel, n_gated=len(gated), n_plain=len(plain),
                          final_norm=final_g is not None),
        grid=(nt,),
        in_specs=specs,
        out_specs=x_spec,
        out_shape=jax.ShapeDtypeStruct(x.shape, F32),
        compiler_params=_params("arbitrary"),
        name="out_proj",
    )(*args)


def _top3_mask(scores, n_valid):
    nb = scores.shape[0]
    blk = lax.broadcasted_iota(jnp.int32, scores.shape, 0)
    s = jnp.where(blk < n_valid, scores, NEG_INF)
    picked = jnp.zeros(scores.shape, F32)
    for _ in range(MOBA_TOPK):
        mx = jnp.max(s, axis=0, keepdims=True)
        first = jnp.min(jnp.where(s == mx, blk, nb), axis=0, keepdims=True)
        hit = blk == first
        picked = jnp.where(hit, 1.0, picked)
        s = jnp.where(hit, NEG_INF, s)
    return jnp.where(blk < n_valid, picked, 0.0)


V_AUG = HEAD_DIM + 16


def _moba_prompt_kernel(q_ref, k_ref, v_ref, bown_ref, badj_ref, o_ref,
                        k16_s, vT_s, kbar_s, qT16_s, sel_s, m_s, acc_s,
                        s0_s, s1_s, p0_s, p1_s, a0_s, a1_s, *, nblk, chunk, sub):
    g = pl.program_id(1)
    i = pl.program_id(2)
    rows = GROUP * MOBA_BLOCK
    nch = rows // chunk

    @pl.when((g == 0) & (i == 0))
    def _per_batch():
        ones = jnp.ones((V_AUG - HEAD_DIM, MOBA_BLOCK), BF16)
        for j in range(nblk):
            kb = k_ref[j * MOBA_BLOCK:(j + 1) * MOBA_BLOCK, :]
            k16_s[j * MOBA_BLOCK:(j + 1) * MOBA_BLOCK, :] = kb.astype(BF16)
            kbar_s[j:j + 1, :] = jnp.mean(kb, axis=0, keepdims=True)
            v_t = v_ref[j * MOBA_BLOCK:(j + 1) * MOBA_BLOCK, :].T.astype(BF16)
            for kv in range(N_KV):
                vT_s[kv, j] = jnp.concatenate([v_t[kv * HEAD_DIM:(kv + 1) * HEAD_DIM], ones], axis=0)

    qT = (q_ref[...] * (SCALE * LOG2E)).T
    qs = jnp.concatenate([qT[h * HEAD_DIM:(h + 1) * HEAD_DIM, :] for h in range(GROUP)], axis=1)
    qs16 = qs.astype(BF16)
    zeros = jnp.zeros_like(qs16)
    qpad16 = jnp.where(g == 0, jnp.concatenate([qs16, zeros], axis=0), jnp.concatenate([zeros, qs16], axis=0))
    for c in range(nch):
        qT16_s[c] = qpad16[:, c * chunk:(c + 1) * chunk]
    kbar = kbar_s[...]
    kbar_g = jnp.where(g == 0, kbar[:, :HEAD_DIM], kbar[:, HEAD_DIM:])
    sel_s[...] = _top3_mask(_dot_f32(kbar_g, qs), i)
    m_s[...] = jnp.full(m_s.shape, NEG_INF, F32)
    acc_s[...] = jnp.zeros(acc_s.shape, F32)

    n_far = i - 1
    sbuf, pbuf, abuf = (s0_s, s1_s), (p0_s, p1_s), (a0_s, a1_s)

    def blk_of(t):
        far_j = jnp.clip(t - 2, 0, jnp.maximum(n_far - 1, 0))
        return jnp.where(t == 0, i, jnp.where(t == 1, jnp.maximum(i - 1, 0), far_j))

    per = chunk // sub

    def stage_qk(t, slot, c):
        kb = k16_s[pl.ds(pl.multiple_of(blk_of(t) * MOBA_BLOCK, MOBA_BLOCK), MOBA_BLOCK), :]
        s_t = _dot(kb, qT16_s[c])
        for k in range(per):
            sbuf[slot][c * per + k] = s_t[:, k * sub:(k + 1) * sub]

    def stage_softmax(slot, c, on, bias_ref):
        ln = slice(c * sub, (c + 1) * sub)
        s_t = sbuf[slot][c]
        if bias_ref is not None:
            s_t = s_t + bias_ref[c]
        m_old = m_s[:, ln]
        m_new = jnp.maximum(m_old, jnp.where(on, jnp.max(s_t, axis=0, keepdims=True), NEG_INF))
        pbuf[slot][c] = jnp.exp2(s_t - jnp.where(on, m_new, jnp.inf)).astype(BF16)
        abuf[slot][:, ln] = jnp.exp2(m_old - m_new)
        m_s[:, ln] = m_new

    def stage_pv(t, slot, c):
        p = jnp.concatenate([pbuf[slot][c * per + k] for k in range(per)], axis=1)
        acc_s[c] = acc_s[c] * abuf[slot][:, c * chunk:(c + 1) * chunk] + _dot(vT_s[g, blk_of(t)], p)

    def pipe_half(t, slot, bias_ref=None, pv=True, qk=True):
        own = bias_ref is bown_ref
        valid = None if own else jnp.where(t == 1, i >= 1, t - 2 < n_far)
        j = blk_of(t)
        for c in range(nch):
            if qk:
                stage_qk(t + 1, 1 - slot, c)
            if own:
                on = jnp.full((1, chunk), True)
            else:
                on = (sel_s[pl.ds(j, 1), c * chunk:(c + 1) * chunk] > 0.0) & valid
            for k in range(per):
                stage_softmax(slot, c * per + k, on[:, k * sub:(k + 1) * sub], bias_ref)
            if pv:
                stage_pv(t - 1, 1 - slot, c)

    for c in range(nch):
        stage_qk(0, 0, c)
    pipe_half(0, 0, bown_ref, pv=False)
    pipe_half(1, 1, badj_ref)

    def pair(u, carry):
        pipe_half(2 + 2 * u, 0)
        pipe_half(3 + 2 * u, 1)
        return carry

    n_pairs = jnp.maximum(n_far, 0) // 2
    lax.fori_loop(0, n_pairs, pair, 0)
    t_tail = 2 + 2 * n_pairs
    odd = jnp.maximum(n_far, 0) % 2 == 1

    @pl.when(odd)
    def _tail():
        pipe_half(t_tail, 0, qk=False)
        for c in range(nch):
            stage_pv(t_tail, 0, c)

    @pl.when(jnp.logical_not(odd))
    def _drain():
        for c in range(nch):
            stage_pv(t_tail - 1, 1, c)

    acc = jnp.concatenate([acc_s[c] for c in range(nch)], axis=1)
    o_t = acc[:HEAD_DIM] / acc[HEAD_DIM:HEAD_DIM + 1]
    o_cat = jnp.concatenate([o_t[:, h * MOBA_BLOCK:(h + 1) * MOBA_BLOCK] for h in range(GROUP)], axis=0)
    o_ref[...] = o_cat.T.astype(o_ref.dtype)


def moba_prompt(q, k, v, rel_bias, batch, seq):
    nblk = seq // MOBA_BLOCK
    rows = GROUP * MOBA_BLOCK
    kk = np.arange(MOBA_BLOCK)[:, None]
    qq = (np.arange(rows) % MOBA_BLOCK)[None, :]
    bm_own = _bucket_matrix(qq - kk, qq >= kk)
    bm_adj = _bucket_matrix(MOBA_BLOCK + qq - kk, np.ones((MOBA_BLOCK, rows), bool))
    rbx = jnp.repeat(rel_bias.reshape(N_BUCKETS, N_KV, GROUP).transpose(1, 0, 2), MOBA_BLOCK, axis=2)
    far = rbx[:, N_BUCKETS - 1:N_BUCKETS, :]
    chunk, sub = 256, 128
    nsub = rows // sub

    def chunk_major(tbl):
        return tbl.reshape(N_KV, MOBA_BLOCK, nsub, sub).transpose(0, 2, 1, 3)

    b_own = chunk_major(bias_table(bm_own, rbx, far, LOG2E))
    b_adj = chunk_major(bias_table(bm_adj, rbx, far, LOG2E))
    qw = GROUP * HEAD_DIM
    tbl_spec = pl.BlockSpec((None, nsub, MOBA_BLOCK, sub), lambda b, g, i: (g, 0, 0, 0))
    return pl.pallas_call(
        functools.partial(_moba_prompt_kernel, nblk=nblk, chunk=chunk, sub=sub),
        grid=(batch, N_KV, nblk),
        in_specs=[pl.BlockSpec((MOBA_BLOCK, qw), lambda b, g, i: (b * nblk + i, g)),
                  pl.BlockSpec((None, seq, KV_W), lambda b, g, i: (b, 0, 0)),
                  pl.BlockSpec((None, seq, KV_W), lambda b, g, i: (b, 0, 0)),
                  tbl_spec, tbl_spec],
        out_specs=pl.BlockSpec((MOBA_BLOCK, qw), lambda b, g, i: (b * nblk + i, g)),
        out_shape=jax.ShapeDtypeStruct(q.shape, BF16),
        scratch_shapes=[pltpu.VMEM((seq, KV_W), BF16),
                        pltpu.VMEM((N_KV, nblk, V_AUG, MOBA_BLOCK), BF16),
                        pltpu.VMEM((nblk, KV_W), F32),
                        pltpu.VMEM((rows // chunk, KV_W, chunk), BF16),
                        pltpu.VMEM((nblk, rows), F32),
                        pltpu.VMEM((1, rows), F32),
                        pltpu.VMEM((rows // chunk, V_AUG, chunk), F32),
                        pltpu.VMEM((nsub, MOBA_BLOCK, sub), F32), pltpu.VMEM((nsub, MOBA_BLOCK, sub), F32),
                        pltpu.VMEM((nsub, MOBA_BLOCK, sub), BF16), pltpu.VMEM((nsub, MOBA_BLOCK, sub), BF16),
                        pltpu.VMEM((1, rows), F32), pltpu.VMEM((1, rows), F32)],
        compiler_params=_params("arbitrary", "arbitrary", "arbitrary"),
        name="moba_prompt",
    )(q, k, v, b_own, b_adj)


def _heads_to_rows(q):
    t = q.shape[0]
    lane = lax.broadcasted_iota(jnp.int32, (t, LANES), 1)
    pieces = []
    for h in range(N_HEADS):
        src = q[:, (h // 2) * LANES:(h // 2 + 1) * LANES]
        g = h // GROUP
        if h % 2 != g:
            src = pltpu.roll(src, HEAD_DIM, axis=1)
        keep = lane < HEAD_DIM if g == 0 else lane >= HEAD_DIM
        pieces.append(jnp.where(keep, src, 0.0))
    return jnp.concatenate(pieces, axis=0)


def _rows_to_heads(acc, t):
    lane = lax.broadcasted_iota(jnp.int32, (t, LANES), 1)
    cols = []
    for k in range(N_HEADS // 2):
        a = acc[2 * k * t:(2 * k + 1) * t, :]
        b = acc[(2 * k + 1) * t:(2 * k + 2) * t, :]
        if (2 * k) // GROUP == 1:
            a = pltpu.roll(a, HEAD_DIM, axis=1)
        else:
            b = pltpu.roll(b, HEAD_DIM, axis=1)
        cols.append(jnp.where(lane < HEAD_DIM, a, b))
    return jnp.concatenate(cols, axis=1)


def _pad_rows(x, rows):
    return jnp.concatenate([x, jnp.zeros((rows - x.shape[0], x.shape[1]), x.dtype)], axis=0)


def _moba_sample_kernel(pt_ref, q_ref, kn_ref, vn_ref, blast_ref, bown_ref, e_ref, kpool, vpool, o_ref,
                        kbuf, vbuf, ksem, vsem, s_s, *, n_pages, page, n_samples, unroll):
    s = pl.program_id(0)
    slot = s % 2
    ppb = MOBA_BLOCK // page
    nblk = n_pages // ppb
    t_new = q_ref.shape[0]
    rows = N_HEADS * t_new

    def k_copy(smp, sl, p):
        return pltpu.make_async_copy(kpool.at[pt_ref[smp, p]], kbuf.at[sl, p], ksem.at[sl])

    def v_copy(smp, sl, p):
        return pltpu.make_async_copy(vpool.at[pt_ref[smp, p]], vbuf.at[sl, p], vsem.at[sl])

    def start_all(smp, sl):
        def body(p, c):
            k_copy(smp, sl, p).start()
            v_copy(smp, sl, p).start()
            return c
        lax.fori_loop(0, n_pages, body, 0, unroll=8)

    def wait_all(copy, smp, sl):
        def body(p, c):
            copy(smp, sl, p).wait()
            return c
        lax.fori_loop(0, n_pages, body, 0, unroll=8)

    def block_t(buf, j):
        return jnp.concatenate([buf[slot, j * ppb + t] for t in range(ppb)], axis=1)

    blk_lane = lax.broadcasted_iota(jnp.int32, (KV_W, nblk), 1)

    @pl.when(s == 0)
    def _first():
        start_all(0, 0)

    wait_all(k_copy, s, slot)
    nxt = jnp.minimum(s + 1, n_samples - 1)

    def kbar_body(j, kbar_t):
        for t in range(ppb):
            k_copy(nxt, 1 - slot, j * ppb + t).start()
            v_copy(nxt, 1 - slot, j * ppb + t).start()
        pages = kbuf[slot, j * ppb]
        for t in range(1, ppb):
            pages = pages + kbuf[slot, j * ppb + t]
        col = jnp.sum(pages, axis=1, keepdims=True) * (1.0 / MOBA_BLOCK)
        return jnp.where(blk_lane == j, col, kbar_t)

    kbar_t = lax.fori_loop(0, nblk, kbar_body, jnp.zeros((KV_W, nblk), F32), unroll=unroll[0])

    qbd = _heads_to_rows(q_ref[...] * (SCALE * LOG2E))
    qbd16 = qbd.astype(BF16)

    scores = _dot_f32(qbd, kbar_t)
    sel_t = _top3_mask(scores.T, nblk)
    negm = jnp.where(sel_t.T > 0.0, 0.0, MASK_NEG)
    lhs16 = jnp.concatenate([qbd, negm, jnp.zeros((rows, e_ref.shape[1] - nblk), F32)], axis=1).astype(BF16)

    def logits(j):
        rhs16 = jnp.concatenate([block_t(kbuf, j).astype(BF16), e_ref[j]], axis=0)
        return _dot(lhs16, rhs16)

    def fold(x):
        return x[:, :LANES], x[:, LANES:]

    def far(j, mrun):
        st = logits(j)
        s_s[j] = st
        lo, hi = fold(st)
        return jnp.maximum(mrun, jnp.maximum(lo, hi))

    mrun = lax.fori_loop(0, nblk - 1, far, jnp.full((rows, LANES), NEG_INF, F32), unroll=unroll[1])
    s_last = logits(nblk - 1) + blast_ref[...]
    s_s[nblk - 1] = s_last
    kn16 = _pad_rows(kn_ref[...], LANES).astype(BF16)
    s_own = _dot_nt(qbd16, kn16) + bown_ref[...]
    for part in fold(s_last) + (s_own,):
        mrun = jnp.maximum(mrun, part)
    m = jnp.max(mrun, axis=1, keepdims=True)

    wait_all(v_copy, s, slot)

    def pv(j, carry):
        acc, lsum = carry
        p = jnp.exp2(s_s[j] - m)
        lo, hi = fold(p)
        return acc + _dot_nt(p.astype(BF16), block_t(vbuf, j).astype(BF16)), lsum + (lo + hi)

    p_own = jnp.exp2(s_own - m)
    acc0 = _dot(p_own.astype(BF16), _pad_rows(vn_ref[...], LANES).astype(BF16))
    acc, lsum = lax.fori_loop(0, nblk, pv, (acc0, p_own), unroll=unroll[2])
    den = jnp.sum(lsum, axis=1, keepdims=True)
    o_ref[...] = _rows_to_heads(acc / den, t_new)

    @pl.when(s == n_samples - 1)
    def _drain():
        wait_all(k_copy, nxt, 1 - slot)
        wait_all(v_copy, nxt, 1 - slot)


def moba_sample(q, k_new, v_new, k_pool, v_pool, page_table, rel_bias, t_new):
    n, n_pages = page_table.shape
    page = k_pool.shape[2]
    past = n_pages * page
    nblk = past // MOBA_BLOCK
    rows = N_HEADS * t_new
    tok = (np.arange(rows) % t_new)[None, :]
    kk = np.arange(MOBA_BLOCK)[:, None]
    bm_last = _bucket_matrix(MOBA_BLOCK + tok - kk, np.ones((MOBA_BLOCK, rows), bool))
    ko = np.arange(LANES)[:, None]
    bm_own = _bucket_matrix(tok - ko, (ko <= tok) & (ko < t_new))
    rbx = jnp.repeat(rel_bias, t_new, axis=1)[None]
    far = rbx[:, N_BUCKETS - 1:N_BUCKETS, :]
    b_last = bias_table(bm_last, rbx, far, LOG2E)[0].T
    b_own = bias_table(bm_own, rbx, far, LOG2E)[0].T
    onehot_np = np.zeros((nblk, KV_W, MOBA_BLOCK), np.float32)
    onehot_np[np.arange(nblk), np.arange(nblk)] = 1.0
    onehot = jnp.asarray(onehot_np, BF16)

    grid_spec = pltpu.PrefetchScalarGridSpec(
        num_scalar_prefetch=1,
        grid=(n,),
        in_specs=[pl.BlockSpec((t_new, q.shape[1]), lambda s, pt: (s, 0)),
                  pl.BlockSpec((t_new, KV_W), lambda s, pt: (s, 0)),
                  pl.BlockSpec((t_new, KV_W), lambda s, pt: (s, 0)),
                  pl.BlockSpec((rows, MOBA_BLOCK), lambda s, pt: (0, 0)),
                  pl.BlockSpec((rows, LANES), lambda s, pt: (0, 0)),
                  pl.BlockSpec((nblk, KV_W, MOBA_BLOCK), lambda s, pt: (0, 0, 0)),
                  pl.BlockSpec(memory_space=pl.ANY),
                  pl.BlockSpec(memory_space=pl.ANY)],
        out_specs=pl.BlockSpec((t_new, q.shape[1]), lambda s, pt: (s, 0)),
        scratch_shapes=[pltpu.VMEM((2, n_pages, KV_W, page), F32),
                        pltpu.VMEM((2, n_pages, KV_W, page), F32),
                        pltpu.SemaphoreType.DMA((2,)),
                        pltpu.SemaphoreType.DMA((2,)),
                        pltpu.VMEM((nblk, rows, MOBA_BLOCK), F32)],
    )
    return pl.pallas_call(
        functools.partial(_moba_sample_kernel, n_pages=n_pages, page=page, n_samples=n, unroll=(32, 63, 32)),
        grid_spec=grid_spec,
        out_shape=jax.ShapeDtypeStruct(q.shape, F32),
        compiler_params=_params("arbitrary"),
        name="moba_sample",
    )(page_table, q, k_new, v_new, b_last, b_own, onehot, k_pool, v_pool)


def _conv_tail(y, cb_ref, lg_ref, lb_ref, gate):
    y = y + cb_ref[...]
    mu = jnp.mean(y, axis=-1, keepdims=True)
    var = jnp.mean(jnp.square(y - mu), axis=-1, keepdims=True)
    yn = (y - mu) * lax.rsqrt(var + EPS) * lg_ref[...] + lb_ref[...]
    return _silu(yn) * gate


HALO = 32


CONV_ROWS = 64


def _conv_prompt_kernel(u_ref, prev_ref, g_ref, w_ref, cb_ref, lg_ref, lb_ref, o_ref, xs, xr, ys, *, tl):
    t = pl.program_id(1)
    c = u_ref.shape[1]
    xs[0:HALO, :] = jnp.where(t > 0, prev_ref[...], 0.0)
    xs[HALO:, :] = u_ref[...]
    span = tl + HALO - SUBLANES
    for r in range(1, SUBLANES):
        xr[r - 1] = xs[pl.ds(r, span), :]
    off = HALO - (CONV_W - 1)

    def taps(base, lt):
        ln = slice(lt * LANES, (lt + 1) * LANES)
        acc = jnp.zeros((CONV_ROWS, LANES), F32)
        for k in range(CONV_W):
            a, r = divmod(off + k, SUBLANES)
            src = xs if r == 0 else xr.at[r - 1]
            start = base + a * SUBLANES
            acc = acc + src[start:start + CONV_ROWS, ln] * w_ref[k:k + 1, ln]
        ys[base:base + CONV_ROWS, ln] = acc

    for base in range(0, tl, CONV_ROWS):
        for lt in range(0, c // LANES, 2):
            @pl.when(t >= 0)
            def _():
                taps(base, lt)
                taps(base, lt + 1)
    o_ref[...] = _conv_tail(ys[...], cb_ref, lg_ref, lb_ref, g_ref[...].astype(F32)).astype(o_ref.dtype)


def conv_prompt(u, szb, conv_w, conv_b, ln_g, ln_b, batch, seq, tl=256):
    c = u.shape[1]
    nt = seq // tl
    vec = pl.BlockSpec((1, c), lambda b, t: (0, 0))
    return pl.pallas_call(
        functools.partial(_conv_prompt_kernel, tl=tl),
        grid=(batch, nt),
        in_specs=[pl.BlockSpec((tl, c), lambda b, t: (b * nt + t, 0)),
                  pl.BlockSpec((HALO, c), lambda b, t: (jnp.maximum((b * nt + t) * (tl // HALO) - 1, 0), 0)),
                  pl.BlockSpec((tl, c), lambda b, t: (b * nt + t, 0)),
                  pl.BlockSpec((CONV_W, c), lambda b, t: (0, 0)), vec, vec, vec],
        out_specs=pl.BlockSpec((tl, c), lambda b, t: (b * nt + t, 0)),
        out_shape=jax.ShapeDtypeStruct(u.shape, BF16),
        scratch_shapes=[pltpu.VMEM((HALO + tl, c), F32),
                        pltpu.VMEM((SUBLANES - 1, HALO + tl - SUBLANES, c), F32),
                        pltpu.VMEM((tl, c), F32)],
        compiler_params=_params("arbitrary", "arbitrary"),
        name="conv_prompt",
    )(u, u, szb, conv_w, conv_b.reshape(1, c), ln_g.reshape(1, c), ln_b.reshape(1, c))


def _conv_sample_kernel(xp_ref, g_ref, w_ref, cb_ref, lg_ref, lb_ref, o_ref, *, t_new):
    ns, _, c = xp_ref.shape
    y = jnp.zeros((ns, t_new, c), F32)
    for k in range(CONV_W):
        y = y + xp_ref[:, pl.ds(k, t_new), :] * w_ref[k:k + 1, :]
    o_ref[...] = _conv_tail(y, cb_ref, lg_ref, lb_ref, g_ref[...].astype(F32)).astype(o_ref.dtype)


def conv_sample(xp, szb, conv_w, conv_b, ln_g, ln_b, t_new, ns=8):
    n, rows, c = xp.shape
    vec = pl.BlockSpec((1, c), lambda i: (0, 0))
    return pl.pallas_call(
        functools.partial(_conv_sample_kernel, t_new=t_new),
        grid=(n // ns,),
        in_specs=[pl.BlockSpec((ns, rows, c), lambda i: (i, 0, 0)),
                  pl.BlockSpec((ns, t_new, c), lambda i: (i, 0, 0)),
                  pl.BlockSpec((CONV_W, c), lambda i: (0, 0)), vec, vec, vec],
        out_specs=pl.BlockSpec((ns, t_new, c), lambda i: (i, 0, 0)),
        out_shape=jax.ShapeDtypeStruct((n, t_new, c), BF16),
        compiler_params=_params("arbitrary"),
        name="conv_sample",
    )(xp, szb, conv_w, conv_b.reshape(1, c), ln_g.reshape(1, c), ln_b.reshape(1, c))


def _swa_prompt_kernel(q_ref, kc_ref, kp_ref, vc_ref, vp_ref, bias_ref, sink_ref, o_ref, s_s, p_s, *, chunk, sub):
    n = pl.program_id(1)
    w = SWA_WINDOW
    rows = GROUP * w
    nch, per = rows // chunk, chunk // sub
    gw = GROUP * HEAD_DIM
    kcat = jnp.concatenate([kp_ref[...], kc_ref[...]], axis=0).astype(BF16)
    v_t = jnp.concatenate([vp_ref[...], vc_ref[...]], axis=0).T.astype(BF16)
    ones = jnp.ones((V_AUG - HEAD_DIM, 2 * w), BF16)
    no_prev = (lax.broadcasted_iota(jnp.int32, (2 * w, sub), 0) < w) & (n == 0)
    qpads, v_augs, sinks = [], [], []
    for g in range(N_KV):
        q_t = (q_ref[:, g * gw:(g + 1) * gw] * (SCALE * LOG2E)).T
        qs = jnp.concatenate([q_t[h * HEAD_DIM:(h + 1) * HEAD_DIM, :] for h in range(GROUP)], axis=1)
        zeros = jnp.zeros_like(qs)
        qpads.append(jnp.concatenate([qs, zeros] if g == 0 else [zeros, qs], axis=0).astype(BF16))
        v_augs.append(jnp.concatenate([v_t[g * HEAD_DIM:(g + 1) * HEAD_DIM], ones], axis=0))
        sinks.append(sink_ref[g] * LOG2E)

    def qk(g, c):
        s_t = _dot(kcat, qpads[g][:, c * chunk:(c + 1) * chunk])
        for k in range(per):
            s_s[g, c * per + k] = s_t[:, k * sub:(k + 1) * sub]

    def softmax(g, cs):
        s_t = jnp.where(no_prev, NEG_INF, s_s[g, cs] + bias_ref[g, cs])
        m = jnp.maximum(jnp.max(s_t, axis=0, keepdims=True), sinks[g][:, cs * sub:(cs + 1) * sub])
        p_s[g, cs] = jnp.exp2(s_t - m).astype(BF16)
        return m

    def pv(g, c, ms):
        p = jnp.concatenate([p_s[g, c * per + k] for k in range(per)], axis=1)
        o_t = _dot(v_augs[g], p)
        den = o_t[HEAD_DIM:HEAD_DIM + 1] + jnp.exp2(sinks[g][:, c * chunk:(c + 1) * chunk] - jnp.concatenate(ms, axis=1))
        return o_t[:HEAD_DIM] / den

    for c in range(nch):
        qk(0, c)
    ms = [[], []]
    outs = [[], []]
    for c in range(nch):
        qk(1, c)
        ms[0] += [softmax(0, c * per + k) for k in range(per)]
    for c in range(nch):
        outs[0].append(pv(0, c, ms[0][c * per:(c + 1) * per]))
        ms[1] += [softmax(1, c * per + k) for k in range(per)]
    for c in range(nch):
        outs[1].append(pv(1, c, ms[1][c * per:(c + 1) * per]))
    tiles = []
    for g in range(N_KV):
        o_t = jnp.concatenate(outs[g], axis=1)
        tiles.append(jnp.concatenate([o_t[:, h * w:(h + 1) * w] for h in range(GROUP)], axis=0).T)
    o_ref[...] = jnp.concatenate(tiles, axis=1).astype(o_ref.dtype)


def swa_prompt(q, k, v, sinks, rel_bias, batch, seq):
    w = SWA_WINDOW
    nb = seq // w
    rows = GROUP * w
    kidx = np.arange(2 * w)[:, None]
    qq = (np.arange(rows) % w)[None, :]
    dist = w + qq - kidx
    bm = _bucket_matrix(dist, (dist >= 0) & (dist < w))
    rbx = jnp.repeat(rel_bias.reshape(N_BUCKETS, N_KV, GROUP).transpose(1, 0, 2), w, axis=2)
    chunk, sub = 256, 128
    nsub = rows // sub
    bias = bias_table(bm, rbx, jnp.zeros((N_KV, 1, rows), F32), LOG2E)
    bias = bias.reshape(N_KV, 2 * w, nsub, sub).transpose(0, 2, 1, 3)
    sink_x = jnp.repeat(sinks.reshape(N_KV, 1, GROUP), w, axis=2)
    d = q.shape[1]
    cur = lambda b, n: (b * nb + n, 0)
    prev = lambda b, n: (jnp.maximum(b * nb + n - 1, 0), 0)
    return pl.pallas_call(
        functools.partial(_swa_prompt_kernel, chunk=chunk, sub=sub),
        grid=(batch, nb),
        in_specs=[pl.BlockSpec((w, d), cur),
                  pl.BlockSpec((w, KV_W), cur), pl.BlockSpec((w, KV_W), prev),
                  pl.BlockSpec((w, KV_W), cur), pl.BlockSpec((w, KV_W), prev),
                  pl.BlockSpec((N_KV, nsub, 2 * w, sub), lambda b, n: (0, 0, 0, 0)),
                  pl.BlockSpec((N_KV, 1, rows), lambda b, n: (0, 0, 0))],
        out_specs=pl.BlockSpec((w, d), cur),
        out_shape=jax.ShapeDtypeStruct(q.shape, BF16),
        scratch_shapes=[pltpu.VMEM((N_KV, nsub, 2 * w, sub), F32),
                        pltpu.VMEM((N_KV, nsub, 2 * w, sub), BF16)],
        compiler_params=_params("arbitrary", "arbitrary"),
        name="swa_prompt",
    )(q, k, k, v, v, bias, sink_x)


def _swa_sample_kernel(q_ref, kb_ref, kn_ref, vb_ref, vn_ref, bbuf_ref, bnew_ref, sink_ref, o_ref, kb_out, vb_out,
                       *, t_new):
    ns, _, wb = kb_ref.shape
    sink = sink_ref[...]
    keep = lax.broadcasted_iota(jnp.int32, (KV_W, wb), 1) < wb - t_new

    def shifted(buf_t, new_rows):
        return jnp.where(keep, pltpu.roll(buf_t, wb - t_new, axis=1), pltpu.roll(new_rows.T, wb - t_new, axis=1))

    for i in range(ns):
        rs = slice(i * t_new, (i + 1) * t_new)
        qbd = _heads_to_rows(q_ref[rs, :] * SCALE).astype(BF16)
        kb_t, vb_t = kb_ref[i], vb_ref[i]
        kn, vn = _pad_rows(kn_ref[rs, :], wb), _pad_rows(vn_ref[rs, :], wb)
        s_buf = _dot(qbd, kb_t.astype(BF16)) + bbuf_ref[...]
        s_new = _dot_nt(qbd, kn.astype(BF16)) + bnew_ref[...]
        m = jnp.maximum(jnp.maximum(jnp.max(s_buf, axis=1, keepdims=True), jnp.max(s_new, axis=1, keepdims=True)),
                        sink)
        p_buf, p_new = jnp.exp(s_buf - m), jnp.exp(s_new - m)
        den = jnp.sum(p_buf, axis=1, keepdims=True) + jnp.sum(p_new, axis=1, keepdims=True) + jnp.exp(sink - m)
        acc = _dot_nt(p_buf.astype(BF16), vb_t.astype(BF16)) + _dot(p_new.astype(BF16), vn.astype(BF16))
        o_ref[rs, :] = _rows_to_heads(acc / den, t_new)
        kb_out[i] = shifted(kb_t, kn)
        vb_out[i] = shifted(vb_t, vn)


def swa_sample(q, k_buf_t, v_buf_t, k_new, v_new, sinks, rel_bias, t_new, ns=8):
    n, _, wb = k_buf_t.shape
    assert wb % LANES == 0 and wb >= t_new, "the window buffer fills whole lane tiles"
    rows = N_HEADS * t_new
    tok = (np.arange(rows) % t_new)[None, :]
    dist_buf = tok + wb - np.arange(wb)[:, None]
    j_new = np.arange(wb)[:, None]
    dist_new = tok - j_new
    bm_buf = _bucket_matrix(dist_buf, (dist_buf >= 0) & (dist_buf < SWA_WINDOW))
    bm_new = _bucket_matrix(dist_new, (dist_new >= 0) & (dist_new < SWA_WINDOW) & (j_new < t_new))
    rbx = jnp.repeat(rel_bias, t_new, axis=1)[None]
    zero = jnp.zeros((1, 1, rows), F32)
    b_buf = bias_table(bm_buf, rbx, zero)[0].T
    b_new = bias_table(bm_new, rbx, zero)[0].T
    sink_r = jnp.repeat(sinks, t_new).reshape(rows, 1)
    d = q.shape[1]
    tile = lambda width: pl.BlockSpec((ns * t_new, width), lambda i: (i, 0))
    buf = pl.BlockSpec((ns, KV_W, wb), lambda i: (i, 0, 0))
    table = pl.BlockSpec((rows, wb), lambda i: (0, 0))
    buf_shape = jax.ShapeDtypeStruct(k_buf_t.shape, F32)
    return pl.pallas_call(
        functools.partial(_swa_sample_kernel, t_new=t_new),
        grid=(n // ns,),
        in_specs=[tile(d), buf, tile(KV_W), buf, tile(KV_W), table, table,
                  pl.BlockSpec((rows, 1), lambda i: (0, 0))],
        out_specs=[tile(d), buf, buf],
        out_shape=[jax.ShapeDtypeStruct(q.shape, F32), buf_shape, buf_shape],
        compiler_params=_params("arbitrary"),
        name="swa_sample",
    )(q, k_buf_t, k_new, v_buf_t, v_new, b_buf, b_new, sink_r)


def kernel(x_prompt, x_sample, c_prompt, c_sample, cache_a_k, cache_a_v, page_table, cache_b_conv, cache_c_k, cache_c_v, rel_bias, norm_a, mod_w_a, mod_b_a, w_in_a, conv_w_b, conv_b_b, ln_g_b, ln_b_b, w_out_a, norm_c, mod_w_c, mod_b_c, w_in_c, sinks_c, w_out_c, final_norm):
    batch, seq, d = x_prompt.shape
    n, t_new, _ = x_sample.shape
    n_pool, page = cache_a_k.shape[1], cache_a_k.shape[2]
    hq = N_HEADS * HEAD_DIM
    cb = conv_w_b.shape[2]
    assert norm_a.shape[0] == 1 and norm_c.shape[0] == 1, "one A/B layer followed by one C layer"

    tm_p = 1024
    tm_s = min(256, n * t_new)
    xp = x_prompt.reshape(batch * seq, d)
    xs = x_sample

    c_all = jnp.concatenate([c_prompt, c_sample], axis=0)
    c_rows = -(-c_all.shape[0] // 8) * 8
    c_all = jnp.pad(c_all, ((0, c_rows - c_all.shape[0]), (0, 0)))

    def split_mod(m):
        mp = m[:batch].reshape(batch, 1, 3, d)
        ms = m[batch:batch + n].reshape(n, 1, 3, d)
        return [(mp[:, :, j], ms[:, :, j]) for j in range(3)]

    (sh_a, sc_a, gt_a) = split_mod(modulation(c_all, mod_w_a[0], mod_b_a[0]))
    (sh_c, sc_c, gt_c) = split_mod(modulation(c_all, mod_w_c[0], mod_b_c[0]))

    kv0, kv1 = hq, hq + 2 * KV_W
    za0 = kv1
    ga0 = za0 + hq
    gb0 = ga0 + cb
    zb0 = gb0 + cb
    segs_a = (("raw", 0, hq), ("raw", kv0, kv0 + KV_W), ("raw", kv0 + KV_W, kv1),
              ("silu", za0, ga0), ("glu", ga0, gb0, gb0, zb0), ("silu", zb0, zb0 + cb))
    dts_a = (F32, F32, F32, BF16, F32, BF16)
    w_in_a16 = w_in_a[0].astype(BF16)
    w_out_a16 = w_out_a[0].astype(BF16)
    k_pool = cache_a_k[0].transpose(0, 2, 3, 1).reshape(n_pool, KV_W, page)
    v_pool = cache_a_v[0].transpose(0, 2, 3, 1).reshape(n_pool, KV_W, page)

    segs_ap = segs_a + (("kv_rows", kv0, kv0 + KV_W), ("kv_rows", kv0 + KV_W, kv1))
    q, k, v, sza, u, szb, k_rows, v_rows = ln_inproj(xp, sh_a[0], sc_a[0], norm_a[0], w_in_a16, segs_ap,
                                                     dts_a + (F32, F32), tm_p)
    oa = moba_prompt(q, k.reshape(batch, seq, KV_W), v.reshape(batch, seq, KV_W), rel_bias, batch, seq)
    ob = conv_prompt(u, szb, conv_w_b[0], conv_b_b[0], ln_g_b[0], ln_b_b[0], batch, seq)
    xp1 = out_proj([(oa, sza, w_out_a16[:hq])], [(ob, w_out_a16[hq:])], xp, gt_a[0], None, tm_p)
    ak_p = k_rows.reshape(1, batch, seq, N_KV, KV_W)[..., :HEAD_DIM]
    av_p = v_rows.reshape(1, batch, seq, N_KV, KV_W)[..., :HEAD_DIM]
    bc_p = u.reshape(batch, seq, cb)[:, seq - (CONV_W - 1):]

    q, k, v, sza, u, szb = ln_inproj(xs, sh_a[1], sc_a[1], norm_a[0], w_in_a16, segs_a, dts_a, tm_s)
    oa = moba_sample(q, k, v, k_pool, v_pool, page_table, rel_bias, t_new)
    xcat = jnp.concatenate([cache_b_conv[0], u.reshape(n, t_new, cb)], axis=1)
    ob = conv_sample(xcat, szb.reshape(n, t_new, cb), conv_w_b[0], conv_b_b[0], ln_g_b[0], ln_b_b[0], t_new)
    xs1 = out_proj([(oa, sza, w_out_a16[:hq])], [(ob.reshape(n * t_new, cb), w_out_a16[hq:])],
                   xs, gt_a[1], None, tm_s)
    ak_s, av_s = k, v
    bc_s = xcat[:, t_new:]

    segs_c = (("raw", 0, hq), ("raw", hq, hq + KV_W), ("raw", hq + KV_W, hq + 2 * KV_W),
              ("silu", hq + 2 * KV_W, 2 * hq + 2 * KV_W))
    dts_c = (F32, F32, F32, BF16)
    w_in_c16 = w_in_c[0].astype(BF16)
    w_out_c16 = w_out_c[0].astype(BF16)

    q, k, v, sz = ln_inproj(xp1, sh_c[0], sc_c[0], norm_c[0], w_in_c16, segs_c, dts_c, tm_p)
    o = swa_prompt(q, k, v, sinks_c[0], rel_bias, batch, seq)
    y_prompt = out_proj([(o, sz, w_out_c16)], [], xp1, gt_c[0], final_norm, tm_p)
    wb_p = min(SWA_WINDOW, seq)
    ck_p = k.reshape(batch, seq, KV_W)[:, seq - wb_p:]
    cv_p = v.reshape(batch, seq, KV_W)[:, seq - wb_p:]

    q, k, v, sz = ln_inproj(xs1, sh_c[1], sc_c[1], norm_c[0], w_in_c16, segs_c, dts_c, tm_s)
    wb_s = cache_c_k.shape[2]
    kb_t = cache_c_k[0].transpose(0, 2, 3, 1).reshape(n, KV_W, wb_s)
    vb_t = cache_c_v[0].transpose(0, 2, 3, 1).reshape(n, KV_W, wb_s)
    o, kb_t, vb_t = swa_sample(q, kb_t, vb_t, k, v, sinks_c[0], rel_bias, t_new)
    y_sample = out_proj([(o, sz, w_out_c16)], [], xs1, gt_c[1], final_norm, tm_s)
    ck_s = kb_t.reshape(n, N_KV, HEAD_DIM, wb_s).transpose(0, 3, 1, 2)[None]
    cv_s = vb_t.reshape(n, N_KV, HEAD_DIM, wb_s).transpose(0, 3, 1, 2)[None]

    def kv5(a, lead):
        return a.reshape((1,) + lead + (N_KV, HEAD_DIM))

    return (y_prompt.reshape(batch, seq, d), y_sample,
            ak_p, av_p,
            kv5(ak_s, (n, t_new)), kv5(av_s, (n, t_new)),
            bc_p[None], bc_s[None],
            kv5(ck_p, (batch, wb_p)), kv5(cv_p, (batch, wb_p)),
            ck_s, cv_s)
```

```python
import functools
import math

import jax
import jax.numpy as jnp
import numpy as np
from jax import lax
from jax.experimental import pallas as pl
from jax.experimental.pallas import tpu as pltpu

F32 = jnp.float32
BF16 = jnp.bfloat16
NEG_INF = float("-inf")
MASK_NEG = -1e30

HEAD_DIM = 64
N_HEADS = 16
N_KV = 2
GROUP = N_HEADS // N_KV
KV_W = N_KV * HEAD_DIM
MOBA_BLOCK = 256
MOBA_TOPK = 3
CONV_W = 31
SWA_WINDOW = 128
N_BUCKETS = 32
MAX_DISTANCE = 128
EPS = 1e-6
SCALE = HEAD_DIM ** -0.5
LOG2E = math.log2(math.e)

SUBLANES = 8
LANES = 128
VMEM_LIMIT = 56 * 2**20


def _params(*sem):
    return pltpu.CompilerParams(dimension_semantics=sem, vmem_limit_bytes=VMEM_LIMIT)


def _silu(z):
    return z * jax.nn.sigmoid(z)


def _dot(a, b):
    return jnp.dot(a, b, preferred_element_type=F32)


def _dot_nt(a, b):
    return lax.dot_general(a, b, (((1,), (1,)), ((), ())), preferred_element_type=F32)


def _dot_f32(a, b):
    return jnp.dot(a, b, preferred_element_type=F32, precision=lax.Precision.HIGHEST)


def _t5_bucket_np(n):
    n = np.maximum(n, 0)
    max_exact = N_BUCKETS // 2
    nf = np.maximum(n, 1).astype(np.float32)
    large = max_exact + (np.log(nf / np.float32(max_exact)) / np.float32(math.log(MAX_DISTANCE / max_exact))
                         * np.float32(N_BUCKETS - max_exact)).astype(np.int32)
    large = np.minimum(large, N_BUCKETS - 1)
    return np.where(n < max_exact, n, large).astype(np.int32)


def _bucket_matrix(dist, valid):
    return np.where(valid, _t5_bucket_np(dist), -1).astype(np.int32)


def _bias_kernel(bm_ref, rb_ref, sub_ref, o_ref, *, mul):
    bm = bm_ref[...]
    acc = jnp.zeros(bm.shape, F32)
    for b in range(N_BUCKETS):
        acc = jnp.where(bm == b, rb_ref[b:b + 1, :], acc)
    o_ref[...] = jnp.where(bm < 0, NEG_INF, (acc - sub_ref[...]) * mul)


def bias_table(bm, rbx, sub, mul=1.0):
    g, _, c = rbx.shape
    r = bm.shape[0]
    return pl.pallas_call(
        functools.partial(_bias_kernel, mul=mul),
        grid=(g,),
        in_specs=[pl.BlockSpec((r, c), lambda i: (0, 0)),
                  pl.BlockSpec((None, N_BUCKETS, c), lambda i: (i, 0, 0)),
                  pl.BlockSpec((None, 1, c), lambda i: (i, 0, 0))],
        out_specs=pl.BlockSpec((None, r, c), lambda i: (i, 0, 0)),
        out_shape=jax.ShapeDtypeStruct((g, r, c), F32),
        compiler_params=_params("arbitrary"),
        name="bias_table",
    )(jnp.asarray(bm), rbx, sub)


def _mod_kernel(c_ref, w_ref, b_ref, o_ref):
    o_ref[...] = _dot_f32(_silu(c_ref[...]), w_ref[...]) + b_ref[...]


def modulation(c, w, b):
    n, d = c.shape
    m = w.shape[1]
    tn = 512
    return pl.pallas_call(
        _mod_kernel,
        grid=(m // tn,),
        in_specs=[pl.BlockSpec((n, d), lambda j: (0, 0)),
                  pl.BlockSpec((d, tn), lambda j: (0, j)),
                  pl.BlockSpec((1, tn), lambda j: (0, j))],
        out_specs=pl.BlockSpec((n, tn), lambda j: (0, j)),
        out_shape=jax.ShapeDtypeStruct((n, m), F32),
        compiler_params=_params("arbitrary"),
        name="modulation",
    )(c, w, b.reshape(1, m))


def _ln_inproj_kernel(x_ref, shift_ref, scale_ref, g_ref, w_ref, *out_refs, segs):
    x = x_ref[...]
    y = x * lax.rsqrt(jnp.mean(x * x, axis=-1, keepdims=True) + EPS)
    h = (y * g_ref[...]) * (1.0 + scale_ref[...]) + shift_ref[...]
    h16 = h.reshape(-1, h.shape[-1]).astype(BF16)
    raw = {}
    for o_ref, seg in zip(out_refs, segs, strict=True):
        kind, lo, hi = seg[0], seg[1], seg[2]
        z = raw[lo, hi] if (lo, hi) in raw else _dot(h16, w_ref[:, lo:hi])
        raw[lo, hi] = z
        if kind == "kv_rows":
            tm = z.shape[0]
            o_ref[pl.ds(0, tm, stride=N_KV), :] = z
            o_ref[pl.ds(1, tm, stride=N_KV), :] = pltpu.roll(z, HEAD_DIM, axis=1)
            continue
        if kind == "silu":
            z = _silu(z)
        elif kind == "glu":
            z = z * jax.nn.sigmoid(_dot(h16, w_ref[:, seg[3]:seg[4]]))
        o_ref[...] = z.astype(o_ref.dtype)


def ln_inproj(x, shift, scale, norm_g, w16, segs, out_dtypes, tm):
    d = x.shape[-1]
    if x.ndim == 2:
        r = x.shape[0]
        nt = r // tm
        per_group = nt // shift.shape[0]
        x_spec = pl.BlockSpec((tm, d), lambda i: (i, 0))
        mod_spec = pl.BlockSpec((None, 1, d), lambda i: (i // per_group, 0, 0))
    else:
        n, t_new, _ = x.shape
        r = n * t_new
        nt = r // tm
        x_spec = pl.BlockSpec((tm // t_new, t_new, d), lambda i: (i, 0, 0))
        mod_spec = pl.BlockSpec((tm // t_new, 1, d), lambda i: (i, 0, 0))
    mult = [N_KV if s[0] == "kv_rows" else 1 for s in segs]
    out_shape = [jax.ShapeDtypeStruct((m * r, s[2] - s[1]), dt) for m, s, dt in zip(mult, segs, out_dtypes, strict=True)]
    out_specs = [pl.BlockSpec((m * tm, s[2] - s[1]), lambda i: (i, 0)) for m, s in zip(mult, segs)]
    return pl.pallas_call(
        functools.partial(_ln_inproj_kernel, segs=segs),
        grid=(nt,),
        in_specs=[x_spec, mod_spec, mod_spec,
                  pl.BlockSpec((1, d), lambda i: (0, 0)),
                  pl.BlockSpec(w16.shape, lambda i: (0, 0), pipeline_mode=pl.Buffered(1))],
        out_specs=out_specs,
        out_shape=out_shape,
        compiler_params=_params("arbitrary"),
        name="ln_inproj",
    )(x, shift, scale, norm_g.reshape(1, d), w16)


def _out_proj_kernel(*refs, n_gated, n_plain, final_norm):
    it = iter(refs)
    y = None
    for _ in range(n_gated):
        a_ref, m_ref, w_ref = next(it), next(it), next(it)
        t = _dot((a_ref[...] * m_ref[...].astype(F32)).astype(BF16), w_ref[...])
        y = t if y is None else y + t
    for _ in range(n_plain):
        a_ref, w_ref = next(it), next(it)
        t = _dot(a_ref[...], w_ref[...])
        y = t if y is None else y + t
    x_ref, gate_ref = next(it), next(it)
    xn = x_ref[...] + gate_ref[...] * y.reshape(x_ref.shape)
    if final_norm:
        fg_ref = next(it)
        xn = xn * lax.rsqrt(jnp.mean(xn * xn, axis=-1, keepdims=True) + EPS) * fg_ref[...]
    o_ref = next(it)
    o_ref[...] = xn


def out_proj(gated, plain, x, gate, final_g, tm):
    d = x.shape[-1]
    if x.ndim == 2:
        nt = x.shape[0] // tm
        per_group = nt // gate.shape[0]
        x_spec = pl.BlockSpec((tm, d), lambda i: (i, 0))
        gate_spec = pl.BlockSpec((None, 1, d), lambda i: (i // per_group, 0, 0))
    else:
        n, t_new, _ = x.shape
        nt = n * t_new // tm
        x_spec = pl.BlockSpec((tm // t_new, t_new, d), lambda i: (i, 0, 0))
        gate_spec = pl.BlockSpec((tm // t_new, 1, d), lambda i: (i, 0, 0))
    args, specs = [], []
    for a, m, w in gated:
        args += [a, m, w]
        specs += [pl.BlockSpec((tm, a.shape[1]), lambda i: (i, 0)),
                  pl.BlockSpec((tm, m.shape[1]), lambda i: (i, 0)),
                  pl.BlockSpec(w.shape, lambda i: (0, 0))]
    for a, w in plain:
        args += [a, w]
        specs += [pl.BlockSpec((tm, a.shape[1]), lambda i: (i, 0)),
                  pl.BlockSpec(w.shape, lambda i: (0, 0))]
    args += [x, gate]
    specs += [x_spec, gate_spec]
    if final_g is not None:
        args.append(final_g.reshape(1, d))
        specs.append(pl.BlockSpec((1, d), lambda i: (0, 0)))
    return pl.pallas_call(
        functools.partial(_out_proj_kernel, n_gated=len(gated), n_plain=len(plain),
                          final_norm=final_g is not None),
        grid=(nt,),
        in_specs=specs,
        out_specs=x_spec,
        out_shape=jax.ShapeDtypeStruct(x.shape, F32),
        compiler_params=_params("arbitrary"),
        name="out_proj",
    )(*args)


def _top3_mask(scores, n_valid):
    nb = scores.shape[0]
    blk = lax.broadcasted_iota(jnp.int32, scores.shape, 0)
    s = jnp.where(blk < n_valid, scores, NEG_INF)
    picked = jnp.zeros(scores.shape, F32)
    for _ in range(MOBA_TOPK):
        mx = jnp.max(s, axis=0, keepdims=True)
        first = jnp.min(jnp.where(s == mx, blk, nb), axis=0, keepdims=True)
        hit = blk == first
        picked = jnp.where(hit, 1.0, picked)
        s = jnp.where(hit, NEG_INF, s)
    return jnp.where(blk < n_valid, picked, 0.0)


V_AUG = HEAD_DIM + 16


def _moba_prompt_kernel(q_ref, k_ref, v_ref, bown_ref, badj_ref, o_ref,
                        k16_s, vT_s, kbar_s, qT16_s, sel_s, m_s, acc_s,
                        s0_s, s1_s, p0_s, p1_s, a0_s, a1_s, *, nblk, chunk, sub):
    g = pl.program_id(1)
    i = pl.program_id(2)
    rows = GROUP * MOBA_BLOCK
    nch = rows // chunk

    @pl.when((g == 0) & (i == 0))
    def _per_batch():
        ones = jnp.ones((V_AUG - HEAD_DIM, MOBA_BLOCK), BF16)
        for j in range(nblk):
            kb = k_ref[j * MOBA_BLOCK:(j + 1) * MOBA_BLOCK, :]
            k16_s[j * MOBA_BLOCK:(j + 1) * MOBA_BLOCK, :] = kb.astype(BF16)
            kbar_s[j:j + 1, :] = jnp.mean(kb, axis=0, keepdims=True)
            v_t = v_ref[j * MOBA_BLOCK:(j + 1) * MOBA_BLOCK, :].T.astype(BF16)
            for kv in range(N_KV):
                vT_s[kv, j] = jnp.concatenate([v_t[kv * HEAD_DIM:(kv + 1) * HEAD_DIM], ones], axis=0)

    qT = (q_ref[...] * (SCALE * LOG2E)).T
    qs = jnp.concatenate([qT[h * HEAD_DIM:(h + 1) * HEAD_DIM, :] for h in range(GROUP)], axis=1)
    qs16 = qs.astype(BF16)
    zeros = jnp.zeros_like(qs16)
    qpad16 = jnp.where(g == 0, jnp.concatenate([qs16, zeros], axis=0), jnp.concatenate([zeros, qs16], axis=0))
    for c in range(nch):
        qT16_s[c] = qpad16[:, c * chunk:(c + 1) * chunk]
    kbar = kbar_s[...]
    kbar_g = jnp.where(g == 0, kbar[:, :HEAD_DIM], kbar[:, HEAD_DIM:])
    sel_s[...] = _top3_mask(_dot_f32(kbar_g, qs), i)
    m_s[...] = jnp.full(m_s.shape, NEG_INF, F32)
    acc_s[...] = jnp.zeros(acc_s.shape, F32)

    n_far = i - 1
    sbuf, pbuf, abuf = (s0_s, s1_s), (p0_s, p1_s), (a0_s, a1_s)

    def blk_of(t):
        far_j = jnp.clip(t - 2, 0, jnp.maximum(n_far - 1, 0))
        return jnp.where(t == 0, i, jnp.where(t == 1, jnp.maximum(i - 1, 0), far_j))

    per = chunk // sub

    def stage_qk(t, slot, c):
        kb = k16_s[pl.ds(pl.multiple_of(blk_of(t) * MOBA_BLOCK, MOBA_BLOCK), MOBA_BLOCK), :]
        s_t = _dot(kb, qT16_s[c])
        for k in range(per):
            sbuf[slot][c * per + k] = s_t[:, k * sub:(k + 1) * sub]

    def stage_softmax(slot, c, on, bias_ref):
        ln = slice(c * sub, (c + 1) * sub)
        s_t = sbuf[slot][c]
        if bias_ref is not None:
            s_t = s_t + bias_ref[c]
        m_old = m_s[:, ln]
        m_new = jnp.maximum(m_old, jnp.where(on, jnp.max(s_t, axis=0, keepdims=True), NEG_INF))
        pbuf[slot][c] = jnp.exp2(s_t - jnp.where(on, m_new, jnp.inf)).astype(BF16)
        abuf[slot][:, ln] = jnp.exp2(m_old - m_new)
        m_s[:, ln] = m_new

    def stage_pv(t, slot, c):
        p = jnp.concatenate([pbuf[slot][c * per + k] for k in range(per)], axis=1)
        acc_s[c] = acc_s[c] * abuf[slot][:, c * chunk:(c + 1) * chunk] + _dot(vT_s[g, blk_of(t)], p)

    def pipe_half(t, slot, bias_ref=None, pv=True, qk=True):
        own = bias_ref is bown_ref
        valid = None if own else jnp.where(t == 1, i >= 1, t - 2 < n_far)
        j = blk_of(t)
        for c in range(nch):
            if qk:
                stage_qk(t + 1, 1 - slot, c)
            if own:
                on = jnp.full((1, chunk), True)
            else:
                on = (sel_s[pl.ds(j, 1), c * chunk:(c + 1) * chunk] > 0.0) & valid
            for k in range(per):
                stage_softmax(slot, c * per + k, on[:, k * sub:(k + 1) * sub], bias_ref)
            if pv:
                stage_pv(t - 1, 1 - slot, c)

    for c in range(nch):
        stage_qk(0, 0, c)
    pipe_half(0, 0, bown_ref, pv=False)
    pipe_half(1, 1, badj_ref)

    def pair(u, carry):
        pipe_half(2 + 2 * u, 0)
        pipe_half(3 + 2 * u, 1)
        return carry

    n_pairs = jnp.maximum(n_far, 0) // 2
    lax.fori_loop(0, n_pairs, pair, 0)
    t_tail = 2 + 2 * n_pairs
    odd = jnp.maximum(n_far, 0) % 2 == 1

    @pl.when(odd)
    def _tail():
        pipe_half(t_tail, 0, qk=False)
        for c in range(nch):
            stage_pv(t_tail, 0, c)

    @pl.when(jnp.logical_not(odd))
    def _drain():
        for c in range(nch):
            stage_pv(t_tail - 1, 1, c)

    acc = jnp.concatenate([acc_s[c] for c in range(nch)], axis=1)
    o_t = acc[:HEAD_DIM] / acc[HEAD_DIM:HEAD_DIM + 1]
    o_cat = jnp.concatenate([o_t[:, h * MOBA_BLOCK:(h + 1) * MOBA_BLOCK] for h in range(GROUP)], axis=0)
    o_ref[...] = o_cat.T.astype(o_ref.dtype)


def moba_prompt(q, k, v, rel_bias, batch, seq):
    nblk = seq // MOBA_BLOCK
    rows = GROUP * MOBA_BLOCK
    kk = np.arange(MOBA_BLOCK)[:, None]
    qq = (np.arange(rows) % MOBA_BLOCK)[None, :]
    bm_own = _bucket_matrix(qq - kk, qq >= kk)
    bm_adj = _bucket_matrix(MOBA_BLOCK + qq - kk, np.ones((MOBA_BLOCK, rows), bool))
    rbx = jnp.repeat(rel_bias.reshape(N_BUCKETS, N_KV, GROUP).transpose(1, 0, 2), MOBA_BLOCK, axis=2)
    far = rbx[:, N_BUCKETS - 1:N_BUCKETS, :]
    chunk, sub = 256, 128
    nsub = rows // sub

    def chunk_major(tbl):
        return tbl.reshape(N_KV, MOBA_BLOCK, nsub, sub).transpose(0, 2, 1, 3)

    b_own = chunk_major(bias_table(bm_own, rbx, far, LOG2E))
    b_adj = chunk_major(bias_table(bm_adj, rbx, far, LOG2E))
    qw = GROUP * HEAD_DIM
    tbl_spec = pl.BlockSpec((None, nsub, MOBA_BLOCK, sub), lambda b, g, i: (g, 0, 0, 0))
    return pl.pallas_call(
        functools.partial(_moba_prompt_kernel, nblk=nblk, chunk=chunk, sub=sub),
        grid=(batch, N_KV, nblk),
        in_specs=[pl.BlockSpec((MOBA_BLOCK, qw), lambda b, g, i: (b * nblk + i, g)),
                  pl.BlockSpec((None, seq, KV_W), lambda b, g, i: (b, 0, 0)),
                  pl.BlockSpec((None, seq, KV_W), lambda b, g, i: (b, 0, 0)),
                  tbl_spec, tbl_spec],
        out_specs=pl.BlockSpec((MOBA_BLOCK, qw), lambda b, g, i: (b * nblk + i, g)),
        out_shape=jax.ShapeDtypeStruct(q.shape, BF16),
        scratch_shapes=[pltpu.VMEM((seq, KV_W), BF16),
                        pltpu.VMEM((N_KV, nblk, V_AUG, MOBA_BLOCK), BF16),
                        pltpu.VMEM((nblk, KV_W), F32),
                        pltpu.VMEM((rows // chunk, KV_W, chunk), BF16),
                        pltpu.VMEM((nblk, rows), F32),
                        pltpu.VMEM((1, rows), F32),
                        pltpu.VMEM((rows // chunk, V_AUG, chunk), F32),
                        pltpu.VMEM((nsub, MOBA_BLOCK, sub), F32), pltpu.VMEM((nsub, MOBA_BLOCK, sub), F32),
                        pltpu.VMEM((nsub, MOBA_BLOCK, sub), BF16), pltpu.VMEM((nsub, MOBA_BLOCK, sub), BF16),
                        pltpu.VMEM((1, rows), F32), pltpu.VMEM((1, rows), F32)],
        compiler_params=_params("arbitrary", "arbitrary", "arbitrary"),
        name="moba_prompt",
    )(q, k, v, b_own, b_adj)


def _heads_to_rows(q):
    t = q.shape[0]
    lane = lax.broadcasted_iota(jnp.int32, (t, LANES), 1)
    pieces = []
    for h in range(N_HEADS):
        src = q[:, (h // 2) * LANES:(h // 2 + 1) * LANES]
        g = h // GROUP
        if h % 2 != g:
            src = pltpu.roll(src, HEAD_DIM, axis=1)
        keep = lane < HEAD_DIM if g == 0 else lane >= HEAD_DIM
        pieces.append(jnp.where(keep, src, 0.0))
    return jnp.concatenate(pieces, axis=0)


def _rows_to_heads(acc, t):
    lane = lax.broadcasted_iota(jnp.int32, (t, LANES), 1)
    cols = []
    for k in range(N_HEADS // 2):
        a = acc[2 * k * t:(2 * k + 1) * t, :]
        b = acc[(2 * k + 1) * t:(2 * k + 2) * t, :]
        if (2 * k) // GROUP == 1:
            a = pltpu.roll(a, HEAD_DIM, axis=1)
        else:
            b = pltpu.roll(b, HEAD_DIM, axis=1)
        cols.append(jnp.where(lane < HEAD_DIM, a, b))
    return jnp.concatenate(cols, axis=1)


def _pad_rows(x, rows):
    return jnp.concatenate([x, jnp.zeros((rows - x.shape[0], x.shape[1]), x.dtype)], axis=0)


def _moba_sample_kernel(pt_ref, q_ref, kn_ref, vn_ref, blast_ref, bown_ref, e_ref, kpool, vpool, o_ref,
                        kbuf, vbuf, ksem, vsem, s_s, *, n_pages, page, n_samples, unroll):
    s = pl.program_id(0)
    slot = s % 2
    ppb = MOBA_BLOCK // page
    nblk = n_pages // ppb
    t_new = q_ref.shape[0]
    rows = N_HEADS * t_new

    def k_copy(smp, sl, p):
        return pltpu.make_async_copy(kpool.at[pt_ref[smp, p]], kbuf.at[sl, p], ksem.at[sl])

    def v_copy(smp, sl, p):
        return pltpu.make_async_copy(vpool.at[pt_ref[smp, p]], vbuf.at[sl, p], vsem.at[sl])

    def start_all(smp, sl):
        def body(p, c):
            k_copy(smp, sl, p).start()
            v_copy(smp, sl, p).start()
            return c
        lax.fori_loop(0, n_pages, body, 0, unroll=8)

    def wait_all(copy, smp, sl):
        def body(p, c):
            copy(smp, sl, p).wait()
            return c
        lax.fori_loop(0, n_pages, body, 0, unroll=8)

    def block_t(buf, j):
        return jnp.concatenate([buf[slot, j * ppb + t] for t in range(ppb)], axis=1)

    blk_lane = lax.broadcasted_iota(jnp.int32, (KV_W, nblk), 1)

    @pl.when(s == 0)
    def _first():
        start_all(0, 0)

    wait_all(k_copy, s, slot)
    nxt = jnp.minimum(s + 1, n_samples - 1)

    def kbar_body(j, kbar_t):
        for t in range(ppb):
            k_copy(nxt, 1 - slot, j * ppb + t).start()
            v_copy(nxt, 1 - slot, j * ppb + t).start()
        pages = kbuf[slot, j * ppb]
        for t in range(1, ppb):
            pages = pages + kbuf[slot, j * ppb + t]
        col = jnp.sum(pages, axis=1, keepdims=True) * (1.0 / MOBA_BLOCK)
        return jnp.where(blk_lane == j, col, kbar_t)

    kbar_t = lax.fori_loop(0, nblk, kbar_body, jnp.zeros((KV_W, nblk), F32), unroll=unroll[0])

    qbd = _heads_to_rows(q_ref[...] * (SCALE * LOG2E))
    qbd16 = qbd.astype(BF16)

    scores = _dot_f32(qbd, kbar_t)
    sel_t = _top3_mask(scores.T, nblk)
    negm = jnp.where(sel_t.T > 0.0, 0.0, MASK_NEG)
    lhs16 = jnp.concatenate([qbd, negm, jnp.zeros((rows, e_ref.shape[1] - nblk), F32)], axis=1).astype(BF16)

    def logits(j):
        rhs16 = jnp.concatenate([block_t(kbuf, j).astype(BF16), e_ref[j]], axis=0)
        return _dot(lhs16, rhs16)

    def fold(x):
        return x[:, :LANES], x[:, LANES:]

    def far(j, mrun):
        st = logits(j)
        s_s[j] = st
        lo, hi = fold(st)
        return jnp.maximum(mrun, jnp.maximum(lo, hi))

    mrun = lax.fori_loop(0, nblk - 1, far, jnp.full((rows, LANES), NEG_INF, F32), unroll=unroll[1])
    s_last = logits(nblk - 1) + blast_ref[...]
    s_s[nblk - 1] = s_last
    kn16 = _pad_rows(kn_ref[...], LANES).astype(BF16)
    s_own = _dot_nt(qbd16, kn16) + bown_ref[...]
    for part in fold(s_last) + (s_own,):
        mrun = jnp.maximum(mrun, part)
    m = jnp.max(mrun, axis=1, keepdims=True)

    wait_all(v_copy, s, slot)

    def pv(j, carry):
        acc, lsum = carry
        p = jnp.exp2(s_s[j] - m)
        lo, hi = fold(p)
        return acc + _dot_nt(p.astype(BF16), block_t(vbuf, j).astype(BF16)), lsum + (lo + hi)

    p_own = jnp.exp2(s_own - m)
    acc0 = _dot(p_own.astype(BF16), _pad_rows(vn_ref[...], LANES).astype(BF16))
    acc, lsum = lax.fori_loop(0, nblk, pv, (acc0, p_own), unroll=unroll[2])
    den = jnp.sum(lsum, axis=1, keepdims=True)
    o_ref[...] = _rows_to_heads(acc / den, t_new)

    @pl.when(s == n_samples - 1)
    def _drain():
        wait_all(k_copy, nxt, 1 - slot)
        wait_all(v_copy, nxt, 1 - slot)


def moba_sample(q, k_new, v_new, k_pool, v_pool, page_table, rel_bias, t_new):
    n, n_pages = page_table.shape
    page = k_pool.shape[2]
    past = n_pages * page
    nblk = past // MOBA_BLOCK
    rows = N_HEADS * t_new
    tok = (np.arange(rows) % t_new)[None, :]
    kk = np.arange(MOBA_BLOCK)[:, None]
    bm_last = _bucket_matrix(MOBA_BLOCK + tok - kk, np.ones((MOBA_BLOCK, rows), bool))
    ko = np.arange(LANES)[:, None]
    bm_own = _bucket_matrix(tok - ko, (ko <= tok) & (ko < t_new))
    rbx = jnp.repeat(rel_bias, t_new, axis=1)[None]
    far = rbx[:, N_BUCKETS - 1:N_BUCKETS, :]
    b_last = bias_table(bm_last, rbx, far, LOG2E)[0].T
    b_own = bias_table(bm_own, rbx, far, LOG2E)[0].T
    onehot_np = np.zeros((nblk, KV_W, MOBA_BLOCK), np.float32)
    onehot_np[np.arange(nblk), np.arange(nblk)] = 1.0
    onehot = jnp.asarray(onehot_np, BF16)

    grid_spec = pltpu.PrefetchScalarGridSpec(
        num_scalar_prefetch=1,
        grid=(n,),
        in_specs=[pl.BlockSpec((t_new, q.shape[1]), lambda s, pt: (s, 0)),
                  pl.BlockSpec((t_new, KV_W), lambda s, pt: (s, 0)),
                  pl.BlockSpec((t_new, KV_W), lambda s, pt: (s, 0)),
                  pl.BlockSpec((rows, MOBA_BLOCK), lambda s, pt: (0, 0)),
                  pl.BlockSpec((rows, LANES), lambda s, pt: (0, 0)),
                  pl.BlockSpec((nblk, KV_W, MOBA_BLOCK), lambda s, pt: (0, 0, 0)),
                  pl.BlockSpec(memory_space=pl.ANY),
                  pl.BlockSpec(memory_space=pl.ANY)],
        out_specs=pl.BlockSpec((t_new, q.shape[1]), lambda s, pt: (s, 0)),
        scratch_shapes=[pltpu.VMEM((2, n_pages, KV_W, page), F32),
                        pltpu.VMEM((2, n_pages, KV_W, page), F32),
                        pltpu.SemaphoreType.DMA((2,)),
                        pltpu.SemaphoreType.DMA((2,)),
                        pltpu.VMEM((nblk, rows, MOBA_BLOCK), F32)],
    )
    return pl.pallas_call(
        functools.partial(_moba_sample_kernel, n_pages=n_pages, page=page, n_samples=n, unroll=(32, 63, 32)),
        grid_spec=grid_spec,
        out_shape=jax.ShapeDtypeStruct(q.shape, F32),
        compiler_params=_params("arbitrary"),
        name="moba_sample",
    )(page_table, q, k_new, v_new, b_last, b_own, onehot, k_pool, v_pool)


def _conv_tail(y, cb_ref, lg_ref, lb_ref, gate):
    y = y + cb_ref[...]
    mu = jnp.mean(y, axis=-1, keepdims=True)
    var = jnp.mean(jnp.square(y - mu), axis=-1, keepdims=True)
    yn = (y - mu) * lax.rsqrt(var + EPS) * lg_ref[...] + lb_ref[...]
    return _silu(yn) * gate


HALO = 32


CONV_ROWS = 64


def _conv_prompt_kernel(u_ref, prev_ref, g_ref, w_ref, cb_ref, lg_ref, lb_ref, o_ref, xs, xr, ys, *, tl):
    t = pl.program_id(1)
    c = u_ref.shape[1]
    xs[0:HALO, :] = jnp.where(t > 0, prev_ref[...], 0.0)
    xs[HALO:, :] = u_ref[...]
    span = tl + HALO - SUBLANES
    for r in range(1, SUBLANES):
        xr[r - 1] = xs[pl.ds(r, span), :]
    off = HALO - (CONV_W - 1)

    def taps(base, lt):
        ln = slice(lt * LANES, (lt + 1) * LANES)
        acc = jnp.zeros((CONV_ROWS, LANES), F32)
        for k in range(CONV_W):
            a, r = divmod(off + k, SUBLANES)
            src = xs if r == 0 else xr.at[r - 1]
            start = base + a * SUBLANES
            acc = acc + src[start:start + CONV_ROWS, ln] * w_ref[k:k + 1, ln]
        ys[base:base + CONV_ROWS, ln] = acc

    for base in range(0, tl, CONV_ROWS):
        for lt in range(0, c // LANES, 2):
            @pl.when(t >= 0)
            def _():
                taps(base, lt)
                taps(base, lt + 1)
    o_ref[...] = _conv_tail(ys[...], cb_ref, lg_ref, lb_ref, g_ref[...].astype(F32)).astype(o_ref.dtype)


def conv_prompt(u, szb, conv_w, conv_b, ln_g, ln_b, batch, seq, tl=256):
    c = u.shape[1]
    nt = seq // tl
    vec = pl.BlockSpec((1, c), lambda b, t: (0, 0))
    return pl.pallas_call(
        functools.partial(_conv_prompt_kernel, tl=tl),
        grid=(batch, nt),
        in_specs=[pl.BlockSpec((tl, c), lambda b, t: (b * nt + t, 0)),
                  pl.BlockSpec((HALO, c), lambda b, t: (jnp.maximum((b * nt + t) * (tl // HALO) - 1, 0), 0)),
                  pl.BlockSpec((tl, c), lambda b, t: (b * nt + t, 0)),
                  pl.BlockSpec((CONV_W, c), lambda b, t: (0, 0)), vec, vec, vec],
        out_specs=pl.BlockSpec((tl, c), lambda b, t: (b * nt + t, 0)),
        out_shape=jax.ShapeDtypeStruct(u.shape, BF16),
        scratch_shapes=[pltpu.VMEM((HALO + tl, c), F32),
                        pltpu.VMEM((SUBLANES - 1, HALO + tl - SUBLANES, c), F32),
                        pltpu.VMEM((tl, c), F32)],
        compiler_params=_params("arbitrary", "arbitrary"),
        name="conv_prompt",
    )(u, u, szb, conv_w, conv_b.reshape(1, c), ln_g.reshape(1, c), ln_b.reshape(1, c))


def _conv_sample_kernel(h_ref, u_ref, g_ref, w_ref, cb_ref, lg_ref, lb_ref, o_ref, hn_ref, *, t_new):
    n_hist = h_ref.shape[0]
    c = h_ref.shape[2]

    def row(r):
        return h_ref.at[r] if r < n_hist else u_ref.at[r - n_hist]

    for t in range(t_new):
        for lt in range(c // LANES):
            ln = slice(lt * LANES, (lt + 1) * LANES)
            acc = row(t)[:, ln] * w_ref[0:1, ln]
            for k in range(1, CONV_W):
                acc = acc + row(t + k)[:, ln] * w_ref[k:k + 1, ln]
            o_ref[t, :, ln] = acc
    for t in range(t_new):
        o_ref[t] = _conv_tail(o_ref[t], cb_ref, lg_ref, lb_ref, g_ref[t].astype(F32))
    hn_ref[0:n_hist - t_new] = h_ref[t_new:n_hist]
    hn_ref[n_hist - t_new:n_hist] = u_ref[...]


def conv_sample(hist_t, u_t, szb_t, conv_w, conv_b, ln_g, ln_b, ns=16):
    n_hist, n, c = hist_t.shape
    t_new = u_t.shape[0]
    vec = pl.BlockSpec((1, c), lambda i: (0, 0))
    hist = pl.BlockSpec((n_hist, ns, c), lambda i: (0, i, 0))
    new = pl.BlockSpec((t_new, ns, c), lambda i: (0, i, 0))
    return pl.pallas_call(
        functools.partial(_conv_sample_kernel, t_new=t_new),
        grid=(n // ns,),
        in_specs=[hist, new, new, pl.BlockSpec((CONV_W, c), lambda i: (0, 0)), vec, vec, vec],
        out_specs=[new, hist],
        out_shape=[jax.ShapeDtypeStruct(u_t.shape, F32), jax.ShapeDtypeStruct(hist_t.shape, F32)],
        compiler_params=_params("arbitrary"),
        name="conv_sample",
    )(hist_t, u_t, szb_t, conv_w, conv_b.reshape(1, c), ln_g.reshape(1, c), ln_b.reshape(1, c))


def _swa_prompt_kernel(q_ref, kc_ref, kp_ref, vc_ref, vp_ref, bias_ref, sink_ref, o_ref, s_s, p_s, *, chunk, sub):
    n = pl.program_id(1)
    w = SWA_WINDOW
    rows = GROUP * w
    nch, per = rows // chunk, chunk // sub
    gw = GROUP * HEAD_DIM
    kcat = jnp.concatenate([kp_ref[...], kc_ref[...]], axis=0).astype(BF16)
    v_t = jnp.concatenate([vp_ref[...], vc_ref[...]], axis=0).T.astype(BF16)
    ones = jnp.ones((V_AUG - HEAD_DIM, 2 * w), BF16)
    no_prev = (lax.broadcasted_iota(jnp.int32, (2 * w, sub), 0) < w) & (n == 0)
    qpads, v_augs, sinks = [], [], []
    for g in range(N_KV):
        q_t = (q_ref[:, g * gw:(g + 1) * gw] * (SCALE * LOG2E)).T
        qs = jnp.concatenate([q_t[h * HEAD_DIM:(h + 1) * HEAD_DIM, :] for h in range(GROUP)], axis=1)
        zeros = jnp.zeros_like(qs)
        qpads.append(jnp.concatenate([qs, zeros] if g == 0 else [zeros, qs], axis=0).astype(BF16))
        v_augs.append(jnp.concatenate([v_t[g * HEAD_DIM:(g + 1) * HEAD_DIM], ones], axis=0))
        sinks.append(sink_ref[g] * LOG2E)

    def qk(g, c):
        s_t = _dot(kcat, qpads[g][:, c * chunk:(c + 1) * chunk])
        for k in range(per):
            s_s[g, c * per + k] = s_t[:, k * sub:(k + 1) * sub]

    def softmax(g, cs):
        s_t = jnp.where(no_prev, NEG_INF, s_s[g, cs] + bias_ref[g, cs])
        m = jnp.maximum(jnp.max(s_t, axis=0, keepdims=True), sinks[g][:, cs * sub:(cs + 1) * sub])
        p_s[g, cs] = jnp.exp2(s_t - m).astype(BF16)
        return m

    def pv(g, c, ms):
        p = jnp.concatenate([p_s[g, c * per + k] for k in range(per)], axis=1)
        o_t = _dot(v_augs[g], p)
        den = o_t[HEAD_DIM:HEAD_DIM + 1] + jnp.exp2(sinks[g][:, c * chunk:(c + 1) * chunk] - jnp.concatenate(ms, axis=1))
        return o_t[:HEAD_DIM] / den

    for c in range(nch):
        qk(0, c)
    ms = [[], []]
    outs = [[], []]
    for c in range(nch):
        qk(1, c)
        ms[0] += [softmax(0, c * per + k) for k in range(per)]
    for c in range(nch):
        outs[0].append(pv(0, c, ms[0][c * per:(c + 1) * per]))
        ms[1] += [softmax(1, c * per + k) for k in range(per)]
    for c in range(nch):
        outs[1].append(pv(1, c, ms[1][c * per:(c + 1) * per]))
    tiles = []
    for g in range(N_KV):
        o_t = jnp.concatenate(outs[g], axis=1)
        tiles.append(jnp.concatenate([o_t[:, h * w:(h + 1) * w] for h in range(GROUP)], axis=0).T)
    o_ref[...] = jnp.concatenate(tiles, axis=1).astype(o_ref.dtype)


def swa_prompt(q, k, v, sinks, rel_bias, batch, seq):
    w = SWA_WINDOW
    nb = seq // w
    rows = GROUP * w
    kidx = np.arange(2 * w)[:, None]
    qq = (np.arange(rows) % w)[None, :]
    dist = w + qq - kidx
    bm = _bucket_matrix(dist, (dist >= 0) & (dist < w))
    rbx = jnp.repeat(rel_bias.reshape(N_BUCKETS, N_KV, GROUP).transpose(1, 0, 2), w, axis=2)
    chunk, sub = 256, 128
    nsub = rows // sub
    bias = bias_table(bm, rbx, jnp.zeros((N_KV, 1, rows), F32), LOG2E)
    bias = bias.reshape(N_KV, 2 * w, nsub, sub).transpose(0, 2, 1, 3)
    sink_x = jnp.repeat(sinks.reshape(N_KV, 1, GROUP), w, axis=2)
    d = q.shape[1]
    cur = lambda b, n: (b * nb + n, 0)
    prev = lambda b, n: (jnp.maximum(b * nb + n - 1, 0), 0)
    return pl.pallas_call(
        functools.partial(_swa_prompt_kernel, chunk=chunk, sub=sub),
        grid=(batch, nb),
        in_specs=[pl.BlockSpec((w, d), cur),
                  pl.BlockSpec((w, KV_W), cur), pl.BlockSpec((w, KV_W), prev),
                  pl.BlockSpec((w, KV_W), cur), pl.BlockSpec((w, KV_W), prev),
                  pl.BlockSpec((N_KV, nsub, 2 * w, sub), lambda b, n: (0, 0, 0, 0)),
                  pl.BlockSpec((N_KV, 1, rows), lambda b, n: (0, 0, 0))],
        out_specs=pl.BlockSpec((w, d), cur),
        out_shape=jax.ShapeDtypeStruct(q.shape, BF16),
        scratch_shapes=[pltpu.VMEM((N_KV, nsub, 2 * w, sub), F32),
                        pltpu.VMEM((N_KV, nsub, 2 * w, sub), BF16)],
        compiler_params=_params("arbitrary", "arbitrary"),
        name="swa_prompt",
    )(q, k, k, v, v, bias, sink_x)


def _swa_sample_kernel(q_ref, kb_ref, kn_ref, vb_ref, vn_ref, bbuf_ref, bnew_ref, sink_ref, o_ref, kb_out, vb_out,
                       *, t_new):
    ns, _, wb = kb_ref.shape
    sink = sink_ref[...]
    keep = lax.broadcasted_iota(jnp.int32, (KV_W, wb), 1) < wb - t_new

    def shifted(buf_t, new_rows):
        return jnp.where(keep, pltpu.roll(buf_t, wb - t_new, axis=1), pltpu.roll(new_rows.T, wb - t_new, axis=1))

    for i in range(ns):
        rs = slice(i * t_new, (i + 1) * t_new)
        qbd = _heads_to_rows(q_ref[rs, :] * SCALE).astype(BF16)
        kb_t, vb_t = kb_ref[i], vb_ref[i]
        kn, vn = _pad_rows(kn_ref[rs, :], wb), _pad_rows(vn_ref[rs, :], wb)
        s_buf = _dot(qbd, kb_t.astype(BF16)) + bbuf_ref[...]
        s_new = _dot_nt(qbd, kn.astype(BF16)) + bnew_ref[...]
        m = jnp.maximum(jnp.maximum(jnp.max(s_buf, axis=1, keepdims=True), jnp.max(s_new, axis=1, keepdims=True)),
                        sink)
        p_buf, p_new = jnp.exp(s_buf - m), jnp.exp(s_new - m)
        den = jnp.sum(p_buf, axis=1, keepdims=True) + jnp.sum(p_new, axis=1, keepdims=True) + jnp.exp(sink - m)
        acc = _dot_nt(p_buf.astype(BF16), vb_t.astype(BF16)) + _dot(p_new.astype(BF16), vn.astype(BF16))
        o_ref[rs, :] = _rows_to_heads(acc / den, t_new)
        kb_out[i] = shifted(kb_t, kn)
        vb_out[i] = shifted(vb_t, vn)


def swa_sample(q, k_buf_t, v_buf_t, k_new, v_new, sinks, rel_bias, t_new, ns=8):
    n, _, wb = k_buf_t.shape
    assert wb % LANES == 0 and wb >= t_new, "the window buffer fills whole lane tiles"
    rows = N_HEADS * t_new
    tok = (np.arange(rows) % t_new)[None, :]
    dist_buf = tok + wb - np.arange(wb)[:, None]
    j_new = np.arange(wb)[:, None]
    dist_new = tok - j_new
    bm_buf = _bucket_matrix(dist_buf, (dist_buf >= 0) & (dist_buf < SWA_WINDOW))
    bm_new = _bucket_matrix(dist_new, (dist_new >= 0) & (dist_new < SWA_WINDOW) & (j_new < t_new))
    rbx = jnp.repeat(rel_bias, t_new, axis=1)[None]
    zero = jnp.zeros((1, 1, rows), F32)
    b_buf = bias_table(bm_buf, rbx, zero)[0].T
    b_new = bias_table(bm_new, rbx, zero)[0].T
    sink_r = jnp.repeat(sinks, t_new).reshape(rows, 1)
    d = q.shape[1]
    tile = lambda width: pl.BlockSpec((ns * t_new, width), lambda i: (i, 0))
    buf = pl.BlockSpec((ns, KV_W, wb), lambda i: (i, 0, 0))
    table = pl.BlockSpec((rows, wb), lambda i: (0, 0))
    buf_shape = jax.ShapeDtypeStruct(k_buf_t.shape, F32)
    return pl.pallas_call(
        functools.partial(_swa_sample_kernel, t_new=t_new),
        grid=(n // ns,),
        in_specs=[tile(d), buf, tile(KV_W), buf, tile(KV_W), table, table,
                  pl.BlockSpec((rows, 1), lambda i: (0, 0))],
        out_specs=[tile(d), buf, buf],
        out_shape=[jax.ShapeDtypeStruct(q.shape, F32), buf_shape, buf_shape],
        compiler_params=_params("arbitrary"),
        name="swa_sample",
    )(q, k_buf_t, k_new, v_buf_t, v_new, b_buf, b_new, sink_r)


def kernel(x_prompt, x_sample, c_prompt, c_sample, cache_a_k, cache_a_v, page_table, cache_b_conv, cache_c_k, cache_c_v, rel_bias, norm_a, mod_w_a, mod_b_a, w_in_a, conv_w_b, conv_b_b, ln_g_b, ln_b_b, w_out_a, norm_c, mod_w_c, mod_b_c, w_in_c, sinks_c, w_out_c, final_norm):
    batch, seq, d = x_prompt.shape
    n, t_new, _ = x_sample.shape
    n_pool, page = cache_a_k.shape[1], cache_a_k.shape[2]
    hq = N_HEADS * HEAD_DIM
    cb = conv_w_b.shape[2]
    assert norm_a.shape[0] == 1 and norm_c.shape[0] == 1, "one A/B layer followed by one C layer"

    tm_p = 1024
    tm_s = min(256, n * t_new)
    xp = x_prompt.reshape(batch * seq, d)
    xs = x_sample

    c_all = jnp.concatenate([c_prompt, c_sample], axis=0)
    c_rows = -(-c_all.shape[0] // 8) * 8
    c_all = jnp.pad(c_all, ((0, c_rows - c_all.shape[0]), (0, 0)))

    def split_mod(m):
        mp = m[:batch].reshape(batch, 1, 3, d)
        ms = m[batch:batch + n].reshape(n, 1, 3, d)
        return [(mp[:, :, j], ms[:, :, j]) for j in range(3)]

    (sh_a, sc_a, gt_a) = split_mod(modulation(c_all, mod_w_a[0], mod_b_a[0]))
    (sh_c, sc_c, gt_c) = split_mod(modulation(c_all, mod_w_c[0], mod_b_c[0]))

    kv0, kv1 = hq, hq + 2 * KV_W
    za0 = kv1
    ga0 = za0 + hq
    gb0 = ga0 + cb
    zb0 = gb0 + cb
    segs_a = (("raw", 0, hq), ("raw", kv0, kv0 + KV_W), ("raw", kv0 + KV_W, kv1),
              ("silu", za0, ga0), ("glu", ga0, gb0, gb0, zb0), ("silu", zb0, zb0 + cb))
    dts_a = (F32, F32, F32, BF16, F32, BF16)
    w_in_a16 = w_in_a[0].astype(BF16)
    w_out_a16 = w_out_a[0].astype(BF16)
    k_pool = cache_a_k[0].transpose(0, 2, 3, 1).reshape(n_pool, KV_W, page)
    v_pool = cache_a_v[0].transpose(0, 2, 3, 1).reshape(n_pool, KV_W, page)

    segs_ap = segs_a + (("kv_rows", kv0, kv0 + KV_W), ("kv_rows", kv0 + KV_W, kv1))
    q, k, v, sza, u, szb, k_rows, v_rows = ln_inproj(xp, sh_a[0], sc_a[0], norm_a[0], w_in_a16, segs_ap,
                                                     dts_a + (F32, F32), tm_p)
    oa = moba_prompt(q, k.reshape(batch, seq, KV_W), v.reshape(batch, seq, KV_W), rel_bias, batch, seq)
    ob = conv_prompt(u, szb, conv_w_b[0], conv_b_b[0], ln_g_b[0], ln_b_b[0], batch, seq)
    xp1 = out_proj([(oa, sza, w_out_a16[:hq])], [(ob, w_out_a16[hq:])], xp, gt_a[0], None, tm_p)
    ak_p = k_rows.reshape(1, batch, seq, N_KV, KV_W)[..., :HEAD_DIM]
    av_p = v_rows.reshape(1, batch, seq, N_KV, KV_W)[..., :HEAD_DIM]
    bc_p = u.reshape(batch, seq, cb)[:, seq - (CONV_W - 1):]

    q, k, v, sza, u, szb = ln_inproj(xs, sh_a[1], sc_a[1], norm_a[0], w_in_a16, segs_a, dts_a, tm_s)
    oa = moba_sample(q, k, v, k_pool, v_pool, page_table, rel_bias, t_new)
    to_time_major = lambda a: a.reshape(n, t_new, cb).transpose(1, 0, 2)
    ob_t, hist_t = conv_sample(cache_b_conv[0].transpose(1, 0, 2), to_time_major(u), to_time_major(szb),
                               conv_w_b[0], conv_b_b[0], ln_g_b[0], ln_b_b[0])
    ob = ob_t.transpose(1, 0, 2).reshape(n * t_new, cb).astype(BF16)
    xs1 = out_proj([(oa, sza, w_out_a16[:hq])], [(ob, w_out_a16[hq:])], xs, gt_a[1], None, tm_s)
    ak_s, av_s = k, v
    bc_s = hist_t.transpose(1, 0, 2)

    segs_c = (("raw", 0, hq), ("raw", hq, hq + KV_W), ("raw", hq + KV_W, hq + 2 * KV_W),
              ("silu", hq + 2 * KV_W, 2 * hq + 2 * KV_W))
    dts_c = (F32, F32, F32, BF16)
    w_in_c16 = w_in_c[0].astype(BF16)
    w_out_c16 = w_out_c[0].astype(BF16)

    q, k, v, sz = ln_inproj(xp1, sh_c[0], sc_c[0], norm_c[0], w_in_c16, segs_c, dts_c, tm_p)
    o = swa_prompt(q, k, v, sinks_c[0], rel_bias, batch, seq)
    y_prompt = out_proj([(o, sz, w_out_c16)], [], xp1, gt_c[0], final_norm, tm_p)
    wb_p = min(SWA_WINDOW, seq)
    ck_p = k.reshape(batch, seq, KV_W)[:, seq - wb_p:]
    cv_p = v.reshape(batch, seq, KV_W)[:, seq - wb_p:]

    q, k, v, sz = ln_inproj(xs1, sh_c[1], sc_c[1], norm_c[0], w_in_c16, segs_c, dts_c, tm_s)
    wb_s = cache_c_k.shape[2]
    kb_t = cache_c_k[0].transpose(0, 2, 3, 1).reshape(n, KV_W, wb_s)
    vb_t = cache_c_v[0].transpose(0, 2, 3, 1).reshape(n, KV_W, wb_s)
    o, kb_t, vb_t = swa_sample(q, kb_t, vb_t, k, v, sinks_c[0], rel_bias, t_new)
    y_sample = out_proj([(o, sz, w_out_c16)], [], xs1, gt_c[1], final_norm, tm_s)
    ck_s = kb_t.reshape(n, N_KV, HEAD_DIM, wb_s).transpose(0, 3, 1, 2)[None]
    cv_s = vb_t.reshape(n, N_KV, HEAD_DIM, wb_s).transpose(0, 3, 1, 2)[None]

    def kv5(a, lead):
        return a.reshape((1,) + lead + (N_KV, HEAD_DIM))

    return (y_prompt.reshape(batch, seq, d), y_sample,
            ak_p, av_p,
            kv5(ak_s, (n, t_new)), kv5(av_s, (n, t_new)),
            bc_p[None], bc_s[None],
            kv5(ck_p, (batch, wb_p)), kv5(cv_p, (batch, wb_p)),
            ck_s, cv_s)
```

```python
import functools
import math

import jax
import jax.numpy as jnp
import numpy as np
from jax import lax
from jax.experimental import pallas as pl
from jax.experimental.pallas import tpu as pltpu

F32 = jnp.float32
BF16 = jnp.bfloat16
NEG_INF = float("-inf")
MASK_NEG = -1e30

HEAD_DIM = 64
N_HEADS = 16
N_KV = 2
GROUP = N_HEADS // N_KV
KV_W = N_KV * HEAD_DIM
MOBA_BLOCK = 256
MOBA_TOPK = 3
CONV_W = 31
SWA_WINDOW = 128
N_BUCKETS = 32
MAX_DISTANCE = 128
EPS = 1e-6
SCALE = HEAD_DIM ** -0.5
LOG2E = math.log2(math.e)

SUBLANES = 8
LANES = 128
VMEM_LIMIT = 56 * 2**20


def _params(*sem):
    return pltpu.CompilerParams(dimension_semantics=sem, vmem_limit_bytes=VMEM_LIMIT)


def _silu(z):
    return z * jax.nn.sigmoid(z)


def _dot(a, b):
    return jnp.dot(a, b, preferred_element_type=F32)


def _dot_nt(a, b):
    return lax.dot_general(a, b, (((1,), (1,)), ((), ())), preferred_element_type=F32)


def _dot_f32(a, b):
    return jnp.dot(a, b, preferred_element_type=F32, precision=lax.Precision.HIGHEST)


def _t5_bucket_np(n):
    n = np.maximum(n, 0)
    max_exact = N_BUCKETS // 2
    nf = np.maximum(n, 1).astype(np.float32)
    large = max_exact + (np.log(nf / np.float32(max_exact)) / np.float32(math.log(MAX_DISTANCE / max_exact))
                         * np.float32(N_BUCKETS - max_exact)).astype(np.int32)
    large = np.minimum(large, N_BUCKETS - 1)
    return np.where(n < max_exact, n, large).astype(np.int32)


def _bucket_matrix(dist, valid):
    return np.where(valid, _t5_bucket_np(dist), -1).astype(np.int32)


def _bias_kernel(bm_ref, rb_ref, sub_ref, o_ref, *, mul):
    bm = bm_ref[...]
    acc = jnp.zeros(bm.shape, F32)
    for b in range(N_BUCKETS):
        acc = jnp.where(bm == b, rb_ref[b:b + 1, :], acc)
    o_ref[...] = jnp.where(bm < 0, NEG_INF, (acc - sub_ref[...]) * mul)


def bias_table(bm, rbx, sub, mul=1.0):
    g, _, c = rbx.shape
    r = bm.shape[0]
    return pl.pallas_call(
        functools.partial(_bias_kernel, mul=mul),
        grid=(g,),
        in_specs=[pl.BlockSpec((r, c), lambda i: (0, 0)),
                  pl.BlockSpec((None, N_BUCKETS, c), lambda i: (i, 0, 0)),
                  pl.BlockSpec((None, 1, c), lambda i: (i, 0, 0))],
        out_specs=pl.BlockSpec((None, r, c), lambda i: (i, 0, 0)),
        out_shape=jax.ShapeDtypeStruct((g, r, c), F32),
        compiler_params=_params("arbitrary"),
        name="bias_table",
    )(jnp.asarray(bm), rbx, sub)


def _mod_kernel(c_ref, w_ref, b_ref, o_ref):
    o_ref[...] = _dot_f32(_silu(c_ref[...]), w_ref[...]) + b_ref[...]


def modulation(c, w, b):
    n, d = c.shape
    m = w.shape[1]
    tn = 512
    return pl.pallas_call(
        _mod_kernel,
        grid=(m // tn,),
        in_specs=[pl.BlockSpec((n, d), lambda j: (0, 0)),
                  pl.BlockSpec((d, tn), lambda j: (0, j)),
                  pl.BlockSpec((1, tn), lambda j: (0, j))],
        out_specs=pl.BlockSpec((n, tn), lambda j: (0, j)),
        out_shape=jax.ShapeDtypeStruct((n, m), F32),
        compiler_params=_params("arbitrary"),
        name="modulation",
    )(c, w, b.reshape(1, m))


def _ln_inproj_kernel(x_ref, shift_ref, scale_ref, g_ref, w_ref, *out_refs, segs):
    x = x_ref[...]
    y = x * lax.rsqrt(jnp.mean(x * x, axis=-1, keepdims=True) + EPS)
    h = (y * g_ref[...]) * (1.0 + scale_ref[...]) + shift_ref[...]
    h16 = h.reshape(-1, h.shape[-1]).astype(BF16)
    raw = {}
    for o_ref, seg in zip(out_refs, segs, strict=True):
        kind, lo, hi = seg[0], seg[1], seg[2]
        z = raw[lo, hi] if (lo, hi) in raw else _dot(h16, w_ref[:, lo:hi])
        raw[lo, hi] = z
        if kind == "kv_rows":
            tm = z.shape[0]
            o_ref[pl.ds(0, tm, stride=N_KV), :] = z
            o_ref[pl.ds(1, tm, stride=N_KV), :] = pltpu.roll(z, HEAD_DIM, axis=1)
            continue
        if kind == "silu":
            z = _silu(z)
        elif kind == "glu":
            z = z * jax.nn.sigmoid(_dot(h16, w_ref[:, seg[3]:seg[4]]))
        o_ref[...] = z.astype(o_ref.dtype)


def ln_inproj(x, shift, scale, norm_g, w16, segs, out_dtypes, tm):
    d = x.shape[-1]
    if x.ndim == 2:
        r = x.shape[0]
        nt = r // tm
        per_group = nt // shift.shape[0]
        x_spec = pl.BlockSpec((tm, d), lambda i: (i, 0))
        mod_spec = pl.BlockSpec((None, 1, d), lambda i: (i // per_group, 0, 0))
    else:
        n, t_new, _ = x.shape
        r = n * t_new
        nt = r // tm
        x_spec = pl.BlockSpec((tm // t_new, t_new, d), lambda i: (i, 0, 0))
        mod_spec = pl.BlockSpec((tm // t_new, 1, d), lambda i: (i, 0, 0))
    mult = [N_KV if s[0] == "kv_rows" else 1 for s in segs]
    out_shape = [jax.ShapeDtypeStruct((m * r, s[2] - s[1]), dt) for m, s, dt in zip(mult, segs, out_dtypes, strict=True)]
    out_specs = [pl.BlockSpec((m * tm, s[2] - s[1]), lambda i: (i, 0)) for m, s in zip(mult, segs)]
    return pl.pallas_call(
        functools.partial(_ln_inproj_kernel, segs=segs),
        grid=(nt,),
        in_specs=[x_spec, mod_spec, mod_spec,
                  pl.BlockSpec((1, d), lambda i: (0, 0)),
                  pl.BlockSpec(w16.shape, lambda i: (0, 0), pipeline_mode=pl.Buffered(1))],
        out_specs=out_specs,
        out_shape=out_shape,
        compiler_params=_params("arbitrary"),
        name="ln_inproj",
    )(x, shift, scale, norm_g.reshape(1, d), w16)


def _out_proj_kernel(*refs, n_gated, n_plain, final_norm):
    it = iter(refs)
    y = None
    for _ in range(n_gated):
        a_ref, m_ref, w_ref = next(it), next(it), next(it)
        t = _dot((a_ref[...] * m_ref[...].astype(F32)).astype(BF16), w_ref[...])
        y = t if y is None else y + t
    for _ in range(n_plain):
        a_ref, w_ref = next(it), next(it)
        t = _dot(a_ref[...], w_ref[...])
        y = t if y is None else y + t
    x_ref, gate_ref = next(it), next(it)
    xn = x_ref[...] + gate_ref[...] * y.reshape(x_ref.shape)
    if final_norm:
        fg_ref = next(it)
        xn = xn * lax.rsqrt(jnp.mean(xn * xn, axis=-1, keepdims=True) + EPS) * fg_ref[...]
    o_ref = next(it)
    o_ref[...] = xn


def out_proj(gated, plain, x, gate, final_g, tm):
    d = x.shape[-1]
    if x.ndim == 2:
        nt = x.shape[0] // tm
        per_group = nt // gate.shape[0]
        x_spec = pl.BlockSpec((tm, d), lambda i: (i, 0))
        gate_spec = pl.BlockSpec((None, 1, d), lambda i: (i // per_group, 0, 0))
    else:
        n, t_new, _ = x.shape
        nt = n * t_new // tm
        x_spec = pl.BlockSpec((tm // t_new, t_new, d), lambda i: (i, 0, 0))
        gate_spec = pl.BlockSpec((tm // t_new, 1, d), lambda i: (i, 0, 0))
    args, specs = [], []
    for a, m, w in gated:
        args += [a, m, w]
        specs += [pl.BlockSpec((tm, a.shape[1]), lambda i: (i, 0)),
                  pl.BlockSpec((tm, m.shape[1]), lambda i: (i, 0)),
                  pl.BlockSpec(w.shape, lambda i: (0, 0))]
    for a, w in plain:
        args += [a, w]
        specs += [pl.BlockSpec((tm, a.shape[1]), lambda i: (i, 0)),
                  pl.BlockSpec(w.shape, lambda i: (0, 0))]
    args += [x, gate]
    specs += [x_spec, gate_spec]
    if final_g is not None:
        args.append(final_g.reshape(1, d))
        specs.append(pl.BlockSpec((1, d), lambda i: (0, 0)))
    return pl.pallas_call(
        functools.partial(_out_proj_kernel, n_gated=len(gated), n_plain=len(plain),
                          final_norm=final_g is not None),
        grid=(nt,),
        in_specs=specs,
        out_specs=x_spec,
        out_shape=jax.ShapeDtypeStruct(x.shape, F32),
        compiler_params=_params("arbitrary"),
        name="out_proj",
    )(*args)


def _top3_mask(scores, n_valid):
    nb = scores.shape[0]
    blk = lax.broadcasted_iota(jnp.int32, scores.shape, 0)
    s = jnp.where(blk < n_valid, scores, NEG_INF)
    picked = jnp.zeros(scores.shape, F32)
    for _ in range(MOBA_TOPK):
        mx = jnp.max(s, axis=0, keepdims=True)
        first = jnp.min(jnp.where(s == mx, blk, nb), axis=0, keepdims=True)
        hit = blk == first
        picked = jnp.where(hit, 1.0, picked)
        s = jnp.where(hit, NEG_INF, s)
    return jnp.where(blk < n_valid, picked, 0.0)


V_AUG = HEAD_DIM + 16


def _moba_prompt_kernel(q_ref, k_ref, v_ref, bown_ref, badj_ref, o_ref,
                        k16_s, vT_s, kbar_s, qT16_s, sel_s, m_s, acc_s,
                        s0_s, s1_s, p0_s, p1_s, a0_s, a1_s, *, nblk, chunk, sub):
    g = pl.program_id(1)
    i = pl.program_id(2)
    rows = GROUP * MOBA_BLOCK
    nch = rows // chunk

    @pl.when((g == 0) & (i == 0))
    def _per_batch():
        ones = jnp.ones((V_AUG - HEAD_DIM, MOBA_BLOCK), BF16)
        for j in range(nblk):
            kb = k_ref[j * MOBA_BLOCK:(j + 1) * MOBA_BLOCK, :]
            k16_s[j * MOBA_BLOCK:(j + 1) * MOBA_BLOCK, :] = kb.astype(BF16)
            kbar_s[j:j + 1, :] = jnp.mean(kb, axis=0, keepdims=True)
            v_t = v_ref[j * MOBA_BLOCK:(j + 1) * MOBA_BLOCK, :].T.astype(BF16)
            for kv in range(N_KV):
                vT_s[kv, j] = jnp.concatenate([v_t[kv * HEAD_DIM:(kv + 1) * HEAD_DIM], ones], axis=0)

    qT = (q_ref[...] * (SCALE * LOG2E)).T
    qs = jnp.concatenate([qT[h * HEAD_DIM:(h + 1) * HEAD_DIM, :] for h in range(GROUP)], axis=1)
    qs16 = qs.astype(BF16)
    zeros = jnp.zeros_like(qs16)
    qpad16 = jnp.where(g == 0, jnp.concatenate([qs16, zeros], axis=0), jnp.concatenate([zeros, qs16], axis=0))
    for c in range(nch):
        qT16_s[c] = qpad16[:, c * chunk:(c + 1) * chunk]
    kbar = kbar_s[...]
    kbar_g = jnp.where(g == 0, kbar[:, :HEAD_DIM], kbar[:, HEAD_DIM:])
    sel_s[...] = _top3_mask(_dot_f32(kbar_g, qs), i)
    m_s[...] = jnp.full(m_s.shape, NEG_INF, F32)
    acc_s[...] = jnp.zeros(acc_s.shape, F32)

    n_far = i - 1
    sbuf, pbuf, abuf = (s0_s, s1_s), (p0_s, p1_s), (a0_s, a1_s)

    def blk_of(t):
        far_j = jnp.clip(t - 2, 0, jnp.maximum(n_far - 1, 0))
        return jnp.where(t == 0, i, jnp.where(t == 1, jnp.maximum(i - 1, 0), far_j))

    per = chunk // sub

    def stage_qk(t, slot, c):
        kb = k16_s[pl.ds(pl.multiple_of(blk_of(t) * MOBA_BLOCK, MOBA_BLOCK), MOBA_BLOCK), :]
        s_t = _dot(kb, qT16_s[c])
        for k in range(per):
            sbuf[slot][c * per + k] = s_t[:, k * sub:(k + 1) * sub]

    def stage_softmax(slot, c, on, bias_ref):
        ln = slice(c * sub, (c + 1) * sub)
        s_t = sbuf[slot][c]
        if bias_ref is not None:
            s_t = s_t + bias_ref[c]
        m_old = m_s[:, ln]
        m_new = jnp.maximum(m_old, jnp.where(on, jnp.max(s_t, axis=0, keepdims=True), NEG_INF))
        pbuf[slot][c] = jnp.exp2(s_t - jnp.where(on, m_new, jnp.inf)).astype(BF16)
        abuf[slot][:, ln] = jnp.exp2(m_old - m_new)
        m_s[:, ln] = m_new

    def stage_pv(t, slot, c):
        p = jnp.concatenate([pbuf[slot][c * per + k] for k in range(per)], axis=1)
        acc_s[c] = acc_s[c] * abuf[slot][:, c * chunk:(c + 1) * chunk] + _dot(vT_s[g, blk_of(t)], p)

    def pipe_half(t, slot, bias_ref=None, pv=True, qk=True):
        own = bias_ref is bown_ref
        valid = None if own else jnp.where(t == 1, i >= 1, t - 2 < n_far)
        j = blk_of(t)
        for c in range(nch):
            if qk:
                stage_qk(t + 1, 1 - slot, c)
            if own:
                on = jnp.full((1, chunk), True)
            else:
                on = (sel_s[pl.ds(j, 1), c * chunk:(c + 1) * chunk] > 0.0) & valid
            for k in range(per):
                stage_softmax(slot, c * per + k, on[:, k * sub:(k + 1) * sub], bias_ref)
            if pv:
                stage_pv(t - 1, 1 - slot, c)

    for c in range(nch):
        stage_qk(0, 0, c)
    pipe_half(0, 0, bown_ref, pv=False)
    pipe_half(1, 1, badj_ref)

    def pair(u, carry):
        pipe_half(2 + 2 * u, 0)
        pipe_half(3 + 2 * u, 1)
        return carry

    n_pairs = jnp.maximum(n_far, 0) // 2
    lax.fori_loop(0, n_pairs, pair, 0)
    t_tail = 2 + 2 * n_pairs
    odd = jnp.maximum(n_far, 0) % 2 == 1

    @pl.when(odd)
    def _tail():
        pipe_half(t_tail, 0, qk=False)
        for c in range(nch):
            stage_pv(t_tail, 0, c)

    @pl.when(jnp.logical_not(odd))
    def _drain():
        for c in range(nch):
            stage_pv(t_tail - 1, 1, c)

    acc = jnp.concatenate([acc_s[c] for c in range(nch)], axis=1)
    o_t = acc[:HEAD_DIM] / acc[HEAD_DIM:HEAD_DIM + 1]
    o_cat = jnp.concatenate([o_t[:, h * MOBA_BLOCK:(h + 1) * MOBA_BLOCK] for h in range(GROUP)], axis=0)
    o_ref[...] = o_cat.T.astype(o_ref.dtype)


def moba_prompt(q, k, v, rel_bias, batch, seq):
    nblk = seq // MOBA_BLOCK
    rows = GROUP * MOBA_BLOCK
    kk = np.arange(MOBA_BLOCK)[:, None]
    qq = (np.arange(rows) % MOBA_BLOCK)[None, :]
    bm_own = _bucket_matrix(qq - kk, qq >= kk)
    bm_adj = _bucket_matrix(MOBA_BLOCK + qq - kk, np.ones((MOBA_BLOCK, rows), bool))
    rbx = jnp.repeat(rel_bias.reshape(N_BUCKETS, N_KV, GROUP).transpose(1, 0, 2), MOBA_BLOCK, axis=2)
    far = rbx[:, N_BUCKETS - 1:N_BUCKETS, :]
    chunk, sub = 256, 128
    nsub = rows // sub

    def chunk_major(tbl):
        return tbl.reshape(N_KV, MOBA_BLOCK, nsub, sub).transpose(0, 2, 1, 3)

    b_own = chunk_major(bias_table(bm_own, rbx, far, LOG2E))
    b_adj = chunk_major(bias_table(bm_adj, rbx, far, LOG2E))
    qw = GROUP * HEAD_DIM
    tbl_spec = pl.BlockSpec((None, nsub, MOBA_BLOCK, sub), lambda b, g, i: (g, 0, 0, 0))
    return pl.pallas_call(
        functools.partial(_moba_prompt_kernel, nblk=nblk, chunk=chunk, sub=sub),
        grid=(batch, N_KV, nblk),
        in_specs=[pl.BlockSpec((MOBA_BLOCK, qw), lambda b, g, i: (b * nblk + i, g)),
                  pl.BlockSpec((None, seq, KV_W), lambda b, g, i: (b, 0, 0)),
                  pl.BlockSpec((None, seq, KV_W), lambda b, g, i: (b, 0, 0)),
                  tbl_spec, tbl_spec],
        out_specs=pl.BlockSpec((MOBA_BLOCK, qw), lambda b, g, i: (b * nblk + i, g)),
        out_shape=jax.ShapeDtypeStruct(q.shape, BF16),
        scratch_shapes=[pltpu.VMEM((seq, KV_W), BF16),
                        pltpu.VMEM((N_KV, nblk, V_AUG, MOBA_BLOCK), BF16),
                        pltpu.VMEM((nblk, KV_W), F32),
                        pltpu.VMEM((rows // chunk, KV_W, chunk), BF16),
                        pltpu.VMEM((nblk, rows), F32),
                        pltpu.VMEM((1, rows), F32),
                        pltpu.VMEM((rows // chunk, V_AUG, chunk), F32),
                        pltpu.VMEM((nsub, MOBA_BLOCK, sub), F32), pltpu.VMEM((nsub, MOBA_BLOCK, sub), F32),
                        pltpu.VMEM((nsub, MOBA_BLOCK, sub), BF16), pltpu.VMEM((nsub, MOBA_BLOCK, sub), BF16),
                        pltpu.VMEM((1, rows), F32), pltpu.VMEM((1, rows), F32)],
        compiler_params=_params("arbitrary", "arbitrary", "arbitrary"),
        name="moba_prompt",
    )(q, k, v, b_own, b_adj)


def _heads_to_rows(q):
    t = q.shape[0]
    lane = lax.broadcasted_iota(jnp.int32, (t, LANES), 1)
    pieces = []
    for h in range(N_HEADS):
        src = q[:, (h // 2) * LANES:(h // 2 + 1) * LANES]
        g = h // GROUP
        if h % 2 != g:
            src = pltpu.roll(src, HEAD_DIM, axis=1)
        keep = lane < HEAD_DIM if g == 0 else lane >= HEAD_DIM
        pieces.append(jnp.where(keep, src, 0.0))
    return jnp.concatenate(pieces, axis=0)


def _rows_to_heads(acc, t):
    lane = lax.broadcasted_iota(jnp.int32, (t, LANES), 1)
    cols = []
    for k in range(N_HEADS // 2):
        a = acc[2 * k * t:(2 * k + 1) * t, :]
        b = acc[(2 * k + 1) * t:(2 * k + 2) * t, :]
        if (2 * k) // GROUP == 1:
            a = pltpu.roll(a, HEAD_DIM, axis=1)
        else:
            b = pltpu.roll(b, HEAD_DIM, axis=1)
        cols.append(jnp.where(lane < HEAD_DIM, a, b))
    return jnp.concatenate(cols, axis=1)


def _pad_rows(x, rows):
    return jnp.concatenate([x, jnp.zeros((rows - x.shape[0], x.shape[1]), x.dtype)], axis=0)


def _moba_sample_kernel(pt_ref, q_ref, kn_ref, vn_ref, blast_ref, bown_ref, e_ref, kpool, vpool, o_ref,
                        kbuf, vbuf, ksem, vsem, s_s, *, n_pages, page, n_samples, unroll):
    s = pl.program_id(0)
    slot = s % 2
    ppb = MOBA_BLOCK // page
    nblk = n_pages // ppb
    t_new = q_ref.shape[0]
    rows = N_HEADS * t_new

    def k_copy(smp, sl, p):
        return pltpu.make_async_copy(kpool.at[pt_ref[smp, p]], kbuf.at[sl, p], ksem.at[sl])

    def v_copy(smp, sl, p):
        return pltpu.make_async_copy(vpool.at[pt_ref[smp, p]], vbuf.at[sl, p], vsem.at[sl])

    def start_all(smp, sl):
        def body(p, c):
            k_copy(smp, sl, p).start()
            v_copy(smp, sl, p).start()
            return c
        lax.fori_loop(0, n_pages, body, 0, unroll=8)

    def wait_all(copy, smp, sl):
        def body(p, c):
            copy(smp, sl, p).wait()
            return c
        lax.fori_loop(0, n_pages, body, 0, unroll=8)

    def block_t(buf, j):
        return jnp.concatenate([buf[slot, j * ppb + t] for t in range(ppb)], axis=1)

    blk_lane = lax.broadcasted_iota(jnp.int32, (KV_W, nblk), 1)

    @pl.when(s == 0)
    def _first():
        start_all(0, 0)

    wait_all(k_copy, s, slot)
    nxt = jnp.minimum(s + 1, n_samples - 1)

    def kbar_body(j, kbar_t):
        for t in range(ppb):
            k_copy(nxt, 1 - slot, j * ppb + t).start()
            v_copy(nxt, 1 - slot, j * ppb + t).start()
        pages = kbuf[slot, j * ppb]
        for t in range(1, ppb):
            pages = pages + kbuf[slot, j * ppb + t]
        col = jnp.sum(pages, axis=1, keepdims=True) * (1.0 / MOBA_BLOCK)
        return jnp.where(blk_lane == j, col, kbar_t)

    kbar_t = lax.fori_loop(0, nblk, kbar_body, jnp.zeros((KV_W, nblk), F32), unroll=unroll[0])

    qbd = _heads_to_rows(q_ref[...] * (SCALE * LOG2E))
    qbd16 = qbd.astype(BF16)

    scores = _dot_f32(qbd, kbar_t)
    sel_t = _top3_mask(scores.T, nblk)
    negm = jnp.where(sel_t.T > 0.0, 0.0, MASK_NEG)
    lhs16 = jnp.concatenate([qbd, negm, jnp.zeros((rows, e_ref.shape[1] - nblk), F32)], axis=1).astype(BF16)

    def logits(j):
        rhs16 = jnp.concatenate([block_t(kbuf, j).astype(BF16), e_ref[j]], axis=0)
        return _dot(lhs16, rhs16)

    def fold(x):
        return x[:, :LANES], x[:, LANES:]

    def far(j, mrun):
        st = logits(j)
        s_s[j] = st
        lo, hi = fold(st)
        return jnp.maximum(mrun, jnp.maximum(lo, hi))

    mrun = lax.fori_loop(0, nblk - 1, far, jnp.full((rows, LANES), NEG_INF, F32), unroll=unroll[1])
    s_last = logits(nblk - 1) + blast_ref[...]
    s_s[nblk - 1] = s_last
    kn16 = _pad_rows(kn_ref[...], LANES).astype(BF16)
    s_own = _dot_nt(qbd16, kn16) + bown_ref[...]
    for part in fold(s_last) + (s_own,):
        mrun = jnp.maximum(mrun, part)
    m = jnp.max(mrun, axis=1, keepdims=True)

    wait_all(v_copy, s, slot)

    def pv(j, carry):
        acc, lsum = carry
        p = jnp.exp2(s_s[j] - m)
        lo, hi = fold(p)
        return acc + _dot_nt(p.astype(BF16), block_t(vbuf, j).astype(BF16)), lsum + (lo + hi)

    p_own = jnp.exp2(s_own - m)
    acc0 = _dot(p_own.astype(BF16), _pad_rows(vn_ref[...], LANES).astype(BF16))
    acc, lsum = lax.fori_loop(0, nblk, pv, (acc0, p_own), unroll=unroll[2])
    den = jnp.sum(lsum, axis=1, keepdims=True)
    o_ref[...] = _rows_to_heads(acc / den, t_new)

    @pl.when(s == n_samples - 1)
    def _drain():
        wait_all(k_copy, nxt, 1 - slot)
        wait_all(v_copy, nxt, 1 - slot)


def moba_sample(q, k_new, v_new, k_pool, v_pool, page_table, rel_bias, t_new):
    n, n_pages = page_table.shape
    page = k_pool.shape[2]
    past = n_pages * page
    nblk = past // MOBA_BLOCK
    rows = N_HEADS * t_new
    tok = (np.arange(rows) % t_new)[None, :]
    kk = np.arange(MOBA_BLOCK)[:, None]
    bm_last = _bucket_matrix(MOBA_BLOCK + tok - kk, np.ones((MOBA_BLOCK, rows), bool))
    ko = np.arange(LANES)[:, None]
    bm_own = _bucket_matrix(tok - ko, (ko <= tok) & (ko < t_new))
    rbx = jnp.repeat(rel_bias, t_new, axis=1)[None]
    far = rbx[:, N_BUCKETS - 1:N_BUCKETS, :]
    b_last = bias_table(bm_last, rbx, far, LOG2E)[0].T
    b_own = bias_table(bm_own, rbx, far, LOG2E)[0].T
    onehot_np = np.zeros((nblk, KV_W, MOBA_BLOCK), np.float32)
    onehot_np[np.arange(nblk), np.arange(nblk)] = 1.0
    onehot = jnp.asarray(onehot_np, BF16)

    grid_spec = pltpu.PrefetchScalarGridSpec(
        num_scalar_prefetch=1,
        grid=(n,),
        in_specs=[pl.BlockSpec((t_new, q.shape[1]), lambda s, pt: (s, 0)),
                  pl.BlockSpec((t_new, KV_W), lambda s, pt: (s, 0)),
                  pl.BlockSpec((t_new, KV_W), lambda s, pt: (s, 0)),
                  pl.BlockSpec((rows, MOBA_BLOCK), lambda s, pt: (0, 0)),
                  pl.BlockSpec((rows, LANES), lambda s, pt: (0, 0)),
                  pl.BlockSpec((nblk, KV_W, MOBA_BLOCK), lambda s, pt: (0, 0, 0)),
                  pl.BlockSpec(memory_space=pl.ANY),
                  pl.BlockSpec(memory_space=pl.ANY)],
        out_specs=pl.BlockSpec((t_new, q.shape[1]), lambda s, pt: (s, 0)),
        scratch_shapes=[pltpu.VMEM((2, n_pages, KV_W, page), F32),
                        pltpu.VMEM((2, n_pages, KV_W, page), F32),
                        pltpu.SemaphoreType.DMA((2,)),
                        pltpu.SemaphoreType.DMA((2,)),
                        pltpu.VMEM((nblk, rows, MOBA_BLOCK), F32)],
    )
    return pl.pallas_call(
        functools.partial(_moba_sample_kernel, n_pages=n_pages, page=page, n_samples=n, unroll=(32, 63, 32)),
        grid_spec=grid_spec,
        out_shape=jax.ShapeDtypeStruct(q.shape, F32),
        compiler_params=_params("arbitrary"),
        name="moba_sample",
    )(page_table, q, k_new, v_new, b_last, b_own, onehot, k_pool, v_pool)


def _conv_tail(y, cb_ref, lg_ref, lb_ref, gate):
    y = y + cb_ref[...]
    mu = jnp.mean(y, axis=-1, keepdims=True)
    var = jnp.mean(jnp.square(y - mu), axis=-1, keepdims=True)
    yn = (y - mu) * lax.rsqrt(var + EPS) * lg_ref[...] + lb_ref[...]
    return _silu(yn) * gate


HALO = 32


CONV_ROWS = 64


def _conv_prompt_kernel(u_ref, prev_ref, g_ref, w_ref, cb_ref, lg_ref, lb_ref, o_ref, xs, xr, ys, *, tl):
    t = pl.program_id(1)
    c = u_ref.shape[1]
    xs[0:HALO, :] = jnp.where(t > 0, prev_ref[...], 0.0)
    xs[HALO:, :] = u_ref[...]
    span = tl + HALO - SUBLANES
    for r in range(1, SUBLANES):
        xr[r - 1] = xs[pl.ds(r, span), :]
    off = HALO - (CONV_W - 1)

    def taps(base, lt):
        ln = slice(lt * LANES, (lt + 1) * LANES)
        acc = jnp.zeros((CONV_ROWS, LANES), F32)
        for k in range(CONV_W):
            a, r = divmod(off + k, SUBLANES)
            src = xs if r == 0 else xr.at[r - 1]
            start = base + a * SUBLANES
            acc = acc + src[start:start + CONV_ROWS, ln] * w_ref[k:k + 1, ln]
        ys[base:base + CONV_ROWS, ln] = acc

    for base in range(0, tl, CONV_ROWS):
        for lt in range(0, c // LANES, 2):
            @pl.when(t >= 0)
            def _():
                taps(base, lt)
                taps(base, lt + 1)
    o_ref[...] = _conv_tail(ys[...], cb_ref, lg_ref, lb_ref, g_ref[...].astype(F32)).astype(o_ref.dtype)


def conv_prompt(u, szb, conv_w, conv_b, ln_g, ln_b, batch, seq, tl=256):
    c = u.shape[1]
    nt = seq // tl
    vec = pl.BlockSpec((1, c), lambda b, t: (0, 0))
    return pl.pallas_call(
        functools.partial(_conv_prompt_kernel, tl=tl),
        grid=(batch, nt),
        in_specs=[pl.BlockSpec((tl, c), lambda b, t: (b * nt + t, 0)),
                  pl.BlockSpec((HALO, c), lambda b, t: (jnp.maximum((b * nt + t) * (tl // HALO) - 1, 0), 0)),
                  pl.BlockSpec((tl, c), lambda b, t: (b * nt + t, 0)),
                  pl.BlockSpec((CONV_W, c), lambda b, t: (0, 0)), vec, vec, vec],
        out_specs=pl.BlockSpec((tl, c), lambda b, t: (b * nt + t, 0)),
        out_shape=jax.ShapeDtypeStruct(u.shape, BF16),
        scratch_shapes=[pltpu.VMEM((HALO + tl, c), F32),
                        pltpu.VMEM((SUBLANES - 1, HALO + tl - SUBLANES, c), F32),
                        pltpu.VMEM((tl, c), F32)],
        compiler_params=_params("arbitrary", "arbitrary"),
        name="conv_prompt",
    )(u, u, szb, conv_w, conv_b.reshape(1, c), ln_g.reshape(1, c), ln_b.reshape(1, c))


def _conv_sample_kernel(h_ref, u_ref, g_ref, w_ref, cb_ref, lg_ref, lb_ref, o_ref, hn_ref, *, t_new):
    n_hist = h_ref.shape[0]
    c = h_ref.shape[2]

    def row(r):
        return h_ref.at[r] if r < n_hist else u_ref.at[r - n_hist]

    for t in range(t_new):
        for lt in range(c // LANES):
            ln = slice(lt * LANES, (lt + 1) * LANES)
            acc = row(t)[:, ln] * w_ref[0:1, ln]
            for k in range(1, CONV_W):
                acc = acc + row(t + k)[:, ln] * w_ref[k:k + 1, ln]
            o_ref[t, :, ln] = acc
    for t in range(t_new):
        o_ref[t] = _conv_tail(o_ref[t], cb_ref, lg_ref, lb_ref, g_ref[t].astype(F32))
    hn_ref[0:n_hist - t_new] = h_ref[t_new:n_hist]
    hn_ref[n_hist - t_new:n_hist] = u_ref[...]


def conv_sample(hist_t, u_t, szb_t, conv_w, conv_b, ln_g, ln_b, ns=16):
    n_hist, n, c = hist_t.shape
    t_new = u_t.shape[0]
    vec = pl.BlockSpec((1, c), lambda i: (0, 0))
    hist = pl.BlockSpec((n_hist, ns, c), lambda i: (0, i, 0))
    new = pl.BlockSpec((t_new, ns, c), lambda i: (0, i, 0))
    return pl.pallas_call(
        functools.partial(_conv_sample_kernel, t_new=t_new),
        grid=(n // ns,),
        in_specs=[hist, new, new, pl.BlockSpec((CONV_W, c), lambda i: (0, 0)), vec, vec, vec],
        out_specs=[new, hist],
        out_shape=[jax.ShapeDtypeStruct(u_t.shape, F32), jax.ShapeDtypeStruct(hist_t.shape, F32)],
        compiler_params=_params("arbitrary"),
        name="conv_sample",
    )(hist_t, u_t, szb_t, conv_w, conv_b.reshape(1, c), ln_g.reshape(1, c), ln_b.reshape(1, c))


def _swa_prompt_kernel(q_ref, kc_ref, kp_ref, vc_ref, vp_ref, bias_ref, sink_ref, o_ref, s_s, p_s, *, chunk, sub):
    n = pl.program_id(1)
    w = SWA_WINDOW
    rows = GROUP * w
    nch, per = rows // chunk, chunk // sub
    gw = GROUP * HEAD_DIM
    kcat = jnp.concatenate([kp_ref[...], kc_ref[...]], axis=0).astype(BF16)
    v_t = jnp.concatenate([vp_ref[...], vc_ref[...]], axis=0).T.astype(BF16)
    ones = jnp.ones((V_AUG - HEAD_DIM, 2 * w), BF16)
    no_prev = (lax.broadcasted_iota(jnp.int32, (2 * w, sub), 0) < w) & (n == 0)
    qpads, v_augs, sinks = [], [], []
    for g in range(N_KV):
        q_t = (q_ref[:, g * gw:(g + 1) * gw] * (SCALE * LOG2E)).T
        qs = jnp.concatenate([q_t[h * HEAD_DIM:(h + 1) * HEAD_DIM, :] for h in range(GROUP)], axis=1)
        zeros = jnp.zeros_like(qs)
        qpads.append(jnp.concatenate([qs, zeros] if g == 0 else [zeros, qs], axis=0).astype(BF16))
        v_augs.append(jnp.concatenate([v_t[g * HEAD_DIM:(g + 1) * HEAD_DIM], ones], axis=0))
        sinks.append(sink_ref[g] * LOG2E)

    def qk(g, c):
        s_t = _dot(kcat, qpads[g][:, c * chunk:(c + 1) * chunk])
        for k in range(per):
            s_s[g, c * per + k] = s_t[:, k * sub:(k + 1) * sub]

    def softmax(g, cs):
        s_t = jnp.where(no_prev, NEG_INF, s_s[g, cs] + bias_ref[g, cs])
        m = jnp.maximum(jnp.max(s_t, axis=0, keepdims=True), sinks[g][:, cs * sub:(cs + 1) * sub])
        p_s[g, cs] = jnp.exp2(s_t - m).astype(BF16)
        return m

    def pv(g, c, ms):
        p = jnp.concatenate([p_s[g, c * per + k] for k in range(per)], axis=1)
        o_t = _dot(v_augs[g], p)
        den = o_t[HEAD_DIM:HEAD_DIM + 1] + jnp.exp2(sinks[g][:, c * chunk:(c + 1) * chunk] - jnp.concatenate(ms, axis=1))
        return o_t[:HEAD_DIM] / den

    for c in range(nch):
        qk(0, c)
    ms = [[], []]
    outs = [[], []]
    for c in range(nch):
        qk(1, c)
        ms[0] += [softmax(0, c * per + k) for k in range(per)]
    for c in range(nch):
        outs[0].append(pv(0, c, ms[0][c * per:(c + 1) * per]))
        ms[1] += [softmax(1, c * per + k) for k in range(per)]
    for c in range(nch):
        outs[1].append(pv(1, c, ms[1][c * per:(c + 1) * per]))
    tiles = []
    for g in range(N_KV):
        o_t = jnp.concatenate(outs[g], axis=1)
        tiles.append(jnp.concatenate([o_t[:, h * w:(h + 1) * w] for h in range(GROUP)], axis=0).T)
    o_ref[...] = jnp.concatenate(tiles, axis=1).astype(o_ref.dtype)


def swa_prompt(q, k, v, sinks, rel_bias, batch, seq):
    w = SWA_WINDOW
    nb = seq // w
    rows = GROUP * w
    kidx = np.arange(2 * w)[:, None]
    qq = (np.arange(rows) % w)[None, :]
    dist = w + qq - kidx
    bm = _bucket_matrix(dist, (dist >= 0) & (dist < w))
    rbx = jnp.repeat(rel_bias.reshape(N_BUCKETS, N_KV, GROUP).transpose(1, 0, 2), w, axis=2)
    chunk, sub = 256, 128
    nsub = rows // sub
    bias = bias_table(bm, rbx, jnp.zeros((N_KV, 1, rows), F32), LOG2E)
    bias = bias.reshape(N_KV, 2 * w, nsub, sub).transpose(0, 2, 1, 3)
    sink_x = jnp.repeat(sinks.reshape(N_KV, 1, GROUP), w, axis=2)
    d = q.shape[1]
    cur = lambda b, n: (b * nb + n, 0)
    prev = lambda b, n: (jnp.maximum(b * nb + n - 1, 0), 0)
    return pl.pallas_call(
        functools.partial(_swa_prompt_kernel, chunk=chunk, sub=sub),
        grid=(batch, nb),
        in_specs=[pl.BlockSpec((w, d), cur),
                  pl.BlockSpec((w, KV_W), cur), pl.BlockSpec((w, KV_W), prev),
                  pl.BlockSpec((w, KV_W), cur), pl.BlockSpec((w, KV_W), prev),
                  pl.BlockSpec((N_KV, nsub, 2 * w, sub), lambda b, n: (0, 0, 0, 0)),
                  pl.BlockSpec((N_KV, 1, rows), lambda b, n: (0, 0, 0))],
        out_specs=pl.BlockSpec((w, d), cur),
        out_shape=jax.ShapeDtypeStruct(q.shape, BF16),
        scratch_shapes=[pltpu.VMEM((N_KV, nsub, 2 * w, sub), F32),
                        pltpu.VMEM((N_KV, nsub, 2 * w, sub), BF16)],
        compiler_params=_params("arbitrary", "arbitrary"),
        name="swa_prompt",
    )(q, k, k, v, v, bias, sink_x)


def _swa_sample_kernel(q_ref, kb_ref, kn_ref, vb_ref, vn_ref, bbuf_ref, bnew_ref, sink_ref, o_ref, kb_out, vb_out,
                       *, t_new):
    ns, _, wb = kb_ref.shape
    sink = sink_ref[...]
    keep = lax.broadcasted_iota(jnp.int32, (KV_W, wb), 1) < wb - t_new

    def shifted(buf_t, new_rows):
        return jnp.where(keep, pltpu.roll(buf_t, wb - t_new, axis=1), pltpu.roll(new_rows.T, wb - t_new, axis=1))

    rs = [slice(i * t_new, (i + 1) * t_new) for i in range(ns)]
    kn = [_pad_rows(kn_ref[r, :], wb) for r in rs]
    vn = [_pad_rows(vn_ref[r, :], wb) for r in rs]
    qbd = [_heads_to_rows(q_ref[r, :] * SCALE).astype(BF16) for r in rs]
    s_buf = [_dot(qbd[i], kb_ref[i].astype(BF16)) + bbuf_ref[...] for i in range(ns)]
    s_new = [_dot_nt(qbd[i], kn[i].astype(BF16)) + bnew_ref[...] for i in range(ns)]
    p_buf, p_new, den = [], [], []
    for i in range(ns):
        m = jnp.maximum(jnp.maximum(jnp.max(s_buf[i], axis=1, keepdims=True),
                                    jnp.max(s_new[i], axis=1, keepdims=True)), sink)
        p_buf.append(jnp.exp(s_buf[i] - m))
        p_new.append(jnp.exp(s_new[i] - m))
        den.append(jnp.sum(p_buf[i], axis=1, keepdims=True) + jnp.sum(p_new[i], axis=1, keepdims=True)
                   + jnp.exp(sink - m))
    for i in range(ns):
        acc = (_dot_nt(p_buf[i].astype(BF16), vb_ref[i].astype(BF16))
               + _dot(p_new[i].astype(BF16), vn[i].astype(BF16)))
        o_ref[rs[i], :] = _rows_to_heads(acc / den[i], t_new)
        kb_out[i] = shifted(kb_ref[i], kn[i])
        vb_out[i] = shifted(vb_ref[i], vn[i])


def swa_sample(q, k_buf_t, v_buf_t, k_new, v_new, sinks, rel_bias, t_new, ns=8):
    n, _, wb = k_buf_t.shape
    assert wb % LANES == 0 and wb >= t_new, "the window buffer fills whole lane tiles"
    rows = N_HEADS * t_new
    tok = (np.arange(rows) % t_new)[None, :]
    dist_buf = tok + wb - np.arange(wb)[:, None]
    j_new = np.arange(wb)[:, None]
    dist_new = tok - j_new
    bm_buf = _bucket_matrix(dist_buf, (dist_buf >= 0) & (dist_buf < SWA_WINDOW))
    bm_new = _bucket_matrix(dist_new, (dist_new >= 0) & (dist_new < SWA_WINDOW) & (j_new < t_new))
    rbx = jnp.repeat(rel_bias, t_new, axis=1)[None]
    zero = jnp.zeros((1, 1, rows), F32)
    b_buf = bias_table(bm_buf, rbx, zero)[0].T
    b_new = bias_table(bm_new, rbx, zero)[0].T
    sink_r = jnp.repeat(sinks, t_new).reshape(rows, 1)
    d = q.shape[1]
    tile = lambda width: pl.BlockSpec((ns * t_new, width), lambda i: (i, 0))
    buf = pl.BlockSpec((ns, KV_W, wb), lambda i: (i, 0, 0))
    table = pl.BlockSpec((rows, wb), lambda i: (0, 0))
    buf_shape = jax.ShapeDtypeStruct(k_buf_t.shape, F32)
    return pl.pallas_call(
        functools.partial(_swa_sample_kernel, t_new=t_new),
        grid=(n // ns,),
        in_specs=[tile(d), buf, tile(KV_W), buf, tile(KV_W), table, table,
                  pl.BlockSpec((rows, 1), lambda i: (0, 0))],
        out_specs=[tile(d), buf, buf],
        out_shape=[jax.ShapeDtypeStruct(q.shape, F32), buf_shape, buf_shape],
        compiler_params=_params("arbitrary"),
        name="swa_sample",
    )(q, k_buf_t, k_new, v_buf_t, v_new, b_buf, b_new, sink_r)


def kernel(x_prompt, x_sample, c_prompt, c_sample, cache_a_k, cache_a_v, page_table, cache_b_conv, cache_c_k, cache_c_v, rel_bias, norm_a, mod_w_a, mod_b_a, w_in_a, conv_w_b, conv_b_b, ln_g_b, ln_b_b, w_out_a, norm_c, mod_w_c, mod_b_c, w_in_c, sinks_c, w_out_c, final_norm):
    batch, seq, d = x_prompt.shape
    n, t_new, _ = x_sample.shape
    n_pool, page = cache_a_k.shape[1], cache_a_k.shape[2]
    hq = N_HEADS * HEAD_DIM
    cb = conv_w_b.shape[2]
    assert norm_a.shape[0] == 1 and norm_c.shape[0] == 1, "one A/B layer followed by one C layer"

    tm_p = 1024
    tm_s = min(256, n * t_new)
    xp = x_prompt.reshape(batch * seq, d)
    xs = x_sample

    c_all = jnp.concatenate([c_prompt, c_sample], axis=0)
    c_rows = -(-c_all.shape[0] // 8) * 8
    c_all = jnp.pad(c_all, ((0, c_rows - c_all.shape[0]), (0, 0)))

    def split_mod(m):
        mp = m[:batch].reshape(batch, 1, 3, d)
        ms = m[batch:batch + n].reshape(n, 1, 3, d)
        return [(mp[:, :, j], ms[:, :, j]) for j in range(3)]

    (sh_a, sc_a, gt_a) = split_mod(modulation(c_all, mod_w_a[0], mod_b_a[0]))
    (sh_c, sc_c, gt_c) = split_mod(modulation(c_all, mod_w_c[0], mod_b_c[0]))

    kv0, kv1 = hq, hq + 2 * KV_W
    za0 = kv1
    ga0 = za0 + hq
    gb0 = ga0 + cb
    zb0 = gb0 + cb
    segs_a = (("raw", 0, hq), ("raw", kv0, kv0 + KV_W), ("raw", kv0 + KV_W, kv1),
              ("silu", za0, ga0), ("glu", ga0, gb0, gb0, zb0), ("silu", zb0, zb0 + cb))
    dts_a = (F32, F32, F32, BF16, F32, BF16)
    w_in_a16 = w_in_a[0].astype(BF16)
    w_out_a16 = w_out_a[0].astype(BF16)
    k_pool = cache_a_k[0].transpose(0, 2, 3, 1).reshape(n_pool, KV_W, page)
    v_pool = cache_a_v[0].transpose(0, 2, 3, 1).reshape(n_pool, KV_W, page)

    segs_ap = segs_a + (("kv_rows", kv0, kv0 + KV_W), ("kv_rows", kv0 + KV_W, kv1))
    q, k, v, sza, u, szb, k_rows, v_rows = ln_inproj(xp, sh_a[0], sc_a[0], norm_a[0], w_in_a16, segs_ap,
                                                     dts_a + (F32, F32), tm_p)
    oa = moba_prompt(q, k.reshape(batch, seq, KV_W), v.reshape(batch, seq, KV_W), rel_bias, batch, seq)
    ob = conv_prompt(u, szb, conv_w_b[0], conv_b_b[0], ln_g_b[0], ln_b_b[0], batch, seq)
    xp1 = out_proj([(oa, sza, w_out_a16[:hq])], [(ob, w_out_a16[hq:])], xp, gt_a[0], None, tm_p)
    ak_p = k_rows.reshape(1, batch, seq, N_KV, KV_W)[..., :HEAD_DIM]
    av_p = v_rows.reshape(1, batch, seq, N_KV, KV_W)[..., :HEAD_DIM]
    bc_p = u.reshape(batch, seq, cb)[:, seq - (CONV_W - 1):]

    q, k, v, sza, u, szb = ln_inproj(xs, sh_a[1], sc_a[1], norm_a[0], w_in_a16, segs_a, dts_a, tm_s)
    oa = moba_sample(q, k, v, k_pool, v_pool, page_table, rel_bias, t_new)
    to_time_major = lambda a: a.reshape(n, t_new, cb).transpose(1, 0, 2)
    ob_t, hist_t = conv_sample(cache_b_conv[0].transpose(1, 0, 2), to_time_major(u), to_time_major(szb),
                               conv_w_b[0], conv_b_b[0], ln_g_b[0], ln_b_b[0])
    ob = ob_t.transpose(1, 0, 2).reshape(n * t_new, cb).astype(BF16)
    xs1 = out_proj([(oa, sza, w_out_a16[:hq])], [(ob, w_out_a16[hq:])], xs, gt_a[1], None, tm_s)
    ak_s, av_s = k, v
    bc_s = hist_t.transpose(1, 0, 2)

    segs_c = (("raw", 0, hq), ("raw", hq, hq + KV_W), ("raw", hq + KV_W, hq + 2 * KV_W),
              ("silu", hq + 2 * KV_W, 2 * hq + 2 * KV_W))
    dts_c = (F32, F32, F32, BF16)
    w_in_c16 = w_in_c[0].astype(BF16)
    w_out_c16 = w_out_c[0].astype(BF16)

    q, k, v, sz = ln_inproj(xp1, sh_c[0], sc_c[0], norm_c[0], w_in_c16, segs_c, dts_c, tm_p)
    o = swa_prompt(q, k, v, sinks_c[0], rel_bias, batch, seq)
    y_prompt = out_proj([(o, sz, w_out_c16)], [], xp1, gt_c[0], final_norm, tm_p)
    wb_p = min(SWA_WINDOW, seq)
    ck_p = k.reshape(batch, seq, KV_W)[:, seq - wb_p:]
    cv_p = v.reshape(batch, seq, KV_W)[:, seq - wb_p:]

    q, k, v, sz = ln_inproj(xs1, sh_c[1], sc_c[1], norm_c[0], w_in_c16, segs_c, dts_c, tm_s)
    wb_s = cache_c_k.shape[2]
    kb_t = cache_c_k[0].transpose(0, 2, 3, 1).reshape(n, KV_W, wb_s)
    vb_t = cache_c_v[0].transpose(0, 2, 3, 1).reshape(n, KV_W, wb_s)
    o, kb_t, vb_t = swa_sample(q, kb_t, vb_t, k, v, sinks_c[0], rel_bias, t_new)
    y_sample = out_proj([(o, sz, w_out_c16)], [], xs1, gt_c[1], final_norm, tm_s)
    ck_s = kb_t.reshape(n, N_KV, HEAD_DIM, wb_s).transpose(0, 3, 1, 2)[None]
    cv_s = vb_t.reshape(n, N_KV, HEAD_DIM, wb_s).transpose(0, 3, 1, 2)[None]

    def kv5(a, lead):
        return a.reshape((1,) + lead + (N_KV, HEAD_DIM))

    return (y_prompt.reshape(batch, seq, d), y_sample,
            ak_p, av_p,
            kv5(ak_s, (n, t_new)), kv5(av_s, (n, t_new)),
            bc_p[None], bc_s[None],
            kv5(ck_p, (batch, wb_p)), kv5(cv_p, (batch, wb_p)),
            ck_s, cv_s)
```

```python
import functools
import math

import jax
import jax.numpy as jnp
import numpy as np
from jax import lax
from jax.experimental import pallas as pl
from jax.experimental.pallas import tpu as pltpu

F32 = jnp.float32
BF16 = jnp.bfloat16
NEG_INF = float("-inf")
MASK_NEG = -1e30

HEAD_DIM = 64
N_HEADS = 16
N_KV = 2
GROUP = N_HEADS // N_KV
KV_W = N_KV * HEAD_DIM
MOBA_BLOCK = 256
MOBA_TOPK = 3
CONV_W = 31
SWA_WINDOW = 128
N_BUCKETS = 32
MAX_DISTANCE = 128
EPS = 1e-6
SCALE = HEAD_DIM ** -0.5
LOG2E = math.log2(math.e)

SUBLANES = 8
LANES = 128
VMEM_LIMIT = 56 * 2**20


def _params(*sem):
    return pltpu.CompilerParams(dimension_semantics=sem, vmem_limit_bytes=VMEM_LIMIT)


def _silu(z):
    return z * jax.nn.sigmoid(z)


def _dot(a, b):
    return jnp.dot(a, b, preferred_element_type=F32)


def _dot_nt(a, b):
    return lax.dot_general(a, b, (((1,), (1,)), ((), ())), preferred_element_type=F32)


def _dot_f32(a, b):
    return jnp.dot(a, b, preferred_element_type=F32, precision=lax.Precision.HIGHEST)


def _t5_bucket_np(n):
    n = np.maximum(n, 0)
    max_exact = N_BUCKETS // 2
    nf = np.maximum(n, 1).astype(np.float32)
    large = max_exact + (np.log(nf / np.float32(max_exact)) / np.float32(math.log(MAX_DISTANCE / max_exact))
                         * np.float32(N_BUCKETS - max_exact)).astype(np.int32)
    large = np.minimum(large, N_BUCKETS - 1)
    return np.where(n < max_exact, n, large).astype(np.int32)


def _bucket_matrix(dist, valid):
    return np.where(valid, _t5_bucket_np(dist), -1).astype(np.int32)


def _bias_kernel(bm_ref, rb_ref, sub_ref, o_ref, *, mul):
    bm = bm_ref[...]
    acc = jnp.zeros(bm.shape, F32)
    for b in range(N_BUCKETS):
        acc = jnp.where(bm == b, rb_ref[b:b + 1, :], acc)
    o_ref[...] = jnp.where(bm < 0, NEG_INF, (acc - sub_ref[...]) * mul)


def bias_table(bm, rbx, sub, mul=1.0):
    g, _, c = rbx.shape
    r = bm.shape[0]
    return pl.pallas_call(
        functools.partial(_bias_kernel, mul=mul),
        grid=(g,),
        in_specs=[pl.BlockSpec((r, c), lambda i: (0, 0)),
                  pl.BlockSpec((None, N_BUCKETS, c), lambda i: (i, 0, 0)),
                  pl.BlockSpec((None, 1, c), lambda i: (i, 0, 0))],
        out_specs=pl.BlockSpec((None, r, c), lambda i: (i, 0, 0)),
        out_shape=jax.ShapeDtypeStruct((g, r, c), F32),
        compiler_params=_params("arbitrary"),
        name="bias_table",
    )(jnp.asarray(bm), rbx, sub)


def _mod_kernel(c_ref, w_ref, b_ref, o_ref):
    o_ref[...] = _dot_f32(_silu(c_ref[...]), w_ref[...]) + b_ref[...]


def modulation(c, w, b):
    n, d = c.shape
    m = w.shape[1]
    tn = 512
    return pl.pallas_call(
        _mod_kernel,
        grid=(m // tn,),
        in_specs=[pl.BlockSpec((n, d), lambda j: (0, 0)),
                  pl.BlockSpec((d, tn), lambda j: (0, j)),
                  pl.BlockSpec((1, tn), lambda j: (0, j))],
        out_specs=pl.BlockSpec((n, tn), lambda j: (0, j)),
        out_shape=jax.ShapeDtypeStruct((n, m), F32),
        compiler_params=_params("arbitrary"),
        name="modulation",
    )(c, w, b.reshape(1, m))


def _ln_inproj_kernel(x_ref, shift_ref, scale_ref, g_ref, w_ref, *out_refs, segs):
    x = x_ref[...]
    y = x * lax.rsqrt(jnp.mean(x * x, axis=-1, keepdims=True) + EPS)
    h = (y * g_ref[...]) * (1.0 + scale_ref[...]) + shift_ref[...]
    h16 = h.reshape(-1, h.shape[-1]).astype(BF16)
    raw = {}
    for o_ref, seg in zip(out_refs, segs, strict=True):
        kind, lo, hi = seg[0], seg[1], seg[2]
        z = raw[lo, hi] if (lo, hi) in raw else _dot(h16, w_ref[:, lo:hi])
        raw[lo, hi] = z
        if kind == "kv_rows":
            tm = z.shape[0]
            o_ref[pl.ds(0, tm, stride=N_KV), :] = z
            o_ref[pl.ds(1, tm, stride=N_KV), :] = pltpu.roll(z, HEAD_DIM, axis=1)
            continue
        if kind == "silu":
            z = _silu(z)
        elif kind == "glu":
            z = z * jax.nn.sigmoid(_dot(h16, w_ref[:, seg[3]:seg[4]]))
        o_ref[...] = z.astype(o_ref.dtype)


def ln_inproj(x, shift, scale, norm_g, w16, segs, out_dtypes, tm):
    d = x.shape[-1]
    if x.ndim == 2:
        r = x.shape[0]
        nt = r // tm
        per_group = nt // shift.shape[0]
        x_spec = pl.BlockSpec((tm, d), lambda i: (i, 0))
        mod_spec = pl.BlockSpec((None, 1, d), lambda i: (i // per_group, 0, 0))
    else:
        n, t_new, _ = x.shape
        r = n * t_new
        nt = r // tm
        x_spec = pl.BlockSpec((tm // t_new, t_new, d), lambda i: (i, 0, 0))
        mod_spec = pl.BlockSpec((tm // t_new, 1, d), lambda i: (i, 0, 0))
    mult = [N_KV if s[0] == "kv_rows" else 1 for s in segs]
    out_shape = [jax.ShapeDtypeStruct((m * r, s[2] - s[1]), dt) for m, s, dt in zip(mult, segs, out_dtypes, strict=True)]
    out_specs = [pl.BlockSpec((m * tm, s[2] - s[1]), lambda i: (i, 0)) for m, s in zip(mult, segs)]
    return pl.pallas_call(
        functools.partial(_ln_inproj_kernel, segs=segs),
        grid=(nt,),
        in_specs=[x_spec, mod_spec, mod_spec,
                  pl.BlockSpec((1, d), lambda i: (0, 0)),
                  pl.BlockSpec(w16.shape, lambda i: (0, 0), pipeline_mode=pl.Buffered(1))],
        out_specs=out_specs,
        out_shape=out_shape,
        compiler_params=_params("arbitrary"),
        name="ln_inproj",
    )(x, shift, scale, norm_g.reshape(1, d), w16)


def _out_proj_kernel(*refs, n_gated, n_plain, final_norm):
    it = iter(refs)
    y = None
    for _ in range(n_gated):
        a_ref, m_ref, w_ref = next(it), next(it), next(it)
        t = _dot((a_ref[...] * m_ref[...].astype(F32)).astype(BF16), w_ref[...])
        y = t if y is None else y + t
    for _ in range(n_plain):
        a_ref, w_ref = next(it), next(it)
        t = _dot(a_ref[...], w_ref[...])
        y = t if y is None else y + t
    x_ref, gate_ref = next(it), next(it)
    xn = x_ref[...] + gate_ref[...] * y.reshape(x_ref.shape)
    if final_norm:
        fg_ref = next(it)
        xn = xn * lax.rsqrt(jnp.mean(xn * xn, axis=-1, keepdims=True) + EPS) * fg_ref[...]
    o_ref = next(it)
    o_ref[...] = xn


def out_proj(gated, plain, x, gate, final_g, tm):
    d = x.shape[-1]
    if x.ndim == 2:
        nt = x.shape[0] // tm
        per_group = nt // gate.shape[0]
        x_spec = pl.BlockSpec((tm, d), lambda i: (i, 0))
        gate_spec = pl.BlockSpec((None, 1, d), lambda i: (i // per_group, 0, 0))
    else:
        n, t_new, _ = x.shape
        nt = n * t_new // tm
        x_spec = pl.BlockSpec((tm // t_new, t_new, d), lambda i: (i, 0, 0))
        gate_spec = pl.BlockSpec((tm // t_new, 1, d), lambda i: (i, 0, 0))
    args, specs = [], []
    for a, m, w in gated:
        args += [a, m, w]
        specs += [pl.BlockSpec((tm, a.shape[1]), lambda i: (i, 0)),
                  pl.BlockSpec((tm, m.shape[1]), lambda i: (i, 0)),
                  pl.BlockSpec(w.shape, lambda i: (0, 0))]
    for a, w in plain:
        args += [a, w]
        specs += [pl.BlockSpec((tm, a.shape[1]), lambda i: (i, 0)),
                  pl.BlockSpec(w.shape, lambda i: (0, 0))]
    args += [x, gate]
    specs += [x_spec, gate_spec]
    if final_g is not None:
        args.append(final_g.reshape(1, d))
        specs.append(pl.BlockSpec((1, d), lambda i: (0, 0)))
    return pl.pallas_call(
        functools.partial(_out_proj_kernel, n_gated=len(gated), n_plain=len(plain),
                          final_norm=final_g is not None),
        grid=(nt,),
        in_specs=specs,
        out_specs=x_spec,
        out_shape=jax.ShapeDtypeStruct(x.shape, F32),
        compiler_params=_params("arbitrary"),
        name="out_proj",
    )(*args)


def _top3_mask(scores, n_valid):
    nb = scores.shape[0]
    blk = lax.broadcasted_iota(jnp.int32, scores.shape, 0)
    s = jnp.where(blk < n_valid, scores, NEG_INF)
    picked = jnp.zeros(scores.shape, F32)
    for _ in range(MOBA_TOPK):
        mx = jnp.max(s, axis=0, keepdims=True)
        first = jnp.min(jnp.where(s == mx, blk, nb), axis=0, keepdims=True)
        hit = blk == first
        picked = jnp.where(hit, 1.0, picked)
        s = jnp.where(hit, NEG_INF, s)
    return jnp.where(blk < n_valid, picked, 0.0)


V_AUG = HEAD_DIM + 16


def _moba_prompt_kernel(q_ref, k_ref, v_ref, bown_ref, badj_ref, o_ref,
                        k16_s, vT_s, kbar_s, qT16_s, sel_s, m_s, acc_s,
                        s0_s, s1_s, p0_s, p1_s, a0_s, a1_s, *, nblk, chunk, sub):
    g = pl.program_id(1)
    i = pl.program_id(2)
    rows = GROUP * MOBA_BLOCK
    nch = rows // chunk

    @pl.when((g == 0) & (i == 0))
    def _per_batch():
        ones = jnp.ones((V_AUG - HEAD_DIM, MOBA_BLOCK), BF16)
        for j in range(nblk):
            kb = k_ref[j * MOBA_BLOCK:(j + 1) * MOBA_BLOCK, :]
            k16_s[j * MOBA_BLOCK:(j + 1) * MOBA_BLOCK, :] = kb.astype(BF16)
            kbar_s[j:j + 1, :] = jnp.mean(kb, axis=0, keepdims=True)
            v_t = v_ref[j * MOBA_BLOCK:(j + 1) * MOBA_BLOCK, :].T.astype(BF16)
            for kv in range(N_KV):
                vT_s[kv, j] = jnp.concatenate([v_t[kv * HEAD_DIM:(kv + 1) * HEAD_DIM], ones], axis=0)

    qT = (q_ref[...] * (SCALE * LOG2E)).T
    qs = jnp.concatenate([qT[h * HEAD_DIM:(h + 1) * HEAD_DIM, :] for h in range(GROUP)], axis=1)
    qs16 = qs.astype(BF16)
    zeros = jnp.zeros_like(qs16)
    qpad16 = jnp.where(g == 0, jnp.concatenate([qs16, zeros], axis=0), jnp.concatenate([zeros, qs16], axis=0))
    for c in range(nch):
        qT16_s[c] = qpad16[:, c * chunk:(c + 1) * chunk]
    kbar = kbar_s[...]
    kbar_g = jnp.where(g == 0, kbar[:, :HEAD_DIM], kbar[:, HEAD_DIM:])
    sel_s[...] = _top3_mask(_dot_f32(kbar_g, qs), i)
    m_s[...] = jnp.full(m_s.shape, NEG_INF, F32)
    acc_s[...] = jnp.zeros(acc_s.shape, F32)

    n_far = i - 1
    sbuf, pbuf, abuf = (s0_s, s1_s), (p0_s, p1_s), (a0_s, a1_s)

    def blk_of(t):
        far_j = jnp.clip(t - 2, 0, jnp.maximum(n_far - 1, 0))
        return jnp.where(t == 0, i, jnp.where(t == 1, jnp.maximum(i - 1, 0), far_j))

    per = chunk // sub

    def stage_qk(t, slot, c):
        kb = k16_s[pl.ds(pl.multiple_of(blk_of(t) * MOBA_BLOCK, MOBA_BLOCK), MOBA_BLOCK), :]
        s_t = _dot(kb, qT16_s[c])
        for k in range(per):
            sbuf[slot][c * per + k] = s_t[:, k * sub:(k + 1) * sub]

    def stage_softmax(slot, c, on, bias_ref):
        ln = slice(c * sub, (c + 1) * sub)
        s_t = sbuf[slot][c]
        if bias_ref is not None:
            s_t = s_t + bias_ref[c]
        m_old = m_s[:, ln]
        m_new = jnp.maximum(m_old, jnp.where(on, jnp.max(s_t, axis=0, keepdims=True), NEG_INF))
        pbuf[slot][c] = jnp.exp2(s_t - jnp.where(on, m_new, jnp.inf)).astype(BF16)
        abuf[slot][:, ln] = jnp.exp2(m_old - m_new)
        m_s[:, ln] = m_new

    def stage_pv(t, slot, c):
        p = jnp.concatenate([pbuf[slot][c * per + k] for k in range(per)], axis=1)
        acc_s[c] = acc_s[c] * abuf[slot][:, c * chunk:(c + 1) * chunk] + _dot(vT_s[g, blk_of(t)], p)

    def pipe_half(t, slot, bias_ref=None, pv=True, qk=True):
        own = bias_ref is bown_ref
        valid = None if own else jnp.where(t == 1, i >= 1, t - 2 < n_far)
        j = blk_of(t)
        for c in range(nch):
            if qk:
                stage_qk(t + 1, 1 - slot, c)
            if own:
                on = jnp.full((1, chunk), True)
            else:
                on = (sel_s[pl.ds(j, 1), c * chunk:(c + 1) * chunk] > 0.0) & valid
            for k in range(per):
                stage_softmax(slot, c * per + k, on[:, k * sub:(k + 1) * sub], bias_ref)
            if pv:
                stage_pv(t - 1, 1 - slot, c)

    for c in range(nch):
        stage_qk(0, 0, c)
    pipe_half(0, 0, bown_ref, pv=False)
    pipe_half(1, 1, badj_ref)

    def pair(u, carry):
        pipe_half(2 + 2 * u, 0)
        pipe_half(3 + 2 * u, 1)
        return carry

    n_pairs = jnp.maximum(n_far, 0) // 2
    lax.fori_loop(0, n_pairs, pair, 0)
    t_tail = 2 + 2 * n_pairs
    odd = jnp.maximum(n_far, 0) % 2 == 1

    @pl.when(odd)
    def _tail():
        pipe_half(t_tail, 0, qk=False)
        for c in range(nch):
            stage_pv(t_tail, 0, c)

    @pl.when(jnp.logical_not(odd))
    def _drain():
        for c in range(nch):
            stage_pv(t_tail - 1, 1, c)

    acc = jnp.concatenate([acc_s[c] for c in range(nch)], axis=1)
    o_t = acc[:HEAD_DIM] / acc[HEAD_DIM:HEAD_DIM + 1]
    o_cat = jnp.concatenate([o_t[:, h * MOBA_BLOCK:(h + 1) * MOBA_BLOCK] for h in range(GROUP)], axis=0)
    o_ref[...] = o_cat.T.astype(o_ref.dtype)


def moba_prompt(q, k, v, rel_bias, batch, seq):
    nblk = seq // MOBA_BLOCK
    rows = GROUP * MOBA_BLOCK
    kk = np.arange(MOBA_BLOCK)[:, None]
    qq = (np.arange(rows) % MOBA_BLOCK)[None, :]
    bm_own = _bucket_matrix(qq - kk, qq >= kk)
    bm_adj = _bucket_matrix(MOBA_BLOCK + qq - kk, np.ones((MOBA_BLOCK, rows), bool))
    rbx = jnp.repeat(rel_bias.reshape(N_BUCKETS, N_KV, GROUP).transpose(1, 0, 2), MOBA_BLOCK, axis=2)
    far = rbx[:, N_BUCKETS - 1:N_BUCKETS, :]
    chunk, sub = 256, 128
    nsub = rows // sub

    def chunk_major(tbl):
        return tbl.reshape(N_KV, MOBA_BLOCK, nsub, sub).transpose(0, 2, 1, 3)

    b_own = chunk_major(bias_table(bm_own, rbx, far, LOG2E))
    b_adj = chunk_major(bias_table(bm_adj, rbx, far, LOG2E))
    qw = GROUP * HEAD_DIM
    tbl_spec = pl.BlockSpec((None, nsub, MOBA_BLOCK, sub), lambda b, g, i: (g, 0, 0, 0))
    return pl.pallas_call(
        functools.partial(_moba_prompt_kernel, nblk=nblk, chunk=chunk, sub=sub),
        grid=(batch, N_KV, nblk),
        in_specs=[pl.BlockSpec((MOBA_BLOCK, qw), lambda b, g, i: (b * nblk + i, g)),
                  pl.BlockSpec((None, seq, KV_W), lambda b, g, i: (b, 0, 0)),
                  pl.BlockSpec((None, seq, KV_W), lambda b, g, i: (b, 0, 0)),
                  tbl_spec, tbl_spec],
        out_specs=pl.BlockSpec((MOBA_BLOCK, qw), lambda b, g, i: (b * nblk + i, g)),
        out_shape=jax.ShapeDtypeStruct(q.shape, BF16),
        scratch_shapes=[pltpu.VMEM((seq, KV_W), BF16),
                        pltpu.VMEM((N_KV, nblk, V_AUG, MOBA_BLOCK), BF16),
                        pltpu.VMEM((nblk, KV_W), F32),
                        pltpu.VMEM((rows // chunk, KV_W, chunk), BF16),
                        pltpu.VMEM((nblk, rows), F32),
                        pltpu.VMEM((1, rows), F32),
                        pltpu.VMEM((rows // chunk, V_AUG, chunk), F32),
                        pltpu.VMEM((nsub, MOBA_BLOCK, sub), F32), pltpu.VMEM((nsub, MOBA_BLOCK, sub), F32),
                        pltpu.VMEM((nsub, MOBA_BLOCK, sub), BF16), pltpu.VMEM((nsub, MOBA_BLOCK, sub), BF16),
                        pltpu.VMEM((1, rows), F32), pltpu.VMEM((1, rows), F32)],
        compiler_params=_params("arbitrary", "arbitrary", "arbitrary"),
        name="moba_prompt",
    )(q, k, v, b_own, b_adj)


def _heads_to_rows(q):
    t = q.shape[0]
    lane = lax.broadcasted_iota(jnp.int32, (t, LANES), 1)
    pieces = []
    for h in range(N_HEADS):
        src = q[:, (h // 2) * LANES:(h // 2 + 1) * LANES]
        g = h // GROUP
        if h % 2 != g:
            src = pltpu.roll(src, HEAD_DIM, axis=1)
        keep = lane < HEAD_DIM if g == 0 else lane >= HEAD_DIM
        pieces.append(jnp.where(keep, src, 0.0))
    return jnp.concatenate(pieces, axis=0)


def _rows_to_heads(acc, t):
    lane = lax.broadcasted_iota(jnp.int32, (t, LANES), 1)
    cols = []
    for k in range(N_HEADS // 2):
        a = acc[2 * k * t:(2 * k + 1) * t, :]
        b = acc[(2 * k + 1) * t:(2 * k + 2) * t, :]
        if (2 * k) // GROUP == 1:
            a = pltpu.roll(a, HEAD_DIM, axis=1)
        else:
            b = pltpu.roll(b, HEAD_DIM, axis=1)
        cols.append(jnp.where(lane < HEAD_DIM, a, b))
    return jnp.concatenate(cols, axis=1)


def _pad_rows(x, rows):
    return jnp.concatenate([x, jnp.zeros((rows - x.shape[0], x.shape[1]), x.dtype)], axis=0)


def _moba_sample_kernel(pt_ref, q_ref, kn_ref, vn_ref, blast_ref, bown_ref, e_ref, kpool, vpool, o_ref,
                        kbuf, vbuf, ksem, vsem, s_s, *, n_pages, page, n_samples, unroll):
    s = pl.program_id(0)
    slot = s % 2
    ppb = MOBA_BLOCK // page
    nblk = n_pages // ppb
    t_new = q_ref.shape[0]
    rows = N_HEADS * t_new

    def k_copy(smp, sl, p):
        return pltpu.make_async_copy(kpool.at[pt_ref[smp, p]], kbuf.at[sl, p], ksem.at[sl])

    def v_copy(smp, sl, p):
        return pltpu.make_async_copy(vpool.at[pt_ref[smp, p]], vbuf.at[sl, p], vsem.at[sl])

    def start_all(smp, sl):
        def body(p, c):
            k_copy(smp, sl, p).start()
            v_copy(smp, sl, p).start()
            return c
        lax.fori_loop(0, n_pages, body, 0, unroll=8)

    def wait_all(copy, smp, sl):
        def body(p, c):
            copy(smp, sl, p).wait()
            return c
        lax.fori_loop(0, n_pages, body, 0, unroll=8)

    def block_t(buf, j):
        return jnp.concatenate([buf[slot, j * ppb + t] for t in range(ppb)], axis=1)

    blk_lane = lax.broadcasted_iota(jnp.int32, (KV_W, nblk), 1)

    @pl.when(s == 0)
    def _first():
        start_all(0, 0)

    wait_all(k_copy, s, slot)
    nxt = jnp.minimum(s + 1, n_samples - 1)

    def kbar_body(j, kbar_t):
        for t in range(ppb):
            k_copy(nxt, 1 - slot, j * ppb + t).start()
            v_copy(nxt, 1 - slot, j * ppb + t).start()
        pages = kbuf[slot, j * ppb]
        for t in range(1, ppb):
            pages = pages + kbuf[slot, j * ppb + t]
        col = jnp.sum(pages, axis=1, keepdims=True) * (1.0 / MOBA_BLOCK)
        return jnp.where(blk_lane == j, col, kbar_t)

    kbar_t = lax.fori_loop(0, nblk, kbar_body, jnp.zeros((KV_W, nblk), F32), unroll=unroll[0])

    qbd = _heads_to_rows(q_ref[...] * (SCALE * LOG2E))
    qbd16 = qbd.astype(BF16)

    scores = _dot_f32(qbd, kbar_t)
    sel_t = _top3_mask(scores.T, nblk)
    negm = jnp.where(sel_t.T > 0.0, 0.0, MASK_NEG)
    lhs16 = jnp.concatenate([qbd, negm, jnp.zeros((rows, e_ref.shape[1] - nblk), F32)], axis=1).astype(BF16)

    def logits(j):
        rhs16 = jnp.concatenate([block_t(kbuf, j).astype(BF16), e_ref[j]], axis=0)
        return _dot(lhs16, rhs16)

    def fold(x):
        return x[:, :LANES], x[:, LANES:]

    def far(j, mrun):
        st = logits(j)
        s_s[j] = st
        lo, hi = fold(st)
        return jnp.maximum(mrun, jnp.maximum(lo, hi))

    mrun = lax.fori_loop(0, nblk - 1, far, jnp.full((rows, LANES), NEG_INF, F32), unroll=unroll[1])
    s_last = logits(nblk - 1) + blast_ref[...]
    s_s[nblk - 1] = s_last
    kn16 = _pad_rows(kn_ref[...], LANES).astype(BF16)
    s_own = _dot_nt(qbd16, kn16) + bown_ref[...]
    for part in fold(s_last) + (s_own,):
        mrun = jnp.maximum(mrun, part)
    m = jnp.max(mrun, axis=1, keepdims=True)

    wait_all(v_copy, s, slot)

    def pv(j, carry):
        acc, lsum = carry
        p = jnp.exp2(s_s[j] - m)
        lo, hi = fold(p)
        return acc + _dot_nt(p.astype(BF16), block_t(vbuf, j).astype(BF16)), lsum + (lo + hi)

    p_own = jnp.exp2(s_own - m)
    acc0 = _dot(p_own.astype(BF16), _pad_rows(vn_ref[...], LANES).astype(BF16))
    acc, lsum = lax.fori_loop(0, nblk, pv, (acc0, p_own), unroll=unroll[2])
    den = jnp.sum(lsum, axis=1, keepdims=True)
    o_ref[...] = _rows_to_heads(acc / den, t_new)

    @pl.when(s == n_samples - 1)
    def _drain():
        wait_all(k_copy, nxt, 1 - slot)
        wait_all(v_copy, nxt, 1 - slot)


def moba_sample(q, k_new, v_new, k_pool, v_pool, page_table, rel_bias, t_new):
    n, n_pages = page_table.shape
    page = k_pool.shape[2]
    past = n_pages * page
    nblk = past // MOBA_BLOCK
    rows = N_HEADS * t_new
    tok = (np.arange(rows) % t_new)[None, :]
    kk = np.arange(MOBA_BLOCK)[:, None]
    bm_last = _bucket_matrix(MOBA_BLOCK + tok - kk, np.ones((MOBA_BLOCK, rows), bool))
    ko = np.arange(LANES)[:, None]
    bm_own = _bucket_matrix(tok - ko, (ko <= tok) & (ko < t_new))
    rbx = jnp.repeat(rel_bias, t_new, axis=1)[None]
    far = rbx[:, N_BUCKETS - 1:N_BUCKETS, :]
    b_last = bias_table(bm_last, rbx, far, LOG2E)[0].T
    b_own = bias_table(bm_own, rbx, far, LOG2E)[0].T
    onehot_np = np.zeros((nblk, KV_W, MOBA_BLOCK), np.float32)
    onehot_np[np.arange(nblk), np.arange(nblk)] = 1.0
    onehot = jnp.asarray(onehot_np, BF16)

    grid_spec = pltpu.PrefetchScalarGridSpec(
        num_scalar_prefetch=1,
        grid=(n,),
        in_specs=[pl.BlockSpec((t_new, q.shape[1]), lambda s, pt: (s, 0)),
                  pl.BlockSpec((t_new, KV_W), lambda s, pt: (s, 0)),
                  pl.BlockSpec((t_new, KV_W), lambda s, pt: (s, 0)),
                  pl.BlockSpec((rows, MOBA_BLOCK), lambda s, pt: (0, 0)),
                  pl.BlockSpec((rows, LANES), lambda s, pt: (0, 0)),
                  pl.BlockSpec((nblk, KV_W, MOBA_BLOCK), lambda s, pt: (0, 0, 0)),
                  pl.BlockSpec(memory_space=pl.ANY),
                  pl.BlockSpec(memory_space=pl.ANY)],
        out_specs=pl.BlockSpec((t_new, q.shape[1]), lambda s, pt: (s, 0)),
        scratch_shapes=[pltpu.VMEM((2, n_pages, KV_W, page), F32),
                        pltpu.VMEM((2, n_pages, KV_W, page), F32),
                        pltpu.SemaphoreType.DMA((2,)),
                        pltpu.SemaphoreType.DMA((2,)),
                        pltpu.VMEM((nblk, rows, MOBA_BLOCK), F32)],
    )
    return pl.pallas_call(
        functools.partial(_moba_sample_kernel, n_pages=n_pages, page=page, n_samples=n, unroll=(32, 63, 32)),
        grid_spec=grid_spec,
        out_shape=jax.ShapeDtypeStruct(q.shape, F32),
        compiler_params=_params("arbitrary"),
        name="moba_sample",
    )(page_table, q, k_new, v_new, b_last, b_own, onehot, k_pool, v_pool)


def _conv_tail(y, cb_ref, lg_ref, lb_ref, gate):
    y = y + cb_ref[...]
    mu = jnp.mean(y, axis=-1, keepdims=True)
    var = jnp.mean(jnp.square(y - mu), axis=-1, keepdims=True)
    yn = (y - mu) * lax.rsqrt(var + EPS) * lg_ref[...] + lb_ref[...]
    return _silu(yn) * gate


HALO = 32


CONV_ROWS = 64


def _conv_prompt_kernel(u_ref, prev_ref, g_ref, w_ref, cb_ref, lg_ref, lb_ref, o_ref, xs, xr, ys, *, tl):
    t = pl.program_id(1)
    c = u_ref.shape[1]
    xs[0:HALO, :] = jnp.where(t > 0, prev_ref[...], 0.0)
    xs[HALO:, :] = u_ref[...]
    span = tl + HALO - SUBLANES
    for r in range(1, SUBLANES):
        xr[r - 1] = xs[pl.ds(r, span), :]
    off = HALO - (CONV_W - 1)

    def taps(base, lt):
        ln = slice(lt * LANES, (lt + 1) * LANES)
        acc = jnp.zeros((CONV_ROWS, LANES), F32)
        for k in range(CONV_W):
            a, r = divmod(off + k, SUBLANES)
            src = xs if r == 0 else xr.at[r - 1]
            start = base + a * SUBLANES
            acc = acc + src[start:start + CONV_ROWS, ln] * w_ref[k:k + 1, ln]
        ys[base:base + CONV_ROWS, ln] = acc

    for base in range(0, tl, CONV_ROWS):
        for lt in range(0, c // LANES, 2):
            @pl.when(t >= 0)
            def _():
                taps(base, lt)
                taps(base, lt + 1)
    o_ref[...] = _conv_tail(ys[...], cb_ref, lg_ref, lb_ref, g_ref[...].astype(F32)).astype(o_ref.dtype)


def conv_prompt(u, szb, conv_w, conv_b, ln_g, ln_b, batch, seq, tl=512):
    c = u.shape[1]
    nt = seq // tl
    vec = pl.BlockSpec((1, c), lambda b, t: (0, 0))
    return pl.pallas_call(
        functools.partial(_conv_prompt_kernel, tl=tl),
        grid=(batch, nt),
        in_specs=[pl.BlockSpec((tl, c), lambda b, t: (b * nt + t, 0)),
                  pl.BlockSpec((HALO, c), lambda b, t: (jnp.maximum((b * nt + t) * (tl // HALO) - 1, 0), 0)),
                  pl.BlockSpec((tl, c), lambda b, t: (b * nt + t, 0)),
                  pl.BlockSpec((CONV_W, c), lambda b, t: (0, 0)), vec, vec, vec],
        out_specs=pl.BlockSpec((tl, c), lambda b, t: (b * nt + t, 0)),
        out_shape=jax.ShapeDtypeStruct(u.shape, BF16),
        scratch_shapes=[pltpu.VMEM((HALO + tl, c), F32),
                        pltpu.VMEM((SUBLANES - 1, HALO + tl - SUBLANES, c), F32),
                        pltpu.VMEM((tl, c), F32)],
        compiler_params=_params("arbitrary", "arbitrary"),
        name="conv_prompt",
    )(u, u, szb, conv_w, conv_b.reshape(1, c), ln_g.reshape(1, c), ln_b.reshape(1, c))


def _conv_sample_kernel(h_ref, u_ref, g_ref, w_ref, cb_ref, lg_ref, lb_ref, o_ref, hn_ref, *, t_new):
    n_hist = h_ref.shape[0]
    c = h_ref.shape[2]

    def row(r):
        return h_ref.at[r] if r < n_hist else u_ref.at[r - n_hist]

    for t in range(t_new):
        for lt in range(c // LANES):
            ln = slice(lt * LANES, (lt + 1) * LANES)
            acc = row(t)[:, ln] * w_ref[0:1, ln]
            for k in range(1, CONV_W):
                acc = acc + row(t + k)[:, ln] * w_ref[k:k + 1, ln]
            o_ref[t, :, ln] = acc
    for t in range(t_new):
        o_ref[t] = _conv_tail(o_ref[t], cb_ref, lg_ref, lb_ref, g_ref[t].astype(F32))
    hn_ref[0:n_hist - t_new] = h_ref[t_new:n_hist]
    hn_ref[n_hist - t_new:n_hist] = u_ref[...]


def conv_sample(hist_t, u_t, szb_t, conv_w, conv_b, ln_g, ln_b, ns=16):
    n_hist, n, c = hist_t.shape
    t_new = u_t.shape[0]
    vec = pl.BlockSpec((1, c), lambda i: (0, 0))
    hist = pl.BlockSpec((n_hist, ns, c), lambda i: (0, i, 0))
    new = pl.BlockSpec((t_new, ns, c), lambda i: (0, i, 0))
    return pl.pallas_call(
        functools.partial(_conv_sample_kernel, t_new=t_new),
        grid=(n // ns,),
        in_specs=[hist, new, new, pl.BlockSpec((CONV_W, c), lambda i: (0, 0)), vec, vec, vec],
        out_specs=[new, hist],
        out_shape=[jax.ShapeDtypeStruct(u_t.shape, F32), jax.ShapeDtypeStruct(hist_t.shape, F32)],
        compiler_params=_params("arbitrary"),
        name="conv_sample",
    )(hist_t, u_t, szb_t, conv_w, conv_b.reshape(1, c), ln_g.reshape(1, c), ln_b.reshape(1, c))


def _swa_prompt_kernel(q_ref, kc_ref, kp_ref, vc_ref, vp_ref, bias_ref, sink_ref, o_ref, s_s, p_s, *, chunk, sub):
    n = pl.program_id(1)
    w = SWA_WINDOW
    rows = GROUP * w
    nch, per = rows // chunk, chunk // sub
    gw = GROUP * HEAD_DIM
    kcat = jnp.concatenate([kp_ref[...], kc_ref[...]], axis=0).astype(BF16)
    v_t = jnp.concatenate([vp_ref[...], vc_ref[...]], axis=0).T.astype(BF16)
    ones = jnp.ones((V_AUG - HEAD_DIM, 2 * w), BF16)
    no_prev = (lax.broadcasted_iota(jnp.int32, (2 * w, sub), 0) < w) & (n == 0)
    qpads, v_augs, sinks = [], [], []
    for g in range(N_KV):
        q_t = (q_ref[:, g * gw:(g + 1) * gw] * (SCALE * LOG2E)).T
        qs = jnp.concatenate([q_t[h * HEAD_DIM:(h + 1) * HEAD_DIM, :] for h in range(GROUP)], axis=1)
        zeros = jnp.zeros_like(qs)
        qpads.append(jnp.concatenate([qs, zeros] if g == 0 else [zeros, qs], axis=0).astype(BF16))
        v_augs.append(jnp.concatenate([v_t[g * HEAD_DIM:(g + 1) * HEAD_DIM], ones], axis=0))
        sinks.append(sink_ref[g] * LOG2E)

    def qk(g, c):
        s_t = _dot(kcat, qpads[g][:, c * chunk:(c + 1) * chunk])
        for k in range(per):
            s_s[g, c * per + k] = s_t[:, k * sub:(k + 1) * sub]

    def softmax(g, cs):
        s_t = jnp.where(no_prev, NEG_INF, s_s[g, cs] + bias_ref[g, cs])
        m = jnp.maximum(jnp.max(s_t, axis=0, keepdims=True), sinks[g][:, cs * sub:(cs + 1) * sub])
        p_s[g, cs] = jnp.exp2(s_t - m).astype(BF16)
        return m

    def pv(g, c, ms):
        p = jnp.concatenate([p_s[g, c * per + k] for k in range(per)], axis=1)
        o_t = _dot(v_augs[g], p)
        den = o_t[HEAD_DIM:HEAD_DIM + 1] + jnp.exp2(sinks[g][:, c * chunk:(c + 1) * chunk] - jnp.concatenate(ms, axis=1))
        return o_t[:HEAD_DIM] / den

    for c in range(nch):
        qk(0, c)
    ms = [[], []]
    outs = [[], []]
    for c in range(nch):
        qk(1, c)
        ms[0] += [softmax(0, c * per + k) for k in range(per)]
    for c in range(nch):
        outs[0].append(pv(0, c, ms[0][c * per:(c + 1) * per]))
        ms[1] += [softmax(1, c * per + k) for k in range(per)]
    for c in range(nch):
        outs[1].append(pv(1, c, ms[1][c * per:(c + 1) * per]))
    tiles = []
    for g in range(N_KV):
        o_t = jnp.concatenate(outs[g], axis=1)
        tiles.append(jnp.concatenate([o_t[:, h * w:(h + 1) * w] for h in range(GROUP)], axis=0).T)
    o_ref[...] = jnp.concatenate(tiles, axis=1).astype(o_ref.dtype)


def swa_prompt(q, k, v, sinks, rel_bias, batch, seq):
    w = SWA_WINDOW
    nb = seq // w
    rows = GROUP * w
    kidx = np.arange(2 * w)[:, None]
    qq = (np.arange(rows) % w)[None, :]
    dist = w + qq - kidx
    bm = _bucket_matrix(dist, (dist >= 0) & (dist < w))
    rbx = jnp.repeat(rel_bias.reshape(N_BUCKETS, N_KV, GROUP).transpose(1, 0, 2), w, axis=2)
    chunk, sub = 256, 128
    nsub = rows // sub
    bias = bias_table(bm, rbx, jnp.zeros((N_KV, 1, rows), F32), LOG2E)
    bias = bias.reshape(N_KV, 2 * w, nsub, sub).transpose(0, 2, 1, 3)
    sink_x = jnp.repeat(sinks.reshape(N_KV, 1, GROUP), w, axis=2)
    d = q.shape[1]
    cur = lambda b, n: (b * nb + n, 0)
    prev = lambda b, n: (jnp.maximum(b * nb + n - 1, 0), 0)
    return pl.pallas_call(
        functools.partial(_swa_prompt_kernel, chunk=chunk, sub=sub),
        grid=(batch, nb),
        in_specs=[pl.BlockSpec((w, d), cur),
                  pl.BlockSpec((w, KV_W), cur), pl.BlockSpec((w, KV_W), prev),
                  pl.BlockSpec((w, KV_W), cur), pl.BlockSpec((w, KV_W), prev),
                  pl.BlockSpec((N_KV, nsub, 2 * w, sub), lambda b, n: (0, 0, 0, 0)),
                  pl.BlockSpec((N_KV, 1, rows), lambda b, n: (0, 0, 0))],
        out_specs=pl.BlockSpec((w, d), cur),
        out_shape=jax.ShapeDtypeStruct(q.shape, BF16),
        scratch_shapes=[pltpu.VMEM((N_KV, nsub, 2 * w, sub), F32),
                        pltpu.VMEM((N_KV, nsub, 2 * w, sub), BF16)],
        compiler_params=_params("arbitrary", "arbitrary"),
        name="swa_prompt",
    )(q, k, k, v, v, bias, sink_x)


def _swa_sample_kernel(q_ref, kb_ref, kn_ref, vb_ref, vn_ref, bbuf_ref, bnew_ref, sink_ref, o_ref, kb_out, vb_out,
                       *, t_new):
    ns, _, wb = kb_ref.shape
    sink = sink_ref[...]
    keep = lax.broadcasted_iota(jnp.int32, (KV_W, wb), 1) < wb - t_new

    def shifted(buf_t, new_rows):
        return jnp.where(keep, pltpu.roll(buf_t, wb - t_new, axis=1), pltpu.roll(new_rows.T, wb - t_new, axis=1))

    rs = [slice(i * t_new, (i + 1) * t_new) for i in range(ns)]
    kn = [_pad_rows(kn_ref[r, :], wb) for r in rs]
    vn = [_pad_rows(vn_ref[r, :], wb) for r in rs]
    qbd = [_heads_to_rows(q_ref[r, :] * SCALE).astype(BF16) for r in rs]
    s_buf = [_dot(qbd[i], kb_ref[i].astype(BF16)) + bbuf_ref[...] for i in range(ns)]
    s_new = [_dot_nt(qbd[i], kn[i].astype(BF16)) + bnew_ref[...] for i in range(ns)]
    p_buf, p_new, den = [], [], []
    for i in range(ns):
        m = jnp.maximum(jnp.maximum(jnp.max(s_buf[i], axis=1, keepdims=True),
                                    jnp.max(s_new[i], axis=1, keepdims=True)), sink)
        p_buf.append(jnp.exp(s_buf[i] - m))
        p_new.append(jnp.exp(s_new[i] - m))
        den.append(jnp.sum(p_buf[i], axis=1, keepdims=True) + jnp.sum(p_new[i], axis=1, keepdims=True)
                   + jnp.exp(sink - m))
    for i in range(ns):
        acc = (_dot_nt(p_buf[i].astype(BF16), vb_ref[i].astype(BF16))
               + _dot(p_new[i].astype(BF16), vn[i].astype(BF16)))
        o_ref[rs[i], :] = _rows_to_heads(acc / den[i], t_new)
        kb_out[i] = shifted(kb_ref[i], kn[i])
        vb_out[i] = shifted(vb_ref[i], vn[i])


def swa_sample(q, k_buf_t, v_buf_t, k_new, v_new, sinks, rel_bias, t_new, ns=8):
    n, _, wb = k_buf_t.shape
    assert wb % LANES == 0 and wb >= t_new, "the window buffer fills whole lane tiles"
    rows = N_HEADS * t_new
    tok = (np.arange(rows) % t_new)[None, :]
    dist_buf = tok + wb - np.arange(wb)[:, None]
    j_new = np.arange(wb)[:, None]
    dist_new = tok - j_new
    bm_buf = _bucket_matrix(dist_buf, (dist_buf >= 0) & (dist_buf < SWA_WINDOW))
    bm_new = _bucket_matrix(dist_new, (dist_new >= 0) & (dist_new < SWA_WINDOW) & (j_new < t_new))
    rbx = jnp.repeat(rel_bias, t_new, axis=1)[None]
    zero = jnp.zeros((1, 1, rows), F32)
    b_buf = bias_table(bm_buf, rbx, zero)[0].T
    b_new = bias_table(bm_new, rbx, zero)[0].T
    sink_r = jnp.repeat(sinks, t_new).reshape(rows, 1)
    d = q.shape[1]
    tile = lambda width: pl.BlockSpec((ns * t_new, width), lambda i: (i, 0))
    buf = pl.BlockSpec((ns, KV_W, wb), lambda i: (i, 0, 0))
    table = pl.BlockSpec((rows, wb), lambda i: (0, 0))
    buf_shape = jax.ShapeDtypeStruct(k_buf_t.shape, F32)
    return pl.pallas_call(
        functools.partial(_swa_sample_kernel, t_new=t_new),
        grid=(n // ns,),
        in_specs=[tile(d), buf, tile(KV_W), buf, tile(KV_W), table, table,
                  pl.BlockSpec((rows, 1), lambda i: (0, 0))],
        out_specs=[tile(d), buf, buf],
        out_shape=[jax.ShapeDtypeStruct(q.shape, F32), buf_shape, buf_shape],
        compiler_params=_params("arbitrary"),
        name="swa_sample",
    )(q, k_buf_t, k_new, v_buf_t, v_new, b_buf, b_new, sink_r)


def kernel(x_prompt, x_sample, c_prompt, c_sample, cache_a_k, cache_a_v, page_table, cache_b_conv, cache_c_k, cache_c_v, rel_bias, norm_a, mod_w_a, mod_b_a, w_in_a, conv_w_b, conv_b_b, ln_g_b, ln_b_b, w_out_a, norm_c, mod_w_c, mod_b_c, w_in_c, sinks_c, w_out_c, final_norm):
    batch, seq, d = x_prompt.shape
    n, t_new, _ = x_sample.shape
    n_pool, page = cache_a_k.shape[1], cache_a_k.shape[2]
    hq = N_HEADS * HEAD_DIM
    cb = conv_w_b.shape[2]
    assert norm_a.shape[0] == 1 and norm_c.shape[0] == 1, "one A/B layer followed by one C layer"

    tm_p = 1024
    tm_s = min(256, n * t_new)
    xp = x_prompt.reshape(batch * seq, d)
    xs = x_sample

    c_all = jnp.concatenate([c_prompt, c_sample], axis=0)
    c_rows = -(-c_all.shape[0] // 8) * 8
    c_all = jnp.pad(c_all, ((0, c_rows - c_all.shape[0]), (0, 0)))

    def split_mod(m):
        mp = m[:batch].reshape(batch, 1, 3, d)
        ms = m[batch:batch + n].reshape(n, 1, 3, d)
        return [(mp[:, :, j], ms[:, :, j]) for j in range(3)]

    (sh_a, sc_a, gt_a) = split_mod(modulation(c_all, mod_w_a[0], mod_b_a[0]))
    (sh_c, sc_c, gt_c) = split_mod(modulation(c_all, mod_w_c[0], mod_b_c[0]))

    kv0, kv1 = hq, hq + 2 * KV_W
    za0 = kv1
    ga0 = za0 + hq
    gb0 = ga0 + cb
    zb0 = gb0 + cb
    segs_a = (("raw", 0, hq), ("raw", kv0, kv0 + KV_W), ("raw", kv0 + KV_W, kv1),
              ("silu", za0, ga0), ("glu", ga0, gb0, gb0, zb0), ("silu", zb0, zb0 + cb))
    dts_a = (F32, F32, F32, BF16, F32, BF16)
    w_in_a16 = w_in_a[0].astype(BF16)
    w_out_a16 = w_out_a[0].astype(BF16)
    k_pool = cache_a_k[0].transpose(0, 2, 3, 1).reshape(n_pool, KV_W, page)
    v_pool = cache_a_v[0].transpose(0, 2, 3, 1).reshape(n_pool, KV_W, page)

    segs_ap = segs_a + (("kv_rows", kv0, kv0 + KV_W), ("kv_rows", kv0 + KV_W, kv1))
    q, k, v, sza, u, szb, k_rows, v_rows = ln_inproj(xp, sh_a[0], sc_a[0], norm_a[0], w_in_a16, segs_ap,
                                                     dts_a + (F32, F32), tm_p)
    oa = moba_prompt(q, k.reshape(batch, seq, KV_W), v.reshape(batch, seq, KV_W), rel_bias, batch, seq)
    ob = conv_prompt(u, szb, conv_w_b[0], conv_b_b[0], ln_g_b[0], ln_b_b[0], batch, seq)
    xp1 = out_proj([(oa, sza, w_out_a16[:hq])], [(ob, w_out_a16[hq:])], xp, gt_a[0], None, tm_p)
    ak_p = k_rows.reshape(1, batch, seq, N_KV, KV_W)[..., :HEAD_DIM]
    av_p = v_rows.reshape(1, batch, seq, N_KV, KV_W)[..., :HEAD_DIM]
    bc_p = u.reshape(batch, seq, cb)[:, seq - (CONV_W - 1):]

    q, k, v, sza, u, szb = ln_inproj(xs, sh_a[1], sc_a[1], norm_a[0], w_in_a16, segs_a, dts_a, tm_s)
    oa = moba_sample(q, k, v, k_pool, v_pool, page_table, rel_bias, t_new)
    to_time_major = lambda a: a.reshape(n, t_new, cb).transpose(1, 0, 2)
    ob_t, hist_t = conv_sample(cache_b_conv[0].transpose(1, 0, 2), to_time_major(u), to_time_major(szb),
                               conv_w_b[0], conv_b_b[0], ln_g_b[0], ln_b_b[0])
    ob = ob_t.transpose(1, 0, 2).reshape(n * t_new, cb).astype(BF16)
    xs1 = out_proj([(oa, sza, w_out_a16[:hq])], [(ob, w_out_a16[hq:])], xs, gt_a[1], None, tm_s)
    ak_s, av_s = k, v
    bc_s = hist_t.transpose(1, 0, 2)

    segs_c = (("raw", 0, hq), ("raw", hq, hq + KV_W), ("raw", hq + KV_W, hq + 2 * KV_W),
              ("silu", hq + 2 * KV_W, 2 * hq + 2 * KV_W))
    dts_c = (F32, F32, F32, BF16)
    w_in_c16 = w_in_c[0].astype(BF16)
    w_out_c16 = w_out_c[0].astype(BF16)

    q, k, v, sz = ln_inproj(xp1, sh_c[0], sc_c[0], norm_c[0], w_in_c16, segs_c, dts_c, tm_p)
    o = swa_prompt(q, k, v, sinks_c[0], rel_bias, batch, seq)
    y_prompt = out_proj([(o, sz, w_out_c16)], [], xp1, gt_c[0], final_norm, tm_p)
    wb_p = min(SWA_WINDOW, seq)
    ck_p = k.reshape(batch, seq, KV_W)[:, seq - wb_p:]
    cv_p = v.reshape(batch, seq, KV_W)[:, seq - wb_p:]

    q, k, v, sz = ln_inproj(xs1, sh_c[1], sc_c[1], norm_c[0], w_in_c16, segs_c, dts_c, tm_s)
    wb_s = cache_c_k.shape[2]
    kb_t = cache_c_k[0].transpose(0, 2, 3, 1).reshape(n, KV_W, wb_s)
    vb_t = cache_c_v[0].transpose(0, 2, 3, 1).reshape(n, KV_W, wb_s)
    o, kb_t, vb_t = swa_sample(q, kb_t, vb_t, k, v, sinks_c[0], rel_bias, t_new)
    y_sample = out_proj([(o, sz, w_out_c16)], [], xs1, gt_c[1], final_norm, tm_s)
    ck_s = kb_t.reshape(n, N_KV, HEAD_DIM, wb_s).transpose(0, 3, 1, 2)[None]
    cv_s = vb_t.reshape(n, N_KV, HEAD_DIM, wb_s).transpose(0, 3, 1, 2)[None]

    def kv5(a, lead):
        return a.reshape((1,) + lead + (N_KV, HEAD_DIM))

    return (y_prompt.reshape(batch, seq, d), y_sample,
            ak_p, av_p,
            kv5(ak_s, (n, t_new)), kv5(av_s, (n, t_new)),
            bc_p[None], bc_s[None],
            kv5(ck_p, (batch, wb_p)), kv5(cv_p, (batch, wb_p)),
            ck_s, cv_s)
```

```python
import functools
import math

import jax
import jax.numpy as jnp
import numpy as np
from jax import lax
from jax.experimental import pallas as pl
from jax.experimental.pallas import tpu as pltpu

F32 = jnp.float32
BF16 = jnp.bfloat16
NEG_INF = float("-inf")
MASK_NEG = -1e30

HEAD_DIM = 64
N_HEADS = 16
N_KV = 2
GROUP = N_HEADS // N_KV
KV_W = N_KV * HEAD_DIM
MOBA_BLOCK = 256
MOBA_TOPK = 3
CONV_W = 31
SWA_WINDOW = 128
N_BUCKETS = 32
MAX_DISTANCE = 128
EPS = 1e-6
SCALE = HEAD_DIM ** -0.5
LOG2E = math.log2(math.e)

SUBLANES = 8
LANES = 128
VMEM_LIMIT = 56 * 2**20


def _params(*sem):
    return pltpu.CompilerParams(dimension_semantics=sem, vmem_limit_bytes=VMEM_LIMIT)


def _silu(z):
    return z * jax.nn.sigmoid(z)


def _dot(a, b):
    return jnp.dot(a, b, preferred_element_type=F32)


def _dot_nt(a, b):
    return lax.dot_general(a, b, (((1,), (1,)), ((), ())), preferred_element_type=F32)


def _dot_f32(a, b):
    return jnp.dot(a, b, preferred_element_type=F32, precision=lax.Precision.HIGHEST)


def _t5_bucket_np(n):
    n = np.maximum(n, 0)
    max_exact = N_BUCKETS // 2
    nf = np.maximum(n, 1).astype(np.float32)
    large = max_exact + (np.log(nf / np.float32(max_exact)) / np.float32(math.log(MAX_DISTANCE / max_exact))
                         * np.float32(N_BUCKETS - max_exact)).astype(np.int32)
    large = np.minimum(large, N_BUCKETS - 1)
    return np.where(n < max_exact, n, large).astype(np.int32)


def _bucket_matrix(dist, valid):
    return np.where(valid, _t5_bucket_np(dist), -1).astype(np.int32)


def _bias_kernel(bm_ref, rb_ref, sub_ref, o_ref, *, mul):
    bm = bm_ref[...]
    acc = jnp.zeros(bm.shape, F32)
    for b in range(N_BUCKETS):
        acc = jnp.where(bm == b, rb_ref[b:b + 1, :], acc)
    o_ref[...] = jnp.where(bm < 0, NEG_INF, (acc - sub_ref[...]) * mul)


def bias_table(bm, rbx, sub, mul=1.0):
    g, _, c = rbx.shape
    r = bm.shape[0]
    return pl.pallas_call(
        functools.partial(_bias_kernel, mul=mul),
        grid=(g,),
        in_specs=[pl.BlockSpec((r, c), lambda i: (0, 0)),
                  pl.BlockSpec((None, N_BUCKETS, c), lambda i: (i, 0, 0)),
                  pl.BlockSpec((None, 1, c), lambda i: (i, 0, 0))],
        out_specs=pl.BlockSpec((None, r, c), lambda i: (i, 0, 0)),
        out_shape=jax.ShapeDtypeStruct((g, r, c), F32),
        compiler_params=_params("arbitrary"),
        name="bias_table",
    )(jnp.asarray(bm), rbx, sub)


def _mod_kernel(c_ref, w_ref, b_ref, o_ref):
    o_ref[...] = _dot_f32(_silu(c_ref[...]), w_ref[...]) + b_ref[...]


def modulation(c, w, b):
    n, d = c.shape
    m = w.shape[1]
    tn = 512
    return pl.pallas_call(
        _mod_kernel,
        grid=(m // tn,),
        in_specs=[pl.BlockSpec((n, d), lambda j: (0, 0)),
                  pl.BlockSpec((d, tn), lambda j: (0, j)),
                  pl.BlockSpec((1, tn), lambda j: (0, j))],
        out_specs=pl.BlockSpec((n, tn), lambda j: (0, j)),
        out_shape=jax.ShapeDtypeStruct((n, m), F32),
        compiler_params=_params("arbitrary"),
        name="modulation",
    )(c, w, b.reshape(1, m))


def _ln_inproj_kernel(x_ref, shift_ref, scale_ref, g_ref, w_ref, *out_refs, segs):
    x = x_ref[...]
    y = x * lax.rsqrt(jnp.mean(x * x, axis=-1, keepdims=True) + EPS)
    h = (y * g_ref[...]) * (1.0 + scale_ref[...]) + shift_ref[...]
    h16 = h.reshape(-1, h.shape[-1]).astype(BF16)
    raw = {}
    for o_ref, seg in zip(out_refs, segs, strict=True):
        kind, lo, hi = seg[0], seg[1], seg[2]
        z = raw[lo, hi] if (lo, hi) in raw else _dot(h16, w_ref[:, lo:hi])
        raw[lo, hi] = z
        if kind == "kv_rows":
            tm = z.shape[0]
            o_ref[pl.ds(0, tm, stride=N_KV), :] = z
            o_ref[pl.ds(1, tm, stride=N_KV), :] = pltpu.roll(z, HEAD_DIM, axis=1)
            continue
        if kind == "silu":
            z = _silu(z)
        elif kind == "glu":
            z = z * jax.nn.sigmoid(_dot(h16, w_ref[:, seg[3]:seg[4]]))
        o_ref[...] = z.astype(o_ref.dtype)


def ln_inproj(x, shift, scale, norm_g, w16, segs, out_dtypes, tm):
    d = x.shape[-1]
    if x.ndim == 2:
        r = x.shape[0]
        nt = r // tm
        per_group = nt // shift.shape[0]
        x_spec = pl.BlockSpec((tm, d), lambda i: (i, 0))
        mod_spec = pl.BlockSpec((None, 1, d), lambda i: (i // per_group, 0, 0))
    else:
        n, t_new, _ = x.shape
        r = n * t_new
        nt = r // tm
        x_spec = pl.BlockSpec((tm // t_new, t_new, d), lambda i: (i, 0, 0))
        mod_spec = pl.BlockSpec((tm // t_new, 1, d), lambda i: (i, 0, 0))
    mult = [N_KV if s[0] == "kv_rows" else 1 for s in segs]
    out_shape = [jax.ShapeDtypeStruct((m * r, s[2] - s[1]), dt) for m, s, dt in zip(mult, segs, out_dtypes, strict=True)]
    out_specs = [pl.BlockSpec((m * tm, s[2] - s[1]), lambda i: (i, 0)) for m, s in zip(mult, segs)]
    return pl.pallas_call(
        functools.partial(_ln_inproj_kernel, segs=segs),
        grid=(nt,),
        in_specs=[x_spec, mod_spec, mod_spec,
                  pl.BlockSpec((1, d), lambda i: (0, 0)),
                  pl.BlockSpec(w16.shape, lambda i: (0, 0), pipeline_mode=pl.Buffered(1))],
        out_specs=out_specs,
        out_shape=out_shape,
        compiler_params=_params("arbitrary"),
        name="ln_inproj",
    )(x, shift, scale, norm_g.reshape(1, d), w16)


def _out_proj_kernel(*refs, n_gated, n_plain, final_norm):
    it = iter(refs)
    y = None
    for _ in range(n_gated):
        a_ref, m_ref, w_ref = next(it), next(it), next(it)
        t = _dot((a_ref[...] * m_ref[...].astype(F32)).astype(BF16), w_ref[...])
        y = t if y is None else y + t
    for _ in range(n_plain):
        a_ref, w_ref = next(it), next(it)
        t = _dot(a_ref[...], w_ref[...])
        y = t if y is None else y + t
    x_ref, gate_ref = next(it), next(it)
    xn = x_ref[...] + gate_ref[...] * y.reshape(x_ref.shape)
    if final_norm:
        fg_ref = next(it)
        xn = xn * lax.rsqrt(jnp.mean(xn * xn, axis=-1, keepdims=True) + EPS) * fg_ref[...]
    o_ref = next(it)
    o_ref[...] = xn


def _proj_chain_kernel(oa_ref, sza_ref, w1_ref, ob_ref, w2_ref, x_ref, gate_ref, shift_ref, scale_ref, g_ref, win_ref,
                       x1_ref, *out_refs, segs):
    y = _dot((oa_ref[...] * sza_ref[...].astype(F32)).astype(BF16), w1_ref[...]) + _dot(ob_ref[...], w2_ref[...])
    x1 = x_ref[...] + gate_ref[...] * y
    x1_ref[...] = x1
    yn = x1 * lax.rsqrt(jnp.mean(x1 * x1, axis=-1, keepdims=True) + EPS)
    h16 = ((yn * g_ref[...]) * (1.0 + scale_ref[...]) + shift_ref[...]).astype(BF16)
    for o_ref, seg in zip(out_refs, segs, strict=True):
        z = _dot(h16, win_ref[:, seg[1]:seg[2]])
        o_ref[...] = (_silu(z) if seg[0] == "silu" else z).astype(o_ref.dtype)


def proj_chain(oa, sza, w1, ob, w2, x, gate, shift, scale, norm_g, w_in16, segs, out_dtypes, tm):
    r, d = x.shape
    nt = r // tm
    per_group = nt // gate.shape[0]
    rows = lambda width: pl.BlockSpec((tm, width), lambda i: (i, 0))
    whole = lambda a: pl.BlockSpec(a.shape, lambda i: (0, 0), pipeline_mode=pl.Buffered(1))
    mod = pl.BlockSpec((None, 1, d), lambda i: (i // per_group, 0, 0))
    widths = [s[2] - s[1] for s in segs]
    return pl.pallas_call(
        functools.partial(_proj_chain_kernel, segs=segs),
        grid=(nt,),
        in_specs=[rows(oa.shape[1]), rows(sza.shape[1]), whole(w1), rows(ob.shape[1]), whole(w2), rows(d),
                  mod, mod, mod, pl.BlockSpec((1, d), lambda i: (0, 0)), whole(w_in16)],
        out_specs=[rows(d)] + [rows(w) for w in widths],
        out_shape=[jax.ShapeDtypeStruct((r, d), F32)]
                  + [jax.ShapeDtypeStruct((r, w), dt) for w, dt in zip(widths, out_dtypes, strict=True)],
        compiler_params=_params("arbitrary"),
        name="proj_chain",
    )(oa, sza, w1, ob, w2, x, gate, shift, scale, norm_g.reshape(1, d), w_in16)


def out_proj(gated, plain, x, gate, final_g, tm):
    d = x.shape[-1]
    if x.ndim == 2:
        nt = x.shape[0] // tm
        per_group = nt // gate.shape[0]
        x_spec = pl.BlockSpec((tm, d), lambda i: (i, 0))
        gate_spec = pl.BlockSpec((None, 1, d), lambda i: (i // per_group, 0, 0))
    else:
        n, t_new, _ = x.shape
        nt = n * t_new // tm
        x_spec = pl.BlockSpec((tm // t_new, t_new, d), lambda i: (i, 0, 0))
        gate_spec = pl.BlockSpec((tm // t_new, 1, d), lambda i: (i, 0, 0))
    args, specs = [], []
    for a, m, w in gated:
        args += [a, m, w]
        specs += [pl.BlockSpec((tm, a.shape[1]), lambda i: (i, 0)),
                  pl.BlockSpec((tm, m.shape[1]), lambda i: (i, 0)),
                  pl.BlockSpec(w.shape, lambda i: (0, 0))]
    for a, w in plain:
        args += [a, w]
        specs += [pl.BlockSpec((tm, a.shape[1]), lambda i: (i, 0)),
                  pl.BlockSpec(w.shape, lambda i: (0, 0))]
    args += [x, gate]
    specs += [x_spec, gate_spec]
    if final_g is not None:
        args.append(final_g.reshape(1, d))
        specs.append(pl.BlockSpec((1, d), lambda i: (0, 0)))
    return pl.pallas_call(
        functools.partial(_out_proj_kernel, n_gated=len(gated), n_plain=len(plain),
                          final_norm=final_g is not None),
        grid=(nt,),
        in_specs=specs,
        out_specs=x_spec,
        out_shape=jax.ShapeDtypeStruct(x.shape, F32),
        compiler_params=_params("arbitrary"),
        name="out_proj",
    )(*args)


def _top3_mask(scores, n_valid):
    nb = scores.shape[0]
    blk = lax.broadcasted_iota(jnp.int32, scores.shape, 0)
    s = jnp.where(blk < n_valid, scores, NEG_INF)
    picked = jnp.zeros(scores.shape, F32)
    for _ in range(MOBA_TOPK):
        mx = jnp.max(s, axis=0, keepdims=True)
        first = jnp.min(jnp.where(s == mx, blk, nb), axis=0, keepdims=True)
        hit = blk == first
        picked = jnp.where(hit, 1.0, picked)
        s = jnp.where(hit, NEG_INF, s)
    return jnp.where(blk < n_valid, picked, 0.0)


V_AUG = HEAD_DIM + 16


def _moba_prompt_kernel(q_ref, k_ref, v_ref, bown_ref, badj_ref, o_ref,
                        k16_s, vT_s, kbar_s, qT16_s, sel_s, m_s, acc_s,
                        s0_s, s1_s, p0_s, p1_s, a0_s, a1_s, *, nblk, chunk, sub):
    g = pl.program_id(1)
    i = pl.program_id(2)
    rows = GROUP * MOBA_BLOCK
    nch = rows // chunk

    @pl.when((g == 0) & (i == 0))
    def _per_batch():
        ones = jnp.ones((V_AUG - HEAD_DIM, MOBA_BLOCK), BF16)
        for j in range(nblk):
            kb = k_ref[j * MOBA_BLOCK:(j + 1) * MOBA_BLOCK, :]
            k16_s[j * MOBA_BLOCK:(j + 1) * MOBA_BLOCK, :] = kb.astype(BF16)
            kbar_s[j:j + 1, :] = jnp.mean(kb, axis=0, keepdims=True)
            v_t = v_ref[j * MOBA_BLOCK:(j + 1) * MOBA_BLOCK, :].T.astype(BF16)
            for kv in range(N_KV):
                vT_s[kv, j] = jnp.concatenate([v_t[kv * HEAD_DIM:(kv + 1) * HEAD_DIM], ones], axis=0)

    qT = (q_ref[...] * (SCALE * LOG2E)).T
    qs = jnp.concatenate([qT[h * HEAD_DIM:(h + 1) * HEAD_DIM, :] for h in range(GROUP)], axis=1)
    qs16 = qs.astype(BF16)
    zeros = jnp.zeros_like(qs16)
    qpad16 = jnp.where(g == 0, jnp.concatenate([qs16, zeros], axis=0), jnp.concatenate([zeros, qs16], axis=0))
    for c in range(nch):
        qT16_s[c] = qpad16[:, c * chunk:(c + 1) * chunk]
    kbar = kbar_s[...]
    kbar_g = jnp.where(g == 0, kbar[:, :HEAD_DIM], kbar[:, HEAD_DIM:])
    sel_s[...] = _top3_mask(_dot_f32(kbar_g, qs), i)
    m_s[...] = jnp.full(m_s.shape, NEG_INF, F32)
    acc_s[...] = jnp.zeros(acc_s.shape, F32)

    n_far = i - 1
    sbuf, pbuf, abuf = (s0_s, s1_s), (p0_s, p1_s), (a0_s, a1_s)

    def blk_of(t):
        far_j = jnp.clip(t - 2, 0, jnp.maximum(n_far - 1, 0))
        return jnp.where(t == 0, i, jnp.where(t == 1, jnp.maximum(i - 1, 0), far_j))

    per = chunk // sub

    def stage_qk(t, slot, c):
        kb = k16_s[pl.ds(pl.multiple_of(blk_of(t) * MOBA_BLOCK, MOBA_BLOCK), MOBA_BLOCK), :]
        s_t = _dot(kb, qT16_s[c])
        for k in range(per):
            sbuf[slot][c * per + k] = s_t[:, k * sub:(k + 1) * sub]

    def stage_softmax(slot, c, on, bias_ref):
        ln = slice(c * sub, (c + 1) * sub)
        s_t = sbuf[slot][c]
        if bias_ref is not None:
            s_t = s_t + bias_ref[c]
        m_old = m_s[:, ln]
        m_new = jnp.maximum(m_old, jnp.where(on, jnp.max(s_t, axis=0, keepdims=True), NEG_INF))
        pbuf[slot][c] = jnp.exp2(s_t - jnp.where(on, m_new, jnp.inf)).astype(BF16)
        abuf[slot][:, ln] = jnp.exp2(m_old - m_new)
        m_s[:, ln] = m_new

    def stage_pv(t, slot, c):
        p = jnp.concatenate([pbuf[slot][c * per + k] for k in range(per)], axis=1)
        acc_s[c] = acc_s[c] * abuf[slot][:, c * chunk:(c + 1) * chunk] + _dot(vT_s[g, blk_of(t)], p)

    def pipe_half(t, slot, bias_ref=None, pv=True, qk=True):
        own = bias_ref is bown_ref
        valid = None if own else jnp.where(t == 1, i >= 1, t - 2 < n_far)
        j = blk_of(t)
        for c in range(nch):
            if qk:
                stage_qk(t + 1, 1 - slot, c)
            if own:
                on = jnp.full((1, chunk), True)
            else:
                on = (sel_s[pl.ds(j, 1), c * chunk:(c + 1) * chunk] > 0.0) & valid
            for k in range(per):
                stage_softmax(slot, c * per + k, on[:, k * sub:(k + 1) * sub], bias_ref)
            if pv:
                stage_pv(t - 1, 1 - slot, c)

    for c in range(nch):
        stage_qk(0, 0, c)
    pipe_half(0, 0, bown_ref, pv=False)
    pipe_half(1, 1, badj_ref)

    def pair(u, carry):
        pipe_half(2 + 2 * u, 0)
        pipe_half(3 + 2 * u, 1)
        return carry

    n_pairs = jnp.maximum(n_far, 0) // 2
    lax.fori_loop(0, n_pairs, pair, 0)
    t_tail = 2 + 2 * n_pairs
    odd = jnp.maximum(n_far, 0) % 2 == 1

    @pl.when(odd)
    def _tail():
        pipe_half(t_tail, 0, qk=False)
        for c in range(nch):
            stage_pv(t_tail, 0, c)

    @pl.when(jnp.logical_not(odd))
    def _drain():
        for c in range(nch):
            stage_pv(t_tail - 1, 1, c)

    acc = jnp.concatenate([acc_s[c] for c in range(nch)], axis=1)
    o_t = acc[:HEAD_DIM] / acc[HEAD_DIM:HEAD_DIM + 1]
    o_cat = jnp.concatenate([o_t[:, h * MOBA_BLOCK:(h + 1) * MOBA_BLOCK] for h in range(GROUP)], axis=0)
    o_ref[...] = o_cat.T.astype(o_ref.dtype)


def moba_prompt(q, k, v, rel_bias, batch, seq):
    nblk = seq // MOBA_BLOCK
    rows = GROUP * MOBA_BLOCK
    kk = np.arange(MOBA_BLOCK)[:, None]
    qq = (np.arange(rows) % MOBA_BLOCK)[None, :]
    bm_own = _bucket_matrix(qq - kk, qq >= kk)
    bm_adj = _bucket_matrix(MOBA_BLOCK + qq - kk, np.ones((MOBA_BLOCK, rows), bool))
    rbx = jnp.repeat(rel_bias.reshape(N_BUCKETS, N_KV, GROUP).transpose(1, 0, 2), MOBA_BLOCK, axis=2)
    far = rbx[:, N_BUCKETS - 1:N_BUCKETS, :]
    chunk, sub = 256, 128
    nsub = rows // sub

    def chunk_major(tbl):
        return tbl.reshape(N_KV, MOBA_BLOCK, nsub, sub).transpose(0, 2, 1, 3)

    b_own = chunk_major(bias_table(bm_own, rbx, far, LOG2E))
    b_adj = chunk_major(bias_table(bm_adj, rbx, far, LOG2E))
    qw = GROUP * HEAD_DIM
    tbl_spec = pl.BlockSpec((None, nsub, MOBA_BLOCK, sub), lambda b, g, i: (g, 0, 0, 0))
    return pl.pallas_call(
        functools.partial(_moba_prompt_kernel, nblk=nblk, chunk=chunk, sub=sub),
        grid=(batch, N_KV, nblk),
        in_specs=[pl.BlockSpec((MOBA_BLOCK, qw), lambda b, g, i: (b * nblk + i, g)),
                  pl.BlockSpec((None, seq, KV_W), lambda b, g, i: (b, 0, 0)),
                  pl.BlockSpec((None, seq, KV_W), lambda b, g, i: (b, 0, 0)),
                  tbl_spec, tbl_spec],
        out_specs=pl.BlockSpec((MOBA_BLOCK, qw), lambda b, g, i: (b * nblk + i, g)),
        out_shape=jax.ShapeDtypeStruct(q.shape, BF16),
        scratch_shapes=[pltpu.VMEM((seq, KV_W), BF16),
                        pltpu.VMEM((N_KV, nblk, V_AUG, MOBA_BLOCK), BF16),
                        pltpu.VMEM((nblk, KV_W), F32),
                        pltpu.VMEM((rows // chunk, KV_W, chunk), BF16),
                        pltpu.VMEM((nblk, rows), F32),
                        pltpu.VMEM((1, rows), F32),
                        pltpu.VMEM((rows // chunk, V_AUG, chunk), F32),
                        pltpu.VMEM((nsub, MOBA_BLOCK, sub), F32), pltpu.VMEM((nsub, MOBA_BLOCK, sub), F32),
                        pltpu.VMEM((nsub, MOBA_BLOCK, sub), BF16), pltpu.VMEM((nsub, MOBA_BLOCK, sub), BF16),
                        pltpu.VMEM((1, rows), F32), pltpu.VMEM((1, rows), F32)],
        compiler_params=_params("arbitrary", "arbitrary", "arbitrary"),
        name="moba_prompt",
    )(q, k, v, b_own, b_adj)


def _heads_to_rows(q):
    t = q.shape[0]
    lane = lax.broadcasted_iota(jnp.int32, (t, LANES), 1)
    pieces = []
    for h in range(N_HEADS):
        src = q[:, (h // 2) * LANES:(h // 2 + 1) * LANES]
        g = h // GROUP
        if h % 2 != g:
            src = pltpu.roll(src, HEAD_DIM, axis=1)
        keep = lane < HEAD_DIM if g == 0 else lane >= HEAD_DIM
        pieces.append(jnp.where(keep, src, 0.0))
    return jnp.concatenate(pieces, axis=0)


def _rows_to_heads(acc, t):
    lane = lax.broadcasted_iota(jnp.int32, (t, LANES), 1)
    cols = []
    for k in range(N_HEADS // 2):
        a = acc[2 * k * t:(2 * k + 1) * t, :]
        b = acc[(2 * k + 1) * t:(2 * k + 2) * t, :]
        if (2 * k) // GROUP == 1:
            a = pltpu.roll(a, HEAD_DIM, axis=1)
        else:
            b = pltpu.roll(b, HEAD_DIM, axis=1)
        cols.append(jnp.where(lane < HEAD_DIM, a, b))
    return jnp.concatenate(cols, axis=1)


def _pad_rows(x, rows):
    return jnp.concatenate([x, jnp.zeros((rows - x.shape[0], x.shape[1]), x.dtype)], axis=0)


def _moba_sample_kernel(pt_ref, q_ref, kn_ref, vn_ref, blast_ref, bown_ref, e_ref, kpool, vpool, o_ref,
                        kbuf, vbuf, ksem, vsem, s_s, *, n_pages, page, n_samples, unroll):
    s = pl.program_id(0)
    slot = s % 2
    ppb = MOBA_BLOCK // page
    nblk = n_pages // ppb
    t_new = q_ref.shape[0]
    rows = N_HEADS * t_new

    def k_copy(smp, sl, p):
        return pltpu.make_async_copy(kpool.at[pt_ref[smp, p]], kbuf.at[sl, p], ksem.at[sl])

    def v_copy(smp, sl, p):
        return pltpu.make_async_copy(vpool.at[pt_ref[smp, p]], vbuf.at[sl, p], vsem.at[sl])

    def start_all(smp, sl):
        def body(p, c):
            k_copy(smp, sl, p).start()
            v_copy(smp, sl, p).start()
            return c
        lax.fori_loop(0, n_pages, body, 0, unroll=8)

    def wait_all(copy, smp, sl):
        def body(p, c):
            copy(smp, sl, p).wait()
            return c
        lax.fori_loop(0, n_pages, body, 0, unroll=8)

    def block_t(buf, j):
        return jnp.concatenate([buf[slot, j * ppb + t] for t in range(ppb)], axis=1)

    blk_lane = lax.broadcasted_iota(jnp.int32, (KV_W, nblk), 1)

    @pl.when(s == 0)
    def _first():
        start_all(0, 0)

    wait_all(k_copy, s, slot)
    nxt = jnp.minimum(s + 1, n_samples - 1)

    def kbar_body(j, kbar_t):
        for t in range(ppb):
            k_copy(nxt, 1 - slot, j * ppb + t).start()
            v_copy(nxt, 1 - slot, j * ppb + t).start()
        pages = kbuf[slot, j * ppb]
        for t in range(1, ppb):
            pages = pages + kbuf[slot, j * ppb + t]
        col = jnp.sum(pages, axis=1, keepdims=True) * (1.0 / MOBA_BLOCK)
        return jnp.where(blk_lane == j, col, kbar_t)

    kbar_t = lax.fori_loop(0, nblk, kbar_body, jnp.zeros((KV_W, nblk), F32), unroll=unroll[0])

    qbd = _heads_to_rows(q_ref[...] * (SCALE * LOG2E))
    qbd16 = qbd.astype(BF16)

    scores = _dot_f32(qbd, kbar_t)
    sel_t = _top3_mask(scores.T, nblk)
    negm = jnp.where(sel_t.T > 0.0, 0.0, MASK_NEG)
    lhs16 = jnp.concatenate([qbd, negm, jnp.zeros((rows, e_ref.shape[1] - nblk), F32)], axis=1).astype(BF16)

    def logits(j):
        rhs16 = jnp.concatenate([block_t(kbuf, j).astype(BF16), e_ref[j]], axis=0)
        return _dot(lhs16, rhs16)

    def fold(x):
        return x[:, :LANES], x[:, LANES:]

    def far(j, mrun):
        st = logits(j)
        s_s[j] = st
        lo, hi = fold(st)
        return jnp.maximum(mrun, jnp.maximum(lo, hi))

    mrun = lax.fori_loop(0, nblk - 1, far, jnp.full((rows, LANES), NEG_INF, F32), unroll=unroll[1])
    s_last = logits(nblk - 1) + blast_ref[...]
    s_s[nblk - 1] = s_last
    kn16 = _pad_rows(kn_ref[...], LANES).astype(BF16)
    s_own = _dot_nt(qbd16, kn16) + bown_ref[...]
    for part in fold(s_last) + (s_own,):
        mrun = jnp.maximum(mrun, part)
    m = jnp.max(mrun, axis=1, keepdims=True)

    wait_all(v_copy, s, slot)

    def pv(j, carry):
        acc, lsum = carry
        p = jnp.exp2(s_s[j] - m)
        lo, hi = fold(p)
        return acc + _dot_nt(p.astype(BF16), block_t(vbuf, j).astype(BF16)), lsum + (lo + hi)

    p_own = jnp.exp2(s_own - m)
    acc0 = _dot(p_own.astype(BF16), _pad_rows(vn_ref[...], LANES).astype(BF16))
    acc, lsum = lax.fori_loop(0, nblk, pv, (acc0, p_own), unroll=unroll[2])
    den = jnp.sum(lsum, axis=1, keepdims=True)
    o_ref[...] = _rows_to_heads(acc / den, t_new)

    @pl.when(s == n_samples - 1)
    def _drain():
        wait_all(k_copy, nxt, 1 - slot)
        wait_all(v_copy, nxt, 1 - slot)


def moba_sample(q, k_new, v_new, k_pool, v_pool, page_table, rel_bias, t_new):
    n, n_pages = page_table.shape
    page = k_pool.shape[2]
    past = n_pages * page
    nblk = past // MOBA_BLOCK
    rows = N_HEADS * t_new
    tok = (np.arange(rows) % t_new)[None, :]
    kk = np.arange(MOBA_BLOCK)[:, None]
    bm_last = _bucket_matrix(MOBA_BLOCK + tok - kk, np.ones((MOBA_BLOCK, rows), bool))
    ko = np.arange(LANES)[:, None]
    bm_own = _bucket_matrix(tok - ko, (ko <= tok) & (ko < t_new))
    rbx = jnp.repeat(rel_bias, t_new, axis=1)[None]
    far = rbx[:, N_BUCKETS - 1:N_BUCKETS, :]
    b_last = bias_table(bm_last, rbx, far, LOG2E)[0].T
    b_own = bias_table(bm_own, rbx, far, LOG2E)[0].T
    onehot_np = np.zeros((nblk, KV_W, MOBA_BLOCK), np.float32)
    onehot_np[np.arange(nblk), np.arange(nblk)] = 1.0
    onehot = jnp.asarray(onehot_np, BF16)

    grid_spec = pltpu.PrefetchScalarGridSpec(
        num_scalar_prefetch=1,
        grid=(n,),
        in_specs=[pl.BlockSpec((t_new, q.shape[1]), lambda s, pt: (s, 0)),
                  pl.BlockSpec((t_new, KV_W), lambda s, pt: (s, 0)),
                  pl.BlockSpec((t_new, KV_W), lambda s, pt: (s, 0)),
                  pl.BlockSpec((rows, MOBA_BLOCK), lambda s, pt: (0, 0)),
                  pl.BlockSpec((rows, LANES), lambda s, pt: (0, 0)),
                  pl.BlockSpec((nblk, KV_W, MOBA_BLOCK), lambda s, pt: (0, 0, 0)),
                  pl.BlockSpec(memory_space=pl.ANY),
                  pl.BlockSpec(memory_space=pl.ANY)],
        out_specs=pl.BlockSpec((t_new, q.shape[1]), lambda s, pt: (s, 0)),
        scratch_shapes=[pltpu.VMEM((2, n_pages, KV_W, page), F32),
                        pltpu.VMEM((2, n_pages, KV_W, page), F32),
                        pltpu.SemaphoreType.DMA((2,)),
                        pltpu.SemaphoreType.DMA((2,)),
                        pltpu.VMEM((nblk, rows, MOBA_BLOCK), F32)],
    )
    return pl.pallas_call(
        functools.partial(_moba_sample_kernel, n_pages=n_pages, page=page, n_samples=n, unroll=(32, 63, 32)),
        grid_spec=grid_spec,
        out_shape=jax.ShapeDtypeStruct(q.shape, F32),
        compiler_params=_params("arbitrary"),
        name="moba_sample",
    )(page_table, q, k_new, v_new, b_last, b_own, onehot, k_pool, v_pool)


def _conv_tail(y, cb_ref, lg_ref, lb_ref, gate):
    y = y + cb_ref[...]
    mu = jnp.mean(y, axis=-1, keepdims=True)
    var = jnp.mean(jnp.square(y - mu), axis=-1, keepdims=True)
    yn = (y - mu) * lax.rsqrt(var + EPS) * lg_ref[...] + lb_ref[...]
    return _silu(yn) * gate


HALO = 32


CONV_ROWS = 64


def _conv_prompt_kernel(u_ref, prev_ref, g_ref, w_ref, cb_ref, lg_ref, lb_ref, o_ref, xs, xr, ys, *, tl):
    t = pl.program_id(1)
    c = u_ref.shape[1]
    xs[0:HALO, :] = jnp.where(t > 0, prev_ref[...], 0.0)
    xs[HALO:, :] = u_ref[...]
    span = tl + HALO - SUBLANES
    for r in range(1, SUBLANES):
        xr[r - 1] = xs[pl.ds(r, span), :]
    off = HALO - (CONV_W - 1)

    def taps(base, lt):
        ln = slice(lt * LANES, (lt + 1) * LANES)
        acc = jnp.zeros((CONV_ROWS, LANES), F32)
        for k in range(CONV_W):
            a, r = divmod(off + k, SUBLANES)
            src = xs if r == 0 else xr.at[r - 1]
            start = base + a * SUBLANES
            acc = acc + src[start:start + CONV_ROWS, ln] * w_ref[k:k + 1, ln]
        ys[base:base + CONV_ROWS, ln] = acc

    for base in range(0, tl, CONV_ROWS):
        for lt in range(0, c // LANES, 2):
            @pl.when(t >= 0)
            def _():
                taps(base, lt)
                taps(base, lt + 1)
    o_ref[...] = _conv_tail(ys[...], cb_ref, lg_ref, lb_ref, g_ref[...].astype(F32)).astype(o_ref.dtype)


def conv_prompt(u, szb, conv_w, conv_b, ln_g, ln_b, batch, seq, tl=512):
    c = u.shape[1]
    nt = seq // tl
    vec = pl.BlockSpec((1, c), lambda b, t: (0, 0))
    return pl.pallas_call(
        functools.partial(_conv_prompt_kernel, tl=tl),
        grid=(batch, nt),
        in_specs=[pl.BlockSpec((tl, c), lambda b, t: (b * nt + t, 0)),
                  pl.BlockSpec((HALO, c), lambda b, t: (jnp.maximum((b * nt + t) * (tl // HALO) - 1, 0), 0)),
                  pl.BlockSpec((tl, c), lambda b, t: (b * nt + t, 0)),
                  pl.BlockSpec((CONV_W, c), lambda b, t: (0, 0)), vec, vec, vec],
        out_specs=pl.BlockSpec((tl, c), lambda b, t: (b * nt + t, 0)),
        out_shape=jax.ShapeDtypeStruct(u.shape, BF16),
        scratch_shapes=[pltpu.VMEM((HALO + tl, c), F32),
                        pltpu.VMEM((SUBLANES - 1, HALO + tl - SUBLANES, c), F32),
                        pltpu.VMEM((tl, c), F32)],
        compiler_params=_params("arbitrary", "arbitrary"),
        name="conv_prompt",
    )(u, u, szb, conv_w, conv_b.reshape(1, c), ln_g.reshape(1, c), ln_b.reshape(1, c))


def _conv_sample_kernel(h_ref, u_ref, g_ref, w_ref, cb_ref, lg_ref, lb_ref, o_ref, hn_ref, *, t_new):
    n_hist = h_ref.shape[0]
    c = h_ref.shape[2]

    def row(r):
        return h_ref.at[r] if r < n_hist else u_ref.at[r - n_hist]

    for t in range(t_new):
        for lt in range(c // LANES):
            ln = slice(lt * LANES, (lt + 1) * LANES)
            acc = row(t)[:, ln] * w_ref[0:1, ln]
            for k in range(1, CONV_W):
                acc = acc + row(t + k)[:, ln] * w_ref[k:k + 1, ln]
            o_ref[t, :, ln] = acc
    for t in range(t_new):
        o_ref[t] = _conv_tail(o_ref[t], cb_ref, lg_ref, lb_ref, g_ref[t].astype(F32))
    hn_ref[0:n_hist - t_new] = h_ref[t_new:n_hist]
    hn_ref[n_hist - t_new:n_hist] = u_ref[...]


def conv_sample(hist_t, u_t, szb_t, conv_w, conv_b, ln_g, ln_b, ns=16):
    n_hist, n, c = hist_t.shape
    t_new = u_t.shape[0]
    vec = pl.BlockSpec((1, c), lambda i: (0, 0))
    hist = pl.BlockSpec((n_hist, ns, c), lambda i: (0, i, 0))
    new = pl.BlockSpec((t_new, ns, c), lambda i: (0, i, 0))
    return pl.pallas_call(
        functools.partial(_conv_sample_kernel, t_new=t_new),
        grid=(n // ns,),
        in_specs=[hist, new, new, pl.BlockSpec((CONV_W, c), lambda i: (0, 0)), vec, vec, vec],
        out_specs=[new, hist],
        out_shape=[jax.ShapeDtypeStruct(u_t.shape, F32), jax.ShapeDtypeStruct(hist_t.shape, F32)],
        compiler_params=_params("arbitrary"),
        name="conv_sample",
    )(hist_t, u_t, szb_t, conv_w, conv_b.reshape(1, c), ln_g.reshape(1, c), ln_b.reshape(1, c))


def _swa_prompt_kernel(q_ref, kc_ref, kp_ref, vc_ref, vp_ref, bias_ref, sink_ref, o_ref, s_s, p_s, *, chunk, sub):
    n = pl.program_id(1)
    w = SWA_WINDOW
    rows = GROUP * w
    nch, per = rows // chunk, chunk // sub
    gw = GROUP * HEAD_DIM
    kcat = jnp.concatenate([kp_ref[...], kc_ref[...]], axis=0).astype(BF16)
    v_t = jnp.concatenate([vp_ref[...], vc_ref[...]], axis=0).T.astype(BF16)
    ones = jnp.ones((V_AUG - HEAD_DIM, 2 * w), BF16)
    no_prev = (lax.broadcasted_iota(jnp.int32, (2 * w, sub), 0) < w) & (n == 0)
    qpads, v_augs, sinks = [], [], []
    for g in range(N_KV):
        q_t = (q_ref[:, g * gw:(g + 1) * gw] * (SCALE * LOG2E)).T
        qs = jnp.concatenate([q_t[h * HEAD_DIM:(h + 1) * HEAD_DIM, :] for h in range(GROUP)], axis=1)
        zeros = jnp.zeros_like(qs)
        qpads.append(jnp.concatenate([qs, zeros] if g == 0 else [zeros, qs], axis=0).astype(BF16))
        v_augs.append(jnp.concatenate([v_t[g * HEAD_DIM:(g + 1) * HEAD_DIM], ones], axis=0))
        sinks.append(sink_ref[g] * LOG2E)

    def qk(g, c):
        s_t = _dot(kcat, qpads[g][:, c * chunk:(c + 1) * chunk])
        for k in range(per):
            s_s[g, c * per + k] = s_t[:, k * sub:(k + 1) * sub]

    def softmax(g, cs):
        s_t = jnp.where(no_prev, NEG_INF, s_s[g, cs] + bias_ref[g, cs])
        m = jnp.maximum(jnp.max(s_t, axis=0, keepdims=True), sinks[g][:, cs * sub:(cs + 1) * sub])
        p_s[g, cs] = jnp.exp2(s_t - m).astype(BF16)
        return m

    def pv(g, c, ms):
        p = jnp.concatenate([p_s[g, c * per + k] for k in range(per)], axis=1)
        o_t = _dot(v_augs[g], p)
        den = o_t[HEAD_DIM:HEAD_DIM + 1] + jnp.exp2(sinks[g][:, c * chunk:(c + 1) * chunk] - jnp.concatenate(ms, axis=1))
        return o_t[:HEAD_DIM] / den

    for c in range(nch):
        qk(0, c)
    ms = [[], []]
    outs = [[], []]
    for c in range(nch):
        qk(1, c)
        ms[0] += [softmax(0, c * per + k) for k in range(per)]
    for c in range(nch):
        outs[0].append(pv(0, c, ms[0][c * per:(c + 1) * per]))
        ms[1] += [softmax(1, c * per + k) for k in range(per)]
    for c in range(nch):
        outs[1].append(pv(1, c, ms[1][c * per:(c + 1) * per]))
    tiles = []
    for g in range(N_KV):
        o_t = jnp.concatenate(outs[g], axis=1)
        tiles.append(jnp.concatenate([o_t[:, h * w:(h + 1) * w] for h in range(GROUP)], axis=0).T)
    o_ref[...] = jnp.concatenate(tiles, axis=1).astype(o_ref.dtype)


def swa_prompt(q, k, v, sinks, rel_bias, batch, seq):
    w = SWA_WINDOW
    nb = seq // w
    rows = GROUP * w
    kidx = np.arange(2 * w)[:, None]
    qq = (np.arange(rows) % w)[None, :]
    dist = w + qq - kidx
    bm = _bucket_matrix(dist, (dist >= 0) & (dist < w))
    rbx = jnp.repeat(rel_bias.reshape(N_BUCKETS, N_KV, GROUP).transpose(1, 0, 2), w, axis=2)
    chunk, sub = 256, 128
    nsub = rows // sub
    bias = bias_table(bm, rbx, jnp.zeros((N_KV, 1, rows), F32), LOG2E)
    bias = bias.reshape(N_KV, 2 * w, nsub, sub).transpose(0, 2, 1, 3)
    sink_x = jnp.repeat(sinks.reshape(N_KV, 1, GROUP), w, axis=2)
    d = q.shape[1]
    cur = lambda b, n: (b * nb + n, 0)
    prev = lambda b, n: (jnp.maximum(b * nb + n - 1, 0), 0)
    return pl.pallas_call(
        functools.partial(_swa_prompt_kernel, chunk=chunk, sub=sub),
        grid=(batch, nb),
        in_specs=[pl.BlockSpec((w, d), cur),
                  pl.BlockSpec((w, KV_W), cur), pl.BlockSpec((w, KV_W), prev),
                  pl.BlockSpec((w, KV_W), cur), pl.BlockSpec((w, KV_W), prev),
                  pl.BlockSpec((N_KV, nsub, 2 * w, sub), lambda b, n: (0, 0, 0, 0)),
                  pl.BlockSpec((N_KV, 1, rows), lambda b, n: (0, 0, 0))],
        out_specs=pl.BlockSpec((w, d), cur),
        out_shape=jax.ShapeDtypeStruct(q.shape, BF16),
        scratch_shapes=[pltpu.VMEM((N_KV, nsub, 2 * w, sub), F32),
                        pltpu.VMEM((N_KV, nsub, 2 * w, sub), BF16)],
        compiler_params=_params("arbitrary", "arbitrary"),
        name="swa_prompt",
    )(q, k, k, v, v, bias, sink_x)


def _swa_sample_kernel(q_ref, kb_ref, kn_ref, vb_ref, vn_ref, bbuf_ref, bnew_ref, sink_ref, o_ref, kb_out, vb_out,
                       *, t_new):
    ns, _, wb = kb_ref.shape
    sink = sink_ref[...]
    keep = lax.broadcasted_iota(jnp.int32, (KV_W, wb), 1) < wb - t_new

    def shifted(buf_t, new_rows):
        return jnp.where(keep, pltpu.roll(buf_t, wb - t_new, axis=1), pltpu.roll(new_rows.T, wb - t_new, axis=1))

    rs = [slice(i * t_new, (i + 1) * t_new) for i in range(ns)]
    kn = [_pad_rows(kn_ref[r, :], wb) for r in rs]
    vn = [_pad_rows(vn_ref[r, :], wb) for r in rs]
    qbd = [_heads_to_rows(q_ref[r, :] * SCALE).astype(BF16) for r in rs]
    s_buf = [_dot(qbd[i], kb_ref[i].astype(BF16)) + bbuf_ref[...] for i in range(ns)]
    s_new = [_dot_nt(qbd[i], kn[i].astype(BF16)) + bnew_ref[...] for i in range(ns)]
    p_buf, p_new, den = [], [], []
    for i in range(ns):
        m = jnp.maximum(jnp.maximum(jnp.max(s_buf[i], axis=1, keepdims=True),
                                    jnp.max(s_new[i], axis=1, keepdims=True)), sink)
        p_buf.append(jnp.exp(s_buf[i] - m))
        p_new.append(jnp.exp(s_new[i] - m))
        den.append(jnp.sum(p_buf[i], axis=1, keepdims=True) + jnp.sum(p_new[i], axis=1, keepdims=True)
                   + jnp.exp(sink - m))
    for i in range(ns):
        acc = (_dot_nt(p_buf[i].astype(BF16), vb_ref[i].astype(BF16))
               + _dot(p_new[i].astype(BF16), vn[i].astype(BF16)))
        o_ref[rs[i], :] = _rows_to_heads(acc / den[i], t_new)
        kb_out[i] = shifted(kb_ref[i], kn[i])
        vb_out[i] = shifted(vb_ref[i], vn[i])


def swa_sample(q, k_buf_t, v_buf_t, k_new, v_new, sinks, rel_bias, t_new, ns=8):
    n, _, wb = k_buf_t.shape
    assert wb % LANES == 0 and wb >= t_new, "the window buffer fills whole lane tiles"
    rows = N_HEADS * t_new
    tok = (np.arange(rows) % t_new)[None, :]
    dist_buf = tok + wb - np.arange(wb)[:, None]
    j_new = np.arange(wb)[:, None]
    dist_new = tok - j_new
    bm_buf = _bucket_matrix(dist_buf, (dist_buf >= 0) & (dist_buf < SWA_WINDOW))
    bm_new = _bucket_matrix(dist_new, (dist_new >= 0) & (dist_new < SWA_WINDOW) & (j_new < t_new))
    rbx = jnp.repeat(rel_bias, t_new, axis=1)[None]
    zero = jnp.zeros((1, 1, rows), F32)
    b_buf = bias_table(bm_buf, rbx, zero)[0].T
    b_new = bias_table(bm_new, rbx, zero)[0].T
    sink_r = jnp.repeat(sinks, t_new).reshape(rows, 1)
    d = q.shape[1]
    tile = lambda width: pl.BlockSpec((ns * t_new, width), lambda i: (i, 0))
    buf = pl.BlockSpec((ns, KV_W, wb), lambda i: (i, 0, 0))
    table = pl.BlockSpec((rows, wb), lambda i: (0, 0))
    buf_shape = jax.ShapeDtypeStruct(k_buf_t.shape, F32)
    return pl.pallas_call(
        functools.partial(_swa_sample_kernel, t_new=t_new),
        grid=(n // ns,),
        in_specs=[tile(d), buf, tile(KV_W), buf, tile(KV_W), table, table,
                  pl.BlockSpec((rows, 1), lambda i: (0, 0))],
        out_specs=[tile(d), buf, buf],
        out_shape=[jax.ShapeDtypeStruct(q.shape, F32), buf_shape, buf_shape],
        compiler_params=_params("arbitrary"),
        name="swa_sample",
    )(q, k_buf_t, k_new, v_buf_t, v_new, b_buf, b_new, sink_r)


def kernel(x_prompt, x_sample, c_prompt, c_sample, cache_a_k, cache_a_v, page_table, cache_b_conv, cache_c_k, cache_c_v, rel_bias, norm_a, mod_w_a, mod_b_a, w_in_a, conv_w_b, conv_b_b, ln_g_b, ln_b_b, w_out_a, norm_c, mod_w_c, mod_b_c, w_in_c, sinks_c, w_out_c, final_norm):
    batch, seq, d = x_prompt.shape
    n, t_new, _ = x_sample.shape
    n_pool, page = cache_a_k.shape[1], cache_a_k.shape[2]
    hq = N_HEADS * HEAD_DIM
    cb = conv_w_b.shape[2]
    assert norm_a.shape[0] == 1 and norm_c.shape[0] == 1, "one A/B layer followed by one C layer"

    tm_p = 1024
    tm_s = min(256, n * t_new)
    xp = x_prompt.reshape(batch * seq, d)
    xs = x_sample

    c_all = jnp.concatenate([c_prompt, c_sample], axis=0)
    c_rows = -(-c_all.shape[0] // 8) * 8
    c_all = jnp.pad(c_all, ((0, c_rows - c_all.shape[0]), (0, 0)))

    def split_mod(m):
        mp = m[:batch].reshape(batch, 1, 3, d)
        ms = m[batch:batch + n].reshape(n, 1, 3, d)
        return [(mp[:, :, j], ms[:, :, j]) for j in range(3)]

    (sh_a, sc_a, gt_a) = split_mod(modulation(c_all, mod_w_a[0], mod_b_a[0]))
    (sh_c, sc_c, gt_c) = split_mod(modulation(c_all, mod_w_c[0], mod_b_c[0]))

    kv0, kv1 = hq, hq + 2 * KV_W
    za0 = kv1
    ga0 = za0 + hq
    gb0 = ga0 + cb
    zb0 = gb0 + cb
    segs_a = (("raw", 0, hq), ("raw", kv0, kv0 + KV_W), ("raw", kv0 + KV_W, kv1),
              ("silu", za0, ga0), ("glu", ga0, gb0, gb0, zb0), ("silu", zb0, zb0 + cb))
    dts_a = (F32, F32, F32, BF16, F32, BF16)
    w_in_a16 = w_in_a[0].astype(BF16)
    w_out_a16 = w_out_a[0].astype(BF16)
    k_pool = cache_a_k[0].transpose(0, 2, 3, 1).reshape(n_pool, KV_W, page)
    v_pool = cache_a_v[0].transpose(0, 2, 3, 1).reshape(n_pool, KV_W, page)

    segs_ap = segs_a + (("kv_rows", kv0, kv0 + KV_W), ("kv_rows", kv0 + KV_W, kv1))
    q, k, v, sza, u, szb, k_rows, v_rows = ln_inproj(xp, sh_a[0], sc_a[0], norm_a[0], w_in_a16, segs_ap,
                                                     dts_a + (F32, F32), tm_p)
    oa = moba_prompt(q, k.reshape(batch, seq, KV_W), v.reshape(batch, seq, KV_W), rel_bias, batch, seq)
    ob = conv_prompt(u, szb, conv_w_b[0], conv_b_b[0], ln_g_b[0], ln_b_b[0], batch, seq)
    oa_p, sza_p, ob_p = oa, sza, ob
    ak_p = k_rows.reshape(1, batch, seq, N_KV, KV_W)[..., :HEAD_DIM]
    av_p = v_rows.reshape(1, batch, seq, N_KV, KV_W)[..., :HEAD_DIM]
    bc_p = u.reshape(batch, seq, cb)[:, seq - (CONV_W - 1):]

    q, k, v, sza, u, szb = ln_inproj(xs, sh_a[1], sc_a[1], norm_a[0], w_in_a16, segs_a, dts_a, tm_s)
    oa = moba_sample(q, k, v, k_pool, v_pool, page_table, rel_bias, t_new)
    to_time_major = lambda a: a.reshape(n, t_new, cb).transpose(1, 0, 2)
    ob_t, hist_t = conv_sample(cache_b_conv[0].transpose(1, 0, 2), to_time_major(u), to_time_major(szb),
                               conv_w_b[0], conv_b_b[0], ln_g_b[0], ln_b_b[0])
    ob = ob_t.transpose(1, 0, 2).reshape(n * t_new, cb).astype(BF16)
    xs1 = out_proj([(oa, sza, w_out_a16[:hq])], [(ob, w_out_a16[hq:])], xs, gt_a[1], None, tm_s)
    ak_s, av_s = k, v
    bc_s = hist_t.transpose(1, 0, 2)

    segs_c = (("raw", 0, hq), ("raw", hq, hq + KV_W), ("raw", hq + KV_W, hq + 2 * KV_W),
              ("silu", hq + 2 * KV_W, 2 * hq + 2 * KV_W))
    dts_c = (F32, F32, F32, BF16)
    w_in_c16 = w_in_c[0].astype(BF16)
    w_out_c16 = w_out_c[0].astype(BF16)

    xp1, q, k, v, sz = proj_chain(oa_p, sza_p, w_out_a16[:hq], ob_p, w_out_a16[hq:], xp, gt_a[0], sh_c[0], sc_c[0],
                                  norm_c[0], w_in_c16, segs_c, dts_c, 512)
    o = swa_prompt(q, k, v, sinks_c[0], rel_bias, batch, seq)
    y_prompt = out_proj([(o, sz, w_out_c16)], [], xp1, gt_c[0], final_norm, tm_p)
    wb_p = min(SWA_WINDOW, seq)
    ck_p = k.reshape(batch, seq, KV_W)[:, seq - wb_p:]
    cv_p = v.reshape(batch, seq, KV_W)[:, seq - wb_p:]

    q, k, v, sz = ln_inproj(xs1, sh_c[1], sc_c[1], norm_c[0], w_in_c16, segs_c, dts_c, tm_s)
    wb_s = cache_c_k.shape[2]
    kb_t = cache_c_k[0].transpose(0, 2, 3, 1).reshape(n, KV_W, wb_s)
    vb_t = cache_c_v[0].transpose(0, 2, 3, 1).reshape(n, KV_W, wb_s)
    o, kb_t, vb_t = swa_sample(q, kb_t, vb_t, k, v, sinks_c[0], rel_bias, t_new)
    y_sample = out_proj([(o, sz, w_out_c16)], [], xs1, gt_c[1], final_norm, tm_s)
    ck_s = kb_t.reshape(n, N_KV, HEAD_DIM, wb_s).transpose(0, 3, 1, 2)[None]
    cv_s = vb_t.reshape(n, N_KV, HEAD_DIM, wb_s).transpose(0, 3, 1, 2)[None]

    def kv5(a, lead):
        return a.reshape((1,) + lead + (N_KV, HEAD_DIM))

    return (y_prompt.reshape(batch, seq, d), y_sample,
            ak_p, av_p,
            kv5(ak_s, (n, t_new)), kv5(av_s, (n, t_new)),
            bc_p[None], bc_s[None],
            kv5(ck_p, (batch, wb_p)), kv5(cv_p, (batch, wb_p)),
            ck_s, cv_s)
```
